```python
import math
import jax
import jax.numpy as jnp
from jax import lax
import numpy as np

D_MODEL = 1024
BATCH = 4
SEQ = 8192
DEPTH = 2

HEAD_DIM = 64
BRANCH_WIDTH = D_MODEL // 2
N_BRANCH = 3
CONV_WIDTH = BRANCH_WIDTH
CONV_K = 3
FOX_HEADS = BRANCH_WIDTH // HEAD_DIM
SWA_HEADS = BRANCH_WIDTH // HEAD_DIM
SWA_KV_HEADS = 2
SWA_GROUP = SWA_HEADS // SWA_KV_HEADS
WINDOW = 128
BLOCK = 128
N_BUCKETS = 32
MAX_DISTANCE = WINDOW
MEM_LEN = 256
X_HEADS = 4
X_HEAD_DIM = D_MODEL // X_HEADS
_FF_RAW = -(-8 * D_MODEL // 3)
D_FF = -(-_FF_RAW // 256) * 256
IN_COLS = (3 * CONV_WIDTH + 3 * FOX_HEADS * HEAD_DIM + FOX_HEADS
           + (SWA_HEADS + 2 * SWA_KV_HEADS) * HEAD_DIM + N_BRANCH * D_MODEL)
RMS_EPS = 1e-6
NEG_INF = -1e30

kernel_name = "hybrid_conv_fox_swa_block"


def rms_norm(x, g):
    xf = x.astype(jnp.float32)
    y = xf * lax.rsqrt(jnp.mean(xf * xf, axis=-1, keepdims=True) + RMS_EPS)
    return (y * g.astype(jnp.float32)).astype(x.dtype)


def split_proj(proj):
    sizes = ([CONV_WIDTH] * 3 + [FOX_HEADS * HEAD_DIM] * 3 + [FOX_HEADS]
             + [SWA_HEADS * HEAD_DIM, SWA_KV_HEADS * HEAD_DIM, SWA_KV_HEADS * HEAD_DIM]
             + [N_BRANCH * D_MODEL])
    parts, off = [], 0
    for s in sizes:
        parts.append(proj[..., off:off + s])
        off += s
    return parts


def short_conv_branch(gate_b, gate_c, u, conv_w):
    z = gate_c * u
    y = lax.conv_general_dilated(
        z, conv_w[:, None, :].astype(z.dtype), window_strides=(1,),
        padding=[(CONV_K - 1, 0)], dimension_numbers=('NWC', 'WIO', 'NWC'),
        feature_group_count=CONV_WIDTH)
    return gate_b * y


def fox_branch(q, k, v, f_logit, f_bias):
    b, s = q.shape[0], q.shape[1]
    nb = s // BLOCK
    q = q.reshape(b, s, FOX_HEADS, HEAD_DIM)
    k = k.reshape(b, s, FOX_HEADS, HEAD_DIM)
    v = v.reshape(b, s, FOX_HEADS, HEAD_DIM)
    log_f = jax.nn.log_sigmoid(f_logit.astype(jnp.float32) + f_bias.astype(jnp.float32))
    c = jnp.cumsum(log_f, axis=1)
    c_k = c.transpose(0, 2, 1)
    q_blocks = q.reshape(b, nb, BLOCK, FOX_HEADS, HEAD_DIM).transpose(1, 0, 2, 3, 4)
    c_blocks = c.reshape(b, nb, BLOCK, FOX_HEADS).transpose(1, 0, 2, 3)
    starts = jnp.arange(nb, dtype=jnp.int32) * BLOCK
    k_pos = jnp.arange(s, dtype=jnp.int32)
    scale = HEAD_DIM ** -0.5

    def one_block(args):
        qi, ci, start = args
        logits = jnp.einsum('bqhd,bkhd->bhqk', qi, k).astype(jnp.float32) * scale
        logits = logits + ci.transpose(0, 2, 1)[..., None] - c_k[:, :, None, :]
        q_pos = start + jnp.arange(BLOCK, dtype=jnp.int32)
        causal = k_pos[None, :] <= q_pos[:, None]
        logits = jnp.where(causal[None, None], logits, NEG_INF)
        p = jax.nn.softmax(logits, axis=-1)
        return jnp.einsum('bhqk,bkhd->bqhd', p.astype(v.dtype), v)

    out = lax.map(one_block, (q_blocks, c_blocks, starts))
    return out.transpose(1, 0, 2, 3, 4).reshape(b, s, FOX_HEADS * HEAD_DIM)


def t5_bucket(n):
    n = jnp.maximum(n, 0)
    max_exact = N_BUCKETS // 2
    large = max_exact + (
        jnp.log(jnp.maximum(n, 1).astype(jnp.float32) / max_exact)
        / math.log(MAX_DISTANCE / max_exact) * (N_BUCKETS - max_exact)).astype(jnp.int32)
    large = jnp.minimum(large, N_BUCKETS - 1)
    return jnp.where(n < max_exact, n, large)


def swa_sink_branch(q, k, v, rel_bias, sink):
    b, s = q.shape[0], q.shape[1]
    nb = s // BLOCK
    qb = q.reshape(b, nb, BLOCK, SWA_KV_HEADS, SWA_GROUP, HEAD_DIM)
    kb = k.reshape(b, nb, BLOCK, SWA_KV_HEADS, HEAD_DIM)
    vb = v.reshape(b, nb, BLOCK, SWA_KV_HEADS, HEAD_DIM)

    def band(t):
        prev = jnp.concatenate([jnp.zeros_like(t[:, :1]), t[:, :-1]], axis=1)
        return jnp.concatenate([prev, t], axis=2)

    k_band, v_band = band(kb), band(vb)
    tq = jnp.arange(BLOCK, dtype=jnp.int32)
    sk = jnp.arange(2 * BLOCK, dtype=jnp.int32)
    dist = BLOCK + tq[:, None] - sk[None, :]
    in_window = (dist >= 0) & (dist < WINDOW)
    key_pos = jnp.arange(nb, dtype=jnp.int32)[:, None] * BLOCK - BLOCK + sk[None, :]
    mask = in_window[None] & (key_pos >= 0)[:, None, :]
    bias = rel_bias.astype(jnp.float32)[t5_bucket(dist)]
    bias = bias.reshape(BLOCK, 2 * BLOCK, SWA_KV_HEADS, SWA_GROUP).transpose(2, 3, 0, 1)
    logits = jnp.einsum('bnqkgd,bnskd->bnkgqs', qb, k_band).astype(jnp.float32) * HEAD_DIM ** -0.5
    logits = jnp.where(mask[None, :, None, None], logits + bias, NEG_INF)
    sink_l = sink.astype(jnp.float32).reshape(SWA_KV_HEADS, SWA_GROUP)[None, None, :, :, None]
    m = jnp.maximum(logits.max(axis=-1), sink_l)
    p = jnp.exp(logits - m[..., None])
    denom = p.sum(axis=-1) + jnp.exp(sink_l - m)
    p = p / denom[..., None]
    out = jnp.einsum('bnkgqs,bnskd->bnqkgd', p.astype(v.dtype), v_band)
    return out.reshape(b, s, SWA_HEADS * HEAD_DIM)


def cross_attention(xn, mem_n, w_q, w_kv, w_o):
    b, s = xn.shape[0], xn.shape[1]
    q = (xn @ w_q).reshape(b, s, X_HEADS, X_HEAD_DIM)
    kv = mem_n @ w_kv
    k = kv[..., :X_HEADS * X_HEAD_DIM].reshape(b, -1, X_HEADS, X_HEAD_DIM)
    v = kv[..., X_HEADS * X_HEAD_DIM:].reshape(b, -1, X_HEADS, X_HEAD_DIM)
    logits = jnp.einsum('bshd,bmhd->bhsm', q, k).astype(jnp.float32) * X_HEAD_DIM ** -0.5
    p = jax.nn.softmax(logits, axis=-1)
    o = jnp.einsum('bhsm,bmhd->bshd', p.astype(v.dtype), v).reshape(b, s, X_HEADS * X_HEAD_DIM)
    return o @ w_o


def swiglu(xn, w_gate, w_up, w_down):
    return (jax.nn.silu(xn @ w_gate) * (xn @ w_up)) @ w_down


def setup_inputs(seed: int = 0) -> dict:
    key = jax.random.key(seed)
    ks = jax.random.split(key, 22)
    f32 = jnp.float32

    def nrm(k, shape, scale):
        return scale * jax.random.normal(k, shape, f32)

    def gain(k, shape):
        return 1.0 + 0.1 * jax.random.normal(k, shape, f32)

    return {
        "x": nrm(ks[0], (BATCH, SEQ, D_MODEL), 1.0),
        "mem": nrm(ks[1], (BATCH, MEM_LEN, D_MODEL), 1.0),
        "mix_norm_g": gain(ks[2], (DEPTH, D_MODEL)),
        "w_in": nrm(ks[3], (DEPTH, D_MODEL, IN_COLS), D_MODEL ** -0.5),
        "forget_bias": 4.0 + 0.5 * jax.random.normal(ks[4], (DEPTH, FOX_HEADS), f32),
        "conv_w": nrm(ks[5], (DEPTH, CONV_K, CONV_WIDTH), CONV_K ** -0.5),
        "sink": nrm(ks[6], (DEPTH, SWA_HEADS), 0.5),
        "w_branch": nrm(ks[7], (DEPTH, N_BRANCH, BRANCH_WIDTH, D_MODEL), BRANCH_WIDTH ** -0.5),
        "w_mix_out": nrm(ks[8], (DEPTH, D_MODEL, D_MODEL), D_MODEL ** -0.5),
        "rel_bias": nrm(ks[9], (N_BUCKETS, SWA_HEADS), 0.5),
        "xattn_norm_g": gain(ks[10], (DEPTH, D_MODEL)),
        "mem_norm_g": gain(ks[11], (DEPTH, D_MODEL)),
        "w_xq": nrm(ks[12], (DEPTH, D_MODEL, X_HEADS * X_HEAD_DIM), D_MODEL ** -0.5),
        "w_xkv": nrm(ks[13], (DEPTH, D_MODEL, 2 * X_HEADS * X_HEAD_DIM), D_MODEL ** -0.5),
        "w_xo": nrm(ks[14], (DEPTH, X_HEADS * X_HEAD_DIM, D_MODEL), (X_HEADS * X_HEAD_DIM) ** -0.5),
        "ffn_norm_g": gain(ks[15], (DEPTH, D_MODEL)),
        "w_ffn_gate": nrm(ks[16], (DEPTH, D_MODEL, D_FF), D_MODEL ** -0.5),
        "w_ffn_up": nrm(ks[17], (DEPTH, D_MODEL, D_FF), D_MODEL ** -0.5),
        "w_ffn_down": nrm(ks[18], (DEPTH, D_FF, D_MODEL), D_FF ** -0.5),
        "final_norm_g": gain(ks[19], (D_MODEL,)),
    }


def reference(x, mem, mix_norm_g, w_in, forget_bias, conv_w, sink, w_branch, w_mix_out,
              rel_bias, xattn_norm_g, mem_norm_g, w_xq, w_xkv, w_xo, ffn_norm_g,
              w_ffn_gate, w_ffn_up, w_ffn_down, final_norm_g):
    b, s = x.shape[0], x.shape[1]
    for l in range(DEPTH):
        h = rms_norm(x, mix_norm_g[l])
        (c_b, c_c, c_u, f_q, f_k, f_v, f_g, s_q, s_k, s_v, gate_logits) = split_proj(h @ w_in[l])
        y_conv = short_conv_branch(c_b, c_c, c_u, conv_w[l])
        y_fox = fox_branch(f_q, f_k, f_v, f_g, forget_bias[l])
        y_swa = swa_sink_branch(s_q, s_k, s_v, rel_bias, sink[l])
        gates = jax.nn.sigmoid(gate_logits.reshape(b, s, N_BRANCH, D_MODEL))
        merged = (gates[:, :, 0] * (y_conv @ w_branch[l, 0])
                  + gates[:, :, 1] * (y_fox @ w_branch[l, 1])
                  + gates[:, :, 2] * (y_swa @ w_branch[l, 2]))
        x = x + merged @ w_mix_out[l]
        x = x + cross_attention(rms_norm(x, xattn_norm_g[l]), rms_norm(mem, mem_norm_g[l]),
                                w_xq[l], w_xkv[l], w_xo[l])
        x = x + swiglu(rms_norm(x, ffn_norm_g[l]), w_ffn_gate[l], w_ffn_up[l], w_ffn_down[l])
    return rms_norm(x, final_norm_g)
```

```python
import functools
import math

import jax
import jax.numpy as jnp
import numpy as np
from jax import lax
from jax.experimental import pallas as pl
from jax.experimental.pallas import tpu as pltpu

F32 = jnp.float32
BF16 = jnp.bfloat16

D_MODEL = 1024
HEAD_DIM = 64
BRANCH = 512
FOX_HEADS = 8
SWA_HEADS = 8
SWA_KV = 2
SWA_GROUP = SWA_HEADS // SWA_KV
WINDOW = 128
N_BUCKETS = 32
X_HEADS = 4
X_HEAD_DIM = D_MODEL // X_HEADS
RMS_EPS = 1e-6
NEG = -1e30
LOG2E = math.log2(math.e)

LANES = 128
SUBLANES = 8
VMEM_LIMIT_BYTES = 56 * 1024 * 1024

GATE_OFF = 0
CONV_OFF = 3 * D_MODEL
FOX_OFF = CONV_OFF + 3 * BRANCH
SWA_OFF = FOX_OFF + 3 * BRANCH
P_COLS = SWA_OFF + BRANCH + 2 * SWA_KV * HEAD_DIM
AUG = 8
K_AUG = 128
V_ROWS = 80


def _params(*sem):
    return pltpu.CompilerParams(dimension_semantics=sem, vmem_limit_bytes=VMEM_LIMIT_BYTES)


def _rms(x, g):
    return x * lax.rsqrt(jnp.mean(x * x, axis=-1, keepdims=True) + RMS_EPS) * g


def _const_spec(shape):
    nd = len(shape)
    return pl.BlockSpec(shape, lambda *_: (0,) * nd, pipeline_mode=pl.Buffered(1))


def _norm_proj_kernel(x_ref, g_ref, w_ref, wfg_ref, p_ref, fg_ref, h_scr):
    @pl.when(pl.program_id(1) == 0)
    def _():
        h = _rms(x_ref[...], g_ref[...]).astype(BF16)
        h_scr[...] = h
        fg_ref[...] = jnp.dot(h, wfg_ref[...], preferred_element_type=F32)

    p_ref[...] = jnp.dot(h_scr[...], w_ref[...], preferred_element_type=F32).astype(BF16)


def _norm_proj(x, g, w, wfg, tm, tn):
    n, d = x.shape
    cols = w.shape[1]
    return pl.pallas_call(
        _norm_proj_kernel,
        grid=(n // tm, cols // tn),
        in_specs=[
            pl.BlockSpec((tm, d), lambda i, j: (i, 0)),
            pl.BlockSpec((1, d), lambda i, j: (0, 0)),
            pl.BlockSpec((d, tn), lambda i, j: (0, j)),
            pl.BlockSpec((d, LANES), lambda i, j: (0, 0)),
        ],
        out_specs=[
            pl.BlockSpec((tm, tn), lambda i, j: (i, j)),
            pl.BlockSpec((tm, LANES), lambda i, j: (i, 0)),
        ],
        out_shape=[jax.ShapeDtypeStruct((n, cols), BF16), jax.ShapeDtypeStruct((n, LANES), F32)],
        scratch_shapes=[pltpu.VMEM((tm, d), BF16)],
        compiler_params=_params("arbitrary", "arbitrary"),
        name="norm_proj",
    )(x, g.reshape(1, d), w, wfg)


def _forget_cumsum_kernel(fg_ref, b_ref, c1_ref, c2_ref, c3_ref):
    lf = jax.nn.log_sigmoid(fg_ref[0] + b_ref[...]) * LOG2E
    s = lf.shape[1]
    idx = lax.broadcasted_iota(jnp.int32, lf.shape, 1)
    c = lf
    d = 1
    while d < s:
        c = c + jnp.where(idx >= d, pltpu.roll(c, d, axis=1), 0.0)
        d *= 2
    c1 = c.astype(BF16)
    r1 = c - c1.astype(F32)
    c2 = r1.astype(BF16)
    c3 = (r1 - c2.astype(F32)).astype(BF16)
    c1_ref[0] = c1
    c2_ref[0] = c2
    c3_ref[0] = c3


def _forget_cumsum(fgT, bias):
    b, h, s = fgT.shape
    spec = pl.BlockSpec((1, h, s), lambda i: (i, 0, 0))
    return pl.pallas_call(
        _forget_cumsum_kernel,
        grid=(b,),
        in_specs=[spec, pl.BlockSpec((h, 1), lambda i: (0, 0))],
        out_specs=[spec, spec, spec],
        out_shape=[jax.ShapeDtypeStruct((b, h, s), BF16)] * 3,
        compiler_params=_params("arbitrary"),
        name="forget_cumsum",
    )(fgT, bias.reshape(h, 1))


def _fox_kernel(qT_ref, k_ref, vT_ref, o_ref, *, tq):
    i = pl.program_id(1)
    qT = qT_ref[0, 0]

    def step(j, carry, masked):
        m, acc = carry
        off = pl.multiple_of(j * tq, tq)
        s = jnp.dot(k_ref[0, 0, pl.ds(off, tq), :], qT, preferred_element_type=F32)
        if masked:
            key = lax.broadcasted_iota(jnp.int32, s.shape, 0)
            qry = lax.broadcasted_iota(jnp.int32, s.shape, 1)
            s = jnp.where(key <= qry, s, NEG)
        m_new = jnp.maximum(m, jnp.max(s, axis=0, keepdims=True))
        alpha = jnp.exp2(m - m_new)
        p = jnp.exp2(s - m_new).astype(BF16)
        pv = jnp.dot(vT_ref[0, 0, :, pl.ds(off, tq)], p, preferred_element_type=F32)
        return m_new, alpha * acc + pv

    carry = (jnp.full((1, tq), NEG, F32), jnp.zeros((V_ROWS, tq), F32))
    carry = lax.fori_loop(0, i, functools.partial(step, masked=False), carry)
    _, acc = step(i, carry, masked=True)
    o_ref[0, 0] = (acc[:HEAD_DIM] / acc[HEAD_DIM:HEAD_DIM + 1]).astype(BF16)


def _fox_attention(qT, k, vT, tq):
    b, h, _, s = qT.shape
    return pl.pallas_call(
        functools.partial(_fox_kernel, tq=tq),
        grid=(b * h, s // tq),
        in_specs=[
            pl.BlockSpec((1, 1, K_AUG, tq), lambda g, i: (g // h, g % h, 0, i)),
            pl.BlockSpec((1, 1, s, K_AUG), lambda g, i: (g // h, g % h, 0, 0)),
            pl.BlockSpec((1, 1, V_ROWS, s), lambda g, i: (g // h, g % h, 0, 0)),
        ],
        out_specs=pl.BlockSpec((1, 1, HEAD_DIM, tq), lambda g, i: (g // h, g % h, 0, i)),
        out_shape=jax.ShapeDtypeStruct((b, h, HEAD_DIM, s), BF16),
        compiler_params=_params("arbitrary", "arbitrary"),
        name="fox_attention",
    )(qT, k, vT)


def _swa_bias_kernel(rb_ref, bucket_ref, win_ref, o_ref):
    h = pl.program_id(0)
    bucket = bucket_ref[...]
    bias = jnp.zeros(bucket.shape, F32)
    for b in range(N_BUCKETS):
        bias = jnp.where(bucket == b, rb_ref[b, h] * LOG2E, bias)
    o_ref[0, 0] = jnp.where(win_ref[0] > 0, bias, NEG)
    o_ref[1, 0] = jnp.where(win_ref[1] > 0, bias, NEG)


def _swa_bias_table(rel_bias):
    tq = np.arange(WINDOW)[None, :]
    sk = np.arange(2 * WINDOW)[:, None]
    dist = WINDOW + tq - sk
    n = jnp.maximum(jnp.asarray(dist, jnp.int32), 0)
    max_exact = N_BUCKETS // 2
    large = max_exact + (jnp.log(jnp.maximum(n, 1).astype(F32) / max_exact)
                         / math.log(WINDOW / max_exact) * (N_BUCKETS - max_exact)).astype(jnp.int32)
    bucket = jnp.where(n < max_exact, n, jnp.minimum(large, N_BUCKETS - 1))
    in_window = (dist >= 0) & (dist < WINDOW)
    win = np.stack([in_window, in_window & (sk >= WINDOW)]).astype(np.int32)
    full = lambda shape: pl.BlockSpec(shape, lambda h: (0,) * len(shape))
    return pl.pallas_call(
        _swa_bias_kernel,
        grid=(SWA_HEADS,),
        in_specs=[pl.BlockSpec(memory_space=pltpu.SMEM), full(bucket.shape), full(win.shape)],
        out_specs=pl.BlockSpec((2, 1, 2 * WINDOW, WINDOW), lambda h: (0, h, 0, 0)),
        out_shape=jax.ShapeDtypeStruct((2, SWA_HEADS, 2 * WINDOW, WINDOW), F32),
        compiler_params=_params("arbitrary"),
        name="swa_bias_table",
    )(rel_bias, bucket, jnp.asarray(win))


def _swa_kernel(sink_ref, qT_ref, kp_ref, kc_ref, vp_ref, vc_ref, bias_ref, o_ref, *, nblk):
    first = (pl.program_id(1) == 0).astype(jnp.int32)
    keys = jnp.concatenate([kp_ref[0], kc_ref[0]], axis=0)
    vals = jnp.concatenate([vp_ref[0], vc_ref[0]], axis=2)
    for blk in range(nblk):
        lo = blk * WINDOW
        sel = first if blk == 0 else 0
        for kv in range(SWA_KV):
            kb = keys[lo:lo + 2 * WINDOW, kv * HEAD_DIM:(kv + 1) * HEAD_DIM]
            vb = vals[kv, :, lo:lo + 2 * WINDOW]
            for g in range(SWA_GROUP):
                h = kv * SWA_GROUP + g
                s = jnp.dot(kb, qT_ref[0, h, :, lo:lo + WINDOW], preferred_element_type=F32)
                s = s + bias_ref[sel, h]
                sink = sink_ref[h] * LOG2E
                m = jnp.maximum(jnp.max(s, axis=0, keepdims=True), sink)
                p = jnp.exp2(s - m).astype(BF16)
                pv = jnp.dot(vb, p, preferred_element_type=F32)
                denom = pv[HEAD_DIM:HEAD_DIM + 1] + jnp.exp2(sink - m)
                o_ref[0, h, :, lo:lo + WINDOW] = (pv[:HEAD_DIM] / denom).astype(BF16)


def _swa_attention(sink, qT, k, vT, bias, nblk):
    b, hq, _, s = qT.shape
    ts = nblk * WINDOW
    prev = lambda i: jnp.maximum(i * nblk - 1, 0)
    return pl.pallas_call(
        functools.partial(_swa_kernel, nblk=nblk),
        grid=(b, s // ts),
        in_specs=[
            pl.BlockSpec(memory_space=pltpu.SMEM),
            pl.BlockSpec((1, hq, HEAD_DIM, ts), lambda bi, i: (bi, 0, 0, i)),
            pl.BlockSpec((1, WINDOW, SWA_KV * HEAD_DIM), lambda bi, i: (bi, prev(i), 0)),
            pl.BlockSpec((1, ts, SWA_KV * HEAD_DIM), lambda bi, i: (bi, i, 0)),
            pl.BlockSpec((1, SWA_KV, V_ROWS, WINDOW), lambda bi, i: (bi, 0, 0, prev(i))),
            pl.BlockSpec((1, SWA_KV, V_ROWS, ts), lambda bi, i: (bi, 0, 0, i)),
            _const_spec(bias.shape),
        ],
        out_specs=pl.BlockSpec((1, hq, HEAD_DIM, ts), lambda bi, i: (bi, 0, 0, i)),
        out_shape=jax.ShapeDtypeStruct((b, hq, HEAD_DIM, s), BF16),
        compiler_params=_params("arbitrary", "arbitrary"),
        name="swa_attention",
    )(sink, qT, k, k, vT, vT, bias)


def _merge_kernel(x_ref, gate_ref, cb_ref, cc_ref, cu_ref, cch_ref, cuh_ref, yf_ref, ys_ref,
                  cw_ref, wb_ref, wo_ref, o_ref, *, tiles_per_seq):
    tm = x_ref.shape[0]
    z = cc_ref[...].astype(F32) * cu_ref[...].astype(F32)
    zh = cch_ref[...].astype(F32) * cuh_ref[...].astype(F32)
    zh = jnp.where(pl.program_id(0) % tiles_per_seq == 0, 0.0, zh)
    row = lax.broadcasted_iota(jnp.int32, z.shape, 0)
    z1 = jnp.where(row == 0, zh[7:8], pltpu.roll(z, 1, axis=0))
    z2 = jnp.where(row == 0, zh[6:7], jnp.where(row == 1, zh[7:8], pltpu.roll(z, 2, axis=0)))
    cw = cw_ref[...]
    y_conv = cb_ref[...].astype(F32) * (cw[0:1] * z2 + cw[1:2] * z1 + cw[2:3] * z)
    branches = (y_conv.astype(BF16), yf_ref[...], ys_ref[...])
    merged = None
    for b, y in enumerate(branches):
        gate = jax.nn.sigmoid(gate_ref[:, b * D_MODEL:(b + 1) * D_MODEL].astype(F32))
        term = gate * jnp.dot(y, wb_ref[b], preferred_element_type=F32)
        merged = term if merged is None else merged + term
    o_ref[...] = x_ref[...] + jnp.dot(merged.astype(BF16), wo_ref[...], preferred_element_type=F32)


def _merge(x, p, y_fox, y_swa, conv_w, w_branch, w_out, seq, tm):
    n, d = x.shape
    cblk = CONV_OFF // BRANCH
    halo = lambda i: jnp.maximum(i * (tm // SUBLANES) - 1, 0)
    row = lambda c: pl.BlockSpec((tm, BRANCH), lambda i: (i, c))
    return pl.pallas_call(
        functools.partial(_merge_kernel, tiles_per_seq=seq // tm),
        grid=(n // tm,),
        in_specs=[
            pl.BlockSpec((tm, d), lambda i: (i, 0)),
            pl.BlockSpec((tm, 3 * D_MODEL), lambda i: (i, GATE_OFF // (3 * D_MODEL))),
            row(cblk), row(cblk + 1), row(cblk + 2),
            pl.BlockSpec((SUBLANES, BRANCH), lambda i: (halo(i), cblk + 1)),
            pl.BlockSpec((SUBLANES, BRANCH), lambda i: (halo(i), cblk + 2)),
            pl.BlockSpec((tm, BRANCH), lambda i: (i, 0)),
            pl.BlockSpec((tm, BRANCH), lambda i: (i, 0)),
            _const_spec(conv_w.shape), _const_spec(w_branch.shape), _const_spec(w_out.shape),
        ],
        out_specs=pl.BlockSpec((tm, d), lambda i: (i, 0)),
        out_shape=jax.ShapeDtypeStruct((n, d), F32),
        compiler_params=_params("arbitrary"),
        name="conv_merge_out",
    )(x, p, p, p, p, p, p, y_fox, y_swa, conv_w, w_branch, w_out)


def _xattn_kernel(x_ref, g_ref, wq_ref, kT_ref, v_ref, wo_ref, o_ref):
    x = x_ref[...]
    q = jnp.dot(_rms(x, g_ref[...]).astype(BF16), wq_ref[...], preferred_element_type=F32).astype(BF16)
    heads = []
    for h in range(X_HEADS):
        sl = slice(h * X_HEAD_DIM, (h + 1) * X_HEAD_DIM)
        s = jnp.dot(q[:, sl], kT_ref[0, sl, :], preferred_element_type=F32)
        p = jnp.exp2(s - jnp.max(s, axis=-1, keepdims=True))
        denom = jnp.sum(p, axis=-1, keepdims=True)
        o = jnp.dot(p.astype(BF16), v_ref[0, :, sl], preferred_element_type=F32)
        heads.append((o / denom).astype(BF16))
    o_ref[...] = x + jnp.dot(jnp.concatenate(heads, axis=1), wo_ref[...], preferred_element_type=F32)


def _xattn(x, g, wq, kT, v, wo, seq, tm):
    n, d = x.shape
    tiles = seq // tm
    return pl.pallas_call(
        _xattn_kernel,
        grid=(n // tm,),
        in_specs=[
            pl.BlockSpec((tm, d), lambda i: (i, 0)),
            _const_spec((1, d)),
            _const_spec(wq.shape),
            pl.BlockSpec((1,) + kT.shape[1:], lambda i: (i // tiles, 0, 0)),
            pl.BlockSpec((1,) + v.shape[1:], lambda i: (i // tiles, 0, 0)),
            _const_spec(wo.shape),
        ],
        out_specs=pl.BlockSpec((tm, d), lambda i: (i, 0)),
        out_shape=jax.ShapeDtypeStruct((n, d), F32),
        compiler_params=_params("arbitrary"),
        name="cross_attention",
    )(x, g.reshape(1, d), wq, kT, v, wo)


def _swiglu_kernel(x_ref, g_ref, wg_ref, wu_ref, wd_ref, gf_ref, o_ref, *, final_norm):
    x = x_ref[...]
    xn = _rms(x, g_ref[...]).astype(BF16)
    gate = jnp.dot(xn, wg_ref[...], preferred_element_type=F32)
    up = jnp.dot(xn, wu_ref[...], preferred_element_type=F32)
    hidden = (gate * jax.nn.sigmoid(gate) * up).astype(BF16)
    y = x + jnp.dot(hidden, wd_ref[...], preferred_element_type=F32)
    o_ref[...] = _rms(y, gf_ref[...]) if final_norm else y


def _swiglu(x, g, wg, wu, wd, g_final, final_norm, tm):
    n, d = x.shape
    return pl.pallas_call(
        functools.partial(_swiglu_kernel, final_norm=final_norm),
        grid=(n // tm,),
        in_specs=[
            pl.BlockSpec((tm, d), lambda i: (i, 0)),
            _const_spec((1, d)),
            _const_spec(wg.shape), _const_spec(wu.shape), _const_spec(wd.shape),
            _const_spec((1, d)),
        ],
        out_specs=pl.BlockSpec((tm, d), lambda i: (i, 0)),
        out_shape=jax.ShapeDtypeStruct((n, d), F32),
        compiler_params=_params("arbitrary"),
        name="swiglu",
    )(x, g.reshape(1, d), wg, wu, wd, g_final.reshape(1, d))


def _pack_in_proj(w_in):
    sizes = [BRANCH] * 3 + [BRANCH] * 3 + [FOX_HEADS] + [BRANCH, SWA_KV * HEAD_DIM, SWA_KV * HEAD_DIM] + [3 * D_MODEL]
    offs = np.concatenate([[0], np.cumsum(sizes)])
    c_b, c_c, c_u, f_q, f_k, f_v, f_g, s_q, s_k, s_v, gates = (w_in[:, offs[t]:offs[t + 1]] for t in range(len(sizes)))
    qscale = HEAD_DIM ** -0.5 * LOG2E
    main = jnp.concatenate([gates, c_b, c_c, c_u, f_q * qscale, f_k, f_v, s_q * qscale, s_k, s_v], axis=1)
    fg = jnp.pad(f_g, ((0, 0), (0, LANES - FOX_HEADS)))
    return main.astype(BF16), fg.astype(BF16)


def _heads_T(a, b, s, h):
    return a.reshape(b, s, h, HEAD_DIM).transpose(0, 2, 3, 1)


def _with_ones_rows(vT):
    b, h, d, s = vT.shape
    ones = jnp.ones((b, h, 1, s), vT.dtype)
    return jnp.concatenate([vT, ones, jnp.zeros((b, h, V_ROWS - d - 1, s), vT.dtype)], axis=2)


def kernel(x, mem, mix_norm_g, w_in, forget_bias, conv_w, sink, w_branch, w_mix_out, rel_bias,
           xattn_norm_g, mem_norm_g, w_xq, w_xkv, w_xo, ffn_norm_g, w_ffn_gate, w_ffn_up, w_ffn_down,
           final_norm_g):
    b, s, d = x.shape
    n = b * s
    depth = w_in.shape[0]
    mem_len = mem.shape[1]
    xf = x.reshape(n, d)
    memf = mem.reshape(b * mem_len, d)
    swa_bias = _swa_bias_table(rel_bias)
    zeros_fg = jnp.zeros((d, LANES), BF16)
    for l in range(depth):
        w_main, w_fg = _pack_in_proj(w_in[l])
        p, fg = _norm_proj(xf, mix_norm_g[l], w_main, w_fg, tm=1024, tn=P_COLS // 3)

        fgT = fg[:, :FOX_HEADS].reshape(b, s, FOX_HEADS).transpose(0, 2, 1)
        c1, c2, c3 = _forget_cumsum(fgT, forget_bias[l])
        one = jnp.ones_like(c1)
        zero = jnp.zeros_like(c1)
        aug_q = jnp.stack([c1, c2, c3, one, one, one, zero, zero], axis=2)
        aug_k = jnp.stack([one, one, one, -c1, -c2, -c3, zero, zero], axis=3)
        pad_q = jnp.zeros((b, FOX_HEADS, K_AUG - HEAD_DIM - AUG, s), BF16)
        pad_k = jnp.zeros((b, FOX_HEADS, s, K_AUG - HEAD_DIM - AUG), BF16)
        fq = _heads_T(p[:, FOX_OFF:FOX_OFF + BRANCH], b, s, FOX_HEADS)
        fk = p[:, FOX_OFF + BRANCH:FOX_OFF + 2 * BRANCH].reshape(b, s, FOX_HEADS, HEAD_DIM).transpose(0, 2, 1, 3)
        fv = _heads_T(p[:, FOX_OFF + 2 * BRANCH:FOX_OFF + 3 * BRANCH], b, s, FOX_HEADS)
        qT_aug = jnp.concatenate([fq, aug_q, pad_q], axis=2)
        k_aug = jnp.concatenate([fk, aug_k, pad_k], axis=3)
        y_fox = _fox_attention(qT_aug, k_aug, _with_ones_rows(fv), tq=512)
        y_fox = y_fox.transpose(0, 3, 1, 2).reshape(n, BRANCH)

        sq = _heads_T(p[:, SWA_OFF:SWA_OFF + BRANCH], b, s, SWA_HEADS)
        sk = p[:, SWA_OFF + BRANCH:SWA_OFF + BRANCH + SWA_KV * HEAD_DIM].reshape(b, s, SWA_KV * HEAD_DIM)
        sv = _heads_T(p[:, SWA_OFF + BRANCH + SWA_KV * HEAD_DIM:P_COLS], b, s, SWA_KV)
        y_swa = _swa_attention(sink[l], sq, sk, _with_ones_rows(sv), swa_bias, nblk=4)
        y_swa = y_swa.transpose(0, 3, 1, 2).reshape(n, BRANCH)

        xf = _merge(xf, p, y_fox, y_swa, conv_w[l], w_branch[l].astype(BF16), w_mix_out[l].astype(BF16),
                    seq=s, tm=512)

        kv, _ = _norm_proj(memf, mem_norm_g[l], w_xkv[l].astype(BF16), zeros_fg, tm=b * mem_len, tn=1024)
        kT = kv[:, :d].reshape(b, mem_len, d).transpose(0, 2, 1)
        v = kv[:, d:].reshape(b, mem_len, d)
        wq = (w_xq[l] * (X_HEAD_DIM ** -0.5 * LOG2E)).astype(BF16)
        xf = _xattn(xf, xattn_norm_g[l], wq, kT, v, w_xo[l].astype(BF16), seq=s, tm=512)

        xf = _swiglu(xf, ffn_norm_g[l], w_ffn_gate[l].astype(BF16), w_ffn_up[l].astype(BF16),
                     w_ffn_down[l].astype(BF16), final_norm_g, final_norm=(l == depth - 1), tm=512)
    return xf.reshape(b, s, d)
```

```python
import functools
import math

import jax
import jax.numpy as jnp
import numpy as np
from jax import lax
from jax.experimental import pallas as pl
from jax.experimental.pallas import tpu as pltpu

F32 = jnp.float32
BF16 = jnp.bfloat16

D_MODEL = 1024
HEAD_DIM = 64
BRANCH = 512
FOX_HEADS = 8
SWA_HEADS = 8
SWA_KV = 2
SWA_GROUP = SWA_HEADS // SWA_KV
WINDOW = 128
N_BUCKETS = 32
X_HEADS = 4
X_HEAD_DIM = D_MODEL // X_HEADS
RMS_EPS = 1e-6
NEG = -1e30
LOG2E = math.log2(math.e)

LANES = 128
SUBLANES = 8
VMEM_LIMIT_BYTES = 56 * 1024 * 1024

GATE_OFF = 0
CONV_OFF = 3 * D_MODEL
FOX_OFF = CONV_OFF + 3 * BRANCH
SWA_OFF = FOX_OFF + 3 * BRANCH
P_COLS = SWA_OFF + BRANCH + 2 * SWA_KV * HEAD_DIM
AUG = 8
K_AUG = 128
V_ROWS = 80


def _params(*sem):
    return pltpu.CompilerParams(dimension_semantics=sem, vmem_limit_bytes=VMEM_LIMIT_BYTES)


def _rms(x, g):
    return x * lax.rsqrt(jnp.mean(x * x, axis=-1, keepdims=True) + RMS_EPS) * g


def _const_spec(shape):
    nd = len(shape)
    return pl.BlockSpec(shape, lambda *_: (0,) * nd, pipeline_mode=pl.Buffered(1))


def _norm_proj_kernel(x_ref, g_ref, w_ref, wfg_ref, p_ref, fg_ref, h_scr):
    @pl.when(pl.program_id(1) == 0)
    def _():
        h = _rms(x_ref[...], g_ref[...]).astype(BF16)
        h_scr[...] = h
        fg_ref[...] = jnp.dot(h, wfg_ref[...], preferred_element_type=F32)

    p_ref[...] = jnp.dot(h_scr[...], w_ref[...], preferred_element_type=F32).astype(BF16)


def _norm_proj(x, g, w, wfg, tm, tn):
    n, d = x.shape
    cols = w.shape[1]
    return pl.pallas_call(
        _norm_proj_kernel,
        grid=(n // tm, cols // tn),
        in_specs=[
            pl.BlockSpec((tm, d), lambda i, j: (i, 0)),
            pl.BlockSpec((1, d), lambda i, j: (0, 0)),
            pl.BlockSpec((d, tn), lambda i, j: (0, j)),
            pl.BlockSpec((d, LANES), lambda i, j: (0, 0)),
        ],
        out_specs=[
            pl.BlockSpec((tm, tn), lambda i, j: (i, j)),
            pl.BlockSpec((tm, LANES), lambda i, j: (i, 0)),
        ],
        out_shape=[jax.ShapeDtypeStruct((n, cols), BF16), jax.ShapeDtypeStruct((n, LANES), F32)],
        scratch_shapes=[pltpu.VMEM((tm, d), BF16)],
        compiler_params=_params("arbitrary", "arbitrary"),
        name="norm_proj",
    )(x, g.reshape(1, d), w, wfg)


def _forget_cumsum_kernel(fg_ref, b_ref, c1_ref, c2_ref, c3_ref):
    lf = jax.nn.log_sigmoid(fg_ref[0] + b_ref[...]) * LOG2E
    s = lf.shape[1]
    idx = lax.broadcasted_iota(jnp.int32, lf.shape, 1)
    c = lf
    d = 1
    while d < s:
        c = c + jnp.where(idx >= d, pltpu.roll(c, d, axis=1), 0.0)
        d *= 2
    c1 = c.astype(BF16)
    r1 = c - c1.astype(F32)
    c2 = r1.astype(BF16)
    c3 = (r1 - c2.astype(F32)).astype(BF16)
    c1_ref[0] = c1
    c2_ref[0] = c2
    c3_ref[0] = c3


def _forget_cumsum(fgT, bias):
    b, h, s = fgT.shape
    spec = pl.BlockSpec((1, h, s), lambda i: (i, 0, 0))
    return pl.pallas_call(
        _forget_cumsum_kernel,
        grid=(b,),
        in_specs=[spec, pl.BlockSpec((h, 1), lambda i: (0, 0))],
        out_specs=[spec, spec, spec],
        out_shape=[jax.ShapeDtypeStruct((b, h, s), BF16)] * 3,
        compiler_params=_params("arbitrary"),
        name="forget_cumsum",
    )(fgT, bias.reshape(h, 1))


def _fox_kernel(qT_ref, k_ref, vT_ref, o_ref, s_buf, p_buf, acc_ref, *, tq, tk):
    i = pl.program_id(1)

    def scores(t, slot):
        off = pl.multiple_of(t * tk, tk)
        s_buf[slot] = jnp.dot(k_ref[0, 0, pl.ds(off, tk), :], qT_ref[0, 0], preferred_element_type=F32)

    def softmax(slot, m, mask=None):
        s = s_buf[slot]
        if mask is not None:
            s = jnp.where(mask, s, NEG)
        m_new = jnp.maximum(m, jnp.max(s, axis=0, keepdims=True))
        p_buf[slot] = jnp.exp2(s - m_new).astype(BF16)
        return m_new, jnp.exp2(m - m_new)

    def values(t, slot, alpha):
        off = pl.multiple_of(jnp.maximum(t, 0) * tk, tk)
        pv = jnp.dot(vT_ref[0, 0, :, pl.ds(off, tk)], p_buf[slot], preferred_element_type=F32)
        acc_ref[...] = alpha * acc_ref[...] + pv

    p_buf[1] = jnp.zeros(p_buf.shape[1:], BF16)
    acc_ref[...] = jnp.zeros(acc_ref.shape, F32)
    scores(0, 0)

    def pair(u, carry):
        m, alpha_prev = carry
        t = 2 * u
        scores(t + 1, 1)
        m, alpha0 = softmax(0, m)
        values(t - 1, 1, alpha_prev)
        scores(t + 2, 0)
        m, alpha1 = softmax(1, m)
        values(t, 0, alpha0)
        return m, alpha1

    carry = (jnp.full((1, tq), NEG, F32), jnp.ones((1, tq), F32))
    m, alpha_prev = lax.fori_loop(0, i, pair, carry)

    t = 2 * i
    key = lax.broadcasted_iota(jnp.int32, (tk, tq), 0)
    qry = lax.broadcasted_iota(jnp.int32, (tk, tq), 1)
    scores(t + 1, 1)
    m, alpha0 = softmax(0, m, key <= qry)
    values(t - 1, 1, alpha_prev)
    m, alpha1 = softmax(1, m, key + tk <= qry)
    values(t, 0, alpha0)
    values(t + 1, 1, alpha1)
    acc = acc_ref[...]
    o_ref[0, 0] = (acc[:HEAD_DIM] / acc[HEAD_DIM:HEAD_DIM + 1]).astype(BF16)


def _fox_attention(qT, k, vT, tq):
    b, h, _, s = qT.shape
    tk = tq // 2
    return pl.pallas_call(
        functools.partial(_fox_kernel, tq=tq, tk=tk),
        grid=(b * h, s // tq),
        in_specs=[
            pl.BlockSpec((1, 1, K_AUG, tq), lambda g, i: (g // h, g % h, 0, i)),
            pl.BlockSpec((1, 1, s, K_AUG), lambda g, i: (g // h, g % h, 0, 0)),
            pl.BlockSpec((1, 1, V_ROWS, s), lambda g, i: (g // h, g % h, 0, 0)),
        ],
        out_specs=pl.BlockSpec((1, 1, HEAD_DIM, tq), lambda g, i: (g // h, g % h, 0, i)),
        out_shape=jax.ShapeDtypeStruct((b, h, HEAD_DIM, s), BF16),
        scratch_shapes=[pltpu.VMEM((2, tk, tq), F32), pltpu.VMEM((2, tk, tq), BF16), pltpu.VMEM((V_ROWS, tq), F32)],
        compiler_params=_params("arbitrary", "arbitrary"),
        name="fox_attention",
    )(qT, k, vT)


def _swa_bias_kernel(rb_ref, bucket_ref, win_ref, o_ref):
    h = pl.program_id(0)
    bucket = bucket_ref[...]
    bias = jnp.zeros(bucket.shape, F32)
    for b in range(N_BUCKETS):
        bias = jnp.where(bucket == b, rb_ref[b, h] * LOG2E, bias)
    o_ref[0, 0] = jnp.where(win_ref[0] > 0, bias, NEG)
    o_ref[1, 0] = jnp.where(win_ref[1] > 0, bias, NEG)


def _swa_bias_table(rel_bias):
    tq = np.arange(WINDOW)[None, :]
    sk = np.arange(2 * WINDOW)[:, None]
    dist = WINDOW + tq - sk
    n = jnp.maximum(jnp.asarray(dist, jnp.int32), 0)
    max_exact = N_BUCKETS // 2
    large = max_exact + (jnp.log(jnp.maximum(n, 1).astype(F32) / max_exact)
                         / math.log(WINDOW / max_exact) * (N_BUCKETS - max_exact)).astype(jnp.int32)
    bucket = jnp.where(n < max_exact, n, jnp.minimum(large, N_BUCKETS - 1))
    in_window = (dist >= 0) & (dist < WINDOW)
    win = np.stack([in_window, in_window & (sk >= WINDOW)]).astype(np.int32)
    full = lambda shape: pl.BlockSpec(shape, lambda h: (0,) * len(shape))
    return pl.pallas_call(
        _swa_bias_kernel,
        grid=(SWA_HEADS,),
        in_specs=[pl.BlockSpec(memory_space=pltpu.SMEM), full(bucket.shape), full(win.shape)],
        out_specs=pl.BlockSpec((2, 1, 2 * WINDOW, WINDOW), lambda h: (0, h, 0, 0)),
        out_shape=jax.ShapeDtypeStruct((2, SWA_HEADS, 2 * WINDOW, WINDOW), F32),
        compiler_params=_params("arbitrary"),
        name="swa_bias_table",
    )(rel_bias, bucket, jnp.asarray(win))


def _swa_kernel(sink_ref, qT_ref, kp_ref, kc_ref, vp_ref, vc_ref, bias_ref, o_ref, *, nblk):
    first = (pl.program_id(1) == 0).astype(jnp.int32)
    keys = jnp.concatenate([kp_ref[0], kc_ref[0]], axis=0)
    vals = jnp.concatenate([vp_ref[0], vc_ref[0]], axis=2)
    for blk in range(nblk):
        lo = blk * WINDOW
        sel = first if blk == 0 else 0
        for kv in range(SWA_KV):
            kb = keys[lo:lo + 2 * WINDOW, kv * HEAD_DIM:(kv + 1) * HEAD_DIM]
            vb = vals[kv, :, lo:lo + 2 * WINDOW]
            for g in range(SWA_GROUP):
                h = kv * SWA_GROUP + g
                s = jnp.dot(kb, qT_ref[0, h, :, lo:lo + WINDOW], preferred_element_type=F32)
                s = s + bias_ref[sel, h]
                sink = sink_ref[h] * LOG2E
                m = jnp.maximum(jnp.max(s, axis=0, keepdims=True), sink)
                p = jnp.exp2(s - m).astype(BF16)
                pv = jnp.dot(vb, p, preferred_element_type=F32)
                denom = pv[HEAD_DIM:HEAD_DIM + 1] + jnp.exp2(sink - m)
                o_ref[0, h, :, lo:lo + WINDOW] = (pv[:HEAD_DIM] / denom).astype(BF16)


def _swa_attention(sink, qT, k, vT, bias, nblk):
    b, hq, _, s = qT.shape
    ts = nblk * WINDOW
    prev = lambda i: jnp.maximum(i * nblk - 1, 0)
    return pl.pallas_call(
        functools.partial(_swa_kernel, nblk=nblk),
        grid=(b, s // ts),
        in_specs=[
            pl.BlockSpec(memory_space=pltpu.SMEM),
            pl.BlockSpec((1, hq, HEAD_DIM, ts), lambda bi, i: (bi, 0, 0, i)),
            pl.BlockSpec((1, WINDOW, SWA_KV * HEAD_DIM), lambda bi, i: (bi, prev(i), 0)),
            pl.BlockSpec((1, ts, SWA_KV * HEAD_DIM), lambda bi, i: (bi, i, 0)),
            pl.BlockSpec((1, SWA_KV, V_ROWS, WINDOW), lambda bi, i: (bi, 0, 0, prev(i))),
            pl.BlockSpec((1, SWA_KV, V_ROWS, ts), lambda bi, i: (bi, 0, 0, i)),
            _const_spec(bias.shape),
        ],
        out_specs=pl.BlockSpec((1, hq, HEAD_DIM, ts), lambda bi, i: (bi, 0, 0, i)),
        out_shape=jax.ShapeDtypeStruct((b, hq, HEAD_DIM, s), BF16),
        compiler_params=_params("arbitrary", "arbitrary"),
        name="swa_attention",
    )(sink, qT, k, k, vT, vT, bias)


def _merge_kernel(x_ref, gate_ref, cb_ref, cc_ref, cu_ref, cch_ref, cuh_ref, yf_ref, ys_ref,
                  cw_ref, wb_ref, wo_ref, o_ref, *, tiles_per_seq):
    tm = x_ref.shape[0]
    z = cc_ref[...].astype(F32) * cu_ref[...].astype(F32)
    zh = cch_ref[...].astype(F32) * cuh_ref[...].astype(F32)
    zh = jnp.where(pl.program_id(0) % tiles_per_seq == 0, 0.0, zh)
    row = lax.broadcasted_iota(jnp.int32, z.shape, 0)
    z1 = jnp.where(row == 0, zh[7:8], pltpu.roll(z, 1, axis=0))
    z2 = jnp.where(row == 0, zh[6:7], jnp.where(row == 1, zh[7:8], pltpu.roll(z, 2, axis=0)))
    cw = cw_ref[...]
    y_conv = cb_ref[...].astype(F32) * (cw[0:1] * z2 + cw[1:2] * z1 + cw[2:3] * z)
    branches = (y_conv.astype(BF16), yf_ref[...], ys_ref[...])
    merged = None
    for b, y in enumerate(branches):
        gate = jax.nn.sigmoid(gate_ref[:, b * D_MODEL:(b + 1) * D_MODEL].astype(F32))
        term = gate * jnp.dot(y, wb_ref[b], preferred_element_type=F32)
        merged = term if merged is None else merged + term
    o_ref[...] = x_ref[...] + jnp.dot(merged.astype(BF16), wo_ref[...], preferred_element_type=F32)


def _merge(x, p, y_fox, y_swa, conv_w, w_branch, w_out, seq, tm):
    n, d = x.shape
    cblk = CONV_OFF // BRANCH
    halo = lambda i: jnp.maximum(i * (tm // SUBLANES) - 1, 0)
    row = lambda c: pl.BlockSpec((tm, BRANCH), lambda i: (i, c))
    return pl.pallas_call(
        functools.partial(_merge_kernel, tiles_per_seq=seq // tm),
        grid=(n // tm,),
        in_specs=[
            pl.BlockSpec((tm, d), lambda i: (i, 0)),
            pl.BlockSpec((tm, 3 * D_MODEL), lambda i: (i, GATE_OFF // (3 * D_MODEL))),
            row(cblk), row(cblk + 1), row(cblk + 2),
            pl.BlockSpec((SUBLANES, BRANCH), lambda i: (halo(i), cblk + 1)),
            pl.BlockSpec((SUBLANES, BRANCH), lambda i: (halo(i), cblk + 2)),
            pl.BlockSpec((tm, BRANCH), lambda i: (i, 0)),
            pl.BlockSpec((tm, BRANCH), lambda i: (i, 0)),
            _const_spec(conv_w.shape), _const_spec(w_branch.shape), _const_spec(w_out.shape),
        ],
        out_specs=pl.BlockSpec((tm, d), lambda i: (i, 0)),
        out_shape=jax.ShapeDtypeStruct((n, d), F32),
        compiler_params=_params("arbitrary"),
        name="conv_merge_out",
    )(x, p, p, p, p, p, p, y_fox, y_swa, conv_w, w_branch, w_out)


def _xattn_kernel(x_ref, g_ref, wq_ref, kT_ref, v_ref, wo_ref, o_ref):
    x = x_ref[...]
    q = jnp.dot(_rms(x, g_ref[...]).astype(BF16), wq_ref[...], preferred_element_type=F32).astype(BF16)
    heads = []
    for h in range(X_HEADS):
        sl = slice(h * X_HEAD_DIM, (h + 1) * X_HEAD_DIM)
        s = jnp.dot(q[:, sl], kT_ref[0, sl, :], preferred_element_type=F32)
        p = jnp.exp2(s - jnp.max(s, axis=-1, keepdims=True))
        denom = jnp.sum(p, axis=-1, keepdims=True)
        o = jnp.dot(p.astype(BF16), v_ref[0, :, sl], preferred_element_type=F32)
        heads.append((o / denom).astype(BF16))
    o_ref[...] = x + jnp.dot(jnp.concatenate(heads, axis=1), wo_ref[...], preferred_element_type=F32)


def _xattn(x, g, wq, kT, v, wo, seq, tm):
    n, d = x.shape
    tiles = seq // tm
    return pl.pallas_call(
        _xattn_kernel,
        grid=(n // tm,),
        in_specs=[
            pl.BlockSpec((tm, d), lambda i: (i, 0)),
            _const_spec((1, d)),
            _const_spec(wq.shape),
            pl.BlockSpec((1,) + kT.shape[1:], lambda i: (i // tiles, 0, 0)),
            pl.BlockSpec((1,) + v.shape[1:], lambda i: (i // tiles, 0, 0)),
            _const_spec(wo.shape),
        ],
        out_specs=pl.BlockSpec((tm, d), lambda i: (i, 0)),
        out_shape=jax.ShapeDtypeStruct((n, d), F32),
        compiler_params=_params("arbitrary"),
        name="cross_attention",
    )(x, g.reshape(1, d), wq, kT, v, wo)


def _swiglu_kernel(x_ref, g_ref, wg_ref, wu_ref, wd_ref, gf_ref, o_ref, *, final_norm):
    x = x_ref[...]
    xn = _rms(x, g_ref[...]).astype(BF16)
    gate = jnp.dot(xn, wg_ref[...], preferred_element_type=F32)
    up = jnp.dot(xn, wu_ref[...], preferred_element_type=F32)
    hidden = (gate * jax.nn.sigmoid(gate) * up).astype(BF16)
    y = x + jnp.dot(hidden, wd_ref[...], preferred_element_type=F32)
    o_ref[...] = _rms(y, gf_ref[...]) if final_norm else y


def _swiglu(x, g, wg, wu, wd, g_final, final_norm, tm):
    n, d = x.shape
    return pl.pallas_call(
        functools.partial(_swiglu_kernel, final_norm=final_norm),
        grid=(n // tm,),
        in_specs=[
            pl.BlockSpec((tm, d), lambda i: (i, 0)),
            _const_spec((1, d)),
            _const_spec(wg.shape), _const_spec(wu.shape), _const_spec(wd.shape),
            _const_spec((1, d)),
        ],
        out_specs=pl.BlockSpec((tm, d), lambda i: (i, 0)),
        out_shape=jax.ShapeDtypeStruct((n, d), F32),
        compiler_params=_params("arbitrary"),
        name="swiglu",
    )(x, g.reshape(1, d), wg, wu, wd, g_final.reshape(1, d))


def _pack_in_proj(w_in):
    sizes = [BRANCH] * 3 + [BRANCH] * 3 + [FOX_HEADS] + [BRANCH, SWA_KV * HEAD_DIM, SWA_KV * HEAD_DIM] + [3 * D_MODEL]
    offs = np.concatenate([[0], np.cumsum(sizes)])
    c_b, c_c, c_u, f_q, f_k, f_v, f_g, s_q, s_k, s_v, gates = (w_in[:, offs[t]:offs[t + 1]] for t in range(len(sizes)))
    qscale = HEAD_DIM ** -0.5 * LOG2E
    main = jnp.concatenate([gates, c_b, c_c, c_u, f_q * qscale, f_k, f_v, s_q * qscale, s_k, s_v], axis=1)
    fg = jnp.pad(f_g, ((0, 0), (0, LANES - FOX_HEADS)))
    return main.astype(BF16), fg.astype(BF16)


def _heads_T(a, b, s, h):
    return a.reshape(b, s, h, HEAD_DIM).transpose(0, 2, 3, 1)


def _with_ones_rows(vT):
    b, h, d, s = vT.shape
    ones = jnp.ones((b, h, 1, s), vT.dtype)
    return jnp.concatenate([vT, ones, jnp.zeros((b, h, V_ROWS - d - 1, s), vT.dtype)], axis=2)


def kernel(x, mem, mix_norm_g, w_in, forget_bias, conv_w, sink, w_branch, w_mix_out, rel_bias,
           xattn_norm_g, mem_norm_g, w_xq, w_xkv, w_xo, ffn_norm_g, w_ffn_gate, w_ffn_up, w_ffn_down,
           final_norm_g):
    b, s, d = x.shape
    n = b * s
    depth = w_in.shape[0]
    mem_len = mem.shape[1]
    xf = x.reshape(n, d)
    memf = mem.reshape(b * mem_len, d)
    swa_bias = _swa_bias_table(rel_bias)
    zeros_fg = jnp.zeros((d, LANES), BF16)
    for l in range(depth):
        w_main, w_fg = _pack_in_proj(w_in[l])
        p, fg = _norm_proj(xf, mix_norm_g[l], w_main, w_fg, tm=1024, tn=P_COLS // 3)

        fgT = fg[:, :FOX_HEADS].reshape(b, s, FOX_HEADS).transpose(0, 2, 1)
        c1, c2, c3 = _forget_cumsum(fgT, forget_bias[l])
        one = jnp.ones_like(c1)
        zero = jnp.zeros_like(c1)
        aug_q = jnp.stack([c1, c2, c3, one, one, one, zero, zero], axis=2)
        aug_k = jnp.stack([one, one, one, -c1, -c2, -c3, zero, zero], axis=3)
        pad_q = jnp.zeros((b, FOX_HEADS, K_AUG - HEAD_DIM - AUG, s), BF16)
        pad_k = jnp.zeros((b, FOX_HEADS, s, K_AUG - HEAD_DIM - AUG), BF16)
        fq = _heads_T(p[:, FOX_OFF:FOX_OFF + BRANCH], b, s, FOX_HEADS)
        fk = p[:, FOX_OFF + BRANCH:FOX_OFF + 2 * BRANCH].reshape(b, s, FOX_HEADS, HEAD_DIM).transpose(0, 2, 1, 3)
        fv = _heads_T(p[:, FOX_OFF + 2 * BRANCH:FOX_OFF + 3 * BRANCH], b, s, FOX_HEADS)
        qT_aug = jnp.concatenate([fq, aug_q, pad_q], axis=2)
        k_aug = jnp.concatenate([fk, aug_k, pad_k], axis=3)
        y_fox = _fox_attention(qT_aug, k_aug, _with_ones_rows(fv), tq=1024)
        y_fox = y_fox.transpose(0, 3, 1, 2).reshape(n, BRANCH)

        sq = _heads_T(p[:, SWA_OFF:SWA_OFF + BRANCH], b, s, SWA_HEADS)
        sk = p[:, SWA_OFF + BRANCH:SWA_OFF + BRANCH + SWA_KV * HEAD_DIM].reshape(b, s, SWA_KV * HEAD_DIM)
        sv = _heads_T(p[:, SWA_OFF + BRANCH + SWA_KV * HEAD_DIM:P_COLS], b, s, SWA_KV)
        y_swa = _swa_attention(sink[l], sq, sk, _with_ones_rows(sv), swa_bias, nblk=4)
        y_swa = y_swa.transpose(0, 3, 1, 2).reshape(n, BRANCH)

        xf = _merge(xf, p, y_fox, y_swa, conv_w[l], w_branch[l].astype(BF16), w_mix_out[l].astype(BF16),
                    seq=s, tm=512)

        kv, _ = _norm_proj(memf, mem_norm_g[l], w_xkv[l].astype(BF16), zeros_fg, tm=b * mem_len, tn=1024)
        kT = kv[:, :d].reshape(b, mem_len, d).transpose(0, 2, 1)
        v = kv[:, d:].reshape(b, mem_len, d)
        wq = (w_xq[l] * (X_HEAD_DIM ** -0.5 * LOG2E)).astype(BF16)
        xf = _xattn(xf, xattn_norm_g[l], wq, kT, v, w_xo[l].astype(BF16), seq=s, tm=512)

        xf = _swiglu(xf, ffn_norm_g[l], w_ffn_gate[l].astype(BF16), w_ffn_up[l].astype(BF16),
                     w_ffn_down[l].astype(BF16), final_norm_g, final_norm=(l == depth - 1), tm=512)
    return xf.reshape(b, s, d)
```

```python
import functools
import math

import jax
import jax.numpy as jnp
import numpy as np
from jax import lax
from jax.experimental import pallas as pl
from jax.experimental.pallas import tpu as pltpu

F32 = jnp.float32
BF16 = jnp.bfloat16

D_MODEL = 1024
HEAD_DIM = 64
BRANCH = 512
FOX_HEADS = 8
SWA_HEADS = 8
SWA_KV = 2
SWA_GROUP = SWA_HEADS // SWA_KV
WINDOW = 128
N_BUCKETS = 32
X_HEADS = 4
X_HEAD_DIM = D_MODEL // X_HEADS
RMS_EPS = 1e-6
NEG = -1e30
LOG2E = math.log2(math.e)

LANES = 128
SUBLANES = 8
VMEM_LIMIT_BYTES = 56 * 1024 * 1024

GATE_OFF = 0
CONV_OFF = 3 * D_MODEL
FOXK_OFF = CONV_OFF + 3 * BRANCH
SWAK_OFF = FOXK_OFF + BRANCH
PN_COLS = SWAK_OFF + SWA_KV * HEAD_DIM
FOXQ_ROW = 0
FOXV_ROW = BRANCH
SWAQ_ROW = 2 * BRANCH
SWAV_ROW = 3 * BRANCH
PT_ROWS = SWAV_ROW + SWA_KV * HEAD_DIM

AUG_STRIDE = 8
AUG_EVEN = (HEAD_DIM, HEAD_DIM + AUG_STRIDE, HEAD_DIM + 2 * AUG_STRIDE)
AUG_ODD = (0, AUG_STRIDE, 2 * AUG_STRIDE)
V_ROWS = 80


def _params(*sem):
    return pltpu.CompilerParams(dimension_semantics=sem, vmem_limit_bytes=VMEM_LIMIT_BYTES)


def _rms(x, g):
    return x * lax.rsqrt(jnp.mean(x * x, axis=-1, keepdims=True) + RMS_EPS) * g


def _const_spec(shape):
    nd = len(shape)
    return pl.BlockSpec(shape, lambda *_: (0,) * nd, pipeline_mode=pl.Buffered(1))


def _one_hot_rows(shape, axis, positions):
    idx = lax.broadcasted_iota(jnp.int32, shape, axis)
    hit = idx == positions[0]
    for p in positions[1:]:
        hit = hit | (idx == p)
    return jnp.where(hit, 1.0, 0.0)


def _in_proj_kernel(x_ref, g_ref, wn_ref, wt_ref, wfg_ref, pn_ref, pt_ref, fg_ref, *, chunks):
    h = _rms(x_ref[...], g_ref[...]).astype(BF16)
    fg_ref[...] = jnp.dot(h, wfg_ref[...], preferred_element_type=F32)
    pt_ref[...] = lax.dot_general(wt_ref[...], h, (((1,), (1,)), ((), ())),
                                  preferred_element_type=F32).astype(BF16)
    for lo, hi in chunks:
        pn_ref[:, lo:hi] = jnp.dot(h, wn_ref[:, lo:hi], preferred_element_type=F32).astype(BF16)


def _in_proj(x, g, wn, wt, wfg, tm):
    n, d = x.shape
    step = 10 * LANES
    chunks = tuple((lo, min(lo + step, PN_COLS)) for lo in range(0, PN_COLS, step))
    return pl.pallas_call(
        functools.partial(_in_proj_kernel, chunks=chunks),
        grid=(n // tm,),
        in_specs=[
            pl.BlockSpec((tm, d), lambda i: (i, 0)),
            _const_spec((1, d)), _const_spec(wn.shape), _const_spec(wt.shape), _const_spec(wfg.shape),
        ],
        out_specs=[
            pl.BlockSpec((tm, PN_COLS), lambda i: (i, 0)),
            pl.BlockSpec((PT_ROWS, tm), lambda i: (0, i)),
            pl.BlockSpec((tm, LANES), lambda i: (i, 0)),
        ],
        out_shape=[jax.ShapeDtypeStruct((n, PN_COLS), BF16), jax.ShapeDtypeStruct((PT_ROWS, n), BF16),
                   jax.ShapeDtypeStruct((n, LANES), F32)],
        compiler_params=_params("arbitrary"),
        name="in_proj",
    )(x, g.reshape(1, d), wn, wt, wfg)


def _norm_proj_kernel(x_ref, g_ref, w_ref, o_ref):
    h = _rms(x_ref[...], g_ref[...]).astype(BF16)
    o_ref[...] = jnp.dot(h, w_ref[...], preferred_element_type=F32).astype(BF16)


def _norm_proj(x, g, w):
    n, d = x.shape
    cols = w.shape[1]
    return pl.pallas_call(
        _norm_proj_kernel,
        grid=(1,),
        in_specs=[_const_spec((n, d)), _const_spec((1, d)), _const_spec(w.shape)],
        out_specs=pl.BlockSpec((n, cols), lambda i: (0, 0)),
        out_shape=jax.ShapeDtypeStruct((n, cols), BF16),
        compiler_params=_params("arbitrary"),
        name="mem_proj",
    )(x, g.reshape(1, d), w)


def _fox_prep_kernel(fg_ref, b_ref, k_ref, o_ref, carry_ref):
    ts = fg_ref.shape[0]

    @pl.when(pl.program_id(1) == 0)
    def _():
        carry_ref[...] = jnp.zeros(carry_ref.shape, F32)

    lane = lax.broadcasted_iota(jnp.int32, (ts, LANES), 1)
    row = lax.broadcasted_iota(jnp.int32, (ts, LANES), 0)
    c = jnp.where(lane < FOX_HEADS, jax.nn.log_sigmoid(fg_ref[...] + b_ref[...]) * LOG2E, 0.0)
    d = 1
    while d < ts:
        c = c + jnp.where(row >= d, pltpu.roll(c, d, axis=0), 0.0)
        d *= 2
    c = c + carry_ref[0:1, :]
    carry_ref[0:1, :] = c[ts - 1:ts, :]

    c1 = c.astype(BF16).astype(F32)
    r1 = c - c1
    c2 = r1.astype(BF16).astype(F32)
    c3 = (r1 - c2).astype(BF16).astype(F32)
    packed = -jnp.where(lane < AUG_STRIDE, c1,
                        jnp.where(lane < 2 * AUG_STRIDE, pltpu.roll(c2, AUG_STRIDE, axis=1),
                                  pltpu.roll(c3, 2 * AUG_STRIDE, axis=1)))
    keep_even = _one_hot_rows((1, LANES), 1, AUG_EVEN)
    keep_odd = _one_hot_rows((1, LANES), 1, AUG_ODD)
    for h in range(FOX_HEADS):
        k_pair = k_ref[:, (h // 2) * LANES:(h // 2 + 1) * LANES].astype(F32)
        if h % 2 == 0:
            aug = pltpu.roll(packed, (AUG_EVEN[0] - h) % LANES, axis=1) * keep_even
            out = jnp.where(lane < HEAD_DIM, k_pair, aug)
        else:
            aug = pltpu.roll(packed, (AUG_ODD[0] - h) % LANES, axis=1) * keep_odd
            out = jnp.where(lane >= HEAD_DIM, k_pair, aug)
        o_ref[0, h] = out.astype(BF16)


def _fox_prep(fg, bias, pn, b, s, ts):
    kblk = FOXK_OFF // BRANCH
    per = s // ts
    return pl.pallas_call(
        _fox_prep_kernel,
        grid=(b, per),
        in_specs=[
            pl.BlockSpec((ts, LANES), lambda bi, j: (bi * per + j, 0)),
            _const_spec((1, LANES)),
            pl.BlockSpec((ts, BRANCH), lambda bi, j: (bi * per + j, kblk)),
        ],
        out_specs=pl.BlockSpec((1, FOX_HEADS, ts, LANES), lambda bi, j: (bi, 0, j, 0)),
        out_shape=jax.ShapeDtypeStruct((b, FOX_HEADS, s, LANES), BF16),
        scratch_shapes=[pltpu.VMEM((SUBLANES, LANES), F32)],
        compiler_params=_params("arbitrary", "arbitrary"),
        name="fox_prep",
    )(fg, jnp.pad(bias, (0, LANES - FOX_HEADS)).reshape(1, LANES), pn)


def _fox_kernel(q_ref, k_ref, v_ref, o_ref, q_aug, v_aug, s_buf, p_buf, acc_ref, *, tq, tk):
    i = pl.program_id(2)
    seq = v_ref.shape[1]

    @pl.when(i == 0)
    def _():
        ones_row = _one_hot_rows((V_ROWS - HEAD_DIM, seq), 0, (0,)).astype(BF16)
        for hh in range(2):
            v_aug[hh, 0:HEAD_DIM, :] = v_ref[hh * HEAD_DIM:(hh + 1) * HEAD_DIM, :]
            v_aug[hh, HEAD_DIM:V_ROWS, :] = ones_row

    qrow = lax.broadcasted_iota(jnp.int32, (LANES, tq), 0)
    qb = q_ref[...].astype(F32)
    q_aug[0] = jnp.where(qrow < HEAD_DIM, qb, _one_hot_rows((LANES, tq), 0, AUG_EVEN)).astype(BF16)
    q_aug[1] = jnp.where(qrow >= HEAD_DIM, qb, _one_hot_rows((LANES, tq), 0, AUG_ODD)).astype(BF16)

    key = lax.broadcasted_iota(jnp.int32, (tk, tq), 0)
    qry = lax.broadcasted_iota(jnp.int32, (tk, tq), 1)
    outs = []
    for hh in range(2):
        def scores(t, slot):
            off = pl.multiple_of(t * tk, tk)
            s_buf[slot] = jnp.dot(k_ref[0, hh, pl.ds(off, tk), :], q_aug[hh], preferred_element_type=F32)

        def softmax(slot, m, mask=None):
            s = s_buf[slot]
            if mask is not None:
                s = jnp.where(mask, s, NEG)
            m_new = jnp.maximum(m, jnp.max(s, axis=0, keepdims=True))
            p_buf[slot] = jnp.exp2(s - m_new).astype(BF16)
            return m_new, jnp.exp2(m - m_new)

        def values(t, slot, alpha):
            off = pl.multiple_of(jnp.maximum(t, 0) * tk, tk)
            pv = jnp.dot(v_aug[hh, :, pl.ds(off, tk)], p_buf[slot], preferred_element_type=F32)
            acc_ref[...] = alpha * acc_ref[...] + pv

        p_buf[1] = jnp.zeros(p_buf.shape[1:], BF16)
        acc_ref[...] = jnp.zeros(acc_ref.shape, F32)
        scores(0, 0)

        def pair(u, carry):
            m, alpha_prev = carry
            t = 2 * u
            scores(t + 1, 1)
            m, alpha0 = softmax(0, m)
            values(t - 1, 1, alpha_prev)
            scores(t + 2, 0)
            m, alpha1 = softmax(1, m)
            values(t, 0, alpha0)
            return m, alpha1

        carry = (jnp.full((1, tq), NEG, F32), jnp.ones((1, tq), F32))
        m, alpha_prev = lax.fori_loop(0, i, pair, carry)

        t = 2 * i
        scores(t + 1, 1)
        m, alpha0 = softmax(0, m, key <= qry)
        values(t - 1, 1, alpha_prev)
        m, alpha1 = softmax(1, m, key + tk <= qry)
        values(t, 0, alpha0)
        values(t + 1, 1, alpha1)
        acc = acc_ref[...]
        outs.append(acc[:HEAD_DIM] / acc[HEAD_DIM:HEAD_DIM + 1])
    o_ref[...] = jnp.concatenate(outs, axis=0).T.astype(BF16)


def _fox_attention(pt, k_aug, b, s, tq):
    tk = tq // 2
    per = s // tq
    qblk = FOXQ_ROW // LANES
    vblk = FOXV_ROW // LANES
    return pl.pallas_call(
        functools.partial(_fox_kernel, tq=tq, tk=tk),
        grid=(b, FOX_HEADS // 2, per),
        in_specs=[
            pl.BlockSpec((LANES, tq), lambda bi, pr, i: (qblk + pr, bi * per + i)),
            pl.BlockSpec((1, 2, s, LANES), lambda bi, pr, i: (bi, pr, 0, 0)),
            pl.BlockSpec((LANES, s), lambda bi, pr, i: (vblk + pr, bi)),
        ],
        out_specs=pl.BlockSpec((tq, LANES), lambda bi, pr, i: (bi * per + i, pr)),
        out_shape=jax.ShapeDtypeStruct((b * s, BRANCH), BF16),
        scratch_shapes=[
            pltpu.VMEM((2, LANES, tq), BF16),
            pltpu.VMEM((2, V_ROWS, s), BF16),
            pltpu.VMEM((2, tk, tq), F32),
            pltpu.VMEM((2, tk, tq), BF16),
            pltpu.VMEM((V_ROWS, tq), F32),
        ],
        compiler_params=_params("arbitrary", "arbitrary", "arbitrary"),
        name="fox_attention",
    )(pt, k_aug, pt)


def _swa_bias_kernel(rb_ref, bucket_ref, win_ref, o_ref):
    h = pl.program_id(0)
    bucket = bucket_ref[...]
    bias = jnp.zeros(bucket.shape, F32)
    for b in range(N_BUCKETS):
        bias = jnp.where(bucket == b, rb_ref[b, h] * LOG2E, bias)
    o_ref[0, 0] = jnp.where(win_ref[0] > 0, bias, NEG)
    o_ref[1, 0] = jnp.where(win_ref[1] > 0, bias, NEG)


def _swa_bias_table(rel_bias):
    tq = np.arange(WINDOW)[None, :]
    sk = np.arange(2 * WINDOW)[:, None]
    dist = WINDOW + tq - sk
    n = jnp.maximum(jnp.asarray(dist, jnp.int32), 0)
    max_exact = N_BUCKETS // 2
    large = max_exact + (jnp.log(jnp.maximum(n, 1).astype(F32) / max_exact)
                         / math.log(WINDOW / max_exact) * (N_BUCKETS - max_exact)).astype(jnp.int32)
    bucket = jnp.where(n < max_exact, n, jnp.minimum(large, N_BUCKETS - 1))
    in_window = (dist >= 0) & (dist < WINDOW)
    win = np.stack([in_window, in_window & (sk >= WINDOW)]).astype(np.int32)
    full = lambda shape: pl.BlockSpec(shape, lambda h: (0,) * len(shape))
    return pl.pallas_call(
        _swa_bias_kernel,
        grid=(SWA_HEADS,),
        in_specs=[pl.BlockSpec(memory_space=pltpu.SMEM), full(bucket.shape), full(win.shape)],
        out_specs=pl.BlockSpec((2, 1, 2 * WINDOW, WINDOW), lambda h: (0, h // SWA_GROUP, 0, h % SWA_GROUP)),
        out_shape=jax.ShapeDtypeStruct((2, SWA_KV, 2 * WINDOW, SWA_GROUP * WINDOW), F32),
        compiler_params=_params("arbitrary"),
        name="swa_bias_table",
    )(rel_bias, bucket, jnp.asarray(win))


def _swa_kernel(sink_ref, q_ref, kp_ref, kc_ref, vp_ref, vc_ref, bias_ref, o_ref, *, nblk):
    first = (pl.program_id(1) == 0).astype(jnp.int32)
    keys = jnp.concatenate([kp_ref[...], kc_ref[...]], axis=0)
    vals = jnp.concatenate([vp_ref[...], vc_ref[...]], axis=1)
    glane = lax.broadcasted_iota(jnp.int32, (1, SWA_GROUP * WINDOW), 1) // WINDOW
    zeros = jnp.zeros((HEAD_DIM, SWA_GROUP * WINDOW), BF16)
    for kv in range(SWA_KV):
        sink = jnp.zeros((1, SWA_GROUP * WINDOW), F32)
        for g in range(SWA_GROUP):
            sink = jnp.where(glane == g, sink_ref[kv * SWA_GROUP + g] * LOG2E, sink)
        for blk in range(nblk):
            lo = blk * WINDOW
            sel = first if blk == 0 else 0
            qg = jnp.concatenate([q_ref[(kv * SWA_GROUP + g) * HEAD_DIM:(kv * SWA_GROUP + g + 1) * HEAD_DIM,
                                        lo:lo + WINDOW] for g in range(SWA_GROUP)], axis=1)
            q_pad = jnp.concatenate([qg, zeros] if kv == 0 else [zeros, qg], axis=0)
            s = jnp.dot(keys[lo:lo + 2 * WINDOW, :], q_pad, preferred_element_type=F32) + bias_ref[sel, kv]
            m = jnp.maximum(jnp.max(s, axis=0, keepdims=True), sink)
            p = jnp.exp2(s - m)
            denom = jnp.sum(p, axis=0, keepdims=True) + jnp.exp2(sink - m)
            vb = vals[kv * HEAD_DIM:(kv + 1) * HEAD_DIM, lo:lo + 2 * WINDOW]
            o = jnp.dot(vb, p.astype(BF16), preferred_element_type=F32) / denom
            for pr in range(SWA_GROUP // 2):
                two = jnp.concatenate([o[:, (2 * pr) * WINDOW:(2 * pr + 1) * WINDOW],
                                       o[:, (2 * pr + 1) * WINDOW:(2 * pr + 2) * WINDOW]], axis=0)
                col = (kv * (SWA_GROUP // 2) + pr) * LANES
                o_ref[lo:lo + WINDOW, col:col + LANES] = two.T.astype(BF16)


def _swa_attention(sink, pt, pn, bias, b, s, nblk):
    ts = nblk * WINDOW
    per = s // ts
    perw = s // WINDOW
    qblk = SWAQ_ROW // BRANCH
    vblk = SWAV_ROW // LANES
    kblk = SWAK_OFF // LANES
    prev = lambda bi, i: bi * perw + jnp.maximum(i * nblk - 1, 0)
    return pl.pallas_call(
        functools.partial(_swa_kernel, nblk=nblk),
        grid=(b, per),
        in_specs=[
            pl.BlockSpec(memory_space=pltpu.SMEM),
            pl.BlockSpec((BRANCH, ts), lambda bi, i: (qblk, bi * per + i)),
            pl.BlockSpec((WINDOW, LANES), lambda bi, i: (prev(bi, i), kblk)),
            pl.BlockSpec((ts, LANES), lambda bi, i: (bi * per + i, kblk)),
            pl.BlockSpec((LANES, WINDOW), lambda bi, i: (vblk, prev(bi, i))),
            pl.BlockSpec((LANES, ts), lambda bi, i: (vblk, bi * per + i)),
            _const_spec(bias.shape),
        ],
        out_specs=pl.BlockSpec((ts, BRANCH), lambda bi, i: (bi * per + i, 0)),
        out_shape=jax.ShapeDtypeStruct((b * s, BRANCH), BF16),
        compiler_params=_params("arbitrary", "arbitrary"),
        name="swa_attention",
    )(sink, pt, pn, pn, pt, pt, bias)


def _merge_kernel(x_ref, gate_ref, cb_ref, cc_ref, cu_ref, cch_ref, cuh_ref, yf_ref, ys_ref,
                  cw_ref, wb_ref, wo_ref, o_ref, *, tiles_per_seq):
    z = cc_ref[...].astype(F32) * cu_ref[...].astype(F32)
    zh = cch_ref[...].astype(F32) * cuh_ref[...].astype(F32)
    zh = jnp.where(pl.program_id(0) % tiles_per_seq == 0, 0.0, zh)
    row = lax.broadcasted_iota(jnp.int32, z.shape, 0)
    z1 = jnp.where(row == 0, zh[7:8], pltpu.roll(z, 1, axis=0))
    z2 = jnp.where(row == 0, zh[6:7], jnp.where(row == 1, zh[7:8], pltpu.roll(z, 2, axis=0)))
    cw = cw_ref[...]
    y_conv = cb_ref[...].astype(F32) * (cw[0:1] * z2 + cw[1:2] * z1 + cw[2:3] * z)
    branches = (y_conv.astype(BF16), yf_ref[...], ys_ref[...])
    merged = None
    for b, y in enumerate(branches):
        gate = jax.nn.sigmoid(gate_ref[:, b * D_MODEL:(b + 1) * D_MODEL].astype(F32))
        term = gate * jnp.dot(y, wb_ref[b], preferred_element_type=F32)
        merged = term if merged is None else merged + term
    o_ref[...] = x_ref[...] + jnp.dot(merged.astype(BF16), wo_ref[...], preferred_element_type=F32)


def _merge(x, pn, y_fox, y_swa, conv_w, w_branch, w_out, seq, tm):
    n, d = x.shape
    cblk = CONV_OFF // BRANCH
    halo = lambda i: jnp.maximum(i * (tm // SUBLANES) - 1, 0)
    row = lambda c: pl.BlockSpec((tm, BRANCH), lambda i: (i, c))
    return pl.pallas_call(
        functools.partial(_merge_kernel, tiles_per_seq=seq // tm),
        grid=(n // tm,),
        in_specs=[
            pl.BlockSpec((tm, d), lambda i: (i, 0)),
            pl.BlockSpec((tm, 3 * D_MODEL), lambda i: (i, GATE_OFF // (3 * D_MODEL))),
            row(cblk), row(cblk + 1), row(cblk + 2),
            pl.BlockSpec((SUBLANES, BRANCH), lambda i: (halo(i), cblk + 1)),
            pl.BlockSpec((SUBLANES, BRANCH), lambda i: (halo(i), cblk + 2)),
            pl.BlockSpec((tm, BRANCH), lambda i: (i, 0)),
            pl.BlockSpec((tm, BRANCH), lambda i: (i, 0)),
            _const_spec(conv_w.shape), _const_spec(w_branch.shape), _const_spec(w_out.shape),
        ],
        out_specs=pl.BlockSpec((tm, d), lambda i: (i, 0)),
        out_shape=jax.ShapeDtypeStruct((n, d), F32),
        compiler_params=_params("arbitrary"),
        name="conv_merge_out",
    )(x, pn, pn, pn, pn, pn, pn, y_fox, y_swa, conv_w, w_branch, w_out)


def _xattn_kernel(x_ref, g_ref, wq_ref, kT_ref, v_ref, wo_ref, o_ref):
    x = x_ref[...]
    q = jnp.dot(_rms(x, g_ref[...]).astype(BF16), wq_ref[...], preferred_element_type=F32).astype(BF16)
    heads = []
    for h in range(X_HEADS):
        sl = slice(h * X_HEAD_DIM, (h + 1) * X_HEAD_DIM)
        s = jnp.dot(q[:, sl], kT_ref[0, sl, :], preferred_element_type=F32)
        p = jnp.exp2(s - jnp.max(s, axis=-1, keepdims=True))
        denom = jnp.sum(p, axis=-1, keepdims=True)
        o = jnp.dot(p.astype(BF16), v_ref[0, :, sl], preferred_element_type=F32)
        heads.append((o / denom).astype(BF16))
    o_ref[...] = x + jnp.dot(jnp.concatenate(heads, axis=1), wo_ref[...], preferred_element_type=F32)


def _xattn(x, g, wq, kT, v, wo, seq, tm):
    n, d = x.shape
    tiles = seq // tm
    return pl.pallas_call(
        _xattn_kernel,
        grid=(n // tm,),
        in_specs=[
            pl.BlockSpec((tm, d), lambda i: (i, 0)),
            _const_spec((1, d)),
            _const_spec(wq.shape),
            pl.BlockSpec((1,) + kT.shape[1:], lambda i: (i // tiles, 0, 0)),
            pl.BlockSpec((1,) + v.shape[1:], lambda i: (i // tiles, 0, 0)),
            _const_spec(wo.shape),
        ],
        out_specs=pl.BlockSpec((tm, d), lambda i: (i, 0)),
        out_shape=jax.ShapeDtypeStruct((n, d), F32),
        compiler_params=_params("arbitrary"),
        name="cross_attention",
    )(x, g.reshape(1, d), wq, kT, v, wo)


def _swiglu_kernel(x_ref, g_ref, wg_ref, wu_ref, wd_ref, gf_ref, o_ref, *, final_norm):
    x = x_ref[...]
    xn = _rms(x, g_ref[...]).astype(BF16)
    gate = jnp.dot(xn, wg_ref[...], preferred_element_type=F32)
    up = jnp.dot(xn, wu_ref[...], preferred_element_type=F32)
    hidden = (gate * jax.nn.sigmoid(gate) * up).astype(BF16)
    y = x + jnp.dot(hidden, wd_ref[...], preferred_element_type=F32)
    o_ref[...] = _rms(y, gf_ref[...]) if final_norm else y


def _swiglu(x, g, wg, wu, wd, g_final, final_norm, tm):
    n, d = x.shape
    return pl.pallas_call(
        functools.partial(_swiglu_kernel, final_norm=final_norm),
        grid=(n // tm,),
        in_specs=[
            pl.BlockSpec((tm, d), lambda i: (i, 0)),
            _const_spec((1, d)),
            _const_spec(wg.shape), _const_spec(wu.shape), _const_spec(wd.shape),
            _const_spec((1, d)),
        ],
        out_specs=pl.BlockSpec((tm, d), lambda i: (i, 0)),
        out_shape=jax.ShapeDtypeStruct((n, d), F32),
        compiler_params=_params("arbitrary"),
        name="swiglu",
    )(x, g.reshape(1, d), wg, wu, wd, g_final.reshape(1, d))


def _pack_in_proj(w_in):
    sizes = [BRANCH] * 3 + [BRANCH] * 3 + [FOX_HEADS] + [BRANCH, SWA_KV * HEAD_DIM, SWA_KV * HEAD_DIM] + [3 * D_MODEL]
    offs = np.concatenate([[0], np.cumsum(sizes)])
    c_b, c_c, c_u, f_q, f_k, f_v, f_g, s_q, s_k, s_v, gates = (w_in[:, offs[t]:offs[t + 1]] for t in range(len(sizes)))
    qscale = HEAD_DIM ** -0.5 * LOG2E
    wn = jnp.concatenate([gates, c_b, c_c, c_u, f_k, s_k], axis=1)
    wt = jnp.concatenate([f_q * qscale, f_v, s_q * qscale, s_v], axis=1).T
    wfg = jnp.pad(f_g, ((0, 0), (0, LANES - FOX_HEADS)))
    return wn.astype(BF16), wt.astype(BF16), wfg.astype(BF16)


def kernel(x, mem, mix_norm_g, w_in, forget_bias, conv_w, sink, w_branch, w_mix_out, rel_bias,
           xattn_norm_g, mem_norm_g, w_xq, w_xkv, w_xo, ffn_norm_g, w_ffn_gate, w_ffn_up, w_ffn_down,
           final_norm_g):
    b, s, d = x.shape
    n = b * s
    depth = w_in.shape[0]
    mem_len = mem.shape[1]
    xf = x.reshape(n, d)
    memf = mem.reshape(b * mem_len, d)
    swa_bias = _swa_bias_table(rel_bias)
    for l in range(depth):
        wn, wt, wfg = _pack_in_proj(w_in[l])
        pn, pt, fg = _in_proj(xf, mix_norm_g[l], wn, wt, wfg, tm=512)

        k_aug = _fox_prep(fg, forget_bias[l], pn, b, s, ts=1024)
        y_fox = _fox_attention(pt, k_aug, b, s, tq=1024)
        y_swa = _swa_attention(sink[l], pt, pn, swa_bias, b, s, nblk=4)
        xf = _merge(xf, pn, y_fox, y_swa, conv_w[l], w_branch[l].astype(BF16), w_mix_out[l].astype(BF16),
                    seq=s, tm=512)

        kv = _norm_proj(memf, mem_norm_g[l], w_xkv[l].astype(BF16))
        kT = kv[:, :d].reshape(b, mem_len, d).transpose(0, 2, 1)
        v = kv[:, d:].reshape(b, mem_len, d)
        wq = (w_xq[l] * (X_HEAD_DIM ** -0.5 * LOG2E)).astype(BF16)
        xf = _xattn(xf, xattn_norm_g[l], wq, kT, v, w_xo[l].astype(BF16), seq=s, tm=512)

        xf = _swiglu(xf, ffn_norm_g[l], w_ffn_gate[l].astype(BF16), w_ffn_up[l].astype(BF16),
                     w_ffn_down[l].astype(BF16), final_norm_g, final_norm=(l == depth - 1), tm=512)
    return xf.reshape(b, s, d)
```

```python
import functools
import math

import jax
import jax.numpy as jnp
import numpy as np
from jax import lax
from jax.experimental import pallas as pl
from jax.experimental.pallas import tpu as pltpu

F32 = jnp.float32
BF16 = jnp.bfloat16

D_MODEL = 1024
HEAD_DIM = 64
BRANCH = 512
FOX_HEADS = 8
SWA_HEADS = 8
SWA_KV = 2
SWA_GROUP = SWA_HEADS // SWA_KV
WINDOW = 128
N_BUCKETS = 32
X_HEADS = 4
X_HEAD_DIM = D_MODEL // X_HEADS
RMS_EPS = 1e-6
NEG = -1e30
LOG2E = math.log2(math.e)

LANES = 128
SUBLANES = 8
VMEM_LIMIT_BYTES = 56 * 1024 * 1024

GATE_OFF = 0
CONV_OFF = 3 * D_MODEL
FOXK_OFF = CONV_OFF + 3 * BRANCH
SWAK_OFF = FOXK_OFF + BRANCH
PN_COLS = SWAK_OFF + SWA_KV * HEAD_DIM
FOXQ_ROW = 0
FOXV_ROW = BRANCH
SWAQ_ROW = 2 * BRANCH
SWAV_ROW = 3 * BRANCH
PT_ROWS = SWAV_ROW + SWA_KV * HEAD_DIM

AUG_STRIDE = 8
AUG_EVEN = (HEAD_DIM, HEAD_DIM + AUG_STRIDE, HEAD_DIM + 2 * AUG_STRIDE)
AUG_ODD = (0, AUG_STRIDE, 2 * AUG_STRIDE)
V_ROWS = 80


def _params(*sem):
    return pltpu.CompilerParams(dimension_semantics=sem, vmem_limit_bytes=VMEM_LIMIT_BYTES)


def _rms(x, g):
    return x * lax.rsqrt(jnp.mean(x * x, axis=-1, keepdims=True) + RMS_EPS) * g


def _const_spec(shape):
    nd = len(shape)
    return pl.BlockSpec(shape, lambda *_: (0,) * nd, pipeline_mode=pl.Buffered(1))


def _one_hot_rows(shape, axis, positions):
    idx = lax.broadcasted_iota(jnp.int32, shape, axis)
    hit = idx == positions[0]
    for p in positions[1:]:
        hit = hit | (idx == p)
    return jnp.where(hit, 1.0, 0.0)


def _in_proj_kernel(x_ref, g_ref, wn_ref, wt_ref, wfg_ref, pn_ref, pt_ref, fg_ref, *, chunks):
    h = _rms(x_ref[...], g_ref[...]).astype(BF16)
    fg_ref[...] = jnp.dot(h, wfg_ref[...], preferred_element_type=F32)
    pt_ref[...] = lax.dot_general(wt_ref[...], h, (((1,), (1,)), ((), ())),
                                  preferred_element_type=F32).astype(BF16)
    for lo, hi in chunks:
        pn_ref[:, lo:hi] = jnp.dot(h, wn_ref[:, lo:hi], preferred_element_type=F32).astype(BF16)


def _in_proj(x, g, wn, wt, wfg, tm):
    n, d = x.shape
    step = 10 * LANES
    chunks = tuple((lo, min(lo + step, PN_COLS)) for lo in range(0, PN_COLS, step))
    return pl.pallas_call(
        functools.partial(_in_proj_kernel, chunks=chunks),
        grid=(n // tm,),
        in_specs=[
            pl.BlockSpec((tm, d), lambda i: (i, 0)),
            _const_spec((1, d)), _const_spec(wn.shape), _const_spec(wt.shape), _const_spec(wfg.shape),
        ],
        out_specs=[
            pl.BlockSpec((tm, PN_COLS), lambda i: (i, 0)),
            pl.BlockSpec((PT_ROWS, tm), lambda i: (0, i)),
            pl.BlockSpec((tm, LANES), lambda i: (i, 0)),
        ],
        out_shape=[jax.ShapeDtypeStruct((n, PN_COLS), BF16), jax.ShapeDtypeStruct((PT_ROWS, n), BF16),
                   jax.ShapeDtypeStruct((n, LANES), F32)],
        compiler_params=_params("arbitrary"),
        name="in_proj",
    )(x, g.reshape(1, d), wn, wt, wfg)


def _norm_proj_kernel(x_ref, g_ref, w_ref, o_ref):
    h = _rms(x_ref[...], g_ref[...]).astype(BF16)
    o_ref[...] = jnp.dot(h, w_ref[...], preferred_element_type=F32).astype(BF16)


def _norm_proj(x, g, w):
    n, d = x.shape
    cols = w.shape[1]
    return pl.pallas_call(
        _norm_proj_kernel,
        grid=(1,),
        in_specs=[_const_spec((n, d)), _const_spec((1, d)), _const_spec(w.shape)],
        out_specs=pl.BlockSpec((n, cols), lambda i: (0, 0)),
        out_shape=jax.ShapeDtypeStruct((n, cols), BF16),
        compiler_params=_params("arbitrary"),
        name="mem_proj",
    )(x, g.reshape(1, d), w)


def _fox_prep_kernel(fg_ref, b_ref, k_ref, o_ref, carry_ref):
    ts = fg_ref.shape[0]

    @pl.when(pl.program_id(1) == 0)
    def _():
        carry_ref[...] = jnp.zeros(carry_ref.shape, F32)

    lane = lax.broadcasted_iota(jnp.int32, (ts, LANES), 1)
    row = lax.broadcasted_iota(jnp.int32, (ts, LANES), 0)
    c = jnp.where(lane < FOX_HEADS, jax.nn.log_sigmoid(fg_ref[...] + b_ref[...]) * LOG2E, 0.0)
    d = 1
    while d < ts:
        c = c + jnp.where(row >= d, pltpu.roll(c, d, axis=0), 0.0)
        d *= 2
    c = c + carry_ref[0:1, :]
    carry_ref[0:1, :] = c[ts - 1:ts, :]

    c1 = c.astype(BF16).astype(F32)
    r1 = c - c1
    c2 = r1.astype(BF16).astype(F32)
    c3 = (r1 - c2).astype(BF16).astype(F32)
    packed = -jnp.where(lane < AUG_STRIDE, c1,
                        jnp.where(lane < 2 * AUG_STRIDE, pltpu.roll(c2, AUG_STRIDE, axis=1),
                                  pltpu.roll(c3, 2 * AUG_STRIDE, axis=1)))
    keep_even = _one_hot_rows((1, LANES), 1, AUG_EVEN)
    keep_odd = _one_hot_rows((1, LANES), 1, AUG_ODD)
    for h in range(FOX_HEADS):
        k_pair = k_ref[:, (h // 2) * LANES:(h // 2 + 1) * LANES].astype(F32)
        if h % 2 == 0:
            aug = pltpu.roll(packed, (AUG_EVEN[0] - h) % LANES, axis=1) * keep_even
            out = jnp.where(lane < HEAD_DIM, k_pair, aug)
        else:
            aug = pltpu.roll(packed, (AUG_ODD[0] - h) % LANES, axis=1) * keep_odd
            out = jnp.where(lane >= HEAD_DIM, k_pair, aug)
        o_ref[0, h] = out.astype(BF16)


def _fox_prep(fg, bias, pn, b, s, ts):
    kblk = FOXK_OFF // BRANCH
    per = s // ts
    return pl.pallas_call(
        _fox_prep_kernel,
        grid=(b, per),
        in_specs=[
            pl.BlockSpec((ts, LANES), lambda bi, j: (bi * per + j, 0)),
            _const_spec((1, LANES)),
            pl.BlockSpec((ts, BRANCH), lambda bi, j: (bi * per + j, kblk)),
        ],
        out_specs=pl.BlockSpec((1, FOX_HEADS, ts, LANES), lambda bi, j: (bi, 0, j, 0)),
        out_shape=jax.ShapeDtypeStruct((b, FOX_HEADS, s, LANES), BF16),
        scratch_shapes=[pltpu.VMEM((SUBLANES, LANES), F32)],
        compiler_params=_params("arbitrary", "arbitrary"),
        name="fox_prep",
    )(fg, jnp.pad(bias, (0, LANES - FOX_HEADS)).reshape(1, LANES), pn)


def _fox_kernel(q_ref, k_ref, v_ref, o_ref, q_aug, v_aug, s_buf, p_buf, acc_ref, *, tq, tk):
    i = pl.program_id(2)
    seq = v_ref.shape[1]

    @pl.when(i == 0)
    def _():
        ones_row = _one_hot_rows((V_ROWS - HEAD_DIM, seq), 0, (0,)).astype(BF16)
        for hh in range(2):
            v_aug[hh, 0:HEAD_DIM, :] = v_ref[hh * HEAD_DIM:(hh + 1) * HEAD_DIM, :]
            v_aug[hh, HEAD_DIM:V_ROWS, :] = ones_row

    qrow = lax.broadcasted_iota(jnp.int32, (LANES, tq), 0)
    qb = q_ref[...].astype(F32)
    q_aug[0] = jnp.where(qrow < HEAD_DIM, qb, _one_hot_rows((LANES, tq), 0, AUG_EVEN)).astype(BF16)
    q_aug[1] = jnp.where(qrow >= HEAD_DIM, qb, _one_hot_rows((LANES, tq), 0, AUG_ODD)).astype(BF16)

    heads = (0, 1)

    def scores(t, slot):
        off = pl.multiple_of(t * tk, tk)
        for hh in heads:
            s_buf[hh, slot] = jnp.dot(k_ref[0, hh, pl.ds(off, tk), :], q_aug[hh], preferred_element_type=F32)

    def softmax(slot, ms, mask=None):
        out = []
        for hh in heads:
            s = s_buf[hh, slot]
            if mask is not None:
                s = jnp.where(mask, s, NEG)
            m_new = jnp.maximum(ms[hh], jnp.max(s, axis=0, keepdims=True))
            p_buf[hh, slot] = jnp.exp2(s - m_new).astype(BF16)
            out.append((m_new, jnp.exp2(ms[hh] - m_new)))
        return tuple(o[0] for o in out), tuple(o[1] for o in out)

    def values(t, slot, alphas):
        off = pl.multiple_of(jnp.maximum(t, 0) * tk, tk)
        for hh in heads:
            pv = jnp.dot(v_aug[hh, :, pl.ds(off, tk)], p_buf[hh, slot], preferred_element_type=F32)
            acc_ref[hh] = alphas[hh] * acc_ref[hh] + pv

    for hh in heads:
        p_buf[hh, 1] = jnp.zeros(p_buf.shape[2:], BF16)
        acc_ref[hh] = jnp.zeros(acc_ref.shape[1:], F32)
    scores(0, 0)

    def pair(u, carry):
        ms, alpha_prev = carry
        t = 2 * u
        scores(t + 1, 1)
        ms, alpha0 = softmax(0, ms)
        values(t - 1, 1, alpha_prev)
        scores(t + 2, 0)
        ms, alpha1 = softmax(1, ms)
        values(t, 0, alpha0)
        return ms, alpha1

    carry = (tuple(jnp.full((1, tq), NEG, F32) for _ in heads), tuple(jnp.ones((1, tq), F32) for _ in heads))
    ms, alpha_prev = lax.fori_loop(0, i, pair, carry)

    t = 2 * i
    key = lax.broadcasted_iota(jnp.int32, (tk, tq), 0)
    qry = lax.broadcasted_iota(jnp.int32, (tk, tq), 1)
    scores(t + 1, 1)
    ms, alpha0 = softmax(0, ms, key <= qry)
    values(t - 1, 1, alpha_prev)
    ms, alpha1 = softmax(1, ms, key + tk <= qry)
    values(t, 0, alpha0)
    values(t + 1, 1, alpha1)
    outs = []
    for hh in heads:
        acc = acc_ref[hh]
        outs.append(acc[:HEAD_DIM] / acc[HEAD_DIM:HEAD_DIM + 1])
    o_ref[...] = jnp.concatenate(outs, axis=0).T.astype(BF16)


def _fox_attention(pt, k_aug, b, s, tq):
    tk = tq // 2
    per = s // tq
    qblk = FOXQ_ROW // LANES
    vblk = FOXV_ROW // LANES
    return pl.pallas_call(
        functools.partial(_fox_kernel, tq=tq, tk=tk),
        grid=(b, FOX_HEADS // 2, per),
        in_specs=[
            pl.BlockSpec((LANES, tq), lambda bi, pr, i: (qblk + pr, bi * per + i)),
            pl.BlockSpec((1, 2, s, LANES), lambda bi, pr, i: (bi, pr, 0, 0)),
            pl.BlockSpec((LANES, s), lambda bi, pr, i: (vblk + pr, bi)),
        ],
        out_specs=pl.BlockSpec((tq, LANES), lambda bi, pr, i: (bi * per + i, pr)),
        out_shape=jax.ShapeDtypeStruct((b * s, BRANCH), BF16),
        scratch_shapes=[
            pltpu.VMEM((2, LANES, tq), BF16),
            pltpu.VMEM((2, V_ROWS, s), BF16),
            pltpu.VMEM((2, 2, tk, tq), F32),
            pltpu.VMEM((2, 2, tk, tq), BF16),
            pltpu.VMEM((2, V_ROWS, tq), F32),
        ],
        compiler_params=_params("arbitrary", "arbitrary", "arbitrary"),
        name="fox_attention",
    )(pt, k_aug, pt)


def _swa_bias_kernel(rb_ref, bucket_ref, win_ref, o_ref):
    h = pl.program_id(0)
    bucket = bucket_ref[...]
    bias = jnp.zeros(bucket.shape, F32)
    for b in range(N_BUCKETS):
        bias = jnp.where(bucket == b, rb_ref[b, h] * LOG2E, bias)
    o_ref[0, 0] = jnp.where(win_ref[0] > 0, bias, NEG)
    o_ref[1, 0] = jnp.where(win_ref[1] > 0, bias, NEG)


def _swa_bias_table(rel_bias):
    tq = np.arange(WINDOW)[None, :]
    sk = np.arange(2 * WINDOW)[:, None]
    dist = WINDOW + tq - sk
    n = jnp.maximum(jnp.asarray(dist, jnp.int32), 0)
    max_exact = N_BUCKETS // 2
    large = max_exact + (jnp.log(jnp.maximum(n, 1).astype(F32) / max_exact)
                         / math.log(WINDOW / max_exact) * (N_BUCKETS - max_exact)).astype(jnp.int32)
    bucket = jnp.where(n < max_exact, n, jnp.minimum(large, N_BUCKETS - 1))
    in_window = (dist >= 0) & (dist < WINDOW)
    win = np.stack([in_window, in_window & (sk >= WINDOW)]).astype(np.int32)
    full = lambda shape: pl.BlockSpec(shape, lambda h: (0,) * len(shape))
    return pl.pallas_call(
        _swa_bias_kernel,
        grid=(SWA_HEADS,),
        in_specs=[pl.BlockSpec(memory_space=pltpu.SMEM), full(bucket.shape), full(win.shape)],
        out_specs=pl.BlockSpec((2, 1, 2 * WINDOW, WINDOW), lambda h: (0, h // SWA_GROUP, 0, h % SWA_GROUP)),
        out_shape=jax.ShapeDtypeStruct((2, SWA_KV, 2 * WINDOW, SWA_GROUP * WINDOW), F32),
        compiler_params=_params("arbitrary"),
        name="swa_bias_table",
    )(rel_bias, bucket, jnp.asarray(win))


def _swa_kernel(sink_ref, q_ref, kp_ref, kc_ref, vp_ref, vc_ref, bias_ref, o_ref, *, nblk):
    first = (pl.program_id(1) == 0).astype(jnp.int32)
    keys = jnp.concatenate([kp_ref[...], kc_ref[...]], axis=0)
    vals = jnp.concatenate([vp_ref[...], vc_ref[...]], axis=1)
    glane = lax.broadcasted_iota(jnp.int32, (1, SWA_GROUP * WINDOW), 1) // WINDOW
    zeros = jnp.zeros((HEAD_DIM, SWA_GROUP * WINDOW), BF16)
    for kv in range(SWA_KV):
        sink = jnp.zeros((1, SWA_GROUP * WINDOW), F32)
        for g in range(SWA_GROUP):
            sink = jnp.where(glane == g, sink_ref[kv * SWA_GROUP + g] * LOG2E, sink)
        for blk in range(nblk):
            lo = blk * WINDOW
            sel = first if blk == 0 else 0
            qg = jnp.concatenate([q_ref[(kv * SWA_GROUP + g) * HEAD_DIM:(kv * SWA_GROUP + g + 1) * HEAD_DIM,
                                        lo:lo + WINDOW] for g in range(SWA_GROUP)], axis=1)
            q_pad = jnp.concatenate([qg, zeros] if kv == 0 else [zeros, qg], axis=0)
            s = jnp.dot(keys[lo:lo + 2 * WINDOW, :], q_pad, preferred_element_type=F32) + bias_ref[sel, kv]
            m = jnp.maximum(jnp.max(s, axis=0, keepdims=True), sink)
            p = jnp.exp2(s - m)
            denom = jnp.sum(p, axis=0, keepdims=True) + jnp.exp2(sink - m)
            vb = vals[kv * HEAD_DIM:(kv + 1) * HEAD_DIM, lo:lo + 2 * WINDOW]
            o = jnp.dot(vb, p.astype(BF16), preferred_element_type=F32) / denom
            for pr in range(SWA_GROUP // 2):
                two = jnp.concatenate([o[:, (2 * pr) * WINDOW:(2 * pr + 1) * WINDOW],
                                       o[:, (2 * pr + 1) * WINDOW:(2 * pr + 2) * WINDOW]], axis=0)
                col = (kv * (SWA_GROUP // 2) + pr) * LANES
                o_ref[lo:lo + WINDOW, col:col + LANES] = two.T.astype(BF16)


def _swa_attention(sink, pt, pn, bias, b, s, nblk):
    ts = nblk * WINDOW
    per = s // ts
    perw = s // WINDOW
    qblk = SWAQ_ROW // BRANCH
    vblk = SWAV_ROW // LANES
    kblk = SWAK_OFF // LANES
    prev = lambda bi, i: bi * perw + jnp.maximum(i * nblk - 1, 0)
    return pl.pallas_call(
        functools.partial(_swa_kernel, nblk=nblk),
        grid=(b, per),
        in_specs=[
            pl.BlockSpec(memory_space=pltpu.SMEM),
            pl.BlockSpec((BRANCH, ts), lambda bi, i: (qblk, bi * per + i)),
            pl.BlockSpec((WINDOW, LANES), lambda bi, i: (prev(bi, i), kblk)),
            pl.BlockSpec((ts, LANES), lambda bi, i: (bi * per + i, kblk)),
            pl.BlockSpec((LANES, WINDOW), lambda bi, i: (vblk, prev(bi, i))),
            pl.BlockSpec((LANES, ts), lambda bi, i: (vblk, bi * per + i)),
            _const_spec(bias.shape),
        ],
        out_specs=pl.BlockSpec((ts, BRANCH), lambda bi, i: (bi * per + i, 0)),
        out_shape=jax.ShapeDtypeStruct((b * s, BRANCH), BF16),
        compiler_params=_params("arbitrary", "arbitrary"),
        name="swa_attention",
    )(sink, pt, pn, pn, pt, pt, bias)


def _merge_kernel(x_ref, gate_ref, cb_ref, cc_ref, cu_ref, cch_ref, cuh_ref, yf_ref, ys_ref,
                  cw_ref, wb_ref, wo_ref, o_ref, *, tiles_per_seq):
    z = cc_ref[...].astype(F32) * cu_ref[...].astype(F32)
    zh = cch_ref[...].astype(F32) * cuh_ref[...].astype(F32)
    zh = jnp.where(pl.program_id(0) % tiles_per_seq == 0, 0.0, zh)
    row = lax.broadcasted_iota(jnp.int32, z.shape, 0)
    z1 = jnp.where(row == 0, zh[7:8], pltpu.roll(z, 1, axis=0))
    z2 = jnp.where(row == 0, zh[6:7], jnp.where(row == 1, zh[7:8], pltpu.roll(z, 2, axis=0)))
    cw = cw_ref[...]
    y_conv = cb_ref[...].astype(F32) * (cw[0:1] * z2 + cw[1:2] * z1 + cw[2:3] * z)
    branches = (y_conv.astype(BF16), yf_ref[...], ys_ref[...])
    merged = None
    for b, y in enumerate(branches):
        gate = jax.nn.sigmoid(gate_ref[:, b * D_MODEL:(b + 1) * D_MODEL].astype(F32))
        term = gate * jnp.dot(y, wb_ref[b], preferred_element_type=F32)
        merged = term if merged is None else merged + term
    o_ref[...] = x_ref[...] + jnp.dot(merged.astype(BF16), wo_ref[...], preferred_element_type=F32)


def _merge(x, pn, y_fox, y_swa, conv_w, w_branch, w_out, seq, tm):
    n, d = x.shape
    cblk = CONV_OFF // BRANCH
    halo = lambda i: jnp.maximum(i * (tm // SUBLANES) - 1, 0)
    row = lambda c: pl.BlockSpec((tm, BRANCH), lambda i: (i, c))
    return pl.pallas_call(
        functools.partial(_merge_kernel, tiles_per_seq=seq // tm),
        grid=(n // tm,),
        in_specs=[
            pl.BlockSpec((tm, d), lambda i: (i, 0)),
            pl.BlockSpec((tm, 3 * D_MODEL), lambda i: (i, GATE_OFF // (3 * D_MODEL))),
            row(cblk), row(cblk + 1), row(cblk + 2),
            pl.BlockSpec((SUBLANES, BRANCH), lambda i: (halo(i), cblk + 1)),
            pl.BlockSpec((SUBLANES, BRANCH), lambda i: (halo(i), cblk + 2)),
            pl.BlockSpec((tm, BRANCH), lambda i: (i, 0)),
            pl.BlockSpec((tm, BRANCH), lambda i: (i, 0)),
            _const_spec(conv_w.shape), _const_spec(w_branch.shape), _const_spec(w_out.shape),
        ],
        out_specs=pl.BlockSpec((tm, d), lambda i: (i, 0)),
        out_shape=jax.ShapeDtypeStruct((n, d), F32),
        compiler_params=_params("arbitrary"),
        name="conv_merge_out",
    )(x, pn, pn, pn, pn, pn, pn, y_fox, y_swa, conv_w, w_branch, w_out)


def _xattn_kernel(x_ref, g_ref, wq_ref, kT_ref, v_ref, wo_ref, o_ref):
    x = x_ref[...]
    q = jnp.dot(_rms(x, g_ref[...]).astype(BF16), wq_ref[...], preferred_element_type=F32).astype(BF16)
    heads = []
    for h in range(X_HEADS):
        sl = slice(h * X_HEAD_DIM, (h + 1) * X_HEAD_DIM)
        s = jnp.dot(q[:, sl], kT_ref[0, sl, :], preferred_element_type=F32)
        p = jnp.exp2(s - jnp.max(s, axis=-1, keepdims=True))
        denom = jnp.sum(p, axis=-1, keepdims=True)
        o = jnp.dot(p.astype(BF16), v_ref[0, :, sl], preferred_element_type=F32)
        heads.append((o / denom).astype(BF16))
    o_ref[...] = x + jnp.dot(jnp.concatenate(heads, axis=1), wo_ref[...], preferred_element_type=F32)


def _xattn(x, g, wq, kT, v, wo, seq, tm):
    n, d = x.shape
    tiles = seq // tm
    return pl.pallas_call(
        _xattn_kernel,
        grid=(n // tm,),
        in_specs=[
            pl.BlockSpec((tm, d), lambda i: (i, 0)),
            _const_spec((1, d)),
            _const_spec(wq.shape),
            pl.BlockSpec((1,) + kT.shape[1:], lambda i: (i // tiles, 0, 0)),
            pl.BlockSpec((1,) + v.shape[1:], lambda i: (i // tiles, 0, 0)),
            _const_spec(wo.shape),
        ],
        out_specs=pl.BlockSpec((tm, d), lambda i: (i, 0)),
        out_shape=jax.ShapeDtypeStruct((n, d), F32),
        compiler_params=_params("arbitrary"),
        name="cross_attention",
    )(x, g.reshape(1, d), wq, kT, v, wo)


def _swiglu_kernel(x_ref, g_ref, wg_ref, wu_ref, wd_ref, gf_ref, o_ref, *, final_norm):
    x = x_ref[...]
    xn = _rms(x, g_ref[...]).astype(BF16)
    gate = jnp.dot(xn, wg_ref[...], preferred_element_type=F32)
    up = jnp.dot(xn, wu_ref[...], preferred_element_type=F32)
    hidden = (gate * jax.nn.sigmoid(gate) * up).astype(BF16)
    y = x + jnp.dot(hidden, wd_ref[...], preferred_element_type=F32)
    o_ref[...] = _rms(y, gf_ref[...]) if final_norm else y


def _swiglu(x, g, wg, wu, wd, g_final, final_norm, tm):
    n, d = x.shape
    return pl.pallas_call(
        functools.partial(_swiglu_kernel, final_norm=final_norm),
        grid=(n // tm,),
        in_specs=[
            pl.BlockSpec((tm, d), lambda i: (i, 0)),
            _const_spec((1, d)),
            _const_spec(wg.shape), _const_spec(wu.shape), _const_spec(wd.shape),
            _const_spec((1, d)),
        ],
        out_specs=pl.BlockSpec((tm, d), lambda i: (i, 0)),
        out_shape=jax.ShapeDtypeStruct((n, d), F32),
        compiler_params=_params("arbitrary"),
        name="swiglu",
    )(x, g.reshape(1, d), wg, wu, wd, g_final.reshape(1, d))


def _pack_in_proj(w_in):
    sizes = [BRANCH] * 3 + [BRANCH] * 3 + [FOX_HEADS] + [BRANCH, SWA_KV * HEAD_DIM, SWA_KV * HEAD_DIM] + [3 * D_MODEL]
    offs = np.concatenate([[0], np.cumsum(sizes)])
    c_b, c_c, c_u, f_q, f_k, f_v, f_g, s_q, s_k, s_v, gates = (w_in[:, offs[t]:offs[t + 1]] for t in range(len(sizes)))
    qscale = HEAD_DIM ** -0.5 * LOG2E
    wn = jnp.concatenate([gates, c_b, c_c, c_u, f_k, s_k], axis=1)
    wt = jnp.concatenate([f_q * qscale, f_v, s_q * qscale, s_v], axis=1).T
    wfg = jnp.pad(f_g, ((0, 0), (0, LANES - FOX_HEADS)))
    return wn.astype(BF16), wt.astype(BF16), wfg.astype(BF16)


def kernel(x, mem, mix_norm_g, w_in, forget_bias, conv_w, sink, w_branch, w_mix_out, rel_bias,
           xattn_norm_g, mem_norm_g, w_xq, w_xkv, w_xo, ffn_norm_g, w_ffn_gate, w_ffn_up, w_ffn_down,
           final_norm_g):
    b, s, d = x.shape
    n = b * s
    depth = w_in.shape[0]
    mem_len = mem.shape[1]
    xf = x.reshape(n, d)
    memf = mem.reshape(b * mem_len, d)
    swa_bias = _swa_bias_table(rel_bias)
    for l in range(depth):
        wn, wt, wfg = _pack_in_proj(w_in[l])
        pn, pt, fg = _in_proj(xf, mix_norm_g[l], wn, wt, wfg, tm=512)

        k_aug = _fox_prep(fg, forget_bias[l], pn, b, s, ts=1024)
        y_fox = _fox_attention(pt, k_aug, b, s, tq=1024)
        y_swa = _swa_attention(sink[l], pt, pn, swa_bias, b, s, nblk=4)
        xf = _merge(xf, pn, y_fox, y_swa, conv_w[l], w_branch[l].astype(BF16), w_mix_out[l].astype(BF16),
                    seq=s, tm=512)

        kv = _norm_proj(memf, mem_norm_g[l], w_xkv[l].astype(BF16))
        kT = kv[:, :d].reshape(b, mem_len, d).transpose(0, 2, 1)
        v = kv[:, d:].reshape(b, mem_len, d)
        wq = (w_xq[l] * (X_HEAD_DIM ** -0.5 * LOG2E)).astype(BF16)
        xf = _xattn(xf, xattn_norm_g[l], wq, kT, v, w_xo[l].astype(BF16), seq=s, tm=512)

        xf = _swiglu(xf, ffn_norm_g[l], w_ffn_gate[l].astype(BF16), w_ffn_up[l].astype(BF16),
                     w_ffn_down[l].astype(BF16), final_norm_g, final_norm=(l == depth - 1), tm=512)
    return xf.reshape(b, s, d)
```

```python
import functools
import math

import jax
import jax.numpy as jnp
import numpy as np
from jax import lax
from jax.experimental import pallas as pl
from jax.experimental.pallas import tpu as pltpu

F32 = jnp.float32
BF16 = jnp.bfloat16

D_MODEL = 1024
HEAD_DIM = 64
BRANCH = 512
FOX_HEADS = 8
SWA_HEADS = 8
SWA_KV = 2
SWA_GROUP = SWA_HEADS // SWA_KV
WINDOW = 128
N_BUCKETS = 32
X_HEADS = 4
X_HEAD_DIM = D_MODEL // X_HEADS
RMS_EPS = 1e-6
NEG = -1e30
LOG2E = math.log2(math.e)

LANES = 128
SUBLANES = 8
VMEM_LIMIT_BYTES = 56 * 1024 * 1024

GATE_OFF = 0
CONV_OFF = 3 * D_MODEL
FOXK_OFF = CONV_OFF + 3 * BRANCH
SWAK_OFF = FOXK_OFF + BRANCH
PN_COLS = SWAK_OFF + SWA_KV * HEAD_DIM
FOXQ_ROW = 0
FOXV_ROW = BRANCH
SWAQ_ROW = 2 * BRANCH
SWAV_ROW = 3 * BRANCH
PT_ROWS = SWAV_ROW + SWA_KV * HEAD_DIM

AUG_STRIDE = 8
AUG_EVEN = (HEAD_DIM, HEAD_DIM + AUG_STRIDE, HEAD_DIM + 2 * AUG_STRIDE)
AUG_ODD = (0, AUG_STRIDE, 2 * AUG_STRIDE)
V_ROWS = 80


def _params(*sem, flags=None):
    return pltpu.CompilerParams(dimension_semantics=sem, vmem_limit_bytes=VMEM_LIMIT_BYTES, flags=flags)


def _rms(x, g):
    return x * lax.rsqrt(jnp.mean(x * x, axis=-1, keepdims=True) + RMS_EPS) * g


def _const_spec(shape):
    nd = len(shape)
    return pl.BlockSpec(shape, lambda *_: (0,) * nd, pipeline_mode=pl.Buffered(1))


def _one_hot_rows(shape, axis, positions):
    idx = lax.broadcasted_iota(jnp.int32, shape, axis)
    hit = idx == positions[0]
    for p in positions[1:]:
        hit = hit | (idx == p)
    return jnp.where(hit, 1.0, 0.0)


def _in_proj_kernel(x_ref, g_ref, wn_ref, wt_ref, wfg_ref, pn_ref, pt_ref, fg_ref, *, chunks):
    h = _rms(x_ref[...], g_ref[...]).astype(BF16)
    fg_ref[...] = jnp.dot(h, wfg_ref[...], preferred_element_type=F32)
    pt_ref[...] = lax.dot_general(wt_ref[...], h, (((1,), (1,)), ((), ())),
                                  preferred_element_type=F32).astype(BF16)
    for lo, hi in chunks:
        pn_ref[:, lo:hi] = jnp.dot(h, wn_ref[:, lo:hi], preferred_element_type=F32).astype(BF16)


def _in_proj(x, g, wn, wt, wfg, tm):
    n, d = x.shape
    step = 10 * LANES
    chunks = tuple((lo, min(lo + step, PN_COLS)) for lo in range(0, PN_COLS, step))
    return pl.pallas_call(
        functools.partial(_in_proj_kernel, chunks=chunks),
        grid=(n // tm,),
        in_specs=[
            pl.BlockSpec((tm, d), lambda i: (i, 0)),
            _const_spec((1, d)), _const_spec(wn.shape), _const_spec(wt.shape), _const_spec(wfg.shape),
        ],
        out_specs=[
            pl.BlockSpec((tm, PN_COLS), lambda i: (i, 0)),
            pl.BlockSpec((PT_ROWS, tm), lambda i: (0, i)),
            pl.BlockSpec((tm, LANES), lambda i: (i, 0)),
        ],
        out_shape=[jax.ShapeDtypeStruct((n, PN_COLS), BF16), jax.ShapeDtypeStruct((PT_ROWS, n), BF16),
                   jax.ShapeDtypeStruct((n, LANES), F32)],
        compiler_params=_params("arbitrary"),
        name="in_proj",
    )(x, g.reshape(1, d), wn, wt, wfg)


def _norm_proj_kernel(x_ref, g_ref, w_ref, o_ref):
    h = _rms(x_ref[...], g_ref[...]).astype(BF16)
    o_ref[...] = jnp.dot(h, w_ref[...], preferred_element_type=F32).astype(BF16)


def _norm_proj(x, g, w):
    n, d = x.shape
    cols = w.shape[1]
    return pl.pallas_call(
        _norm_proj_kernel,
        grid=(1,),
        in_specs=[_const_spec((n, d)), _const_spec((1, d)), _const_spec(w.shape)],
        out_specs=pl.BlockSpec((n, cols), lambda i: (0, 0)),
        out_shape=jax.ShapeDtypeStruct((n, cols), BF16),
        compiler_params=_params("arbitrary"),
        name="mem_proj",
    )(x, g.reshape(1, d), w)


def _fox_prep_kernel(fg_ref, b_ref, k_ref, o_ref, carry_ref):
    ts = fg_ref.shape[0]

    @pl.when(pl.program_id(1) == 0)
    def _():
        carry_ref[...] = jnp.zeros(carry_ref.shape, F32)

    lane = lax.broadcasted_iota(jnp.int32, (ts, LANES), 1)
    row = lax.broadcasted_iota(jnp.int32, (ts, LANES), 0)
    c = jnp.where(lane < FOX_HEADS, jax.nn.log_sigmoid(fg_ref[...] + b_ref[...]) * LOG2E, 0.0)
    d = 1
    while d < ts:
        c = c + jnp.where(row >= d, pltpu.roll(c, d, axis=0), 0.0)
        d *= 2
    c = c + carry_ref[0:1, :]
    carry_ref[0:1, :] = c[ts - 1:ts, :]

    c1 = c.astype(BF16).astype(F32)
    r1 = c - c1
    c2 = r1.astype(BF16).astype(F32)
    c3 = (r1 - c2).astype(BF16).astype(F32)
    packed = -jnp.where(lane < AUG_STRIDE, c1,
                        jnp.where(lane < 2 * AUG_STRIDE, pltpu.roll(c2, AUG_STRIDE, axis=1),
                                  pltpu.roll(c3, 2 * AUG_STRIDE, axis=1)))
    keep_even = _one_hot_rows((1, LANES), 1, AUG_EVEN)
    keep_odd = _one_hot_rows((1, LANES), 1, AUG_ODD)
    for h in range(FOX_HEADS):
        k_pair = k_ref[:, (h // 2) * LANES:(h // 2 + 1) * LANES].astype(F32)
        if h % 2 == 0:
            aug = pltpu.roll(packed, (AUG_EVEN[0] - h) % LANES, axis=1) * keep_even
            out = jnp.where(lane < HEAD_DIM, k_pair, aug)
        else:
            aug = pltpu.roll(packed, (AUG_ODD[0] - h) % LANES, axis=1) * keep_odd
            out = jnp.where(lane >= HEAD_DIM, k_pair, aug)
        o_ref[0, h] = out.astype(BF16)


def _fox_prep(fg, bias, pn, b, s, ts):
    kblk = FOXK_OFF // BRANCH
    per = s // ts
    return pl.pallas_call(
        _fox_prep_kernel,
        grid=(b, per),
        in_specs=[
            pl.BlockSpec((ts, LANES), lambda bi, j: (bi * per + j, 0)),
            _const_spec((1, LANES)),
            pl.BlockSpec((ts, BRANCH), lambda bi, j: (bi * per + j, kblk)),
        ],
        out_specs=pl.BlockSpec((1, FOX_HEADS, ts, LANES), lambda bi, j: (bi, 0, j, 0)),
        out_shape=jax.ShapeDtypeStruct((b, FOX_HEADS, s, LANES), BF16),
        scratch_shapes=[pltpu.VMEM((SUBLANES, LANES), F32)],
        compiler_params=_params("arbitrary", "arbitrary"),
        name="fox_prep",
    )(fg, jnp.pad(bias, (0, LANES - FOX_HEADS)).reshape(1, LANES), pn)


def _fox_kernel(q_ref, k_ref, v_ref, o_ref, q_aug, v_aug, s_buf, mt_buf, p_buf, acc_ref, *, tq, tk, hp):
    i = pl.program_id(2)
    seq = v_ref.shape[1]

    @pl.when(i == 0)
    def _():
        ones_row = _one_hot_rows((V_ROWS - HEAD_DIM, seq), 0, (0,)).astype(BF16)
        for hh in range(hp):
            v_aug[hh, 0:HEAD_DIM, :] = v_ref[hh * HEAD_DIM:(hh + 1) * HEAD_DIM, :]
            v_aug[hh, HEAD_DIM:V_ROWS, :] = ones_row

    qrow = lax.broadcasted_iota(jnp.int32, (LANES, tq), 0)
    for pr in range(hp // 2):
        qb = q_ref[pr * LANES:(pr + 1) * LANES, :].astype(F32)
        q_aug[2 * pr] = jnp.where(qrow < HEAD_DIM, qb, _one_hot_rows((LANES, tq), 0, AUG_EVEN)).astype(BF16)
        q_aug[2 * pr + 1] = jnp.where(qrow >= HEAD_DIM, qb, _one_hot_rows((LANES, tq), 0, AUG_ODD)).astype(BF16)

    heads = tuple(range(hp))

    def scores(t, slot):
        off = pl.multiple_of(t * tk, tk)
        for hh in heads:
            s = jnp.dot(k_ref[0, hh, pl.ds(off, tk), :], q_aug[hh], preferred_element_type=F32)
            s_buf[hh, slot] = s
            mt_buf[hh, slot, 0:1, :] = jnp.max(s, axis=0, keepdims=True)

    def softmax(slot, ms, mask=None):
        out = []
        for hh in heads:
            s = s_buf[hh, slot]
            if mask is None:
                tile_max = mt_buf[hh, slot, 0:1, :]
            else:
                s = jnp.where(mask, s, NEG)
                tile_max = jnp.max(s, axis=0, keepdims=True)
            m_new = jnp.maximum(ms[hh], tile_max)
            p_buf[hh, slot] = jnp.exp2(s - m_new).astype(BF16)
            out.append((m_new, jnp.exp2(ms[hh] - m_new)))
        return tuple(o[0] for o in out), tuple(o[1] for o in out)

    def values(t, slot, alphas):
        off = pl.multiple_of(jnp.maximum(t, 0) * tk, tk)
        for hh in heads:
            pv = jnp.dot(v_aug[hh, :, pl.ds(off, tk)], p_buf[hh, slot], preferred_element_type=F32)
            acc_ref[hh] = alphas[hh] * acc_ref[hh] + pv

    @pl.when(i == 0)
    def _():
        for hh in heads:
            p_buf[hh, 1] = jnp.zeros(p_buf.shape[2:], BF16)
            acc_ref[hh] = jnp.zeros(acc_ref.shape[1:], F32)

    scores(0, 0)

    def pair(u, carry):
        ms, alpha_prev = carry
        t = 2 * u
        scores(t + 1, 1)
        ms, alpha0 = softmax(0, ms)
        values(t - 1, 1, alpha_prev)
        scores(t + 2, 0)
        ms, alpha1 = softmax(1, ms)
        values(t, 0, alpha0)
        return ms, alpha1

    carry = (tuple(jnp.full((1, tq), NEG, F32) for _ in heads), tuple(jnp.ones((1, tq), F32) for _ in heads))
    ms, alpha_prev = lax.fori_loop(0, i, pair, carry)

    t = 2 * i
    key = lax.broadcasted_iota(jnp.int32, (tk, tq), 0)
    qry = lax.broadcasted_iota(jnp.int32, (tk, tq), 1)
    off_b = pl.multiple_of((t + 1) * tk, tk)
    tri = lax.broadcasted_iota(jnp.int32, (tk, tk), 0) <= lax.broadcasted_iota(jnp.int32, (tk, tk), 1)
    s_b = [jnp.where(tri, jnp.dot(k_ref[0, hh, pl.ds(off_b, tk), :], q_aug[hh, :, tk:],
                                  preferred_element_type=F32), NEG) for hh in heads]
    ms, alpha0 = softmax(0, ms, key <= qry)
    values(t - 1, 1, alpha_prev)
    values(t, 0, alpha0)
    outs = []
    for hh in heads:
        mt_buf[hh, 0, 0:1, :] = ms[hh]
        m_old = mt_buf[hh, 0, 0:1, tk:]
        m_new = jnp.maximum(m_old, jnp.max(s_b[hh], axis=0, keepdims=True))
        p_b = jnp.exp2(s_b[hh] - m_new).astype(BF16)
        pv = jnp.dot(v_aug[hh, :, pl.ds(off_b, tk)], p_b, preferred_element_type=F32)
        acc_ref[hh, :, tk:] = jnp.exp2(m_old - m_new) * acc_ref[hh, :, tk:] + pv
        acc = acc_ref[hh]
        outs.append(acc[:HEAD_DIM] / acc[HEAD_DIM:HEAD_DIM + 1])
    o_ref[...] = jnp.concatenate(outs, axis=0).T.astype(BF16)


def _fox_attention(pt, k_aug, b, s, tq, hp):
    tk = tq // 2
    per = s // tq
    rows = hp * HEAD_DIM
    qblk = FOXQ_ROW // rows
    vblk = FOXV_ROW // rows
    return pl.pallas_call(
        functools.partial(_fox_kernel, tq=tq, tk=tk, hp=hp),
        grid=(b, FOX_HEADS // hp, per),
        in_specs=[
            pl.BlockSpec((rows, tq), lambda bi, g, i: (qblk + g, bi * per + i)),
            pl.BlockSpec((1, hp, s, LANES), lambda bi, g, i: (bi, g, 0, 0), pipeline_mode=pl.Buffered(1)),
            pl.BlockSpec((rows, s), lambda bi, g, i: (vblk + g, bi), pipeline_mode=pl.Buffered(1)),
        ],
        out_specs=pl.BlockSpec((tq, rows), lambda bi, g, i: (bi * per + i, g)),
        out_shape=jax.ShapeDtypeStruct((b * s, BRANCH), BF16),
        scratch_shapes=[
            pltpu.VMEM((hp, LANES, tq), BF16),
            pltpu.VMEM((hp, V_ROWS, s), BF16),
            pltpu.VMEM((hp, 2, tk, tq), F32),
            pltpu.VMEM((hp, 2, SUBLANES, tq), F32),
            pltpu.VMEM((hp, 2, tk, tq), BF16),
            pltpu.VMEM((hp, V_ROWS, tq), F32),
        ],
        compiler_params=_params("arbitrary", "arbitrary", "arbitrary"),
        name="fox_attention",
    )(pt, k_aug, pt)


def _swa_bias_kernel(rb_ref, bucket_ref, win_ref, o_ref):
    h = pl.program_id(0)
    bucket = bucket_ref[...]
    bias = jnp.zeros(bucket.shape, F32)
    for b in range(N_BUCKETS):
        bias = jnp.where(bucket == b, rb_ref[b, h] * LOG2E, bias)
    o_ref[0, 0] = jnp.where(win_ref[0] > 0, bias, NEG)
    o_ref[1, 0] = jnp.where(win_ref[1] > 0, bias, NEG)


def _swa_bias_table(rel_bias):
    tq = np.arange(WINDOW)[None, :]
    sk = np.arange(2 * WINDOW)[:, None]
    dist = WINDOW + tq - sk
    n = jnp.maximum(jnp.asarray(dist, jnp.int32), 0)
    max_exact = N_BUCKETS // 2
    large = max_exact + (jnp.log(jnp.maximum(n, 1).astype(F32) / max_exact)
                         / math.log(WINDOW / max_exact) * (N_BUCKETS - max_exact)).astype(jnp.int32)
    bucket = jnp.where(n < max_exact, n, jnp.minimum(large, N_BUCKETS - 1))
    in_window = (dist >= 0) & (dist < WINDOW)
    win = np.stack([in_window, in_window & (sk >= WINDOW)]).astype(np.int32)
    full = lambda shape: pl.BlockSpec(shape, lambda h: (0,) * len(shape))
    return pl.pallas_call(
        _swa_bias_kernel,
        grid=(SWA_HEADS,),
        in_specs=[pl.BlockSpec(memory_space=pltpu.SMEM), full(bucket.shape), full(win.shape)],
        out_specs=pl.BlockSpec((2, 1, 2 * WINDOW, WINDOW), lambda h: (0, h // SWA_GROUP, 0, h % SWA_GROUP)),
        out_shape=jax.ShapeDtypeStruct((2, SWA_KV, 2 * WINDOW, SWA_GROUP * WINDOW), F32),
        compiler_params=_params("arbitrary"),
        name="swa_bias_table",
    )(rel_bias, bucket, jnp.asarray(win))


def _swa_kernel(sink_ref, q_ref, kp_ref, kc_ref, vp_ref, vc_ref, bias_ref, o_ref, *, nblk):
    first = (pl.program_id(1) == 0).astype(jnp.int32)
    keys = jnp.concatenate([kp_ref[...], kc_ref[...]], axis=0)
    vals = jnp.concatenate([vp_ref[...], vc_ref[...]], axis=1)
    glane = lax.broadcasted_iota(jnp.int32, (1, SWA_GROUP * WINDOW), 1) // WINDOW
    zeros = jnp.zeros((HEAD_DIM, SWA_GROUP * WINDOW), BF16)
    for kv in range(SWA_KV):
        sink = jnp.zeros((1, SWA_GROUP * WINDOW), F32)
        for g in range(SWA_GROUP):
            sink = jnp.where(glane == g, sink_ref[kv * SWA_GROUP + g] * LOG2E, sink)
        for blk in range(nblk):
            lo = blk * WINDOW
            sel = first if blk == 0 else 0
            qg = jnp.concatenate([q_ref[(kv * SWA_GROUP + g) * HEAD_DIM:(kv * SWA_GROUP + g + 1) * HEAD_DIM,
                                        lo:lo + WINDOW] for g in range(SWA_GROUP)], axis=1)
            q_pad = jnp.concatenate([qg, zeros] if kv == 0 else [zeros, qg], axis=0)
            s = jnp.dot(keys[lo:lo + 2 * WINDOW, :], q_pad, preferred_element_type=F32) + bias_ref[sel, kv]
            m = jnp.maximum(jnp.max(s, axis=0, keepdims=True), sink)
            p = jnp.exp2(s - m)
            denom = jnp.sum(p, axis=0, keepdims=True) + jnp.exp2(sink - m)
            vb = vals[kv * HEAD_DIM:(kv + 1) * HEAD_DIM, lo:lo + 2 * WINDOW]
            o = jnp.dot(vb, p.astype(BF16), preferred_element_type=F32) / denom
            for pr in range(SWA_GROUP // 2):
                two = jnp.concatenate([o[:, (2 * pr) * WINDOW:(2 * pr + 1) * WINDOW],
                                       o[:, (2 * pr + 1) * WINDOW:(2 * pr + 2) * WINDOW]], axis=0)
                col = (kv * (SWA_GROUP // 2) + pr) * LANES
                o_ref[lo:lo + WINDOW, col:col + LANES] = two.T.astype(BF16)


def _swa_attention(sink, pt, pn, bias, b, s, nblk):
    ts = nblk * WINDOW
    per = s // ts
    perw = s // WINDOW
    qblk = SWAQ_ROW // BRANCH
    vblk = SWAV_ROW // LANES
    kblk = SWAK_OFF // LANES
    prev = lambda bi, i: bi * perw + jnp.maximum(i * nblk - 1, 0)
    return pl.pallas_call(
        functools.partial(_swa_kernel, nblk=nblk),
        grid=(b, per),
        in_specs=[
            pl.BlockSpec(memory_space=pltpu.SMEM),
            pl.BlockSpec((BRANCH, ts), lambda bi, i: (qblk, bi * per + i)),
            pl.BlockSpec((WINDOW, LANES), lambda bi, i: (prev(bi, i), kblk)),
            pl.BlockSpec((ts, LANES), lambda bi, i: (bi * per + i, kblk)),
            pl.BlockSpec((LANES, WINDOW), lambda bi, i: (vblk, prev(bi, i))),
            pl.BlockSpec((LANES, ts), lambda bi, i: (vblk, bi * per + i)),
            _const_spec(bias.shape),
        ],
        out_specs=pl.BlockSpec((ts, BRANCH), lambda bi, i: (bi * per + i, 0)),
        out_shape=jax.ShapeDtypeStruct((b * s, BRANCH), BF16),
        compiler_params=_params("arbitrary", "arbitrary"),
        name="swa_attention",
    )(sink, pt, pn, pn, pt, pt, bias)


def _merge_kernel(x_ref, gate_ref, cb_ref, cc_ref, cu_ref, cch_ref, cuh_ref, yf_ref, ys_ref,
                  cw_ref, wb_ref, wo_ref, o_ref, *, tiles_per_seq):
    z = cc_ref[...].astype(F32) * cu_ref[...].astype(F32)
    zh = cch_ref[...].astype(F32) * cuh_ref[...].astype(F32)
    zh = jnp.where(pl.program_id(0) % tiles_per_seq == 0, 0.0, zh)
    row = lax.broadcasted_iota(jnp.int32, z.shape, 0)
    z1 = jnp.where(row == 0, zh[7:8], pltpu.roll(z, 1, axis=0))
    z2 = jnp.where(row == 0, zh[6:7], jnp.where(row == 1, zh[7:8], pltpu.roll(z, 2, axis=0)))
    cw = cw_ref[...]
    y_conv = cb_ref[...].astype(F32) * (cw[0:1] * z2 + cw[1:2] * z1 + cw[2:3] * z)
    branches = (y_conv.astype(BF16), yf_ref[...], ys_ref[...])
    merged = None
    for b, y in enumerate(branches):
        gate = jax.nn.sigmoid(gate_ref[:, b * D_MODEL:(b + 1) * D_MODEL].astype(F32))
        term = gate * jnp.dot(y, wb_ref[b], preferred_element_type=F32)
        merged = term if merged is None else merged + term
    o_ref[...] = x_ref[...] + jnp.dot(merged.astype(BF16), wo_ref[...], preferred_element_type=F32)


def _merge(x, pn, y_fox, y_swa, conv_w, w_branch, w_out, seq, tm):
    n, d = x.shape
    cblk = CONV_OFF // BRANCH
    halo = lambda i: jnp.maximum(i * (tm // SUBLANES) - 1, 0)
    row = lambda c: pl.BlockSpec((tm, BRANCH), lambda i: (i, c))
    return pl.pallas_call(
        functools.partial(_merge_kernel, tiles_per_seq=seq // tm),
        grid=(n // tm,),
        in_specs=[
            pl.BlockSpec((tm, d), lambda i: (i, 0)),
            pl.BlockSpec((tm, 3 * D_MODEL), lambda i: (i, GATE_OFF // (3 * D_MODEL))),
            row(cblk), row(cblk + 1), row(cblk + 2),
            pl.BlockSpec((SUBLANES, BRANCH), lambda i: (halo(i), cblk + 1)),
            pl.BlockSpec((SUBLANES, BRANCH), lambda i: (halo(i), cblk + 2)),
            pl.BlockSpec((tm, BRANCH), lambda i: (i, 0)),
            pl.BlockSpec((tm, BRANCH), lambda i: (i, 0)),
            _const_spec(conv_w.shape), _const_spec(w_branch.shape), _const_spec(w_out.shape),
        ],
        out_specs=pl.BlockSpec((tm, d), lambda i: (i, 0)),
        out_shape=jax.ShapeDtypeStruct((n, d), F32),
        compiler_params=_params("arbitrary"),
        name="conv_merge_out",
    )(x, pn, pn, pn, pn, pn, pn, y_fox, y_swa, conv_w, w_branch, w_out)


def _xattn_kernel(x_ref, g_ref, wq_ref, kT_ref, v_ref, wo_ref, o_ref):
    x = x_ref[...]
    q = jnp.dot(_rms(x, g_ref[...]).astype(BF16), wq_ref[...], preferred_element_type=F32).astype(BF16)
    heads = []
    for h in range(X_HEADS):
        sl = slice(h * X_HEAD_DIM, (h + 1) * X_HEAD_DIM)
        s = jnp.dot(q[:, sl], kT_ref[0, sl, :], preferred_element_type=F32)
        p = jnp.exp2(s - jnp.max(s, axis=-1, keepdims=True))
        denom = jnp.sum(p, axis=-1, keepdims=True)
        o = jnp.dot(p.astype(BF16), v_ref[0, :, sl], preferred_element_type=F32)
        heads.append((o / denom).astype(BF16))
    o_ref[...] = x + jnp.dot(jnp.concatenate(heads, axis=1), wo_ref[...], preferred_element_type=F32)


def _xattn(x, g, wq, kT, v, wo, seq, tm):
    n, d = x.shape
    tiles = seq // tm
    return pl.pallas_call(
        _xattn_kernel,
        grid=(n // tm,),
        in_specs=[
            pl.BlockSpec((tm, d), lambda i: (i, 0)),
            _const_spec((1, d)),
            _const_spec(wq.shape),
            pl.BlockSpec((1,) + kT.shape[1:], lambda i: (i // tiles, 0, 0)),
            pl.BlockSpec((1,) + v.shape[1:], lambda i: (i // tiles, 0, 0)),
            _const_spec(wo.shape),
        ],
        out_specs=pl.BlockSpec((tm, d), lambda i: (i, 0)),
        out_shape=jax.ShapeDtypeStruct((n, d), F32),
        compiler_params=_params("arbitrary"),
        name="cross_attention",
    )(x, g.reshape(1, d), wq, kT, v, wo)


def _swiglu_kernel(x_ref, g_ref, wg_ref, wu_ref, wd_ref, gf_ref, o_ref, *, final_norm):
    x = x_ref[...]
    xn = _rms(x, g_ref[...]).astype(BF16)
    gate = jnp.dot(xn, wg_ref[...], preferred_element_type=F32)
    up = jnp.dot(xn, wu_ref[...], preferred_element_type=F32)
    hidden = (gate * jax.nn.sigmoid(gate) * up).astype(BF16)
    y = x + jnp.dot(hidden, wd_ref[...], preferred_element_type=F32)
    o_ref[...] = _rms(y, gf_ref[...]) if final_norm else y


def _swiglu(x, g, wg, wu, wd, g_final, final_norm, tm):
    n, d = x.shape
    return pl.pallas_call(
        functools.partial(_swiglu_kernel, final_norm=final_norm),
        grid=(n // tm,),
        in_specs=[
            pl.BlockSpec((tm, d), lambda i: (i, 0)),
            _const_spec((1, d)),
            _const_spec(wg.shape), _const_spec(wu.shape), _const_spec(wd.shape),
            _const_spec((1, d)),
        ],
        out_specs=pl.BlockSpec((tm, d), lambda i: (i, 0)),
        out_shape=jax.ShapeDtypeStruct((n, d), F32),
        compiler_params=_params("arbitrary"),
        name="swiglu",
    )(x, g.reshape(1, d), wg, wu, wd, g_final.reshape(1, d))


def _pack_in_proj(w_in):
    sizes = [BRANCH] * 3 + [BRANCH] * 3 + [FOX_HEADS] + [BRANCH, SWA_KV * HEAD_DIM, SWA_KV * HEAD_DIM] + [3 * D_MODEL]
    offs = np.concatenate([[0], np.cumsum(sizes)])
    c_b, c_c, c_u, f_q, f_k, f_v, f_g, s_q, s_k, s_v, gates = (w_in[:, offs[t]:offs[t + 1]] for t in range(len(sizes)))
    qscale = HEAD_DIM ** -0.5 * LOG2E
    wn = jnp.concatenate([gates, c_b, c_c, c_u, f_k, s_k], axis=1)
    wt = jnp.concatenate([f_q * qscale, f_v, s_q * qscale, s_v], axis=1).T
    wfg = jnp.pad(f_g, ((0, 0), (0, LANES - FOX_HEADS)))
    return wn.astype(BF16), wt.astype(BF16), wfg.astype(BF16)


def kernel(x, mem, mix_norm_g, w_in, forget_bias, conv_w, sink, w_branch, w_mix_out, rel_bias,
           xattn_norm_g, mem_norm_g, w_xq, w_xkv, w_xo, ffn_norm_g, w_ffn_gate, w_ffn_up, w_ffn_down,
           final_norm_g):
    b, s, d = x.shape
    n = b * s
    depth = w_in.shape[0]
    mem_len = mem.shape[1]
    xf = x.reshape(n, d)
    memf = mem.reshape(b * mem_len, d)
    swa_bias = _swa_bias_table(rel_bias)
    for l in range(depth):
        wn, wt, wfg = _pack_in_proj(w_in[l])
        pn, pt, fg = _in_proj(xf, mix_norm_g[l], wn, wt, wfg, tm=512)

        k_aug = _fox_prep(fg, forget_bias[l], pn, b, s, ts=1024)
        y_fox = _fox_attention(pt, k_aug, b, s, tq=1024, hp=4)
        y_swa = _swa_attention(sink[l], pt, pn, swa_bias, b, s, nblk=4)
        xf = _merge(xf, pn, y_fox, y_swa, conv_w[l], w_branch[l].astype(BF16), w_mix_out[l].astype(BF16),
                    seq=s, tm=512)

        kv = _norm_proj(memf, mem_norm_g[l], w_xkv[l].astype(BF16))
        kT = kv[:, :d].reshape(b, mem_len, d).transpose(0, 2, 1)
        v = kv[:, d:].reshape(b, mem_len, d)
        wq = (w_xq[l] * (X_HEAD_DIM ** -0.5 * LOG2E)).astype(BF16)
        xf = _xattn(xf, xattn_norm_g[l], wq, kT, v, w_xo[l].astype(BF16), seq=s, tm=512)

        xf = _swiglu(xf, ffn_norm_g[l], w_ffn_gate[l].astype(BF16), w_ffn_up[l].astype(BF16),
                     w_ffn_down[l].astype(BF16), final_norm_g, final_norm=(l == depth - 1), tm=512)
    return xf.reshape(b, s, d)
```

```python
import functools
import math

import jax
import jax.numpy as jnp
import numpy as np
from jax import lax
from jax.experimental import pallas as pl
from jax.experimental.pallas import tpu as pltpu

F32 = jnp.float32
BF16 = jnp.bfloat16

D_MODEL = 1024
HEAD_DIM = 64
BRANCH = 512
FOX_HEADS = 8
SWA_HEADS = 8
SWA_KV = 2
SWA_GROUP = SWA_HEADS // SWA_KV
WINDOW = 128
N_BUCKETS = 32
X_HEADS = 4
X_HEAD_DIM = D_MODEL // X_HEADS
RMS_EPS = 1e-6
NEG = -1e30
LOG2E = math.log2(math.e)

LANES = 128
SUBLANES = 8
VMEM_LIMIT_BYTES = 56 * 1024 * 1024

GATE_OFF = 0
CONV_OFF = 3 * D_MODEL
FOXK_OFF = CONV_OFF + 3 * BRANCH
SWAK_OFF = FOXK_OFF + BRANCH
PN_COLS = SWAK_OFF + SWA_KV * HEAD_DIM
FOXQ_ROW = 0
FOXV_ROW = BRANCH
SWAQ_ROW = 2 * BRANCH
SWAV_ROW = 3 * BRANCH
PT_ROWS = SWAV_ROW + SWA_KV * HEAD_DIM

AUG_STRIDE = 8
AUG_EVEN = (HEAD_DIM, HEAD_DIM + AUG_STRIDE, HEAD_DIM + 2 * AUG_STRIDE)
AUG_ODD = (0, AUG_STRIDE, 2 * AUG_STRIDE)
V_ROWS = 80


def _params(*sem, flags=None):
    return pltpu.CompilerParams(dimension_semantics=sem, vmem_limit_bytes=VMEM_LIMIT_BYTES, flags=flags)


def _rms(x, g):
    return x * lax.rsqrt(jnp.mean(x * x, axis=-1, keepdims=True) + RMS_EPS) * g


def _const_spec(shape):
    nd = len(shape)
    return pl.BlockSpec(shape, lambda *_: (0,) * nd, pipeline_mode=pl.Buffered(1))


def _one_hot_rows(shape, axis, positions):
    idx = lax.broadcasted_iota(jnp.int32, shape, axis)
    hit = idx == positions[0]
    for p in positions[1:]:
        hit = hit | (idx == p)
    return jnp.where(hit, 1.0, 0.0)


def _in_proj_kernel(x_ref, g_ref, wn_ref, wt_ref, wfg_ref, pn_ref, pt_ref, fg_ref, *, chunks):
    h = _rms(x_ref[...], g_ref[...]).astype(BF16)
    fg_ref[...] = jnp.dot(h, wfg_ref[...], preferred_element_type=F32)
    pt_ref[...] = lax.dot_general(wt_ref[...], h, (((1,), (1,)), ((), ())),
                                  preferred_element_type=F32).astype(BF16)
    for lo, hi in chunks:
        pn_ref[:, lo:hi] = jnp.dot(h, wn_ref[:, lo:hi], preferred_element_type=F32).astype(BF16)


def _in_proj(x, g, wn, wt, wfg, tm):
    n, d = x.shape
    step = 10 * LANES
    chunks = tuple((lo, min(lo + step, PN_COLS)) for lo in range(0, PN_COLS, step))
    return pl.pallas_call(
        functools.partial(_in_proj_kernel, chunks=chunks),
        grid=(n // tm,),
        in_specs=[
            pl.BlockSpec((tm, d), lambda i: (i, 0)),
            _const_spec((1, d)), _const_spec(wn.shape), _const_spec(wt.shape), _const_spec(wfg.shape),
        ],
        out_specs=[
            pl.BlockSpec((tm, PN_COLS), lambda i: (i, 0)),
            pl.BlockSpec((PT_ROWS, tm), lambda i: (0, i)),
            pl.BlockSpec((tm, LANES), lambda i: (i, 0)),
        ],
        out_shape=[jax.ShapeDtypeStruct((n, PN_COLS), BF16), jax.ShapeDtypeStruct((PT_ROWS, n), BF16),
                   jax.ShapeDtypeStruct((n, LANES), F32)],
        compiler_params=_params("arbitrary"),
        name="in_proj",
    )(x, g.reshape(1, d), wn, wt, wfg)


def _norm_proj_kernel(x_ref, g_ref, w_ref, o_ref):
    h = _rms(x_ref[...], g_ref[...]).astype(BF16)
    o_ref[...] = jnp.dot(h, w_ref[...], preferred_element_type=F32).astype(BF16)


def _norm_proj(x, g, w):
    n, d = x.shape
    cols = w.shape[1]
    return pl.pallas_call(
        _norm_proj_kernel,
        grid=(1,),
        in_specs=[_const_spec((n, d)), _const_spec((1, d)), _const_spec(w.shape)],
        out_specs=pl.BlockSpec((n, cols), lambda i: (0, 0)),
        out_shape=jax.ShapeDtypeStruct((n, cols), BF16),
        compiler_params=_params("arbitrary"),
        name="mem_proj",
    )(x, g.reshape(1, d), w)


def _fox_prep_kernel(fg_ref, b_ref, k_ref, o_ref, carry_ref):
    ts = fg_ref.shape[0]

    @pl.when(pl.program_id(1) == 0)
    def _():
        carry_ref[...] = jnp.zeros(carry_ref.shape, F32)

    lane = lax.broadcasted_iota(jnp.int32, (ts, LANES), 1)
    row = lax.broadcasted_iota(jnp.int32, (ts, LANES), 0)
    c = jnp.where(lane < FOX_HEADS, jax.nn.log_sigmoid(fg_ref[...] + b_ref[...]) * LOG2E, 0.0)
    d = 1
    while d < ts:
        c = c + jnp.where(row >= d, pltpu.roll(c, d, axis=0), 0.0)
        d *= 2
    c = c + carry_ref[0:1, :]
    carry_ref[0:1, :] = c[ts - 1:ts, :]

    c1 = c.astype(BF16).astype(F32)
    r1 = c - c1
    c2 = r1.astype(BF16).astype(F32)
    c3 = (r1 - c2).astype(BF16).astype(F32)
    packed = -jnp.where(lane < AUG_STRIDE, c1,
                        jnp.where(lane < 2 * AUG_STRIDE, pltpu.roll(c2, AUG_STRIDE, axis=1),
                                  pltpu.roll(c3, 2 * AUG_STRIDE, axis=1)))
    keep_even = _one_hot_rows((1, LANES), 1, AUG_EVEN)
    keep_odd = _one_hot_rows((1, LANES), 1, AUG_ODD)
    for h in range(FOX_HEADS):
        k_pair = k_ref[:, (h // 2) * LANES:(h // 2 + 1) * LANES].astype(F32)
        if h % 2 == 0:
            aug = pltpu.roll(packed, (AUG_EVEN[0] - h) % LANES, axis=1) * keep_even
            out = jnp.where(lane < HEAD_DIM, k_pair, aug)
        else:
            aug = pltpu.roll(packed, (AUG_ODD[0] - h) % LANES, axis=1) * keep_odd
            out = jnp.where(lane >= HEAD_DIM, k_pair, aug)
        o_ref[0, h] = out.astype(BF16)


def _fox_prep(fg, bias, pn, b, s, ts):
    kblk = FOXK_OFF // BRANCH
    per = s // ts
    return pl.pallas_call(
        _fox_prep_kernel,
        grid=(b, per),
        in_specs=[
            pl.BlockSpec((ts, LANES), lambda bi, j: (bi * per + j, 0)),
            _const_spec((1, LANES)),
            pl.BlockSpec((ts, BRANCH), lambda bi, j: (bi * per + j, kblk)),
        ],
        out_specs=pl.BlockSpec((1, FOX_HEADS, ts, LANES), lambda bi, j: (bi, 0, j, 0)),
        out_shape=jax.ShapeDtypeStruct((b, FOX_HEADS, s, LANES), BF16),
        scratch_shapes=[pltpu.VMEM((SUBLANES, LANES), F32)],
        compiler_params=_params("arbitrary", "arbitrary"),
        name="fox_prep",
    )(fg, jnp.pad(bias, (0, LANES - FOX_HEADS)).reshape(1, LANES), pn)


def _fox_kernel(q_ref, k_ref, v_ref, o_ref, q_aug, v_aug, s_buf, mt_buf, p_buf, acc_ref, *, tq, tk, hp):
    i = pl.program_id(2)
    seq = v_ref.shape[1]

    @pl.when(i == 0)
    def _():
        ones_row = _one_hot_rows((V_ROWS - HEAD_DIM, seq), 0, (0,)).astype(BF16)
        for hh in range(hp):
            v_aug[hh, 0:HEAD_DIM, :] = v_ref[hh * HEAD_DIM:(hh + 1) * HEAD_DIM, :]
            v_aug[hh, HEAD_DIM:V_ROWS, :] = ones_row

    qrow = lax.broadcasted_iota(jnp.int32, (LANES, tq), 0)
    for pr in range(hp // 2):
        qb = q_ref[pr * LANES:(pr + 1) * LANES, :].astype(F32)
        q_aug[2 * pr] = jnp.where(qrow < HEAD_DIM, qb, _one_hot_rows((LANES, tq), 0, AUG_EVEN)).astype(BF16)
        q_aug[2 * pr + 1] = jnp.where(qrow >= HEAD_DIM, qb, _one_hot_rows((LANES, tq), 0, AUG_ODD)).astype(BF16)

    heads = tuple(range(hp))

    def scores(t, slot):
        off = pl.multiple_of(t * tk, tk)
        for hh in heads:
            s = jnp.dot(k_ref[0, hh, pl.ds(off, tk), :], q_aug[hh], preferred_element_type=F32)
            s_buf[hh, slot] = s
            mt_buf[hh, slot, 0:1, :] = jnp.max(s, axis=0, keepdims=True)

    def softmax(slot, ms, mask=None):
        out = []
        for hh in heads:
            s = s_buf[hh, slot]
            if mask is None:
                tile_max = mt_buf[hh, slot, 0:1, :]
            else:
                s = jnp.where(mask, s, NEG)
                tile_max = jnp.max(s, axis=0, keepdims=True)
            m_new = jnp.maximum(ms[hh], tile_max)
            p_buf[hh, slot] = jnp.exp2(s - m_new).astype(BF16)
            out.append((m_new, jnp.exp2(ms[hh] - m_new)))
        return tuple(o[0] for o in out), tuple(o[1] for o in out)

    def values(t, slot, alphas):
        off = pl.multiple_of(jnp.maximum(t, 0) * tk, tk)
        for hh in heads:
            pv = jnp.dot(v_aug[hh, :, pl.ds(off, tk)], p_buf[hh, slot], preferred_element_type=F32)
            acc_ref[hh] = alphas[hh] * acc_ref[hh] + pv

    @pl.when(i == 0)
    def _():
        for hh in heads:
            p_buf[hh, 1] = jnp.zeros(p_buf.shape[2:], BF16)
            acc_ref[hh] = jnp.zeros(acc_ref.shape[1:], F32)

    scores(0, 0)

    def pair(u, carry):
        ms, alpha_prev = carry
        t = 2 * u
        scores(t + 1, 1)
        ms, alpha0 = softmax(0, ms)
        values(t - 1, 1, alpha_prev)
        scores(t + 2, 0)
        ms, alpha1 = softmax(1, ms)
        values(t, 0, alpha0)
        return ms, alpha1

    carry = (tuple(jnp.full((1, tq), NEG, F32) for _ in heads), tuple(jnp.ones((1, tq), F32) for _ in heads))
    ms, alpha_prev = lax.fori_loop(0, i, pair, carry)

    t = 2 * i
    key = lax.broadcasted_iota(jnp.int32, (tk, tq), 0)
    qry = lax.broadcasted_iota(jnp.int32, (tk, tq), 1)
    off_b = pl.multiple_of((t + 1) * tk, tk)
    tri = lax.broadcasted_iota(jnp.int32, (tk, tk), 0) <= lax.broadcasted_iota(jnp.int32, (tk, tk), 1)
    s_b = [jnp.where(tri, jnp.dot(k_ref[0, hh, pl.ds(off_b, tk), :], q_aug[hh, :, tk:],
                                  preferred_element_type=F32), NEG) for hh in heads]
    ms, alpha0 = softmax(0, ms, key <= qry)
    values(t - 1, 1, alpha_prev)
    values(t, 0, alpha0)
    outs = []
    for hh in heads:
        mt_buf[hh, 0, 0:1, :] = ms[hh]
        m_old = mt_buf[hh, 0, 0:1, tk:]
        m_new = jnp.maximum(m_old, jnp.max(s_b[hh], axis=0, keepdims=True))
        p_b = jnp.exp2(s_b[hh] - m_new).astype(BF16)
        pv = jnp.dot(v_aug[hh, :, pl.ds(off_b, tk)], p_b, preferred_element_type=F32)
        acc_ref[hh, :, tk:] = jnp.exp2(m_old - m_new) * acc_ref[hh, :, tk:] + pv
        acc = acc_ref[hh]
        outs.append(acc[:HEAD_DIM] / acc[HEAD_DIM:HEAD_DIM + 1])
    o_ref[...] = jnp.concatenate(outs, axis=0).T.astype(BF16)


def _fox_attention(pt, k_aug, b, s, tq, hp):
    tk = tq // 2
    per = s // tq
    rows = hp * HEAD_DIM
    qblk = FOXQ_ROW // rows
    vblk = FOXV_ROW // rows
    return pl.pallas_call(
        functools.partial(_fox_kernel, tq=tq, tk=tk, hp=hp),
        grid=(b, FOX_HEADS // hp, per),
        in_specs=[
            pl.BlockSpec((rows, tq), lambda bi, g, i: (qblk + g, bi * per + i)),
            pl.BlockSpec((1, hp, s, LANES), lambda bi, g, i: (bi, g, 0, 0), pipeline_mode=pl.Buffered(1)),
            pl.BlockSpec((rows, s), lambda bi, g, i: (vblk + g, bi), pipeline_mode=pl.Buffered(1)),
        ],
        out_specs=pl.BlockSpec((tq, rows), lambda bi, g, i: (bi * per + i, g)),
        out_shape=jax.ShapeDtypeStruct((b * s, BRANCH), BF16),
        scratch_shapes=[
            pltpu.VMEM((hp, LANES, tq), BF16),
            pltpu.VMEM((hp, V_ROWS, s), BF16),
            pltpu.VMEM((hp, 2, tk, tq), F32),
            pltpu.VMEM((hp, 2, SUBLANES, tq), F32),
            pltpu.VMEM((hp, 2, tk, tq), BF16),
            pltpu.VMEM((hp, V_ROWS, tq), F32),
        ],
        compiler_params=_params("arbitrary", "arbitrary", "arbitrary"),
        name="fox_attention",
    )(pt, k_aug, pt)


def _swa_bias_kernel(rb_ref, bucket_ref, win_ref, o_ref):
    h = pl.program_id(0)
    bucket = bucket_ref[...]
    bias = jnp.zeros(bucket.shape, F32)
    for b in range(N_BUCKETS):
        bias = jnp.where(bucket == b, rb_ref[b, h] * LOG2E, bias)
    o_ref[0, 0] = jnp.where(win_ref[0] > 0, bias, NEG)
    o_ref[1, 0] = jnp.where(win_ref[1] > 0, bias, NEG)


def _swa_bias_table(rel_bias):
    tq = np.arange(WINDOW)[None, :]
    sk = np.arange(2 * WINDOW)[:, None]
    dist = WINDOW + tq - sk
    n = jnp.maximum(jnp.asarray(dist, jnp.int32), 0)
    max_exact = N_BUCKETS // 2
    large = max_exact + (jnp.log(jnp.maximum(n, 1).astype(F32) / max_exact)
                         / math.log(WINDOW / max_exact) * (N_BUCKETS - max_exact)).astype(jnp.int32)
    bucket = jnp.where(n < max_exact, n, jnp.minimum(large, N_BUCKETS - 1))
    in_window = (dist >= 0) & (dist < WINDOW)
    win = np.stack([in_window, in_window & (sk >= WINDOW)]).astype(np.int32)
    full = lambda shape: pl.BlockSpec(shape, lambda h: (0,) * len(shape))
    return pl.pallas_call(
        _swa_bias_kernel,
        grid=(SWA_HEADS,),
        in_specs=[pl.BlockSpec(memory_space=pltpu.SMEM), full(bucket.shape), full(win.shape)],
        out_specs=pl.BlockSpec((2, 1, 2 * WINDOW, WINDOW), lambda h: (0, h // SWA_GROUP, 0, h % SWA_GROUP)),
        out_shape=jax.ShapeDtypeStruct((2, SWA_KV, 2 * WINDOW, SWA_GROUP * WINDOW), F32),
        compiler_params=_params("arbitrary"),
        name="swa_bias_table",
    )(rel_bias, bucket, jnp.asarray(win))


def _swa_kernel(sink_ref, q_ref, kp_ref, kc_ref, vp_ref, vc_ref, bias_ref, o_ref,
                k_buf, v_buf, s_buf, mt_buf, p_buf, dn_buf, *, nblk):
    gw = SWA_GROUP * WINDOW
    ts = nblk * WINDOW
    first = pl.program_id(1) == 0
    k_buf[0:WINDOW, :] = kp_ref[...]
    k_buf[WINDOW:WINDOW + ts, :] = kc_ref[...]
    v_buf[:, 0:WINDOW] = vp_ref[...]
    v_buf[:, WINDOW:WINDOW + ts] = vc_ref[...]
    glane = lax.broadcasted_iota(jnp.int32, (1, gw), 1) // WINDOW
    zeros = jnp.zeros((HEAD_DIM, gw), BF16)
    sinks = []
    for kv in range(SWA_KV):
        sink = jnp.zeros((1, gw), F32)
        for g in range(SWA_GROUP):
            sink = jnp.where(glane == g, sink_ref[kv * SWA_GROUP + g] * LOG2E, sink)
        sinks.append(sink)

    def scores(blk, slot):
        lo = pl.multiple_of(blk * WINDOW, WINDOW)
        sel = jnp.logical_and(first, blk == 0).astype(jnp.int32)
        keys = k_buf[pl.ds(lo, 2 * WINDOW), :]
        for kv in range(SWA_KV):
            qg = jnp.concatenate([q_ref[(kv * SWA_GROUP + g) * HEAD_DIM:(kv * SWA_GROUP + g + 1) * HEAD_DIM,
                                        pl.ds(lo, WINDOW)] for g in range(SWA_GROUP)], axis=1)
            q_pad = jnp.concatenate([qg, zeros] if kv == 0 else [zeros, qg], axis=0)
            s = jnp.dot(keys, q_pad, preferred_element_type=F32) + bias_ref[sel, kv]
            s_buf[kv, slot] = s
            mt_buf[kv, slot, 0:1, :] = jnp.max(s, axis=0, keepdims=True)

    def softmax(slot):
        for kv in range(SWA_KV):
            m = jnp.maximum(mt_buf[kv, slot, 0:1, :], sinks[kv])
            p = jnp.exp2(s_buf[kv, slot] - m)
            dn_buf[kv, slot, 0:1, :] = jnp.sum(p, axis=0, keepdims=True) + jnp.exp2(sinks[kv] - m)
            p_buf[kv, slot] = p.astype(BF16)

    def values(blk, slot):
        lo = pl.multiple_of(blk * WINDOW, WINDOW)
        for kv in range(SWA_KV):
            vb = v_buf[kv * HEAD_DIM:(kv + 1) * HEAD_DIM, pl.ds(lo, 2 * WINDOW)]
            o = jnp.dot(vb, p_buf[kv, slot], preferred_element_type=F32) / dn_buf[kv, slot, 0:1, :]
            for pr in range(SWA_GROUP // 2):
                two = jnp.concatenate([o[:, (2 * pr) * WINDOW:(2 * pr + 1) * WINDOW],
                                       o[:, (2 * pr + 1) * WINDOW:(2 * pr + 2) * WINDOW]], axis=0)
                col = (kv * (SWA_GROUP // 2) + pr) * LANES
                o_ref[pl.ds(lo, WINDOW), col:col + LANES] = two.T.astype(BF16)

    scores(jnp.int32(0), 0)
    scores(jnp.int32(1), 1)
    softmax(0)

    def pair(u, carry):
        blk = 2 * u + 1
        scores(blk + 1, 0)
        softmax(1)
        values(blk - 1, 0)
        scores(blk + 2, 1)
        softmax(0)
        values(blk, 1)
        return carry

    lax.fori_loop(0, nblk // 2 - 1, pair, 0)
    softmax(1)
    values(jnp.int32(nblk - 2), 0)
    values(jnp.int32(nblk - 1), 1)


def _swa_attention(sink, pt, pn, bias, b, s, nblk):
    ts = nblk * WINDOW
    per = s // ts
    perw = s // WINDOW
    qblk = SWAQ_ROW // BRANCH
    vblk = SWAV_ROW // LANES
    kblk = SWAK_OFF // LANES
    prev = lambda bi, i: bi * perw + jnp.maximum(i * nblk - 1, 0)
    return pl.pallas_call(
        functools.partial(_swa_kernel, nblk=nblk),
        grid=(b, per),
        in_specs=[
            pl.BlockSpec(memory_space=pltpu.SMEM),
            pl.BlockSpec((BRANCH, ts), lambda bi, i: (qblk, bi * per + i)),
            pl.BlockSpec((WINDOW, LANES), lambda bi, i: (prev(bi, i), kblk)),
            pl.BlockSpec((ts, LANES), lambda bi, i: (bi * per + i, kblk)),
            pl.BlockSpec((LANES, WINDOW), lambda bi, i: (vblk, prev(bi, i))),
            pl.BlockSpec((LANES, ts), lambda bi, i: (vblk, bi * per + i)),
            _const_spec(bias.shape),
        ],
        out_specs=pl.BlockSpec((ts, BRANCH), lambda bi, i: (bi * per + i, 0)),
        out_shape=jax.ShapeDtypeStruct((b * s, BRANCH), BF16),
        scratch_shapes=[
            pltpu.VMEM((ts + WINDOW, LANES), BF16),
            pltpu.VMEM((LANES, ts + WINDOW), BF16),
            pltpu.VMEM((SWA_KV, 2, 2 * WINDOW, SWA_GROUP * WINDOW), F32),
            pltpu.VMEM((SWA_KV, 2, SUBLANES, SWA_GROUP * WINDOW), F32),
            pltpu.VMEM((SWA_KV, 2, 2 * WINDOW, SWA_GROUP * WINDOW), BF16),
            pltpu.VMEM((SWA_KV, 2, SUBLANES, SWA_GROUP * WINDOW), F32),
        ],
        compiler_params=_params("arbitrary", "arbitrary"),
        name="swa_attention",
    )(sink, pt, pn, pn, pt, pt, bias)


def _merge_kernel(x_ref, gate_ref, cb_ref, cc_ref, cu_ref, cch_ref, cuh_ref, yf_ref, ys_ref,
                  cw_ref, wb_ref, wo_ref, o_ref, *, tiles_per_seq):
    z = cc_ref[...].astype(F32) * cu_ref[...].astype(F32)
    zh = cch_ref[...].astype(F32) * cuh_ref[...].astype(F32)
    zh = jnp.where(pl.program_id(0) % tiles_per_seq == 0, 0.0, zh)
    row = lax.broadcasted_iota(jnp.int32, z.shape, 0)
    z1 = jnp.where(row == 0, zh[7:8], pltpu.roll(z, 1, axis=0))
    z2 = jnp.where(row == 0, zh[6:7], jnp.where(row == 1, zh[7:8], pltpu.roll(z, 2, axis=0)))
    cw = cw_ref[...]
    y_conv = cb_ref[...].astype(F32) * (cw[0:1] * z2 + cw[1:2] * z1 + cw[2:3] * z)
    branches = (y_conv.astype(BF16), yf_ref[...], ys_ref[...])
    merged = None
    for b, y in enumerate(branches):
        gate = jax.nn.sigmoid(gate_ref[:, b * D_MODEL:(b + 1) * D_MODEL].astype(F32))
        term = gate * jnp.dot(y, wb_ref[b], preferred_element_type=F32)
        merged = term if merged is None else merged + term
    o_ref[...] = x_ref[...] + jnp.dot(merged.astype(BF16), wo_ref[...], preferred_element_type=F32)


def _merge(x, pn, y_fox, y_swa, conv_w, w_branch, w_out, seq, tm):
    n, d = x.shape
    cblk = CONV_OFF // BRANCH
    halo = lambda i: jnp.maximum(i * (tm // SUBLANES) - 1, 0)
    row = lambda c: pl.BlockSpec((tm, BRANCH), lambda i: (i, c))
    return pl.pallas_call(
        functools.partial(_merge_kernel, tiles_per_seq=seq // tm),
        grid=(n // tm,),
        in_specs=[
            pl.BlockSpec((tm, d), lambda i: (i, 0)),
            pl.BlockSpec((tm, 3 * D_MODEL), lambda i: (i, GATE_OFF // (3 * D_MODEL))),
            row(cblk), row(cblk + 1), row(cblk + 2),
            pl.BlockSpec((SUBLANES, BRANCH), lambda i: (halo(i), cblk + 1)),
            pl.BlockSpec((SUBLANES, BRANCH), lambda i: (halo(i), cblk + 2)),
            pl.BlockSpec((tm, BRANCH), lambda i: (i, 0)),
            pl.BlockSpec((tm, BRANCH), lambda i: (i, 0)),
            _const_spec(conv_w.shape), _const_spec(w_branch.shape), _const_spec(w_out.shape),
        ],
        out_specs=pl.BlockSpec((tm, d), lambda i: (i, 0)),
        out_shape=jax.ShapeDtypeStruct((n, d), F32),
        compiler_params=_params("arbitrary"),
        name="conv_merge_out",
    )(x, pn, pn, pn, pn, pn, pn, y_fox, y_swa, conv_w, w_branch, w_out)


def _xattn_kernel(x_ref, g_ref, wq_ref, kT_ref, v_ref, wo_ref, o_ref):
    x = x_ref[...]
    q = jnp.dot(_rms(x, g_ref[...]).astype(BF16), wq_ref[...], preferred_element_type=F32).astype(BF16)
    heads = []
    for h in range(X_HEADS):
        sl = slice(h * X_HEAD_DIM, (h + 1) * X_HEAD_DIM)
        s = jnp.dot(q[:, sl], kT_ref[0, sl, :], preferred_element_type=F32)
        p = jnp.exp2(s - jnp.max(s, axis=-1, keepdims=True))
        denom = jnp.sum(p, axis=-1, keepdims=True)
        o = jnp.dot(p.astype(BF16), v_ref[0, :, sl], preferred_element_type=F32)
        heads.append((o / denom).astype(BF16))
    o_ref[...] = x + jnp.dot(jnp.concatenate(heads, axis=1), wo_ref[...], preferred_element_type=F32)


def _xattn(x, g, wq, kT, v, wo, seq, tm):
    n, d = x.shape
    tiles = seq // tm
    return pl.pallas_call(
        _xattn_kernel,
        grid=(n // tm,),
        in_specs=[
            pl.BlockSpec((tm, d), lambda i: (i, 0)),
            _const_spec((1, d)),
            _const_spec(wq.shape),
            pl.BlockSpec((1,) + kT.shape[1:], lambda i: (i // tiles, 0, 0)),
            pl.BlockSpec((1,) + v.shape[1:], lambda i: (i // tiles, 0, 0)),
            _const_spec(wo.shape),
        ],
        out_specs=pl.BlockSpec((tm, d), lambda i: (i, 0)),
        out_shape=jax.ShapeDtypeStruct((n, d), F32),
        compiler_params=_params("arbitrary"),
        name="cross_attention",
    )(x, g.reshape(1, d), wq, kT, v, wo)


def _swiglu_kernel(x_ref, g_ref, wg_ref, wu_ref, wd_ref, gf_ref, o_ref, *, final_norm):
    x = x_ref[...]
    xn = _rms(x, g_ref[...]).astype(BF16)
    gate = jnp.dot(xn, wg_ref[...], preferred_element_type=F32)
    up = jnp.dot(xn, wu_ref[...], preferred_element_type=F32)
    hidden = (gate * jax.nn.sigmoid(gate) * up).astype(BF16)
    y = x + jnp.dot(hidden, wd_ref[...], preferred_element_type=F32)
    o_ref[...] = _rms(y, gf_ref[...]) if final_norm else y


def _swiglu(x, g, wg, wu, wd, g_final, final_norm, tm):
    n, d = x.shape
    return pl.pallas_call(
        functools.partial(_swiglu_kernel, final_norm=final_norm),
        grid=(n // tm,),
        in_specs=[
            pl.BlockSpec((tm, d), lambda i: (i, 0)),
            _const_spec((1, d)),
            _const_spec(wg.shape), _const_spec(wu.shape), _const_spec(wd.shape),
            _const_spec((1, d)),
        ],
        out_specs=pl.BlockSpec((tm, d), lambda i: (i, 0)),
        out_shape=jax.ShapeDtypeStruct((n, d), F32),
        compiler_params=_params("arbitrary"),
        name="swiglu",
    )(x, g.reshape(1, d), wg, wu, wd, g_final.reshape(1, d))


def _pack_in_proj(w_in):
    sizes = [BRANCH] * 3 + [BRANCH] * 3 + [FOX_HEADS] + [BRANCH, SWA_KV * HEAD_DIM, SWA_KV * HEAD_DIM] + [3 * D_MODEL]
    offs = np.concatenate([[0], np.cumsum(sizes)])
    c_b, c_c, c_u, f_q, f_k, f_v, f_g, s_q, s_k, s_v, gates = (w_in[:, offs[t]:offs[t + 1]] for t in range(len(sizes)))
    qscale = HEAD_DIM ** -0.5 * LOG2E
    wn = jnp.concatenate([gates, c_b, c_c, c_u, f_k, s_k], axis=1)
    wt = jnp.concatenate([f_q * qscale, f_v, s_q * qscale, s_v], axis=1).T
    wfg = jnp.pad(f_g, ((0, 0), (0, LANES - FOX_HEADS)))
    return wn.astype(BF16), wt.astype(BF16), wfg.astype(BF16)


def kernel(x, mem, mix_norm_g, w_in, forget_bias, conv_w, sink, w_branch, w_mix_out, rel_bias,
           xattn_norm_g, mem_norm_g, w_xq, w_xkv, w_xo, ffn_norm_g, w_ffn_gate, w_ffn_up, w_ffn_down,
           final_norm_g):
    b, s, d = x.shape
    n = b * s
    depth = w_in.shape[0]
    mem_len = mem.shape[1]
    xf = x.reshape(n, d)
    memf = mem.reshape(b * mem_len, d)
    swa_bias = _swa_bias_table(rel_bias)
    for l in range(depth):
        wn, wt, wfg = _pack_in_proj(w_in[l])
        pn, pt, fg = _in_proj(xf, mix_norm_g[l], wn, wt, wfg, tm=512)

        k_aug = _fox_prep(fg, forget_bias[l], pn, b, s, ts=1024)
        y_fox = _fox_attention(pt, k_aug, b, s, tq=1024, hp=4)
        y_swa = _swa_attention(sink[l], pt, pn, swa_bias, b, s, nblk=16)
        xf = _merge(xf, pn, y_fox, y_swa, conv_w[l], w_branch[l].astype(BF16), w_mix_out[l].astype(BF16),
                    seq=s, tm=512)

        kv = _norm_proj(memf, mem_norm_g[l], w_xkv[l].astype(BF16))
        kT = kv[:, :d].reshape(b, mem_len, d).transpose(0, 2, 1)
        v = kv[:, d:].reshape(b, mem_len, d)
        wq = (w_xq[l] * (X_HEAD_DIM ** -0.5 * LOG2E)).astype(BF16)
        xf = _xattn(xf, xattn_norm_g[l], wq, kT, v, w_xo[l].astype(BF16), seq=s, tm=512)

        xf = _swiglu(xf, ffn_norm_g[l], w_ffn_gate[l].astype(BF16), w_ffn_up[l].astype(BF16),
                     w_ffn_down[l].astype(BF16), final_norm_g, final_norm=(l == depth - 1), tm=512)
    return xf.reshape(b, s, d)
```

```python
import functools
import math

import jax
import jax.numpy as jnp
import numpy as np
from jax import lax
from jax.experimental import pallas as pl
from jax.experimental.pallas import tpu as pltpu

F32 = jnp.float32
BF16 = jnp.bfloat16

D_MODEL = 1024
HEAD_DIM = 64
BRANCH = 512
FOX_HEADS = 8
SWA_HEADS = 8
SWA_KV = 2
SWA_GROUP = SWA_HEADS // SWA_KV
WINDOW = 128
N_BUCKETS = 32
X_HEADS = 4
X_HEAD_DIM = D_MODEL // X_HEADS
RMS_EPS = 1e-6
NEG = -1e30
LOG2E = math.log2(math.e)

LANES = 128
SUBLANES = 8
VMEM_LIMIT_BYTES = 56 * 1024 * 1024
PITCH_PAD = LANES

GATE_OFF = 0
CONV_OFF = 3 * D_MODEL
FOXK_OFF = CONV_OFF + 3 * BRANCH
SWAK_OFF = FOXK_OFF + BRANCH
PN_COLS = SWAK_OFF + SWA_KV * HEAD_DIM
FOXQ_ROW = 0
FOXV_ROW = BRANCH
SWAQ_ROW = 2 * BRANCH
SWAV_ROW = 3 * BRANCH
PT_ROWS = SWAV_ROW + SWA_KV * HEAD_DIM

AUG_STRIDE = 8
AUG_EVEN = (HEAD_DIM, HEAD_DIM + AUG_STRIDE, HEAD_DIM + 2 * AUG_STRIDE)
AUG_ODD = (0, AUG_STRIDE, 2 * AUG_STRIDE)
V_ROWS = 80


def _params(*sem, flags=None):
    return pltpu.CompilerParams(dimension_semantics=sem, vmem_limit_bytes=VMEM_LIMIT_BYTES, flags=flags)


def _rms(x, g):
    return x * lax.rsqrt(jnp.mean(x * x, axis=-1, keepdims=True) + RMS_EPS) * g


def _const_spec(shape):
    nd = len(shape)
    return pl.BlockSpec(shape, lambda *_: (0,) * nd, pipeline_mode=pl.Buffered(1))


def _one_hot_rows(shape, axis, positions):
    idx = lax.broadcasted_iota(jnp.int32, shape, axis)
    hit = idx == positions[0]
    for p in positions[1:]:
        hit = hit | (idx == p)
    return jnp.where(hit, 1.0, 0.0)


def _in_proj_kernel(x_ref, g_ref, wn_ref, wt_ref, wfg_ref, pn_ref, pt_ref, fg_ref, *, chunks):
    h = _rms(x_ref[...], g_ref[...]).astype(BF16)
    fg_ref[...] = jnp.dot(h, wfg_ref[...], preferred_element_type=F32)
    pt_ref[...] = lax.dot_general(wt_ref[...], h, (((1,), (1,)), ((), ())),
                                  preferred_element_type=F32).astype(BF16)
    for lo, hi in chunks:
        pn_ref[:, lo:hi] = jnp.dot(h, wn_ref[:, lo:hi], preferred_element_type=F32).astype(BF16)


def _in_proj(x, g, wn, wt, wfg, tm):
    n, d = x.shape
    step = 10 * LANES
    chunks = tuple((lo, min(lo + step, PN_COLS)) for lo in range(0, PN_COLS, step))
    return pl.pallas_call(
        functools.partial(_in_proj_kernel, chunks=chunks),
        grid=(n // tm,),
        in_specs=[
            pl.BlockSpec((tm, d), lambda i: (i, 0)),
            _const_spec((1, d)), _const_spec(wn.shape), _const_spec(wt.shape), _const_spec(wfg.shape),
        ],
        out_specs=[
            pl.BlockSpec((tm, PN_COLS), lambda i: (i, 0)),
            pl.BlockSpec((PT_ROWS, tm), lambda i: (0, i)),
            pl.BlockSpec((tm, LANES), lambda i: (i, 0)),
        ],
        out_shape=[jax.ShapeDtypeStruct((n, PN_COLS), BF16), jax.ShapeDtypeStruct((PT_ROWS, n), BF16),
                   jax.ShapeDtypeStruct((n, LANES), F32)],
        compiler_params=_params("arbitrary"),
        name="in_proj",
    )(x, g.reshape(1, d), wn, wt, wfg)


def _norm_proj_kernel(x_ref, g_ref, w_ref, o_ref):
    h = _rms(x_ref[...], g_ref[...]).astype(BF16)
    o_ref[...] = jnp.dot(h, w_ref[...], preferred_element_type=F32).astype(BF16)


def _norm_proj(x, g, w):
    n, d = x.shape
    cols = w.shape[1]
    return pl.pallas_call(
        _norm_proj_kernel,
        grid=(1,),
        in_specs=[_const_spec((n, d)), _const_spec((1, d)), _const_spec(w.shape)],
        out_specs=pl.BlockSpec((n, cols), lambda i: (0, 0)),
        out_shape=jax.ShapeDtypeStruct((n, cols), BF16),
        compiler_params=_params("arbitrary"),
        name="mem_proj",
    )(x, g.reshape(1, d), w)


def _fox_prep_kernel(fg_ref, b_ref, k_ref, o_ref, carry_ref):
    ts = fg_ref.shape[0]

    @pl.when(pl.program_id(1) == 0)
    def _():
        carry_ref[...] = jnp.zeros(carry_ref.shape, F32)

    lane = lax.broadcasted_iota(jnp.int32, (ts, LANES), 1)
    row = lax.broadcasted_iota(jnp.int32, (ts, LANES), 0)
    c = jnp.where(lane < FOX_HEADS, jax.nn.log_sigmoid(fg_ref[...] + b_ref[...]) * LOG2E, 0.0)
    d = 1
    while d < ts:
        c = c + jnp.where(row >= d, pltpu.roll(c, d, axis=0), 0.0)
        d *= 2
    c = c + carry_ref[0:1, :]
    carry_ref[0:1, :] = c[ts - 1:ts, :]

    c1 = c.astype(BF16).astype(F32)
    r1 = c - c1
    c2 = r1.astype(BF16).astype(F32)
    c3 = (r1 - c2).astype(BF16).astype(F32)
    packed = -jnp.where(lane < AUG_STRIDE, c1,
                        jnp.where(lane < 2 * AUG_STRIDE, pltpu.roll(c2, AUG_STRIDE, axis=1),
                                  pltpu.roll(c3, 2 * AUG_STRIDE, axis=1)))
    keep_even = _one_hot_rows((1, LANES), 1, AUG_EVEN)
    keep_odd = _one_hot_rows((1, LANES), 1, AUG_ODD)
    for h in range(FOX_HEADS):
        k_pair = k_ref[:, (h // 2) * LANES:(h // 2 + 1) * LANES].astype(F32)
        if h % 2 == 0:
            aug = pltpu.roll(packed, (AUG_EVEN[0] - h) % LANES, axis=1) * keep_even
            out = jnp.where(lane < HEAD_DIM, k_pair, aug)
        else:
            aug = pltpu.roll(packed, (AUG_ODD[0] - h) % LANES, axis=1) * keep_odd
            out = jnp.where(lane >= HEAD_DIM, k_pair, aug)
        o_ref[0, h] = out.astype(BF16)


def _fox_prep(fg, bias, pn, b, s, ts):
    kblk = FOXK_OFF // BRANCH
    per = s // ts
    return pl.pallas_call(
        _fox_prep_kernel,
        grid=(b, per),
        in_specs=[
            pl.BlockSpec((ts, LANES), lambda bi, j: (bi * per + j, 0)),
            _const_spec((1, LANES)),
            pl.BlockSpec((ts, BRANCH), lambda bi, j: (bi * per + j, kblk)),
        ],
        out_specs=pl.BlockSpec((1, FOX_HEADS, ts, LANES), lambda bi, j: (bi, 0, j, 0)),
        out_shape=jax.ShapeDtypeStruct((b, FOX_HEADS, s, LANES), BF16),
        scratch_shapes=[pltpu.VMEM((SUBLANES, LANES), F32)],
        compiler_params=_params("arbitrary", "arbitrary"),
        name="fox_prep",
    )(fg, jnp.pad(bias, (0, LANES - FOX_HEADS)).reshape(1, LANES), pn)


def _fox_kernel(q_ref, k_ref, v_ref, o_ref, q_aug, v_aug, s_buf, mt_buf, p_buf, acc_ref, *, tq, tk, hp):
    i = pl.program_id(2)
    seq = v_ref.shape[1]

    @pl.when(i == 0)
    def _():
        ones_row = _one_hot_rows((V_ROWS - HEAD_DIM, seq), 0, (0,)).astype(BF16)
        for hh in range(hp):
            v_aug[hh, 0:HEAD_DIM, :] = v_ref[hh * HEAD_DIM:(hh + 1) * HEAD_DIM, :]
            v_aug[hh, HEAD_DIM:V_ROWS, :] = ones_row

    qrow = lax.broadcasted_iota(jnp.int32, (LANES, tq), 0)
    for pr in range(hp // 2):
        qb = q_ref[pr * LANES:(pr + 1) * LANES, :].astype(F32)
        q_aug[2 * pr] = jnp.where(qrow < HEAD_DIM, qb, _one_hot_rows((LANES, tq), 0, AUG_EVEN)).astype(BF16)
        q_aug[2 * pr + 1] = jnp.where(qrow >= HEAD_DIM, qb, _one_hot_rows((LANES, tq), 0, AUG_ODD)).astype(BF16)

    heads = tuple(range(hp))

    def scores(t, slot):
        off = pl.multiple_of(t * tk, tk)
        for hh in heads:
            s = jnp.dot(k_ref[0, hh, pl.ds(off, tk), :], q_aug[hh], preferred_element_type=F32)
            s_buf[hh, slot, :, 0:tq] = s
            mt_buf[hh, slot, 0:1, :] = jnp.max(s, axis=0, keepdims=True)

    def softmax(slot, ms, mask=None):
        out = []
        for hh in heads:
            s = s_buf[hh, slot, :, 0:tq]
            if mask is None:
                tile_max = mt_buf[hh, slot, 0:1, :]
            else:
                s = jnp.where(mask, s, NEG)
                tile_max = jnp.max(s, axis=0, keepdims=True)
            m_new = jnp.maximum(ms[hh], tile_max)
            p_buf[hh, slot, :, 0:tq] = jnp.exp2(s - m_new).astype(BF16)
            out.append((m_new, jnp.exp2(ms[hh] - m_new)))
        return tuple(o[0] for o in out), tuple(o[1] for o in out)

    def values(t, slot, alphas):
        off = pl.multiple_of(jnp.maximum(t, 0) * tk, tk)
        for hh in heads:
            pv = jnp.dot(v_aug[hh, :, pl.ds(off, tk)], p_buf[hh, slot, :, 0:tq], preferred_element_type=F32)
            acc_ref[hh] = alphas[hh] * acc_ref[hh] + pv

    @pl.when(i == 0)
    def _():
        for hh in heads:
            p_buf[hh, 1, :, 0:tq] = jnp.zeros((tk, tq), BF16)
            acc_ref[hh] = jnp.zeros(acc_ref.shape[1:], F32)

    scores(0, 0)

    def pair(u, carry):
        ms, alpha_prev = carry
        t = 2 * u
        scores(t + 1, 1)
        ms, alpha0 = softmax(0, ms)
        values(t - 1, 1, alpha_prev)
        scores(t + 2, 0)
        ms, alpha1 = softmax(1, ms)
        values(t, 0, alpha0)
        return ms, alpha1

    carry = (tuple(jnp.full((1, tq), NEG, F32) for _ in heads), tuple(jnp.ones((1, tq), F32) for _ in heads))
    ms, alpha_prev = lax.fori_loop(0, i, pair, carry)

    t = 2 * i
    key = lax.broadcasted_iota(jnp.int32, (tk, tq), 0)
    qry = lax.broadcasted_iota(jnp.int32, (tk, tq), 1)
    off_b = pl.multiple_of((t + 1) * tk, tk)
    tri = lax.broadcasted_iota(jnp.int32, (tk, tk), 0) <= lax.broadcasted_iota(jnp.int32, (tk, tk), 1)
    s_b = [jnp.where(tri, jnp.dot(k_ref[0, hh, pl.ds(off_b, tk), :], q_aug[hh, :, tk:],
                                  preferred_element_type=F32), NEG) for hh in heads]
    ms, alpha0 = softmax(0, ms, key <= qry)
    values(t - 1, 1, alpha_prev)
    values(t, 0, alpha0)
    outs = []
    for hh in heads:
        mt_buf[hh, 0, 0:1, :] = ms[hh]
        m_old = mt_buf[hh, 0, 0:1, tk:]
        m_new = jnp.maximum(m_old, jnp.max(s_b[hh], axis=0, keepdims=True))
        p_b = jnp.exp2(s_b[hh] - m_new).astype(BF16)
        pv = jnp.dot(v_aug[hh, :, pl.ds(off_b, tk)], p_b, preferred_element_type=F32)
        acc_ref[hh, :, tk:] = jnp.exp2(m_old - m_new) * acc_ref[hh, :, tk:] + pv
        acc = acc_ref[hh]
        outs.append(acc[:HEAD_DIM] / acc[HEAD_DIM:HEAD_DIM + 1])
    o_ref[...] = jnp.concatenate(outs, axis=0).T.astype(BF16)


def _fox_attention(pt, k_aug, b, s, tq, hp):
    tk = tq // 2
    per = s // tq
    rows = hp * HEAD_DIM
    qblk = FOXQ_ROW // rows
    vblk = FOXV_ROW // rows
    return pl.pallas_call(
        functools.partial(_fox_kernel, tq=tq, tk=tk, hp=hp),
        grid=(b, FOX_HEADS // hp, per),
        in_specs=[
            pl.BlockSpec((rows, tq), lambda bi, g, i: (qblk + g, bi * per + i)),
            pl.BlockSpec((1, hp, s, LANES), lambda bi, g, i: (bi, g, 0, 0), pipeline_mode=pl.Buffered(1)),
            pl.BlockSpec((rows, s), lambda bi, g, i: (vblk + g, bi), pipeline_mode=pl.Buffered(1)),
        ],
        out_specs=pl.BlockSpec((tq, rows), lambda bi, g, i: (bi * per + i, g)),
        out_shape=jax.ShapeDtypeStruct((b * s, BRANCH), BF16),
        scratch_shapes=[
            pltpu.VMEM((hp, LANES, tq), BF16),
            pltpu.VMEM((hp, V_ROWS, s), BF16),
            pltpu.VMEM((hp, 2, tk, tq + PITCH_PAD), F32),
            pltpu.VMEM((hp, 2, SUBLANES, tq), F32),
            pltpu.VMEM((hp, 2, tk, tq + PITCH_PAD), BF16),
            pltpu.VMEM((hp, V_ROWS, tq), F32),
        ],
        compiler_params=_params("arbitrary", "arbitrary", "arbitrary"),
        name="fox_attention",
    )(pt, k_aug, pt)


def _swa_bias_kernel(rb_ref, bucket_ref, win_ref, o_ref):
    h = pl.program_id(0)
    bucket = bucket_ref[...]
    bias = jnp.zeros(bucket.shape, F32)
    for b in range(N_BUCKETS):
        bias = jnp.where(bucket == b, rb_ref[b, h] * LOG2E, bias)
    o_ref[0, 0] = jnp.where(win_ref[0] > 0, bias, NEG)
    o_ref[1, 0] = jnp.where(win_ref[1] > 0, bias, NEG)


def _swa_bias_table(rel_bias):
    tq = np.arange(WINDOW)[None, :]
    sk = np.arange(2 * WINDOW)[:, None]
    dist = WINDOW + tq - sk
    n = jnp.maximum(jnp.asarray(dist, jnp.int32), 0)
    max_exact = N_BUCKETS // 2
    large = max_exact + (jnp.log(jnp.maximum(n, 1).astype(F32) / max_exact)
                         / math.log(WINDOW / max_exact) * (N_BUCKETS - max_exact)).astype(jnp.int32)
    bucket = jnp.where(n < max_exact, n, jnp.minimum(large, N_BUCKETS - 1))
    in_window = (dist >= 0) & (dist < WINDOW)
    win = np.stack([in_window, in_window & (sk >= WINDOW)]).astype(np.int32)
    full = lambda shape: pl.BlockSpec(shape, lambda h: (0,) * len(shape))
    return pl.pallas_call(
        _swa_bias_kernel,
        grid=(SWA_HEADS,),
        in_specs=[pl.BlockSpec(memory_space=pltpu.SMEM), full(bucket.shape), full(win.shape)],
        out_specs=pl.BlockSpec((2, 1, 2 * WINDOW, WINDOW), lambda h: (0, h // SWA_GROUP, 0, h % SWA_GROUP)),
        out_shape=jax.ShapeDtypeStruct((2, SWA_KV, 2 * WINDOW, SWA_GROUP * WINDOW), F32),
        compiler_params=_params("arbitrary"),
        name="swa_bias_table",
    )(rel_bias, bucket, jnp.asarray(win))


def _swa_kernel(sink_ref, q_ref, kp_ref, kc_ref, vp_ref, vc_ref, bias_ref, o_ref,
                k_buf, v_buf, s_buf, mt_buf, p_buf, dn_buf, *, nblk):
    gw = SWA_GROUP * WINDOW
    ts = nblk * WINDOW
    first = pl.program_id(1) == 0
    k_buf[0:WINDOW, :] = kp_ref[...]
    k_buf[WINDOW:WINDOW + ts, :] = kc_ref[...]
    v_buf[:, 0:WINDOW] = vp_ref[...]
    v_buf[:, WINDOW:WINDOW + ts] = vc_ref[...]
    glane = lax.broadcasted_iota(jnp.int32, (1, gw), 1) // WINDOW
    zeros = jnp.zeros((HEAD_DIM, gw), BF16)
    sinks = []
    for kv in range(SWA_KV):
        sink = jnp.zeros((1, gw), F32)
        for g in range(SWA_GROUP):
            sink = jnp.where(glane == g, sink_ref[kv * SWA_GROUP + g] * LOG2E, sink)
        sinks.append(sink)

    def scores(blk, slot):
        lo = pl.multiple_of(blk * WINDOW, WINDOW)
        sel = jnp.logical_and(first, blk == 0).astype(jnp.int32)
        keys = k_buf[pl.ds(lo, 2 * WINDOW), :]
        for kv in range(SWA_KV):
            qg = jnp.concatenate([q_ref[(kv * SWA_GROUP + g) * HEAD_DIM:(kv * SWA_GROUP + g + 1) * HEAD_DIM,
                                        pl.ds(lo, WINDOW)] for g in range(SWA_GROUP)], axis=1)
            q_pad = jnp.concatenate([qg, zeros] if kv == 0 else [zeros, qg], axis=0)
            s = jnp.dot(keys, q_pad, preferred_element_type=F32) + bias_ref[sel, kv]
            s_buf[kv, slot, :, 0:gw] = s
            mt_buf[kv, slot, 0:1, :] = jnp.max(s, axis=0, keepdims=True)

    def softmax(slot):
        for kv in range(SWA_KV):
            m = jnp.maximum(mt_buf[kv, slot, 0:1, :], sinks[kv])
            p = jnp.exp2(s_buf[kv, slot, :, 0:gw] - m)
            dn_buf[kv, slot, 0:1, :] = jnp.sum(p, axis=0, keepdims=True) + jnp.exp2(sinks[kv] - m)
            p_buf[kv, slot, :, 0:gw] = p.astype(BF16)

    def values(blk, slot):
        lo = pl.multiple_of(blk * WINDOW, WINDOW)
        for kv in range(SWA_KV):
            vb = v_buf[kv * HEAD_DIM:(kv + 1) * HEAD_DIM, pl.ds(lo, 2 * WINDOW)]
            o = jnp.dot(vb, p_buf[kv, slot, :, 0:gw], preferred_element_type=F32) / dn_buf[kv, slot, 0:1, :]
            for pr in range(SWA_GROUP // 2):
                two = jnp.concatenate([o[:, (2 * pr) * WINDOW:(2 * pr + 1) * WINDOW],
                                       o[:, (2 * pr + 1) * WINDOW:(2 * pr + 2) * WINDOW]], axis=0)
                col = (kv * (SWA_GROUP // 2) + pr) * LANES
                o_ref[pl.ds(lo, WINDOW), col:col + LANES] = two.T.astype(BF16)

    scores(jnp.int32(0), 0)
    scores(jnp.int32(1), 1)
    softmax(0)

    def pair(u, carry):
        blk = 2 * u + 1
        scores(blk + 1, 0)
        softmax(1)
        values(blk - 1, 0)
        scores(blk + 2, 1)
        softmax(0)
        values(blk, 1)
        return carry

    lax.fori_loop(0, nblk // 2 - 1, pair, 0)
    softmax(1)
    values(jnp.int32(nblk - 2), 0)
    values(jnp.int32(nblk - 1), 1)


def _swa_attention(sink, pt, pn, bias, b, s, nblk):
    ts = nblk * WINDOW
    per = s // ts
    perw = s // WINDOW
    qblk = SWAQ_ROW // BRANCH
    vblk = SWAV_ROW // LANES
    kblk = SWAK_OFF // LANES
    prev = lambda bi, i: bi * perw + jnp.maximum(i * nblk - 1, 0)
    return pl.pallas_call(
        functools.partial(_swa_kernel, nblk=nblk),
        grid=(b, per),
        in_specs=[
            pl.BlockSpec(memory_space=pltpu.SMEM),
            pl.BlockSpec((BRANCH, ts), lambda bi, i: (qblk, bi * per + i)),
            pl.BlockSpec((WINDOW, LANES), lambda bi, i: (prev(bi, i), kblk)),
            pl.BlockSpec((ts, LANES), lambda bi, i: (bi * per + i, kblk)),
            pl.BlockSpec((LANES, WINDOW), lambda bi, i: (vblk, prev(bi, i))),
            pl.BlockSpec((LANES, ts), lambda bi, i: (vblk, bi * per + i)),
            _const_spec(bias.shape),
        ],
        out_specs=pl.BlockSpec((ts, BRANCH), lambda bi, i: (bi * per + i, 0)),
        out_shape=jax.ShapeDtypeStruct((b * s, BRANCH), BF16),
        scratch_shapes=[
            pltpu.VMEM((ts + WINDOW, LANES), BF16),
            pltpu.VMEM((LANES, ts + WINDOW), BF16),
            pltpu.VMEM((SWA_KV, 2, 2 * WINDOW, SWA_GROUP * WINDOW + PITCH_PAD), F32),
            pltpu.VMEM((SWA_KV, 2, SUBLANES, SWA_GROUP * WINDOW), F32),
            pltpu.VMEM((SWA_KV, 2, 2 * WINDOW, SWA_GROUP * WINDOW + PITCH_PAD), BF16),
            pltpu.VMEM((SWA_KV, 2, SUBLANES, SWA_GROUP * WINDOW), F32),
        ],
        compiler_params=_params("arbitrary", "arbitrary"),
        name="swa_attention",
    )(sink, pt, pn, pn, pt, pt, bias)


def _merge_kernel(x_ref, gate_ref, cb_ref, cc_ref, cu_ref, cch_ref, cuh_ref, yf_ref, ys_ref,
                  cw_ref, wb_ref, wo_ref, o_ref, *, tiles_per_seq):
    z = cc_ref[...].astype(F32) * cu_ref[...].astype(F32)
    zh = cch_ref[...].astype(F32) * cuh_ref[...].astype(F32)
    zh = jnp.where(pl.program_id(0) % tiles_per_seq == 0, 0.0, zh)
    row = lax.broadcasted_iota(jnp.int32, z.shape, 0)
    z1 = jnp.where(row == 0, zh[7:8], pltpu.roll(z, 1, axis=0))
    z2 = jnp.where(row == 0, zh[6:7], jnp.where(row == 1, zh[7:8], pltpu.roll(z, 2, axis=0)))
    cw = cw_ref[...]
    y_conv = cb_ref[...].astype(F32) * (cw[0:1] * z2 + cw[1:2] * z1 + cw[2:3] * z)
    branches = (y_conv.astype(BF16), yf_ref[...], ys_ref[...])
    merged = None
    for b, y in enumerate(branches):
        gate = jax.nn.sigmoid(gate_ref[:, b * D_MODEL:(b + 1) * D_MODEL].astype(F32))
        term = gate * jnp.dot(y, wb_ref[b], preferred_element_type=F32)
        merged = term if merged is None else merged + term
    o_ref[...] = x_ref[...] + jnp.dot(merged.astype(BF16), wo_ref[...], preferred_element_type=F32)


def _merge(x, pn, y_fox, y_swa, conv_w, w_branch, w_out, seq, tm):
    n, d = x.shape
    cblk = CONV_OFF // BRANCH
    halo = lambda i: jnp.maximum(i * (tm // SUBLANES) - 1, 0)
    row = lambda c: pl.BlockSpec((tm, BRANCH), lambda i: (i, c))
    return pl.pallas_call(
        functools.partial(_merge_kernel, tiles_per_seq=seq // tm),
        grid=(n // tm,),
        in_specs=[
            pl.BlockSpec((tm, d), lambda i: (i, 0)),
            pl.BlockSpec((tm, 3 * D_MODEL), lambda i: (i, GATE_OFF // (3 * D_MODEL))),
            row(cblk), row(cblk + 1), row(cblk + 2),
            pl.BlockSpec((SUBLANES, BRANCH), lambda i: (halo(i), cblk + 1)),
            pl.BlockSpec((SUBLANES, BRANCH), lambda i: (halo(i), cblk + 2)),
            pl.BlockSpec((tm, BRANCH), lambda i: (i, 0)),
            pl.BlockSpec((tm, BRANCH), lambda i: (i, 0)),
            _const_spec(conv_w.shape), _const_spec(w_branch.shape), _const_spec(w_out.shape),
        ],
        out_specs=pl.BlockSpec((tm, d), lambda i: (i, 0)),
        out_shape=jax.ShapeDtypeStruct((n, d), F32),
        compiler_params=_params("arbitrary"),
        name="conv_merge_out",
    )(x, pn, pn, pn, pn, pn, pn, y_fox, y_swa, conv_w, w_branch, w_out)


def _xattn_kernel(x_ref, g_ref, wq_ref, kT_ref, v_ref, wo_ref, o_ref):
    x = x_ref[...]
    q = jnp.dot(_rms(x, g_ref[...]).astype(BF16), wq_ref[...], preferred_element_type=F32).astype(BF16)
    heads = []
    for h in range(X_HEADS):
        sl = slice(h * X_HEAD_DIM, (h + 1) * X_HEAD_DIM)
        s = jnp.dot(q[:, sl], kT_ref[0, sl, :], preferred_element_type=F32)
        p = jnp.exp2(s - jnp.max(s, axis=-1, keepdims=True))
        denom = jnp.sum(p, axis=-1, keepdims=True)
        o = jnp.dot(p.astype(BF16), v_ref[0, :, sl], preferred_element_type=F32)
        heads.append((o / denom).astype(BF16))
    o_ref[...] = x + jnp.dot(jnp.concatenate(heads, axis=1), wo_ref[...], preferred_element_type=F32)


def _xattn(x, g, wq, kT, v, wo, seq, tm):
    n, d = x.shape
    tiles = seq // tm
    return pl.pallas_call(
        _xattn_kernel,
        grid=(n // tm,),
        in_specs=[
            pl.BlockSpec((tm, d), lambda i: (i, 0)),
            _const_spec((1, d)),
            _const_spec(wq.shape),
            pl.BlockSpec((1,) + kT.shape[1:], lambda i: (i // tiles, 0, 0)),
            pl.BlockSpec((1,) + v.shape[1:], lambda i: (i // tiles, 0, 0)),
            _const_spec(wo.shape),
        ],
        out_specs=pl.BlockSpec((tm, d), lambda i: (i, 0)),
        out_shape=jax.ShapeDtypeStruct((n, d), F32),
        compiler_params=_params("arbitrary"),
        name="cross_attention",
    )(x, g.reshape(1, d), wq, kT, v, wo)


def _swiglu_kernel(x_ref, g_ref, wg_ref, wu_ref, wd_ref, gf_ref, o_ref, *, final_norm):
    x = x_ref[...]
    xn = _rms(x, g_ref[...]).astype(BF16)
    gate = jnp.dot(xn, wg_ref[...], preferred_element_type=F32)
    up = jnp.dot(xn, wu_ref[...], preferred_element_type=F32)
    hidden = (gate * jax.nn.sigmoid(gate) * up).astype(BF16)
    y = x + jnp.dot(hidden, wd_ref[...], preferred_element_type=F32)
    o_ref[...] = _rms(y, gf_ref[...]) if final_norm else y


def _swiglu(x, g, wg, wu, wd, g_final, final_norm, tm):
    n, d = x.shape
    return pl.pallas_call(
        functools.partial(_swiglu_kernel, final_norm=final_norm),
        grid=(n // tm,),
        in_specs=[
            pl.BlockSpec((tm, d), lambda i: (i, 0)),
            _const_spec((1, d)),
            _const_spec(wg.shape), _const_spec(wu.shape), _const_spec(wd.shape),
            _const_spec((1, d)),
        ],
        out_specs=pl.BlockSpec((tm, d), lambda i: (i, 0)),
        out_shape=jax.ShapeDtypeStruct((n, d), F32),
        compiler_params=_params("arbitrary"),
        name="swiglu",
    )(x, g.reshape(1, d), wg, wu, wd, g_final.reshape(1, d))


def _pack_in_proj(w_in):
    sizes = [BRANCH] * 3 + [BRANCH] * 3 + [FOX_HEADS] + [BRANCH, SWA_KV * HEAD_DIM, SWA_KV * HEAD_DIM] + [3 * D_MODEL]
    offs = np.concatenate([[0], np.cumsum(sizes)])
    c_b, c_c, c_u, f_q, f_k, f_v, f_g, s_q, s_k, s_v, gates = (w_in[:, offs[t]:offs[t + 1]] for t in range(len(sizes)))
    qscale = HEAD_DIM ** -0.5 * LOG2E
    wn = jnp.concatenate([gates, c_b, c_c, c_u, f_k, s_k], axis=1)
    wt = jnp.concatenate([f_q * qscale, f_v, s_q * qscale, s_v], axis=1).T
    wfg = jnp.pad(f_g, ((0, 0), (0, LANES - FOX_HEADS)))
    return wn.astype(BF16), wt.astype(BF16), wfg.astype(BF16)


def kernel(x, mem, mix_norm_g, w_in, forget_bias, conv_w, sink, w_branch, w_mix_out, rel_bias,
           xattn_norm_g, mem_norm_g, w_xq, w_xkv, w_xo, ffn_norm_g, w_ffn_gate, w_ffn_up, w_ffn_down,
           final_norm_g):
    b, s, d = x.shape
    n = b * s
    depth = w_in.shape[0]
    mem_len = mem.shape[1]
    xf = x.reshape(n, d)
    memf = mem.reshape(b * mem_len, d)
    swa_bias = _swa_bias_table(rel_bias)
    for l in range(depth):
        wn, wt, wfg = _pack_in_proj(w_in[l])
        pn, pt, fg = _in_proj(xf, mix_norm_g[l], wn, wt, wfg, tm=512)

        k_aug = _fox_prep(fg, forget_bias[l], pn, b, s, ts=1024)
        y_fox = _fox_attention(pt, k_aug, b, s, tq=1024, hp=4)
        y_swa = _swa_attention(sink[l], pt, pn, swa_bias, b, s, nblk=16)
        xf = _merge(xf, pn, y_fox, y_swa, conv_w[l], w_branch[l].astype(BF16), w_mix_out[l].astype(BF16),
                    seq=s, tm=512)

        kv = _norm_proj(memf, mem_norm_g[l], w_xkv[l].astype(BF16))
        kT = kv[:, :d].reshape(b, mem_len, d).transpose(0, 2, 1)
        v = kv[:, d:].reshape(b, mem_len, d)
        wq = (w_xq[l] * (X_HEAD_DIM ** -0.5 * LOG2E)).astype(BF16)
        xf = _xattn(xf, xattn_norm_g[l], wq, kT, v, w_xo[l].astype(BF16), seq=s, tm=512)

        xf = _swiglu(xf, ffn_norm_g[l], w_ffn_gate[l].astype(BF16), w_ffn_up[l].astype(BF16),
                     w_ffn_down[l].astype(BF16), final_norm_g, final_norm=(l == depth - 1), tm=512)
    return xf.reshape(b, s, d)
```

```python
import functools
import math

import jax
import jax.numpy as jnp
import numpy as np
from jax import lax
from jax.experimental import pallas as pl
from jax.experimental.pallas import tpu as pltpu

F32 = jnp.float32
BF16 = jnp.bfloat16

D_MODEL = 1024
HEAD_DIM = 64
BRANCH = 512
FOX_HEADS = 8
SWA_HEADS = 8
SWA_KV = 2
SWA_GROUP = SWA_HEADS // SWA_KV
WINDOW = 128
N_BUCKETS = 32
X_HEADS = 4
X_HEAD_DIM = D_MODEL // X_HEADS
RMS_EPS = 1e-6
NEG = -1e30
LOG2E = math.log2(math.e)

LANES = 128
SUBLANES = 8
VMEM_LIMIT_BYTES = 56 * 1024 * 1024
PITCH_PAD = LANES
MERGE_CHUNK = 256

GATE_OFF = 0
CONV_OFF = 3 * D_MODEL
FOXK_OFF = CONV_OFF + BRANCH
SWAK_OFF = FOXK_OFF + BRANCH
PN_COLS = SWAK_OFF + SWA_KV * HEAD_DIM
FOXQ_ROW = 0
FOXV_ROW = BRANCH
SWAQ_ROW = 2 * BRANCH
SWAV_ROW = 3 * BRANCH
PT_ROWS = SWAV_ROW + SWA_KV * HEAD_DIM

AUG_STRIDE = 8
AUG_EVEN = (HEAD_DIM, HEAD_DIM + AUG_STRIDE, HEAD_DIM + 2 * AUG_STRIDE)
AUG_ODD = (0, AUG_STRIDE, 2 * AUG_STRIDE)
V_ROWS = 80


def _params(*sem, flags=None):
    return pltpu.CompilerParams(dimension_semantics=sem, vmem_limit_bytes=VMEM_LIMIT_BYTES, flags=flags)


def _rms(x, g):
    return x * lax.rsqrt(jnp.mean(x * x, axis=-1, keepdims=True) + RMS_EPS) * g


def _const_spec(shape):
    nd = len(shape)
    return pl.BlockSpec(shape, lambda *_: (0,) * nd, pipeline_mode=pl.Buffered(1))


def _one_hot_rows(shape, axis, positions):
    idx = lax.broadcasted_iota(jnp.int32, shape, axis)
    hit = idx == positions[0]
    for p in positions[1:]:
        hit = hit | (idx == p)
    return jnp.where(hit, 1.0, 0.0)


def _in_proj_kernel(x_ref, g_ref, wn_ref, wc_ref, wt_ref, wfg_ref, cw_ref, pn_ref, pt_ref, fg_ref, z_tail,
                    *, tiles_per_seq):
    tm = x_ref.shape[0]
    h = _rms(x_ref[...], g_ref[...]).astype(BF16)
    fg_ref[...] = jnp.dot(h, wfg_ref[...], preferred_element_type=F32)
    pt_ref[...] = lax.dot_general(wt_ref[...], h, (((1,), (1,)), ((), ())),
                                  preferred_element_type=F32).astype(BF16)
    for lo in range(0, CONV_OFF, D_MODEL):
        pn_ref[:, lo:lo + D_MODEL] = jnp.dot(h, wn_ref[:, lo:lo + D_MODEL], preferred_element_type=F32).astype(BF16)
    pn_ref[:, FOXK_OFF:PN_COLS] = jnp.dot(h, wn_ref[:, CONV_OFF:], preferred_element_type=F32).astype(BF16)

    conv_b, conv_c, conv_u = (jnp.dot(h, wc_ref[:, t * BRANCH:(t + 1) * BRANCH], preferred_element_type=F32)
                              for t in range(3))
    z = conv_c * conv_u
    zh = jnp.where(pl.program_id(0) % tiles_per_seq == 0, 0.0, z_tail[...])
    z_tail[...] = z[tm - SUBLANES:tm, :]
    row = lax.broadcasted_iota(jnp.int32, z.shape, 0)
    z1 = jnp.where(row == 0, zh[7:8], pltpu.roll(z, 1, axis=0))
    z2 = jnp.where(row == 0, zh[6:7], jnp.where(row == 1, zh[7:8], pltpu.roll(z, 2, axis=0)))
    cw = cw_ref[...]
    pn_ref[:, CONV_OFF:FOXK_OFF] = (conv_b * (cw[0:1] * z2 + cw[1:2] * z1 + cw[2:3] * z)).astype(BF16)


def _in_proj(x, g, wn, wc, wt, wfg, conv_w, seq, tm):
    n, d = x.shape
    return pl.pallas_call(
        functools.partial(_in_proj_kernel, tiles_per_seq=seq // tm),
        grid=(n // tm,),
        in_specs=[
            pl.BlockSpec((tm, d), lambda i: (i, 0)),
            _const_spec((1, d)), _const_spec(wn.shape), _const_spec(wc.shape), _const_spec(wt.shape),
            _const_spec(wfg.shape), _const_spec(conv_w.shape),
        ],
        out_specs=[
            pl.BlockSpec((tm, PN_COLS), lambda i: (i, 0)),
            pl.BlockSpec((PT_ROWS, tm), lambda i: (0, i)),
            pl.BlockSpec((tm, LANES), lambda i: (i, 0)),
        ],
        out_shape=[jax.ShapeDtypeStruct((n, PN_COLS), BF16), jax.ShapeDtypeStruct((PT_ROWS, n), BF16),
                   jax.ShapeDtypeStruct((n, LANES), F32)],
        scratch_shapes=[pltpu.VMEM((SUBLANES, BRANCH), F32)],
        compiler_params=_params("arbitrary"),
        name="in_proj",
    )(x, g.reshape(1, d), wn, wc, wt, wfg, conv_w)


def _norm_proj_kernel(x_ref, g_ref, w_ref, o_ref):
    h = _rms(x_ref[...], g_ref[...]).astype(BF16)
    o_ref[...] = jnp.dot(h, w_ref[...], preferred_element_type=F32).astype(BF16)


def _norm_proj(x, g, w):
    n, d = x.shape
    cols = w.shape[1]
    return pl.pallas_call(
        _norm_proj_kernel,
        grid=(1,),
        in_specs=[_const_spec((n, d)), _const_spec((1, d)), _const_spec(w.shape)],
        out_specs=pl.BlockSpec((n, cols), lambda i: (0, 0)),
        out_shape=jax.ShapeDtypeStruct((n, cols), BF16),
        compiler_params=_params("arbitrary"),
        name="mem_proj",
    )(x, g.reshape(1, d), w)


def _fox_prep_kernel(fg_ref, b_ref, k_ref, o_ref, carry_ref):
    ts = fg_ref.shape[0]

    @pl.when(pl.program_id(1) == 0)
    def _():
        carry_ref[...] = jnp.zeros(carry_ref.shape, F32)

    lane = lax.broadcasted_iota(jnp.int32, (ts, LANES), 1)
    row = lax.broadcasted_iota(jnp.int32, (ts, LANES), 0)
    c = jnp.where(lane < FOX_HEADS, jax.nn.log_sigmoid(fg_ref[...] + b_ref[...]) * LOG2E, 0.0)
    d = 1
    while d < ts:
        c = c + jnp.where(row >= d, pltpu.roll(c, d, axis=0), 0.0)
        d *= 2
    c = c + carry_ref[0:1, :]
    carry_ref[0:1, :] = c[ts - 1:ts, :]

    c1 = c.astype(BF16).astype(F32)
    r1 = c - c1
    c2 = r1.astype(BF16).astype(F32)
    c3 = (r1 - c2).astype(BF16).astype(F32)
    packed = -jnp.where(lane < AUG_STRIDE, c1,
                        jnp.where(lane < 2 * AUG_STRIDE, pltpu.roll(c2, AUG_STRIDE, axis=1),
                                  pltpu.roll(c3, 2 * AUG_STRIDE, axis=1)))
    keep_even = _one_hot_rows((1, LANES), 1, AUG_EVEN)
    keep_odd = _one_hot_rows((1, LANES), 1, AUG_ODD)
    for h in range(FOX_HEADS):
        k_pair = k_ref[:, (h // 2) * LANES:(h // 2 + 1) * LANES].astype(F32)
        if h % 2 == 0:
            aug = pltpu.roll(packed, (AUG_EVEN[0] - h) % LANES, axis=1) * keep_even
            out = jnp.where(lane < HEAD_DIM, k_pair, aug)
        else:
            aug = pltpu.roll(packed, (AUG_ODD[0] - h) % LANES, axis=1) * keep_odd
            out = jnp.where(lane >= HEAD_DIM, k_pair, aug)
        o_ref[0, h] = out.astype(BF16)


def _fox_prep(fg, bias, pn, b, s, ts):
    kblk = FOXK_OFF // BRANCH
    per = s // ts
    return pl.pallas_call(
        _fox_prep_kernel,
        grid=(b, per),
        in_specs=[
            pl.BlockSpec((ts, LANES), lambda bi, j: (bi * per + j, 0)),
            _const_spec((1, LANES)),
            pl.BlockSpec((ts, BRANCH), lambda bi, j: (bi * per + j, kblk)),
        ],
        out_specs=pl.BlockSpec((1, FOX_HEADS, ts, LANES), lambda bi, j: (bi, 0, j, 0)),
        out_shape=jax.ShapeDtypeStruct((b, FOX_HEADS, s, LANES), BF16),
        scratch_shapes=[pltpu.VMEM((SUBLANES, LANES), F32)],
        compiler_params=_params("arbitrary", "arbitrary"),
        name="fox_prep",
    )(fg, jnp.pad(bias, (0, LANES - FOX_HEADS)).reshape(1, LANES), pn)


def _fox_kernel(q_ref, k_ref, v_ref, o_ref, q_aug, v_aug, s_buf, mt_buf, p_buf, acc_ref, *, tq, tk, hp):
    i = pl.program_id(2)
    seq = v_ref.shape[1]

    @pl.when(i == 0)
    def _():
        ones_row = _one_hot_rows((V_ROWS - HEAD_DIM, seq), 0, (0,)).astype(BF16)
        for hh in range(hp):
            v_aug[hh, 0:HEAD_DIM, :] = v_ref[hh * HEAD_DIM:(hh + 1) * HEAD_DIM, :]
            v_aug[hh, HEAD_DIM:V_ROWS, :] = ones_row

    qrow = lax.broadcasted_iota(jnp.int32, (LANES, tq), 0)
    for pr in range(hp // 2):
        qb = q_ref[pr * LANES:(pr + 1) * LANES, :].astype(F32)
        q_aug[2 * pr] = jnp.where(qrow < HEAD_DIM, qb, _one_hot_rows((LANES, tq), 0, AUG_EVEN)).astype(BF16)
        q_aug[2 * pr + 1] = jnp.where(qrow >= HEAD_DIM, qb, _one_hot_rows((LANES, tq), 0, AUG_ODD)).astype(BF16)

    heads = tuple(range(hp))

    def scores(t, slot):
        off = pl.multiple_of(t * tk, tk)
        for hh in heads:
            s = jnp.dot(k_ref[0, hh, pl.ds(off, tk), :], q_aug[hh], preferred_element_type=F32)
            s_buf[hh, slot, :, 0:tq] = s
            mt_buf[hh, slot, 0:1, :] = jnp.max(s, axis=0, keepdims=True)

    def softmax(slot, ms, mask=None):
        out = []
        for hh in heads:
            s = s_buf[hh, slot, :, 0:tq]
            if mask is None:
                tile_max = mt_buf[hh, slot, 0:1, :]
            else:
                s = jnp.where(mask, s, NEG)
                tile_max = jnp.max(s, axis=0, keepdims=True)
            m_new = jnp.maximum(ms[hh], tile_max)
            p_buf[hh, slot, :, 0:tq] = jnp.exp2(s - m_new).astype(BF16)
            out.append((m_new, jnp.exp2(ms[hh] - m_new)))
        return tuple(o[0] for o in out), tuple(o[1] for o in out)

    def values(t, slot, alphas):
        off = pl.multiple_of(jnp.maximum(t, 0) * tk, tk)
        for hh in heads:
            pv = jnp.dot(v_aug[hh, :, pl.ds(off, tk)], p_buf[hh, slot, :, 0:tq], preferred_element_type=F32)
            acc_ref[hh] = alphas[hh] * acc_ref[hh] + pv

    @pl.when(i == 0)
    def _():
        for hh in heads:
            p_buf[hh, 1, :, 0:tq] = jnp.zeros((tk, tq), BF16)
            acc_ref[hh] = jnp.zeros(acc_ref.shape[1:], F32)

    scores(0, 0)

    def pair(u, carry):
        ms, alpha_prev = carry
        t = 2 * u
        scores(t + 1, 1)
        ms, alpha0 = softmax(0, ms)
        values(t - 1, 1, alpha_prev)
        scores(t + 2, 0)
        ms, alpha1 = softmax(1, ms)
        values(t, 0, alpha0)
        return ms, alpha1

    carry = (tuple(jnp.full((1, tq), NEG, F32) for _ in heads), tuple(jnp.ones((1, tq), F32) for _ in heads))
    ms, alpha_prev = lax.fori_loop(0, i, pair, carry)

    t = 2 * i
    key = lax.broadcasted_iota(jnp.int32, (tk, tq), 0)
    qry = lax.broadcasted_iota(jnp.int32, (tk, tq), 1)
    off_b = pl.multiple_of((t + 1) * tk, tk)
    tri = lax.broadcasted_iota(jnp.int32, (tk, tk), 0) <= lax.broadcasted_iota(jnp.int32, (tk, tk), 1)
    s_b = [jnp.where(tri, jnp.dot(k_ref[0, hh, pl.ds(off_b, tk), :], q_aug[hh, :, tk:],
                                  preferred_element_type=F32), NEG) for hh in heads]
    ms, alpha0 = softmax(0, ms, key <= qry)
    values(t - 1, 1, alpha_prev)
    values(t, 0, alpha0)
    outs = []
    for hh in heads:
        mt_buf[hh, 0, 0:1, :] = ms[hh]
        m_old = mt_buf[hh, 0, 0:1, tk:]
        m_new = jnp.maximum(m_old, jnp.max(s_b[hh], axis=0, keepdims=True))
        p_b = jnp.exp2(s_b[hh] - m_new).astype(BF16)
        pv = jnp.dot(v_aug[hh, :, pl.ds(off_b, tk)], p_b, preferred_element_type=F32)
        acc_ref[hh, :, tk:] = jnp.exp2(m_old - m_new) * acc_ref[hh, :, tk:] + pv
        acc = acc_ref[hh]
        outs.append(acc[:HEAD_DIM] / acc[HEAD_DIM:HEAD_DIM + 1])
    o_ref[...] = jnp.concatenate(outs, axis=0).T.astype(BF16)


def _fox_attention(pt, k_aug, b, s, tq, hp):
    tk = tq // 2
    per = s // tq
    rows = hp * HEAD_DIM
    qblk = FOXQ_ROW // rows
    vblk = FOXV_ROW // rows
    return pl.pallas_call(
        functools.partial(_fox_kernel, tq=tq, tk=tk, hp=hp),
        grid=(b, FOX_HEADS // hp, per),
        in_specs=[
            pl.BlockSpec((rows, tq), lambda bi, g, i: (qblk + g, bi * per + i)),
            pl.BlockSpec((1, hp, s, LANES), lambda bi, g, i: (bi, g, 0, 0), pipeline_mode=pl.Buffered(1)),
            pl.BlockSpec((rows, s), lambda bi, g, i: (vblk + g, bi), pipeline_mode=pl.Buffered(1)),
        ],
        out_specs=pl.BlockSpec((tq, rows), lambda bi, g, i: (bi * per + i, g)),
        out_shape=jax.ShapeDtypeStruct((b * s, BRANCH), BF16),
        scratch_shapes=[
            pltpu.VMEM((hp, LANES, tq), BF16),
            pltpu.VMEM((hp, V_ROWS, s), BF16),
            pltpu.VMEM((hp, 2, tk, tq + PITCH_PAD), F32),
            pltpu.VMEM((hp, 2, SUBLANES, tq), F32),
            pltpu.VMEM((hp, 2, tk, tq + PITCH_PAD), BF16),
            pltpu.VMEM((hp, V_ROWS, tq), F32),
        ],
        compiler_params=_params("arbitrary", "arbitrary", "arbitrary"),
        name="fox_attention",
    )(pt, k_aug, pt)


def _swa_bias_kernel(rb_ref, bucket_ref, win_ref, o_ref):
    h = pl.program_id(0)
    bucket = bucket_ref[...]
    bias = jnp.zeros(bucket.shape, F32)
    for b in range(N_BUCKETS):
        bias = jnp.where(bucket == b, rb_ref[b, h] * LOG2E, bias)
    o_ref[0, 0] = jnp.where(win_ref[0] > 0, bias, NEG)
    o_ref[1, 0] = jnp.where(win_ref[1] > 0, bias, NEG)


def _swa_bias_table(rel_bias):
    tq = np.arange(WINDOW)[None, :]
    sk = np.arange(2 * WINDOW)[:, None]
    dist = WINDOW + tq - sk
    n = jnp.maximum(jnp.asarray(dist, jnp.int32), 0)
    max_exact = N_BUCKETS // 2
    large = max_exact + (jnp.log(jnp.maximum(n, 1).astype(F32) / max_exact)
                         / math.log(WINDOW / max_exact) * (N_BUCKETS - max_exact)).astype(jnp.int32)
    bucket = jnp.where(n < max_exact, n, jnp.minimum(large, N_BUCKETS - 1))
    in_window = (dist >= 0) & (dist < WINDOW)
    win = np.stack([in_window, in_window & (sk >= WINDOW)]).astype(np.int32)
    full = lambda shape: pl.BlockSpec(shape, lambda h: (0,) * len(shape))
    return pl.pallas_call(
        _swa_bias_kernel,
        grid=(SWA_HEADS,),
        in_specs=[pl.BlockSpec(memory_space=pltpu.SMEM), full(bucket.shape), full(win.shape)],
        out_specs=pl.BlockSpec((2, 1, 2 * WINDOW, WINDOW), lambda h: (0, h // SWA_GROUP, 0, h % SWA_GROUP)),
        out_shape=jax.ShapeDtypeStruct((2, SWA_KV, 2 * WINDOW, SWA_GROUP * WINDOW), F32),
        compiler_params=_params("arbitrary"),
        name="swa_bias_table",
    )(rel_bias, bucket, jnp.asarray(win))


def _swa_kernel(sink_ref, q_ref, kp_ref, kc_ref, vp_ref, vc_ref, bias_ref, o_ref,
                k_buf, v_buf, s_buf, mt_buf, p_buf, dn_buf, *, nblk):
    gw = SWA_GROUP * WINDOW
    ts = nblk * WINDOW
    first = pl.program_id(1) == 0
    k_buf[0:WINDOW, :] = kp_ref[...]
    k_buf[WINDOW:WINDOW + ts, :] = kc_ref[...]
    v_buf[:, 0:WINDOW] = vp_ref[...]
    v_buf[:, WINDOW:WINDOW + ts] = vc_ref[...]
    glane = lax.broadcasted_iota(jnp.int32, (1, gw), 1) // WINDOW
    zeros = jnp.zeros((HEAD_DIM, gw), BF16)
    sinks = []
    for kv in range(SWA_KV):
        sink = jnp.zeros((1, gw), F32)
        for g in range(SWA_GROUP):
            sink = jnp.where(glane == g, sink_ref[kv * SWA_GROUP + g] * LOG2E, sink)
        sinks.append(sink)

    def scores(blk, slot):
        lo = pl.multiple_of(blk * WINDOW, WINDOW)
        sel = jnp.logical_and(first, blk == 0).astype(jnp.int32)
        keys = k_buf[pl.ds(lo, 2 * WINDOW), :]
        for kv in range(SWA_KV):
            qg = jnp.concatenate([q_ref[(kv * SWA_GROUP + g) * HEAD_DIM:(kv * SWA_GROUP + g + 1) * HEAD_DIM,
                                        pl.ds(lo, WINDOW)] for g in range(SWA_GROUP)], axis=1)
            q_pad = jnp.concatenate([qg, zeros] if kv == 0 else [zeros, qg], axis=0)
            s = jnp.dot(keys, q_pad, preferred_element_type=F32) + bias_ref[sel, kv]
            s_buf[kv, slot, :, 0:gw] = s
            mt_buf[kv, slot, 0:1, :] = jnp.max(s, axis=0, keepdims=True)

    def softmax(slot):
        for kv in range(SWA_KV):
            m = jnp.maximum(mt_buf[kv, slot, 0:1, :], sinks[kv])
            p = jnp.exp2(s_buf[kv, slot, :, 0:gw] - m)
            dn_buf[kv, slot, 0:1, :] = jnp.sum(p, axis=0, keepdims=True) + jnp.exp2(sinks[kv] - m)
            p_buf[kv, slot, :, 0:gw] = p.astype(BF16)

    def values(blk, slot):
        lo = pl.multiple_of(blk * WINDOW, WINDOW)
        for kv in range(SWA_KV):
            vb = v_buf[kv * HEAD_DIM:(kv + 1) * HEAD_DIM, pl.ds(lo, 2 * WINDOW)]
            o = jnp.dot(vb, p_buf[kv, slot, :, 0:gw], preferred_element_type=F32) / dn_buf[kv, slot, 0:1, :]
            for pr in range(SWA_GROUP // 2):
                two = jnp.concatenate([o[:, (2 * pr) * WINDOW:(2 * pr + 1) * WINDOW],
                                       o[:, (2 * pr + 1) * WINDOW:(2 * pr + 2) * WINDOW]], axis=0)
                col = (kv * (SWA_GROUP // 2) + pr) * LANES
                o_ref[pl.ds(lo, WINDOW), col:col + LANES] = two.T.astype(BF16)

    scores(jnp.int32(0), 0)
    scores(jnp.int32(1), 1)
    softmax(0)

    def pair(u, carry):
        blk = 2 * u + 1
        scores(blk + 1, 0)
        softmax(1)
        values(blk - 1, 0)
        scores(blk + 2, 1)
        softmax(0)
        values(blk, 1)
        return carry

    lax.fori_loop(0, nblk // 2 - 1, pair, 0)
    softmax(1)
    values(jnp.int32(nblk - 2), 0)
    values(jnp.int32(nblk - 1), 1)


def _swa_attention(sink, pt, pn, bias, b, s, nblk):
    ts = nblk * WINDOW
    per = s // ts
    perw = s // WINDOW
    qblk = SWAQ_ROW // BRANCH
    vblk = SWAV_ROW // LANES
    kblk = SWAK_OFF // LANES
    prev = lambda bi, i: bi * perw + jnp.maximum(i * nblk - 1, 0)
    return pl.pallas_call(
        functools.partial(_swa_kernel, nblk=nblk),
        grid=(b, per),
        in_specs=[
            pl.BlockSpec(memory_space=pltpu.SMEM),
            pl.BlockSpec((BRANCH, ts), lambda bi, i: (qblk, bi * per + i)),
            pl.BlockSpec((WINDOW, LANES), lambda bi, i: (prev(bi, i), kblk)),
            pl.BlockSpec((ts, LANES), lambda bi, i: (bi * per + i, kblk)),
            pl.BlockSpec((LANES, WINDOW), lambda bi, i: (vblk, prev(bi, i))),
            pl.BlockSpec((LANES, ts), lambda bi, i: (vblk, bi * per + i)),
            _const_spec(bias.shape),
        ],
        out_specs=pl.BlockSpec((ts, BRANCH), lambda bi, i: (bi * per + i, 0)),
        out_shape=jax.ShapeDtypeStruct((b * s, BRANCH), BF16),
        scratch_shapes=[
            pltpu.VMEM((ts + WINDOW, LANES), BF16),
            pltpu.VMEM((LANES, ts + WINDOW), BF16),
            pltpu.VMEM((SWA_KV, 2, 2 * WINDOW, SWA_GROUP * WINDOW + PITCH_PAD), F32),
            pltpu.VMEM((SWA_KV, 2, SUBLANES, SWA_GROUP * WINDOW), F32),
            pltpu.VMEM((SWA_KV, 2, 2 * WINDOW, SWA_GROUP * WINDOW + PITCH_PAD), BF16),
            pltpu.VMEM((SWA_KV, 2, SUBLANES, SWA_GROUP * WINDOW), F32),
        ],
        compiler_params=_params("arbitrary", "arbitrary"),
        name="swa_attention",
    )(sink, pt, pn, pn, pt, pt, bias)


def _merge_kernel(x_ref, gate_ref, yc_ref, yf_ref, ys_ref, wb_ref, wo_ref, o_ref, m_scr):
    branches = (yc_ref[...], yf_ref[...], ys_ref[...])
    for lo in range(0, D_MODEL, MERGE_CHUNK):
        merged = None
        for b, y in enumerate(branches):
            gate = jax.nn.sigmoid(gate_ref[:, b * D_MODEL + lo:b * D_MODEL + lo + MERGE_CHUNK].astype(F32))
            term = gate * jnp.dot(y, wb_ref[b, :, lo:lo + MERGE_CHUNK], preferred_element_type=F32)
            merged = term if merged is None else merged + term
        m_scr[:, lo:lo + MERGE_CHUNK] = merged.astype(BF16)
    o_ref[...] = x_ref[...] + jnp.dot(m_scr[...], wo_ref[...], preferred_element_type=F32)


def _merge(x, pn, y_fox, y_swa, w_branch, w_out, tm):
    n, d = x.shape
    return pl.pallas_call(
        _merge_kernel,
        grid=(n // tm,),
        in_specs=[
            pl.BlockSpec((tm, d), lambda i: (i, 0)),
            pl.BlockSpec((tm, 3 * D_MODEL), lambda i: (i, GATE_OFF // (3 * D_MODEL))),
            pl.BlockSpec((tm, BRANCH), lambda i: (i, CONV_OFF // BRANCH)),
            pl.BlockSpec((tm, BRANCH), lambda i: (i, 0)),
            pl.BlockSpec((tm, BRANCH), lambda i: (i, 0)),
            _const_spec(w_branch.shape), _const_spec(w_out.shape),
        ],
        out_specs=pl.BlockSpec((tm, d), lambda i: (i, 0)),
        out_shape=jax.ShapeDtypeStruct((n, d), F32),
        scratch_shapes=[pltpu.VMEM((tm, d), BF16)],
        compiler_params=_params("arbitrary"),
        name="conv_merge_out",
    )(x, pn, pn, y_fox, y_swa, w_branch, w_out)


def _xattn_kernel(x_ref, g_ref, wq_ref, kT_ref, v_ref, wo_ref, o_ref):
    x = x_ref[...]
    q = jnp.dot(_rms(x, g_ref[...]).astype(BF16), wq_ref[...], preferred_element_type=F32).astype(BF16)
    heads = []
    for h in range(X_HEADS):
        sl = slice(h * X_HEAD_DIM, (h + 1) * X_HEAD_DIM)
        s = jnp.dot(q[:, sl], kT_ref[0, sl, :], preferred_element_type=F32)
        p = jnp.exp2(s - jnp.max(s, axis=-1, keepdims=True))
        denom = jnp.sum(p, axis=-1, keepdims=True)
        o = jnp.dot(p.astype(BF16), v_ref[0, :, sl], preferred_element_type=F32)
        heads.append((o / denom).astype(BF16))
    o_ref[...] = x + jnp.dot(jnp.concatenate(heads, axis=1), wo_ref[...], preferred_element_type=F32)


def _xattn(x, g, wq, kT, v, wo, seq, tm):
    n, d = x.shape
    tiles = seq // tm
    return pl.pallas_call(
        _xattn_kernel,
        grid=(n // tm,),
        in_specs=[
            pl.BlockSpec((tm, d), lambda i: (i, 0)),
            _const_spec((1, d)),
            _const_spec(wq.shape),
            pl.BlockSpec((1,) + kT.shape[1:], lambda i: (i // tiles, 0, 0)),
            pl.BlockSpec((1,) + v.shape[1:], lambda i: (i // tiles, 0, 0)),
            _const_spec(wo.shape),
        ],
        out_specs=pl.BlockSpec((tm, d), lambda i: (i, 0)),
        out_shape=jax.ShapeDtypeStruct((n, d), F32),
        compiler_params=_params("arbitrary"),
        name="cross_attention",
    )(x, g.reshape(1, d), wq, kT, v, wo)


def _swiglu_kernel(x_ref, g_ref, wg_ref, wu_ref, wd_ref, gf_ref, o_ref, *, final_norm):
    x = x_ref[...]
    xn = _rms(x, g_ref[...]).astype(BF16)
    gate = jnp.dot(xn, wg_ref[...], preferred_element_type=F32)
    up = jnp.dot(xn, wu_ref[...], preferred_element_type=F32)
    hidden = (gate * jax.nn.sigmoid(gate) * up).astype(BF16)
    y = x + jnp.dot(hidden, wd_ref[...], preferred_element_type=F32)
    o_ref[...] = _rms(y, gf_ref[...]) if final_norm else y


def _swiglu(x, g, wg, wu, wd, g_final, final_norm, tm):
    n, d = x.shape
    return pl.pallas_call(
        functools.partial(_swiglu_kernel, final_norm=final_norm),
        grid=(n // tm,),
        in_specs=[
            pl.BlockSpec((tm, d), lambda i: (i, 0)),
            _const_spec((1, d)),
            _const_spec(wg.shape), _const_spec(wu.shape), _const_spec(wd.shape),
            _const_spec((1, d)),
        ],
        out_specs=pl.BlockSpec((tm, d), lambda i: (i, 0)),
        out_shape=jax.ShapeDtypeStruct((n, d), F32),
        compiler_params=_params("arbitrary"),
        name="swiglu",
    )(x, g.reshape(1, d), wg, wu, wd, g_final.reshape(1, d))


def _pack_in_proj(w_in):
    sizes = [BRANCH] * 3 + [BRANCH] * 3 + [FOX_HEADS] + [BRANCH, SWA_KV * HEAD_DIM, SWA_KV * HEAD_DIM] + [3 * D_MODEL]
    offs = np.concatenate([[0], np.cumsum(sizes)])
    c_b, c_c, c_u, f_q, f_k, f_v, f_g, s_q, s_k, s_v, gates = (w_in[:, offs[t]:offs[t + 1]] for t in range(len(sizes)))
    qscale = HEAD_DIM ** -0.5 * LOG2E
    wn = jnp.concatenate([gates, f_k, s_k], axis=1)
    wc = jnp.concatenate([c_b, c_c, c_u], axis=1)
    wt = jnp.concatenate([f_q * qscale, f_v, s_q * qscale, s_v], axis=1).T
    wfg = jnp.pad(f_g, ((0, 0), (0, LANES - FOX_HEADS)))
    return wn.astype(BF16), wc.astype(BF16), wt.astype(BF16), wfg.astype(BF16)


def kernel(x, mem, mix_norm_g, w_in, forget_bias, conv_w, sink, w_branch, w_mix_out, rel_bias,
           xattn_norm_g, mem_norm_g, w_xq, w_xkv, w_xo, ffn_norm_g, w_ffn_gate, w_ffn_up, w_ffn_down,
           final_norm_g):
    b, s, d = x.shape
    n = b * s
    depth = w_in.shape[0]
    mem_len = mem.shape[1]
    xf = x.reshape(n, d)
    memf = mem.reshape(b * mem_len, d)
    swa_bias = _swa_bias_table(rel_bias)
    for l in range(depth):
        wn, wc, wt, wfg = _pack_in_proj(w_in[l])
        pn, pt, fg = _in_proj(xf, mix_norm_g[l], wn, wc, wt, wfg, conv_w[l], seq=s, tm=512)

        k_aug = _fox_prep(fg, forget_bias[l], pn, b, s, ts=1024)
        y_fox = _fox_attention(pt, k_aug, b, s, tq=1024, hp=4)
        y_swa = _swa_attention(sink[l], pt, pn, swa_bias, b, s, nblk=16)
        xf = _merge(xf, pn, y_fox, y_swa, w_branch[l].astype(BF16), w_mix_out[l].astype(BF16), tm=512)

        kv = _norm_proj(memf, mem_norm_g[l], w_xkv[l].astype(BF16))
        kT = kv[:, :d].reshape(b, mem_len, d).transpose(0, 2, 1)
        v = kv[:, d:].reshape(b, mem_len, d)
        wq = (w_xq[l] * (X_HEAD_DIM ** -0.5 * LOG2E)).astype(BF16)
        xf = _xattn(xf, xattn_norm_g[l], wq, kT, v, w_xo[l].astype(BF16), seq=s, tm=512)

        xf = _swiglu(xf, ffn_norm_g[l], w_ffn_gate[l].astype(BF16), w_ffn_up[l].astype(BF16),
                     w_ffn_down[l].astype(BF16), final_norm_g, final_norm=(l == depth - 1), tm=512)
    return xf.reshape(b, s, d)
```

```python
import functools
import math

import jax
import jax.numpy as jnp
import numpy as np
from jax import lax
from jax.experimental import pallas as pl
from jax.experimental.pallas import tpu as pltpu

F32 = jnp.float32
BF16 = jnp.bfloat16

D_MODEL = 1024
HEAD_DIM = 64
BRANCH = 512
FOX_HEADS = 8
SWA_HEADS = 8
SWA_KV = 2
SWA_GROUP = SWA_HEADS // SWA_KV
WINDOW = 128
N_BUCKETS = 32
X_HEADS = 4
X_HEAD_DIM = D_MODEL // X_HEADS
RMS_EPS = 1e-6
NEG = -1e30
LOG2E = math.log2(math.e)

LANES = 128
SUBLANES = 8
VMEM_LIMIT_BYTES = 56 * 1024 * 1024
PITCH_PAD = LANES
MERGE_CHUNK = 256

ROW_TILE = 512
PREP_TILE = 1024
FOX_TQ = 1024
FOX_HEADS_PER_STEP = 4
SWA_BLOCKS_PER_STEP = 16

GATE_OFF = 0
CONV_OFF = 3 * D_MODEL
FOXK_OFF = CONV_OFF + BRANCH
SWAK_OFF = FOXK_OFF + BRANCH
PN_COLS = SWAK_OFF + SWA_KV * HEAD_DIM
FOXQ_ROW = 0
FOXV_ROW = BRANCH
SWAQ_ROW = 2 * BRANCH
SWAV_ROW = 3 * BRANCH
PT_ROWS = SWAV_ROW + SWA_KV * HEAD_DIM

AUG_STRIDE = 8
AUG_EVEN = (HEAD_DIM, HEAD_DIM + AUG_STRIDE, HEAD_DIM + 2 * AUG_STRIDE)
AUG_ODD = (0, AUG_STRIDE, 2 * AUG_STRIDE)
V_ROWS = 80


def _params(*sem, flags=None):
    return pltpu.CompilerParams(dimension_semantics=sem, vmem_limit_bytes=VMEM_LIMIT_BYTES, flags=flags)


def _rms(x, g):
    return x * lax.rsqrt(jnp.mean(x * x, axis=-1, keepdims=True) + RMS_EPS) * g


def _const_spec(shape):
    nd = len(shape)
    return pl.BlockSpec(shape, lambda *_: (0,) * nd, pipeline_mode=pl.Buffered(1))


def _one_hot_rows(shape, axis, positions):
    idx = lax.broadcasted_iota(jnp.int32, shape, axis)
    hit = idx == positions[0]
    for p in positions[1:]:
        hit = hit | (idx == p)
    return jnp.where(hit, 1.0, 0.0)


def _in_proj_kernel(x_ref, g_ref, wn_ref, wc_ref, wt_ref, wfg_ref, cw_ref, pn_ref, pt_ref, fg_ref, z_tail,
                    *, tiles_per_seq):
    tm = x_ref.shape[0]
    h = _rms(x_ref[...], g_ref[...]).astype(BF16)
    fg_ref[...] = jnp.dot(h, wfg_ref[...], preferred_element_type=F32)
    pt_ref[...] = lax.dot_general(wt_ref[...], h, (((1,), (1,)), ((), ())),
                                  preferred_element_type=F32).astype(BF16)
    for lo in range(0, CONV_OFF, D_MODEL):
        pn_ref[:, lo:lo + D_MODEL] = jnp.dot(h, wn_ref[:, lo:lo + D_MODEL], preferred_element_type=F32).astype(BF16)
    pn_ref[:, FOXK_OFF:PN_COLS] = jnp.dot(h, wn_ref[:, CONV_OFF:], preferred_element_type=F32).astype(BF16)

    conv_b, conv_c, conv_u = (jnp.dot(h, wc_ref[:, t * BRANCH:(t + 1) * BRANCH], preferred_element_type=F32)
                              for t in range(3))
    z = conv_c * conv_u
    @pl.when(pl.program_id(0) == 0)
    def _():
        z_tail[...] = jnp.zeros(z_tail.shape, F32)

    zh = jnp.where(pl.program_id(0) % tiles_per_seq == 0, 0.0, z_tail[...])
    z_tail[...] = z[tm - SUBLANES:tm, :]
    row = lax.broadcasted_iota(jnp.int32, z.shape, 0)
    z1 = jnp.where(row == 0, zh[7:8], pltpu.roll(z, 1, axis=0))
    z2 = jnp.where(row == 0, zh[6:7], jnp.where(row == 1, zh[7:8], pltpu.roll(z, 2, axis=0)))
    cw = cw_ref[...]
    pn_ref[:, CONV_OFF:FOXK_OFF] = (conv_b * (cw[0:1] * z2 + cw[1:2] * z1 + cw[2:3] * z)).astype(BF16)


def _in_proj(x, g, wn, wc, wt, wfg, conv_w, seq, tm):
    n, d = x.shape
    return pl.pallas_call(
        functools.partial(_in_proj_kernel, tiles_per_seq=seq // tm),
        grid=(n // tm,),
        in_specs=[
            pl.BlockSpec((tm, d), lambda i: (i, 0)),
            _const_spec((1, d)), _const_spec(wn.shape), _const_spec(wc.shape), _const_spec(wt.shape),
            _const_spec(wfg.shape), _const_spec(conv_w.shape),
        ],
        out_specs=[
            pl.BlockSpec((tm, PN_COLS), lambda i: (i, 0)),
            pl.BlockSpec((PT_ROWS, tm), lambda i: (0, i)),
            pl.BlockSpec((tm, LANES), lambda i: (i, 0)),
        ],
        out_shape=[jax.ShapeDtypeStruct((n, PN_COLS), BF16), jax.ShapeDtypeStruct((PT_ROWS, n), BF16),
                   jax.ShapeDtypeStruct((n, LANES), F32)],
        scratch_shapes=[pltpu.VMEM((SUBLANES, BRANCH), F32)],
        compiler_params=_params("arbitrary"),
        name="in_proj",
    )(x, g.reshape(1, d), wn, wc, wt, wfg, conv_w)


def _norm_proj_kernel(x_ref, g_ref, w_ref, o_ref):
    h = _rms(x_ref[...], g_ref[...]).astype(BF16)
    o_ref[...] = jnp.dot(h, w_ref[...], preferred_element_type=F32).astype(BF16)


def _norm_proj(x, g, w):
    n, d = x.shape
    cols = w.shape[1]
    return pl.pallas_call(
        _norm_proj_kernel,
        grid=(1,),
        in_specs=[_const_spec((n, d)), _const_spec((1, d)), _const_spec(w.shape)],
        out_specs=pl.BlockSpec((n, cols), lambda i: (0, 0)),
        out_shape=jax.ShapeDtypeStruct((n, cols), BF16),
        compiler_params=_params("arbitrary"),
        name="mem_proj",
    )(x, g.reshape(1, d), w)


def _fox_prep_kernel(fg_ref, b_ref, k_ref, o_ref, carry_ref):
    ts = fg_ref.shape[0]

    @pl.when(pl.program_id(1) == 0)
    def _():
        carry_ref[...] = jnp.zeros(carry_ref.shape, F32)

    lane = lax.broadcasted_iota(jnp.int32, (ts, LANES), 1)
    row = lax.broadcasted_iota(jnp.int32, (ts, LANES), 0)
    c = jnp.where(lane < FOX_HEADS, jax.nn.log_sigmoid(fg_ref[...] + b_ref[...]) * LOG2E, 0.0)
    d = 1
    while d < ts:
        c = c + jnp.where(row >= d, pltpu.roll(c, d, axis=0), 0.0)
        d *= 2
    c = c + carry_ref[0:1, :]
    carry_ref[0:1, :] = c[ts - 1:ts, :]

    c1 = c.astype(BF16).astype(F32)
    r1 = c - c1
    c2 = r1.astype(BF16).astype(F32)
    c3 = (r1 - c2).astype(BF16).astype(F32)
    packed = -jnp.where(lane < AUG_STRIDE, c1,
                        jnp.where(lane < 2 * AUG_STRIDE, pltpu.roll(c2, AUG_STRIDE, axis=1),
                                  pltpu.roll(c3, 2 * AUG_STRIDE, axis=1)))
    keep_even = _one_hot_rows((1, LANES), 1, AUG_EVEN)
    keep_odd = _one_hot_rows((1, LANES), 1, AUG_ODD)
    for h in range(FOX_HEADS):
        k_pair = k_ref[:, (h // 2) * LANES:(h // 2 + 1) * LANES].astype(F32)
        if h % 2 == 0:
            aug = pltpu.roll(packed, (AUG_EVEN[0] - h) % LANES, axis=1) * keep_even
            out = jnp.where(lane < HEAD_DIM, k_pair, aug)
        else:
            aug = pltpu.roll(packed, (AUG_ODD[0] - h) % LANES, axis=1) * keep_odd
            out = jnp.where(lane >= HEAD_DIM, k_pair, aug)
        o_ref[0, h] = out.astype(BF16)


def _fox_prep(fg, bias, pn, b, s, ts):
    kblk = FOXK_OFF // BRANCH
    per = s // ts
    return pl.pallas_call(
        _fox_prep_kernel,
        grid=(b, per),
        in_specs=[
            pl.BlockSpec((ts, LANES), lambda bi, j: (bi * per + j, 0)),
            _const_spec((1, LANES)),
            pl.BlockSpec((ts, BRANCH), lambda bi, j: (bi * per + j, kblk)),
        ],
        out_specs=pl.BlockSpec((1, FOX_HEADS, ts, LANES), lambda bi, j: (bi, 0, j, 0)),
        out_shape=jax.ShapeDtypeStruct((b, FOX_HEADS, s, LANES), BF16),
        scratch_shapes=[pltpu.VMEM((SUBLANES, LANES), F32)],
        compiler_params=_params("arbitrary", "arbitrary"),
        name="fox_prep",
    )(fg, jnp.pad(bias, (0, LANES - FOX_HEADS)).reshape(1, LANES), pn)


def _fox_kernel(q_ref, qn_ref, k_ref, v_ref, o_ref, q_aug, v_aug, s_buf, mt_buf, p_buf, acc_ref, *, tq, tk, hp):
    i = pl.program_id(2)
    seq = v_ref.shape[1]
    cur = i % 2
    heads = tuple(range(hp))

    def build_queries(src_ref, qslot):
        qrow = lax.broadcasted_iota(jnp.int32, (LANES, tq), 0)
        for pr in range(hp // 2):
            qb = src_ref[pr * LANES:(pr + 1) * LANES, :].astype(F32)
            q_aug[qslot, 2 * pr] = jnp.where(qrow < HEAD_DIM, qb, _one_hot_rows((LANES, tq), 0, AUG_EVEN)).astype(BF16)
            q_aug[qslot, 2 * pr + 1] = jnp.where(qrow >= HEAD_DIM, qb,
                                                 _one_hot_rows((LANES, tq), 0, AUG_ODD)).astype(BF16)

    def scores(t, slot, qslot):
        off = pl.multiple_of(t * tk, tk)
        for hh in heads:
            s = jnp.dot(k_ref[0, hh, pl.ds(off, tk), :], q_aug[qslot, hh], preferred_element_type=F32)
            s_buf[hh, slot, :, 0:tq] = s
            mt_buf[hh, slot, 0:1, :] = jnp.max(s, axis=0, keepdims=True)

    def softmax(slot, ms, mask=None):
        out = []
        for hh in heads:
            s = s_buf[hh, slot, :, 0:tq]
            if mask is None:
                tile_max = mt_buf[hh, slot, 0:1, :]
            else:
                s = jnp.where(mask, s, NEG)
                tile_max = jnp.max(s, axis=0, keepdims=True)
            m_new = jnp.maximum(ms[hh], tile_max)
            p_buf[hh, slot, :, 0:tq] = jnp.exp2(s - m_new).astype(BF16)
            out.append((m_new, jnp.exp2(ms[hh] - m_new)))
        return tuple(o[0] for o in out), tuple(o[1] for o in out)

    def values(t, slot, alphas):
        off = pl.multiple_of(jnp.maximum(t, 0) * tk, tk)
        for hh in heads:
            pv = jnp.dot(v_aug[hh, :, pl.ds(off, tk)], p_buf[hh, slot, :, 0:tq], preferred_element_type=F32)
            acc_ref[hh] = alphas[hh] * acc_ref[hh] + pv

    @pl.when(i == 0)
    def _():
        ones_row = _one_hot_rows((V_ROWS - HEAD_DIM, seq), 0, (0,)).astype(BF16)
        for hh in heads:
            v_aug[hh, 0:HEAD_DIM, :] = v_ref[hh * HEAD_DIM:(hh + 1) * HEAD_DIM, :]
            v_aug[hh, HEAD_DIM:V_ROWS, :] = ones_row
            p_buf[hh, 1, :, 0:tq] = jnp.zeros((tk, tq), BF16)
            acc_ref[hh] = jnp.zeros(acc_ref.shape[1:], F32)
        build_queries(q_ref, 0)
        scores(0, 0, 0)

    def pair(u, carry):
        ms, alpha_prev = carry
        t = 2 * u
        scores(t + 1, 1, cur)
        ms, alpha0 = softmax(0, ms)
        values(t - 1, 1, alpha_prev)
        scores(t + 2, 0, cur)
        ms, alpha1 = softmax(1, ms)
        values(t, 0, alpha0)
        return ms, alpha1

    carry = (tuple(jnp.full((1, tq), NEG, F32) for _ in heads), tuple(jnp.ones((1, tq), F32) for _ in heads))
    ms, alpha_prev = lax.fori_loop(0, i, pair, carry)

    t = 2 * i
    key = lax.broadcasted_iota(jnp.int32, (tk, tq), 0)
    qry = lax.broadcasted_iota(jnp.int32, (tk, tq), 1)
    off_b = pl.multiple_of((t + 1) * tk, tk)
    tri = lax.broadcasted_iota(jnp.int32, (tk, tk), 0) <= lax.broadcasted_iota(jnp.int32, (tk, tk), 1)
    s_b = [jnp.where(tri, jnp.dot(k_ref[0, hh, pl.ds(off_b, tk), :], q_aug[cur, hh, :, tk:],
                                  preferred_element_type=F32), NEG) for hh in heads]
    ms, alpha0 = softmax(0, ms, key <= qry)
    build_queries(qn_ref, 1 - cur)
    scores(0, 0, 1 - cur)
    values(t - 1, 1, alpha_prev)
    values(t, 0, alpha0)
    outs = []
    for hh in heads:
        mt_buf[hh, 1, 0:1, :] = ms[hh]
        m_old = mt_buf[hh, 1, 0:1, tk:]
        m_new = jnp.maximum(m_old, jnp.max(s_b[hh], axis=0, keepdims=True))
        p_b = jnp.exp2(s_b[hh] - m_new).astype(BF16)
        pv = jnp.dot(v_aug[hh, :, pl.ds(off_b, tk)], p_b, preferred_element_type=F32)
        acc_ref[hh, :, tk:] = jnp.exp2(m_old - m_new) * acc_ref[hh, :, tk:] + pv
        acc = acc_ref[hh]
        outs.append(acc[:HEAD_DIM] / acc[HEAD_DIM:HEAD_DIM + 1])
    o_ref[...] = jnp.concatenate(outs, axis=0).T.astype(BF16)


def _fox_attention(pt, k_aug, b, s, tq, hp):
    tk = tq // 2
    per = s // tq
    rows = hp * HEAD_DIM
    qblk = FOXQ_ROW // rows
    vblk = FOXV_ROW // rows
    return pl.pallas_call(
        functools.partial(_fox_kernel, tq=tq, tk=tk, hp=hp),
        grid=(b, FOX_HEADS // hp, per),
        in_specs=[
            pl.BlockSpec((rows, tq), lambda bi, g, i: (qblk + g, bi * per + i)),
            pl.BlockSpec((rows, tq), lambda bi, g, i: (qblk + g, bi * per + jnp.minimum(i + 1, per - 1))),
            pl.BlockSpec((1, hp, s, LANES), lambda bi, g, i: (bi, g, 0, 0), pipeline_mode=pl.Buffered(1)),
            pl.BlockSpec((rows, s), lambda bi, g, i: (vblk + g, bi), pipeline_mode=pl.Buffered(1)),
        ],
        out_specs=pl.BlockSpec((tq, rows), lambda bi, g, i: (bi * per + i, g)),
        out_shape=jax.ShapeDtypeStruct((b * s, BRANCH), BF16),
        scratch_shapes=[
            pltpu.VMEM((2, hp, LANES, tq), BF16),
            pltpu.VMEM((hp, V_ROWS, s), BF16),
            pltpu.VMEM((hp, 2, tk, tq + PITCH_PAD), F32),
            pltpu.VMEM((hp, 2, SUBLANES, tq), F32),
            pltpu.VMEM((hp, 2, tk, tq + PITCH_PAD), BF16),
            pltpu.VMEM((hp, V_ROWS, tq), F32),
        ],
        compiler_params=_params("arbitrary", "arbitrary", "arbitrary"),
        name="fox_attention",
    )(pt, pt, k_aug, pt)


def _swa_bias_kernel(rb_ref, bucket_ref, win_ref, o_ref):
    h = pl.program_id(0)
    bucket = bucket_ref[...]
    bias = jnp.zeros(bucket.shape, F32)
    for b in range(N_BUCKETS):
        bias = jnp.where(bucket == b, rb_ref[b, h] * LOG2E, bias)
    o_ref[0, 0] = jnp.where(win_ref[0] > 0, bias, NEG)
    o_ref[1, 0] = jnp.where(win_ref[1] > 0, bias, NEG)


def _swa_bias_table(rel_bias):
    tq = np.arange(WINDOW)[None, :]
    sk = np.arange(2 * WINDOW)[:, None]
    dist = WINDOW + tq - sk
    n = jnp.maximum(jnp.asarray(dist, jnp.int32), 0)
    max_exact = N_BUCKETS // 2
    large = max_exact + (jnp.log(jnp.maximum(n, 1).astype(F32) / max_exact)
                         / math.log(WINDOW / max_exact) * (N_BUCKETS - max_exact)).astype(jnp.int32)
    bucket = jnp.where(n < max_exact, n, jnp.minimum(large, N_BUCKETS - 1))
    in_window = (dist >= 0) & (dist < WINDOW)
    win = np.stack([in_window, in_window & (sk >= WINDOW)]).astype(np.int32)
    full = lambda shape: pl.BlockSpec(shape, lambda h: (0,) * len(shape))
    return pl.pallas_call(
        _swa_bias_kernel,
        grid=(SWA_HEADS,),
        in_specs=[pl.BlockSpec(memory_space=pltpu.SMEM), full(bucket.shape), full(win.shape)],
        out_specs=pl.BlockSpec((2, 1, 2 * WINDOW, WINDOW), lambda h: (0, h // SWA_GROUP, 0, h % SWA_GROUP)),
        out_shape=jax.ShapeDtypeStruct((2, SWA_KV, 2 * WINDOW, SWA_GROUP * WINDOW), F32),
        compiler_params=_params("arbitrary"),
        name="swa_bias_table",
    )(rel_bias, bucket, jnp.asarray(win))


def _swa_kernel(sink_ref, q_ref, kp_ref, kc_ref, vp_ref, vc_ref, bias_ref, o_ref,
                k_buf, v_buf, s_buf, mt_buf, p_buf, dn_buf, *, nblk):
    gw = SWA_GROUP * WINDOW
    ts = nblk * WINDOW
    first = pl.program_id(1) == 0
    k_buf[0:WINDOW, :] = kp_ref[...]
    k_buf[WINDOW:WINDOW + ts, :] = kc_ref[...]
    v_buf[:, 0:WINDOW] = vp_ref[...]
    v_buf[:, WINDOW:WINDOW + ts] = vc_ref[...]
    glane = lax.broadcasted_iota(jnp.int32, (1, gw), 1) // WINDOW
    zeros = jnp.zeros((HEAD_DIM, gw), BF16)
    sinks = []
    for kv in range(SWA_KV):
        sink = jnp.zeros((1, gw), F32)
        for g in range(SWA_GROUP):
            sink = jnp.where(glane == g, sink_ref[kv * SWA_GROUP + g] * LOG2E, sink)
        sinks.append(sink)

    def scores(blk, slot):
        lo = pl.multiple_of(blk * WINDOW, WINDOW)
        sel = jnp.logical_and(first, blk == 0).astype(jnp.int32)
        keys = k_buf[pl.ds(lo, 2 * WINDOW), :]
        for kv in range(SWA_KV):
            qg = jnp.concatenate([q_ref[(kv * SWA_GROUP + g) * HEAD_DIM:(kv * SWA_GROUP + g + 1) * HEAD_DIM,
                                        pl.ds(lo, WINDOW)] for g in range(SWA_GROUP)], axis=1)
            q_pad = jnp.concatenate([qg, zeros] if kv == 0 else [zeros, qg], axis=0)
            s = jnp.dot(keys, q_pad, preferred_element_type=F32) + bias_ref[sel, kv]
            s_buf[kv, slot, :, 0:gw] = s
            mt_buf[kv, slot, 0:1, :] = jnp.max(s, axis=0, keepdims=True)

    def softmax(slot):
        for kv in range(SWA_KV):
            m = jnp.maximum(mt_buf[kv, slot, 0:1, :], sinks[kv])
            p = jnp.exp2(s_buf[kv, slot, :, 0:gw] - m)
            dn_buf[kv, slot, 0:1, :] = jnp.sum(p, axis=0, keepdims=True) + jnp.exp2(sinks[kv] - m)
            p_buf[kv, slot, :, 0:gw] = p.astype(BF16)

    def values(blk, slot):
        lo = pl.multiple_of(blk * WINDOW, WINDOW)
        for kv in range(SWA_KV):
            vb = v_buf[kv * HEAD_DIM:(kv + 1) * HEAD_DIM, pl.ds(lo, 2 * WINDOW)]
            o = jnp.dot(vb, p_buf[kv, slot, :, 0:gw], preferred_element_type=F32) / dn_buf[kv, slot, 0:1, :]
            for pr in range(SWA_GROUP // 2):
                two = jnp.concatenate([o[:, (2 * pr) * WINDOW:(2 * pr + 1) * WINDOW],
                                       o[:, (2 * pr + 1) * WINDOW:(2 * pr + 2) * WINDOW]], axis=0)
                col = (kv * (SWA_GROUP // 2) + pr) * LANES
                o_ref[pl.ds(lo, WINDOW), col:col + LANES] = two.T.astype(BF16)

    scores(jnp.int32(0), 0)
    scores(jnp.int32(1), 1)
    softmax(0)

    def pair(u, carry):
        blk = 2 * u + 1
        scores(blk + 1, 0)
        softmax(1)
        values(blk - 1, 0)
        scores(blk + 2, 1)
        softmax(0)
        values(blk, 1)
        return carry

    lax.fori_loop(0, nblk // 2 - 1, pair, 0)
    softmax(1)
    values(jnp.int32(nblk - 2), 0)
    values(jnp.int32(nblk - 1), 1)


def _swa_attention(sink, pt, pn, bias, b, s, nblk):
    ts = nblk * WINDOW
    per = s // ts
    perw = s // WINDOW
    qblk = SWAQ_ROW // BRANCH
    vblk = SWAV_ROW // LANES
    kblk = SWAK_OFF // LANES
    prev = lambda bi, i: bi * perw + jnp.maximum(i * nblk - 1, 0)
    return pl.pallas_call(
        functools.partial(_swa_kernel, nblk=nblk),
        grid=(b, per),
        in_specs=[
            pl.BlockSpec(memory_space=pltpu.SMEM),
            pl.BlockSpec((BRANCH, ts), lambda bi, i: (qblk, bi * per + i)),
            pl.BlockSpec((WINDOW, LANES), lambda bi, i: (prev(bi, i), kblk)),
            pl.BlockSpec((ts, LANES), lambda bi, i: (bi * per + i, kblk)),
            pl.BlockSpec((LANES, WINDOW), lambda bi, i: (vblk, prev(bi, i))),
            pl.BlockSpec((LANES, ts), lambda bi, i: (vblk, bi * per + i)),
            _const_spec(bias.shape),
        ],
        out_specs=pl.BlockSpec((ts, BRANCH), lambda bi, i: (bi * per + i, 0)),
        out_shape=jax.ShapeDtypeStruct((b * s, BRANCH), BF16),
        scratch_shapes=[
            pltpu.VMEM((ts + WINDOW, LANES), BF16),
            pltpu.VMEM((LANES, ts + WINDOW), BF16),
            pltpu.VMEM((SWA_KV, 2, 2 * WINDOW, SWA_GROUP * WINDOW + PITCH_PAD), F32),
            pltpu.VMEM((SWA_KV, 2, SUBLANES, SWA_GROUP * WINDOW), F32),
            pltpu.VMEM((SWA_KV, 2, 2 * WINDOW, SWA_GROUP * WINDOW + PITCH_PAD), BF16),
            pltpu.VMEM((SWA_KV, 2, SUBLANES, SWA_GROUP * WINDOW), F32),
        ],
        compiler_params=_params("arbitrary", "arbitrary"),
        name="swa_attention",
    )(sink, pt, pn, pn, pt, pt, bias)


def _merge_kernel(x_ref, gate_ref, yc_ref, yf_ref, ys_ref, wb_ref, wo_ref, o_ref, m_scr):
    branches = (yc_ref[...], yf_ref[...], ys_ref[...])
    for lo in range(0, D_MODEL, MERGE_CHUNK):
        merged = None
        for b, y in enumerate(branches):
            gate = jax.nn.sigmoid(gate_ref[:, b * D_MODEL + lo:b * D_MODEL + lo + MERGE_CHUNK].astype(F32))
            term = gate * jnp.dot(y, wb_ref[b, :, lo:lo + MERGE_CHUNK], preferred_element_type=F32)
            merged = term if merged is None else merged + term
        m_scr[:, lo:lo + MERGE_CHUNK] = merged.astype(BF16)
    o_ref[...] = x_ref[...] + jnp.dot(m_scr[...], wo_ref[...], preferred_element_type=F32)


def _merge(x, pn, y_fox, y_swa, w_branch, w_out, tm):
    n, d = x.shape
    return pl.pallas_call(
        _merge_kernel,
        grid=(n // tm,),
        in_specs=[
            pl.BlockSpec((tm, d), lambda i: (i, 0)),
            pl.BlockSpec((tm, 3 * D_MODEL), lambda i: (i, GATE_OFF // (3 * D_MODEL))),
            pl.BlockSpec((tm, BRANCH), lambda i: (i, CONV_OFF // BRANCH)),
            pl.BlockSpec((tm, BRANCH), lambda i: (i, 0)),
            pl.BlockSpec((tm, BRANCH), lambda i: (i, 0)),
            _const_spec(w_branch.shape), _const_spec(w_out.shape),
        ],
        out_specs=pl.BlockSpec((tm, d), lambda i: (i, 0)),
        out_shape=jax.ShapeDtypeStruct((n, d), F32),
        scratch_shapes=[pltpu.VMEM((tm, d), BF16)],
        compiler_params=_params("arbitrary"),
        name="conv_merge_out",
    )(x, pn, pn, y_fox, y_swa, w_branch, w_out)


def _xattn_kernel(x_ref, g_ref, wq_ref, kT_ref, v_ref, wo_ref, o_ref):
    x = x_ref[...]
    q = jnp.dot(_rms(x, g_ref[...]).astype(BF16), wq_ref[...], preferred_element_type=F32).astype(BF16)
    heads = []
    for h in range(X_HEADS):
        sl = slice(h * X_HEAD_DIM, (h + 1) * X_HEAD_DIM)
        s = jnp.dot(q[:, sl], kT_ref[0, sl, :], preferred_element_type=F32)
        p = jnp.exp2(s - jnp.max(s, axis=-1, keepdims=True))
        denom = jnp.sum(p, axis=-1, keepdims=True)
        o = jnp.dot(p.astype(BF16), v_ref[0, :, sl], preferred_element_type=F32)
        heads.append((o / denom).astype(BF16))
    o_ref[...] = x + jnp.dot(jnp.concatenate(heads, axis=1), wo_ref[...], preferred_element_type=F32)


def _xattn(x, g, wq, kT, v, wo, seq, tm):
    n, d = x.shape
    tiles = seq // tm
    return pl.pallas_call(
        _xattn_kernel,
        grid=(n // tm,),
        in_specs=[
            pl.BlockSpec((tm, d), lambda i: (i, 0)),
            _const_spec((1, d)),
            _const_spec(wq.shape),
            pl.BlockSpec((1,) + kT.shape[1:], lambda i: (i // tiles, 0, 0)),
            pl.BlockSpec((1,) + v.shape[1:], lambda i: (i // tiles, 0, 0)),
            _const_spec(wo.shape),
        ],
        out_specs=pl.BlockSpec((tm, d), lambda i: (i, 0)),
        out_shape=jax.ShapeDtypeStruct((n, d), F32),
        compiler_params=_params("arbitrary"),
        name="cross_attention",
    )(x, g.reshape(1, d), wq, kT, v, wo)


def _swiglu_kernel(x_ref, g_ref, wg_ref, wu_ref, wd_ref, gf_ref, o_ref, *, final_norm):
    x = x_ref[...]
    xn = _rms(x, g_ref[...]).astype(BF16)
    gate = jnp.dot(xn, wg_ref[...], preferred_element_type=F32)
    up = jnp.dot(xn, wu_ref[...], preferred_element_type=F32)
    hidden = (gate * jax.nn.sigmoid(gate) * up).astype(BF16)
    y = x + jnp.dot(hidden, wd_ref[...], preferred_element_type=F32)
    o_ref[...] = _rms(y, gf_ref[...]) if final_norm else y


def _swiglu(x, g, wg, wu, wd, g_final, final_norm, tm):
    n, d = x.shape
    return pl.pallas_call(
        functools.partial(_swiglu_kernel, final_norm=final_norm),
        grid=(n // tm,),
        in_specs=[
            pl.BlockSpec((tm, d), lambda i: (i, 0)),
            _const_spec((1, d)),
            _const_spec(wg.shape), _const_spec(wu.shape), _const_spec(wd.shape),
            _const_spec((1, d)),
        ],
        out_specs=pl.BlockSpec((tm, d), lambda i: (i, 0)),
        out_shape=jax.ShapeDtypeStruct((n, d), F32),
        compiler_params=_params("arbitrary"),
        name="swiglu",
    )(x, g.reshape(1, d), wg, wu, wd, g_final.reshape(1, d))


def _pack_in_proj(w_in):
    sizes = [BRANCH] * 3 + [BRANCH] * 3 + [FOX_HEADS] + [BRANCH, SWA_KV * HEAD_DIM, SWA_KV * HEAD_DIM] + [3 * D_MODEL]
    offs = np.concatenate([[0], np.cumsum(sizes)])
    c_b, c_c, c_u, f_q, f_k, f_v, f_g, s_q, s_k, s_v, gates = (w_in[:, offs[t]:offs[t + 1]] for t in range(len(sizes)))
    qscale = HEAD_DIM ** -0.5 * LOG2E
    wn = jnp.concatenate([gates, f_k, s_k], axis=1)
    wc = jnp.concatenate([c_b, c_c, c_u], axis=1)
    wt = jnp.concatenate([f_q * qscale, f_v, s_q * qscale, s_v], axis=1).T
    wfg = jnp.pad(f_g, ((0, 0), (0, LANES - FOX_HEADS)))
    return wn.astype(BF16), wc.astype(BF16), wt.astype(BF16), wfg.astype(BF16)


def kernel(x, mem, mix_norm_g, w_in, forget_bias, conv_w, sink, w_branch, w_mix_out, rel_bias,
           xattn_norm_g, mem_norm_g, w_xq, w_xkv, w_xo, ffn_norm_g, w_ffn_gate, w_ffn_up, w_ffn_down,
           final_norm_g):
    b, s, d = x.shape
    n = b * s
    depth = w_in.shape[0]
    assert d == D_MODEL and s % FOX_TQ == 0 and s % PREP_TILE == 0 and s % ROW_TILE == 0
    assert s % (SWA_BLOCKS_PER_STEP * WINDOW) == 0 and SWA_BLOCKS_PER_STEP % 2 == 0
    mem_len = mem.shape[1]
    xf = x.reshape(n, d)
    memf = mem.reshape(b * mem_len, d)
    swa_bias = _swa_bias_table(rel_bias)
    for l in range(depth):
        wn, wc, wt, wfg = _pack_in_proj(w_in[l])
        pn, pt, fg = _in_proj(xf, mix_norm_g[l], wn, wc, wt, wfg, conv_w[l], seq=s, tm=ROW_TILE)

        k_aug = _fox_prep(fg, forget_bias[l], pn, b, s, ts=PREP_TILE)
        y_fox = _fox_attention(pt, k_aug, b, s, tq=FOX_TQ, hp=FOX_HEADS_PER_STEP)
        y_swa = _swa_attention(sink[l], pt, pn, swa_bias, b, s, nblk=SWA_BLOCKS_PER_STEP)
        xf = _merge(xf, pn, y_fox, y_swa, w_branch[l].astype(BF16), w_mix_out[l].astype(BF16), tm=ROW_TILE)

        kv = _norm_proj(memf, mem_norm_g[l], w_xkv[l].astype(BF16))
        kT = kv[:, :d].reshape(b, mem_len, d).transpose(0, 2, 1)
        v = kv[:, d:].reshape(b, mem_len, d)
        wq = (w_xq[l] * (X_HEAD_DIM ** -0.5 * LOG2E)).astype(BF16)
        xf = _xattn(xf, xattn_norm_g[l], wq, kT, v, w_xo[l].astype(BF16), seq=s, tm=ROW_TILE)

        xf = _swiglu(xf, ffn_norm_g[l], w_ffn_gate[l].astype(BF16), w_ffn_up[l].astype(BF16),
                     w_ffn_down[l].astype(BF16), final_norm_g, final_norm=(l == depth - 1), tm=ROW_TILE)
    return xf.reshape(b, s, d)
```

```python
import functools
import math

import jax
import jax.numpy as jnp
import numpy as np
from jax import lax
from jax.experimental import pallas as pl
from jax.experimental.pallas import tpu as pltpu

F32 = jnp.float32
BF16 = jnp.bfloat16

D_MODEL = 1024
HEAD_DIM = 64
BRANCH = 512
FOX_HEADS = 8
SWA_HEADS = 8
SWA_KV = 2
SWA_GROUP = SWA_HEADS // SWA_KV
WINDOW = 128
N_BUCKETS = 32
X_HEADS = 4
X_HEAD_DIM = D_MODEL // X_HEADS
RMS_EPS = 1e-6
NEG = -1e30
LOG2E = math.log2(math.e)

LANES = 128
SUBLANES = 8
VMEM_LIMIT_BYTES = 56 * 1024 * 1024
PITCH_PAD = LANES
MERGE_CHUNK = 256

ROW_TILE = 512
FOX_TQ = 1024
FOX_HEADS_PER_STEP = 4
SWA_BLOCKS_PER_STEP = 16

GATE_OFF = 0
CONV_OFF = 3 * D_MODEL
SWAK_OFF = CONV_OFF + BRANCH
PN_COLS = SWAK_OFF + SWA_KV * HEAD_DIM
FOXQ_ROW = 0
FOXV_ROW = BRANCH
SWAQ_ROW = 2 * BRANCH
SWAV_ROW = 3 * BRANCH
PT_ROWS = SWAV_ROW + SWA_KV * HEAD_DIM

AUG_STRIDE = 8
AUG_EVEN = (HEAD_DIM, HEAD_DIM + AUG_STRIDE, HEAD_DIM + 2 * AUG_STRIDE)
AUG_ODD = (0, AUG_STRIDE, 2 * AUG_STRIDE)
V_ROWS = 80


def _params(*sem, flags=None):
    return pltpu.CompilerParams(dimension_semantics=sem, vmem_limit_bytes=VMEM_LIMIT_BYTES, flags=flags)


def _rms(x, g):
    return x * lax.rsqrt(jnp.mean(x * x, axis=-1, keepdims=True) + RMS_EPS) * g


def _const_spec(shape):
    nd = len(shape)
    return pl.BlockSpec(shape, lambda *_: (0,) * nd, pipeline_mode=pl.Buffered(1))


def _one_hot_rows(shape, axis, positions):
    idx = lax.broadcasted_iota(jnp.int32, shape, axis)
    hit = idx == positions[0]
    for p in positions[1:]:
        hit = hit | (idx == p)
    return jnp.where(hit, 1.0, 0.0)


def _in_proj_kernel(x_ref, g_ref, wn_ref, wc_ref, wk_ref, wt_ref, wfg_ref, cw_ref, fb_ref,
                    pn_ref, pt_ref, ka_ref, z_tail, c_tail, *, tiles_per_seq):
    tm = x_ref.shape[0]
    first_of_seq = pl.program_id(0) % tiles_per_seq == 0

    @pl.when(pl.program_id(0) == 0)
    def _():
        z_tail[...] = jnp.zeros(z_tail.shape, F32)
        c_tail[...] = jnp.zeros(c_tail.shape, F32)

    h = _rms(x_ref[...], g_ref[...]).astype(BF16)

    fg = jnp.dot(h, wfg_ref[...], preferred_element_type=F32)
    k_fox = jnp.dot(h, wk_ref[...], preferred_element_type=F32)
    lane = lax.broadcasted_iota(jnp.int32, (tm, LANES), 1)
    trow = lax.broadcasted_iota(jnp.int32, (tm, LANES), 0)
    c = jnp.where(lane < FOX_HEADS, jax.nn.log_sigmoid(fg + fb_ref[...]) * LOG2E, 0.0)
    d = 1
    while d < tm:
        c = c + jnp.where(trow >= d, pltpu.roll(c, d, axis=0), 0.0)
        d *= 2
    c = c + jnp.where(first_of_seq, 0.0, c_tail[0:1, :])
    c_tail[0:1, :] = c[tm - 1:tm, :]
    c1 = c.astype(BF16).astype(F32)
    r1 = c - c1
    c2 = r1.astype(BF16).astype(F32)
    c3 = (r1 - c2).astype(BF16).astype(F32)
    packed = -jnp.where(lane < AUG_STRIDE, c1,
                        jnp.where(lane < 2 * AUG_STRIDE, pltpu.roll(c2, AUG_STRIDE, axis=1),
                                  pltpu.roll(c3, 2 * AUG_STRIDE, axis=1)))
    keep_even = _one_hot_rows((1, LANES), 1, AUG_EVEN)
    keep_odd = _one_hot_rows((1, LANES), 1, AUG_ODD)
    for hd in range(FOX_HEADS):
        k_pair = k_fox[:, (hd // 2) * LANES:(hd // 2 + 1) * LANES]
        if hd % 2 == 0:
            aug = pltpu.roll(packed, (AUG_EVEN[0] - hd) % LANES, axis=1) * keep_even
            out = jnp.where(lane < HEAD_DIM, k_pair, aug)
        else:
            aug = pltpu.roll(packed, (AUG_ODD[0] - hd) % LANES, axis=1) * keep_odd
            out = jnp.where(lane >= HEAD_DIM, k_pair, aug)
        ka_ref[0, hd] = out.astype(BF16)

    conv_b, conv_c, conv_u = (jnp.dot(h, wc_ref[:, t * BRANCH:(t + 1) * BRANCH], preferred_element_type=F32)
                              for t in range(3))
    z = conv_c * conv_u
    zh = jnp.where(first_of_seq, 0.0, z_tail[...])
    z_tail[...] = z[tm - SUBLANES:tm, :]
    row = lax.broadcasted_iota(jnp.int32, z.shape, 0)
    z1 = jnp.where(row == 0, zh[7:8], pltpu.roll(z, 1, axis=0))
    z2 = jnp.where(row == 0, zh[6:7], jnp.where(row == 1, zh[7:8], pltpu.roll(z, 2, axis=0)))
    cw = cw_ref[...]
    pn_ref[:, CONV_OFF:SWAK_OFF] = (conv_b * (cw[0:1] * z2 + cw[1:2] * z1 + cw[2:3] * z)).astype(BF16)

    pt_ref[...] = lax.dot_general(wt_ref[...], h, (((1,), (1,)), ((), ())),
                                  preferred_element_type=F32).astype(BF16)
    for lo in range(0, CONV_OFF, D_MODEL):
        pn_ref[:, lo:lo + D_MODEL] = jnp.dot(h, wn_ref[:, lo:lo + D_MODEL], preferred_element_type=F32).astype(BF16)
    pn_ref[:, SWAK_OFF:PN_COLS] = jnp.dot(h, wn_ref[:, CONV_OFF:], preferred_element_type=F32).astype(BF16)


def _in_proj(x, g, wn, wc, wk, wt, wfg, conv_w, forget_bias, seq, tm):
    n, d = x.shape
    tiles = seq // tm
    fb = jnp.pad(forget_bias, (0, LANES - FOX_HEADS)).reshape(1, LANES)
    return pl.pallas_call(
        functools.partial(_in_proj_kernel, tiles_per_seq=tiles),
        grid=(n // tm,),
        in_specs=[
            pl.BlockSpec((tm, d), lambda i: (i, 0)),
            _const_spec((1, d)), _const_spec(wn.shape), _const_spec(wc.shape), _const_spec(wk.shape),
            _const_spec(wt.shape), _const_spec(wfg.shape), _const_spec(conv_w.shape), _const_spec((1, LANES)),
        ],
        out_specs=[
            pl.BlockSpec((tm, PN_COLS), lambda i: (i, 0)),
            pl.BlockSpec((PT_ROWS, tm), lambda i: (0, i)),
            pl.BlockSpec((1, FOX_HEADS, tm, LANES), lambda i: (i // tiles, 0, i % tiles, 0)),
        ],
        out_shape=[jax.ShapeDtypeStruct((n, PN_COLS), BF16), jax.ShapeDtypeStruct((PT_ROWS, n), BF16),
                   jax.ShapeDtypeStruct((n // seq, FOX_HEADS, seq, LANES), BF16)],
        scratch_shapes=[pltpu.VMEM((SUBLANES, BRANCH), F32), pltpu.VMEM((SUBLANES, LANES), F32)],
        compiler_params=_params("arbitrary"),
        name="in_proj",
    )(x, g.reshape(1, d), wn, wc, wk, wt, wfg, conv_w, fb)


def _norm_proj_kernel(x_ref, g_ref, w_ref, o_ref):
    h = _rms(x_ref[...], g_ref[...]).astype(BF16)
    o_ref[...] = jnp.dot(h, w_ref[...], preferred_element_type=F32).astype(BF16)


def _norm_proj(x, g, w):
    n, d = x.shape
    cols = w.shape[1]
    return pl.pallas_call(
        _norm_proj_kernel,
        grid=(1,),
        in_specs=[_const_spec((n, d)), _const_spec((1, d)), _const_spec(w.shape)],
        out_specs=pl.BlockSpec((n, cols), lambda i: (0, 0)),
        out_shape=jax.ShapeDtypeStruct((n, cols), BF16),
        compiler_params=_params("arbitrary"),
        name="mem_proj",
    )(x, g.reshape(1, d), w)


def _fox_kernel(q_ref, qn_ref, k_ref, v_ref, o_ref, q_aug, v_aug, s_buf, mt_buf, p_buf, acc_ref, *, tq, tk, hp):
    i = pl.program_id(2)
    seq = v_ref.shape[1]
    cur = i % 2
    heads = tuple(range(hp))

    def build_queries(src_ref, qslot):
        qrow = lax.broadcasted_iota(jnp.int32, (LANES, tq), 0)
        for pr in range(hp // 2):
            qb = src_ref[pr * LANES:(pr + 1) * LANES, :].astype(F32)
            q_aug[qslot, 2 * pr] = jnp.where(qrow < HEAD_DIM, qb, _one_hot_rows((LANES, tq), 0, AUG_EVEN)).astype(BF16)
            q_aug[qslot, 2 * pr + 1] = jnp.where(qrow >= HEAD_DIM, qb,
                                                 _one_hot_rows((LANES, tq), 0, AUG_ODD)).astype(BF16)

    def scores(t, slot, qslot):
        off = pl.multiple_of(t * tk, tk)
        for hh in heads:
            s = jnp.dot(k_ref[0, hh, pl.ds(off, tk), :], q_aug[qslot, hh], preferred_element_type=F32)
            s_buf[hh, slot, :, 0:tq] = s
            mt_buf[hh, slot, 0:1, :] = jnp.max(s, axis=0, keepdims=True)

    def softmax(slot, ms, mask=None):
        out = []
        for hh in heads:
            s = s_buf[hh, slot, :, 0:tq]
            if mask is None:
                tile_max = mt_buf[hh, slot, 0:1, :]
            else:
                s = jnp.where(mask, s, NEG)
                tile_max = jnp.max(s, axis=0, keepdims=True)
            m_new = jnp.maximum(ms[hh], tile_max)
            p_buf[hh, slot, :, 0:tq] = jnp.exp2(s - m_new).astype(BF16)
            out.append((m_new, jnp.exp2(ms[hh] - m_new)))
        return tuple(o[0] for o in out), tuple(o[1] for o in out)

    def values(t, slot, alphas):
        off = pl.multiple_of(jnp.maximum(t, 0) * tk, tk)
        for hh in heads:
            pv = jnp.dot(v_aug[hh, :, pl.ds(off, tk)], p_buf[hh, slot, :, 0:tq], preferred_element_type=F32)
            acc_ref[hh] = alphas[hh] * acc_ref[hh] + pv

    @pl.when(i == 0)
    def _():
        ones_row = _one_hot_rows((V_ROWS - HEAD_DIM, seq), 0, (0,)).astype(BF16)
        for hh in heads:
            v_aug[hh, 0:HEAD_DIM, :] = v_ref[hh * HEAD_DIM:(hh + 1) * HEAD_DIM, :]
            v_aug[hh, HEAD_DIM:V_ROWS, :] = ones_row
            p_buf[hh, 1, :, 0:tq] = jnp.zeros((tk, tq), BF16)
            acc_ref[hh] = jnp.zeros(acc_ref.shape[1:], F32)
        build_queries(q_ref, 0)
        scores(0, 0, 0)

    def pair(u, carry):
        ms, alpha_prev = carry
        t = 2 * u
        scores(t + 1, 1, cur)
        ms, alpha0 = softmax(0, ms)
        values(t - 1, 1, alpha_prev)
        scores(t + 2, 0, cur)
        ms, alpha1 = softmax(1, ms)
        values(t, 0, alpha0)
        return ms, alpha1

    carry = (tuple(jnp.full((1, tq), NEG, F32) for _ in heads), tuple(jnp.ones((1, tq), F32) for _ in heads))
    ms, alpha_prev = lax.fori_loop(0, i, pair, carry)

    t = 2 * i
    key = lax.broadcasted_iota(jnp.int32, (tk, tq), 0)
    qry = lax.broadcasted_iota(jnp.int32, (tk, tq), 1)
    off_b = pl.multiple_of((t + 1) * tk, tk)
    tri = lax.broadcasted_iota(jnp.int32, (tk, tk), 0) <= lax.broadcasted_iota(jnp.int32, (tk, tk), 1)
    s_b = [jnp.where(tri, jnp.dot(k_ref[0, hh, pl.ds(off_b, tk), :], q_aug[cur, hh, :, tk:],
                                  preferred_element_type=F32), NEG) for hh in heads]
    ms, alpha0 = softmax(0, ms, key <= qry)
    build_queries(qn_ref, 1 - cur)
    scores(0, 0, 1 - cur)
    values(t - 1, 1, alpha_prev)
    values(t, 0, alpha0)
    outs = []
    for hh in heads:
        mt_buf[hh, 1, 0:1, :] = ms[hh]
        m_old = mt_buf[hh, 1, 0:1, tk:]
        m_new = jnp.maximum(m_old, jnp.max(s_b[hh], axis=0, keepdims=True))
        p_b = jnp.exp2(s_b[hh] - m_new).astype(BF16)
        pv = jnp.dot(v_aug[hh, :, pl.ds(off_b, tk)], p_b, preferred_element_type=F32)
        acc_ref[hh, :, tk:] = jnp.exp2(m_old - m_new) * acc_ref[hh, :, tk:] + pv
        acc = acc_ref[hh]
        outs.append(acc[:HEAD_DIM] / acc[HEAD_DIM:HEAD_DIM + 1])
    o_ref[...] = jnp.concatenate(outs, axis=0).T.astype(BF16)


def _fox_attention(pt, k_aug, b, s, tq, hp):
    tk = tq // 2
    per = s // tq
    rows = hp * HEAD_DIM
    qblk = FOXQ_ROW // rows
    vblk = FOXV_ROW // rows
    return pl.pallas_call(
        functools.partial(_fox_kernel, tq=tq, tk=tk, hp=hp),
        grid=(b, FOX_HEADS // hp, per),
        in_specs=[
            pl.BlockSpec((rows, tq), lambda bi, g, i: (qblk + g, bi * per + i)),
            pl.BlockSpec((rows, tq), lambda bi, g, i: (qblk + g, bi * per + jnp.minimum(i + 1, per - 1))),
            pl.BlockSpec((1, hp, s, LANES), lambda bi, g, i: (bi, g, 0, 0), pipeline_mode=pl.Buffered(1)),
            pl.BlockSpec((rows, s), lambda bi, g, i: (vblk + g, bi), pipeline_mode=pl.Buffered(1)),
        ],
        out_specs=pl.BlockSpec((tq, rows), lambda bi, g, i: (bi * per + i, g)),
        out_shape=jax.ShapeDtypeStruct((b * s, BRANCH), BF16),
        scratch_shapes=[
            pltpu.VMEM((2, hp, LANES, tq), BF16),
            pltpu.VMEM((hp, V_ROWS, s), BF16),
            pltpu.VMEM((hp, 2, tk, tq + PITCH_PAD), F32),
            pltpu.VMEM((hp, 2, SUBLANES, tq), F32),
            pltpu.VMEM((hp, 2, tk, tq + PITCH_PAD), BF16),
            pltpu.VMEM((hp, V_ROWS, tq), F32),
        ],
        compiler_params=_params("arbitrary", "arbitrary", "arbitrary"),
        name="fox_attention",
    )(pt, pt, k_aug, pt)


def _swa_bias_kernel(rb_ref, bucket_ref, win_ref, o_ref):
    h = pl.program_id(0)
    bucket = bucket_ref[...]
    bias = jnp.zeros(bucket.shape, F32)
    for b in range(N_BUCKETS):
        bias = jnp.where(bucket == b, rb_ref[b, h] * LOG2E, bias)
    o_ref[0, 0] = jnp.where(win_ref[0] > 0, bias, NEG)
    o_ref[1, 0] = jnp.where(win_ref[1] > 0, bias, NEG)


def _swa_bias_table(rel_bias):
    tq = np.arange(WINDOW)[None, :]
    sk = np.arange(2 * WINDOW)[:, None]
    dist = WINDOW + tq - sk
    n = jnp.maximum(jnp.asarray(dist, jnp.int32), 0)
    max_exact = N_BUCKETS // 2
    large = max_exact + (jnp.log(jnp.maximum(n, 1).astype(F32) / max_exact)
                         / math.log(WINDOW / max_exact) * (N_BUCKETS - max_exact)).astype(jnp.int32)
    bucket = jnp.where(n < max_exact, n, jnp.minimum(large, N_BUCKETS - 1))
    in_window = (dist >= 0) & (dist < WINDOW)
    win = np.stack([in_window, in_window & (sk >= WINDOW)]).astype(np.int32)
    full = lambda shape: pl.BlockSpec(shape, lambda h: (0,) * len(shape))
    return pl.pallas_call(
        _swa_bias_kernel,
        grid=(SWA_HEADS,),
        in_specs=[pl.BlockSpec(memory_space=pltpu.SMEM), full(bucket.shape), full(win.shape)],
        out_specs=pl.BlockSpec((2, 1, 2 * WINDOW, WINDOW), lambda h: (0, h // SWA_GROUP, 0, h % SWA_GROUP)),
        out_shape=jax.ShapeDtypeStruct((2, SWA_KV, 2 * WINDOW, SWA_GROUP * WINDOW), F32),
        compiler_params=_params("arbitrary"),
        name="swa_bias_table",
    )(rel_bias, bucket, jnp.asarray(win))


def _swa_kernel(sink_ref, q_ref, kp_ref, kc_ref, vp_ref, vc_ref, bias_ref, o_ref,
                k_buf, v_buf, s_buf, mt_buf, p_buf, dn_buf, *, nblk):
    gw = SWA_GROUP * WINDOW
    ts = nblk * WINDOW
    first = pl.program_id(1) == 0
    k_buf[0:WINDOW, :] = kp_ref[...]
    k_buf[WINDOW:WINDOW + ts, :] = kc_ref[...]
    v_buf[:, 0:WINDOW] = vp_ref[...]
    v_buf[:, WINDOW:WINDOW + ts] = vc_ref[...]
    glane = lax.broadcasted_iota(jnp.int32, (1, gw), 1) // WINDOW
    zeros = jnp.zeros((HEAD_DIM, gw), BF16)
    sinks = []
    for kv in range(SWA_KV):
        sink = jnp.zeros((1, gw), F32)
        for g in range(SWA_GROUP):
            sink = jnp.where(glane == g, sink_ref[kv * SWA_GROUP + g] * LOG2E, sink)
        sinks.append(sink)

    def scores(blk, slot):
        lo = pl.multiple_of(blk * WINDOW, WINDOW)
        sel = jnp.logical_and(first, blk == 0).astype(jnp.int32)
        keys = k_buf[pl.ds(lo, 2 * WINDOW), :]
        for kv in range(SWA_KV):
            qg = jnp.concatenate([q_ref[(kv * SWA_GROUP + g) * HEAD_DIM:(kv * SWA_GROUP + g + 1) * HEAD_DIM,
                                        pl.ds(lo, WINDOW)] for g in range(SWA_GROUP)], axis=1)
            q_pad = jnp.concatenate([qg, zeros] if kv == 0 else [zeros, qg], axis=0)
            s = jnp.dot(keys, q_pad, preferred_element_type=F32) + bias_ref[sel, kv]
            s_buf[kv, slot, :, 0:gw] = s
            mt_buf[kv, slot, 0:1, :] = jnp.max(s, axis=0, keepdims=True)

    def softmax(slot):
        for kv in range(SWA_KV):
            m = jnp.maximum(mt_buf[kv, slot, 0:1, :], sinks[kv])
            p = jnp.exp2(s_buf[kv, slot, :, 0:gw] - m)
            dn_buf[kv, slot, 0:1, :] = jnp.sum(p, axis=0, keepdims=True) + jnp.exp2(sinks[kv] - m)
            p_buf[kv, slot, :, 0:gw] = p.astype(BF16)

    def values(blk, slot):
        lo = pl.multiple_of(blk * WINDOW, WINDOW)
        for kv in range(SWA_KV):
            vb = v_buf[kv * HEAD_DIM:(kv + 1) * HEAD_DIM, pl.ds(lo, 2 * WINDOW)]
            o = jnp.dot(vb, p_buf[kv, slot, :, 0:gw], preferred_element_type=F32) / dn_buf[kv, slot, 0:1, :]
            for pr in range(SWA_GROUP // 2):
                two = jnp.concatenate([o[:, (2 * pr) * WINDOW:(2 * pr + 1) * WINDOW],
                                       o[:, (2 * pr + 1) * WINDOW:(2 * pr + 2) * WINDOW]], axis=0)
                col = (kv * (SWA_GROUP // 2) + pr) * LANES
                o_ref[pl.ds(lo, WINDOW), col:col + LANES] = two.T.astype(BF16)

    scores(jnp.int32(0), 0)
    scores(jnp.int32(1), 1)
    softmax(0)

    def pair(u, carry):
        blk = 2 * u + 1
        scores(blk + 1, 0)
        softmax(1)
        values(blk - 1, 0)
        scores(blk + 2, 1)
        softmax(0)
        values(blk, 1)
        return carry

    lax.fori_loop(0, nblk // 2 - 1, pair, 0)
    softmax(1)
    values(jnp.int32(nblk - 2), 0)
    values(jnp.int32(nblk - 1), 1)


def _swa_attention(sink, pt, pn, bias, b, s, nblk):
    ts = nblk * WINDOW
    per = s // ts
    perw = s // WINDOW
    qblk = SWAQ_ROW // BRANCH
    vblk = SWAV_ROW // LANES
    kblk = SWAK_OFF // LANES
    prev = lambda bi, i: bi * perw + jnp.maximum(i * nblk - 1, 0)
    return pl.pallas_call(
        functools.partial(_swa_kernel, nblk=nblk),
        grid=(b, per),
        in_specs=[
            pl.BlockSpec(memory_space=pltpu.SMEM),
            pl.BlockSpec((BRANCH, ts), lambda bi, i: (qblk, bi * per + i)),
            pl.BlockSpec((WINDOW, LANES), lambda bi, i: (prev(bi, i), kblk)),
            pl.BlockSpec((ts, LANES), lambda bi, i: (bi * per + i, kblk)),
            pl.BlockSpec((LANES, WINDOW), lambda bi, i: (vblk, prev(bi, i))),
            pl.BlockSpec((LANES, ts), lambda bi, i: (vblk, bi * per + i)),
            _const_spec(bias.shape),
        ],
        out_specs=pl.BlockSpec((ts, BRANCH), lambda bi, i: (bi * per + i, 0)),
        out_shape=jax.ShapeDtypeStruct((b * s, BRANCH), BF16),
        scratch_shapes=[
            pltpu.VMEM((ts + WINDOW, LANES), BF16),
            pltpu.VMEM((LANES, ts + WINDOW), BF16),
            pltpu.VMEM((SWA_KV, 2, 2 * WINDOW, SWA_GROUP * WINDOW + PITCH_PAD), F32),
            pltpu.VMEM((SWA_KV, 2, SUBLANES, SWA_GROUP * WINDOW), F32),
            pltpu.VMEM((SWA_KV, 2, 2 * WINDOW, SWA_GROUP * WINDOW + PITCH_PAD), BF16),
            pltpu.VMEM((SWA_KV, 2, SUBLANES, SWA_GROUP * WINDOW), F32),
        ],
        compiler_params=_params("arbitrary", "arbitrary"),
        name="swa_attention",
    )(sink, pt, pn, pn, pt, pt, bias)


def _merge_kernel(x_ref, gate_ref, yc_ref, yf_ref, ys_ref, wb_ref, wo_ref, o_ref, m_scr):
    branches = (yc_ref[...], yf_ref[...], ys_ref[...])
    for lo in range(0, D_MODEL, MERGE_CHUNK):
        merged = None
        for b, y in enumerate(branches):
            gate = jax.nn.sigmoid(gate_ref[:, b * D_MODEL + lo:b * D_MODEL + lo + MERGE_CHUNK].astype(F32))
            term = gate * jnp.dot(y, wb_ref[b, :, lo:lo + MERGE_CHUNK], preferred_element_type=F32)
            merged = term if merged is None else merged + term
        m_scr[:, lo:lo + MERGE_CHUNK] = merged.astype(BF16)
    o_ref[...] = x_ref[...] + jnp.dot(m_scr[...], wo_ref[...], preferred_element_type=F32)


def _merge(x, pn, y_fox, y_swa, w_branch, w_out, tm):
    n, d = x.shape
    return pl.pallas_call(
        _merge_kernel,
        grid=(n // tm,),
        in_specs=[
            pl.BlockSpec((tm, d), lambda i: (i, 0)),
            pl.BlockSpec((tm, 3 * D_MODEL), lambda i: (i, GATE_OFF // (3 * D_MODEL))),
            pl.BlockSpec((tm, BRANCH), lambda i: (i, CONV_OFF // BRANCH)),
            pl.BlockSpec((tm, BRANCH), lambda i: (i, 0)),
            pl.BlockSpec((tm, BRANCH), lambda i: (i, 0)),
            _const_spec(w_branch.shape), _const_spec(w_out.shape),
        ],
        out_specs=pl.BlockSpec((tm, d), lambda i: (i, 0)),
        out_shape=jax.ShapeDtypeStruct((n, d), F32),
        scratch_shapes=[pltpu.VMEM((tm, d), BF16)],
        compiler_params=_params("arbitrary"),
        name="conv_merge_out",
    )(x, pn, pn, y_fox, y_swa, w_branch, w_out)


def _xattn_kernel(x_ref, g_ref, wq_ref, kT_ref, v_ref, wo_ref, o_ref):
    x = x_ref[...]
    q = jnp.dot(_rms(x, g_ref[...]).astype(BF16), wq_ref[...], preferred_element_type=F32).astype(BF16)
    heads = []
    for h in range(X_HEADS):
        sl = slice(h * X_HEAD_DIM, (h + 1) * X_HEAD_DIM)
        s = jnp.dot(q[:, sl], kT_ref[0, sl, :], preferred_element_type=F32)
        p = jnp.exp2(s - jnp.max(s, axis=-1, keepdims=True))
        denom = jnp.sum(p, axis=-1, keepdims=True)
        o = jnp.dot(p.astype(BF16), v_ref[0, :, sl], preferred_element_type=F32)
        heads.append((o / denom).astype(BF16))
    o_ref[...] = x + jnp.dot(jnp.concatenate(heads, axis=1), wo_ref[...], preferred_element_type=F32)


def _xattn(x, g, wq, kT, v, wo, seq, tm):
    n, d = x.shape
    tiles = seq // tm
    return pl.pallas_call(
        _xattn_kernel,
        grid=(n // tm,),
        in_specs=[
            pl.BlockSpec((tm, d), lambda i: (i, 0)),
            _const_spec((1, d)),
            _const_spec(wq.shape),
            pl.BlockSpec((1,) + kT.shape[1:], lambda i: (i // tiles, 0, 0)),
            pl.BlockSpec((1,) + v.shape[1:], lambda i: (i // tiles, 0, 0)),
            _const_spec(wo.shape),
        ],
        out_specs=pl.BlockSpec((tm, d), lambda i: (i, 0)),
        out_shape=jax.ShapeDtypeStruct((n, d), F32),
        compiler_params=_params("arbitrary"),
        name="cross_attention",
    )(x, g.reshape(1, d), wq, kT, v, wo)


def _swiglu_kernel(x_ref, g_ref, wg_ref, wu_ref, wd_ref, gf_ref, o_ref, *, final_norm):
    x = x_ref[...]
    xn = _rms(x, g_ref[...]).astype(BF16)
    gate = jnp.dot(xn, wg_ref[...], preferred_element_type=F32)
    up = jnp.dot(xn, wu_ref[...], preferred_element_type=F32)
    hidden = (gate * jax.nn.sigmoid(gate) * up).astype(BF16)
    y = x + jnp.dot(hidden, wd_ref[...], preferred_element_type=F32)
    o_ref[...] = _rms(y, gf_ref[...]) if final_norm else y


def _swiglu(x, g, wg, wu, wd, g_final, final_norm, tm):
    n, d = x.shape
    return pl.pallas_call(
        functools.partial(_swiglu_kernel, final_norm=final_norm),
        grid=(n // tm,),
        in_specs=[
            pl.BlockSpec((tm, d), lambda i: (i, 0)),
            _const_spec((1, d)),
            _const_spec(wg.shape), _const_spec(wu.shape), _const_spec(wd.shape),
            _const_spec((1, d)),
        ],
        out_specs=pl.BlockSpec((tm, d), lambda i: (i, 0)),
        out_shape=jax.ShapeDtypeStruct((n, d), F32),
        compiler_params=_params("arbitrary"),
        name="swiglu",
    )(x, g.reshape(1, d), wg, wu, wd, g_final.reshape(1, d))


def _pack_in_proj(w_in):
    sizes = [BRANCH] * 3 + [BRANCH] * 3 + [FOX_HEADS] + [BRANCH, SWA_KV * HEAD_DIM, SWA_KV * HEAD_DIM] + [3 * D_MODEL]
    offs = np.concatenate([[0], np.cumsum(sizes)])
    c_b, c_c, c_u, f_q, f_k, f_v, f_g, s_q, s_k, s_v, gates = (w_in[:, offs[t]:offs[t + 1]] for t in range(len(sizes)))
    qscale = HEAD_DIM ** -0.5 * LOG2E
    wn = jnp.concatenate([gates, s_k], axis=1)
    wc = jnp.concatenate([c_b, c_c, c_u], axis=1)
    wt = jnp.concatenate([f_q * qscale, f_v, s_q * qscale, s_v], axis=1).T
    wfg = jnp.pad(f_g, ((0, 0), (0, LANES - FOX_HEADS)))
    return wn.astype(BF16), wc.astype(BF16), f_k.astype(BF16), wt.astype(BF16), wfg.astype(BF16)


def kernel(x, mem, mix_norm_g, w_in, forget_bias, conv_w, sink, w_branch, w_mix_out, rel_bias,
           xattn_norm_g, mem_norm_g, w_xq, w_xkv, w_xo, ffn_norm_g, w_ffn_gate, w_ffn_up, w_ffn_down,
           final_norm_g):
    b, s, d = x.shape
    n = b * s
    depth = w_in.shape[0]
    assert d == D_MODEL and s % FOX_TQ == 0 and s % ROW_TILE == 0
    assert s % (SWA_BLOCKS_PER_STEP * WINDOW) == 0 and SWA_BLOCKS_PER_STEP % 2 == 0
    mem_len = mem.shape[1]
    xf = x.reshape(n, d)
    memf = mem.reshape(b * mem_len, d)
    swa_bias = _swa_bias_table(rel_bias)
    for l in range(depth):
        wn, wc, wk, wt, wfg = _pack_in_proj(w_in[l])
        pn, pt, k_aug = _in_proj(xf, mix_norm_g[l], wn, wc, wk, wt, wfg, conv_w[l], forget_bias[l],
                                 seq=s, tm=ROW_TILE)
        y_fox = _fox_attention(pt, k_aug, b, s, tq=FOX_TQ, hp=FOX_HEADS_PER_STEP)
        y_swa = _swa_attention(sink[l], pt, pn, swa_bias, b, s, nblk=SWA_BLOCKS_PER_STEP)
        xf = _merge(xf, pn, y_fox, y_swa, w_branch[l].astype(BF16), w_mix_out[l].astype(BF16), tm=ROW_TILE)

        kv = _norm_proj(memf, mem_norm_g[l], w_xkv[l].astype(BF16))
        kT = kv[:, :d].reshape(b, mem_len, d).transpose(0, 2, 1)
        v = kv[:, d:].reshape(b, mem_len, d)
        wq = (w_xq[l] * (X_HEAD_DIM ** -0.5 * LOG2E)).astype(BF16)
        xf = _xattn(xf, xattn_norm_g[l], wq, kT, v, w_xo[l].astype(BF16), seq=s, tm=ROW_TILE)

        xf = _swiglu(xf, ffn_norm_g[l], w_ffn_gate[l].astype(BF16), w_ffn_up[l].astype(BF16),
                     w_ffn_down[l].astype(BF16), final_norm_g, final_norm=(l == depth - 1), tm=ROW_TILE)
    return xf.reshape(b, s, d)
```

```python
import functools
import math

import jax
import jax.numpy as jnp
import numpy as np
from jax import lax
from jax.experimental import pallas as pl
from jax.experimental.pallas import tpu as pltpu

F32 = jnp.float32
BF16 = jnp.bfloat16

D_MODEL = 1024
HEAD_DIM = 64
BRANCH = 512
FOX_HEADS = 8
SWA_HEADS = 8
SWA_KV = 2
SWA_GROUP = SWA_HEADS // SWA_KV
WINDOW = 128
N_BUCKETS = 32
X_HEADS = 4
X_HEAD_DIM = D_MODEL // X_HEADS
RMS_EPS = 1e-6
NEG = -1e30
LOG2E = math.log2(math.e)

LANES = 128
SUBLANES = 8
VMEM_LIMIT_BYTES = 56 * 1024 * 1024
FOX_VMEM_LIMIT_BYTES = 58 * 1024 * 1024
PITCH_PAD = LANES
MERGE_CHUNK = 256
FFN_CHUNK = 256

ROW_TILE = 512
FOX_TQ = 1024
FOX_HEADS_PER_STEP = 4
SWA_BLOCKS_PER_STEP = 16

GATE_OFF = 0
CONV_OFF = 3 * D_MODEL
SWAK_OFF = CONV_OFF + BRANCH
PN_COLS = SWAK_OFF + SWA_KV * HEAD_DIM
FOXQ_ROW = 0
FOXV_ROW = BRANCH
SWAQ_ROW = 2 * BRANCH
SWAV_ROW = 3 * BRANCH
PT_ROWS = SWAV_ROW + SWA_KV * HEAD_DIM

AUG_STRIDE = 8
AUG_EVEN = (HEAD_DIM, HEAD_DIM + AUG_STRIDE, HEAD_DIM + 2 * AUG_STRIDE)
AUG_ODD = (0, AUG_STRIDE, 2 * AUG_STRIDE)
V_ROWS = 80


def _params(*sem, vmem_limit=VMEM_LIMIT_BYTES):
    return pltpu.CompilerParams(dimension_semantics=sem, vmem_limit_bytes=vmem_limit)


def _rms(x, g):
    return x * lax.rsqrt(jnp.mean(x * x, axis=-1, keepdims=True) + RMS_EPS) * g


def _const_spec(shape):
    nd = len(shape)
    return pl.BlockSpec(shape, lambda *_: (0,) * nd, pipeline_mode=pl.Buffered(1))


def _one_hot_rows(shape, axis, positions):
    idx = lax.broadcasted_iota(jnp.int32, shape, axis)
    hit = idx == positions[0]
    for p in positions[1:]:
        hit = hit | (idx == p)
    return jnp.where(hit, 1.0, 0.0)


def _in_proj_kernel(x_ref, g_ref, wn_ref, wc_ref, wk_ref, wt_ref, wfg_ref, cw_ref, fb_ref,
                    pn_ref, pt_ref, ka_ref, z_tail, c_tail, *, tiles_per_seq):
    tm = x_ref.shape[0]
    first_of_seq = pl.program_id(0) % tiles_per_seq == 0

    @pl.when(pl.program_id(0) == 0)
    def _():
        z_tail[...] = jnp.zeros(z_tail.shape, F32)
        c_tail[...] = jnp.zeros(c_tail.shape, F32)

    h = _rms(x_ref[...], g_ref[...]).astype(BF16)

    fg = jnp.dot(h, wfg_ref[...], preferred_element_type=F32)
    k_fox = jnp.dot(h, wk_ref[...], preferred_element_type=F32)
    lane = lax.broadcasted_iota(jnp.int32, (tm, LANES), 1)
    trow = lax.broadcasted_iota(jnp.int32, (tm, LANES), 0)
    c = jnp.where(lane < FOX_HEADS, jax.nn.log_sigmoid(fg + fb_ref[...]) * LOG2E, 0.0)
    d = 1
    while d < tm:
        c = c + jnp.where(trow >= d, pltpu.roll(c, d, axis=0), 0.0)
        d *= 2
    c = c + jnp.where(first_of_seq, 0.0, c_tail[0:1, :])
    c_tail[0:1, :] = c[tm - 1:tm, :]
    c1 = c.astype(BF16).astype(F32)
    r1 = c - c1
    c2 = r1.astype(BF16).astype(F32)
    c3 = (r1 - c2).astype(BF16).astype(F32)
    packed = -jnp.where(lane < AUG_STRIDE, c1,
                        jnp.where(lane < 2 * AUG_STRIDE, pltpu.roll(c2, AUG_STRIDE, axis=1),
                                  pltpu.roll(c3, 2 * AUG_STRIDE, axis=1)))
    keep_even = _one_hot_rows((1, LANES), 1, AUG_EVEN)
    keep_odd = _one_hot_rows((1, LANES), 1, AUG_ODD)
    for hd in range(FOX_HEADS):
        k_pair = k_fox[:, (hd // 2) * LANES:(hd // 2 + 1) * LANES]
        if hd % 2 == 0:
            aug = pltpu.roll(packed, (AUG_EVEN[0] - hd) % LANES, axis=1) * keep_even
            out = jnp.where(lane < HEAD_DIM, k_pair, aug)
        else:
            aug = pltpu.roll(packed, (AUG_ODD[0] - hd) % LANES, axis=1) * keep_odd
            out = jnp.where(lane >= HEAD_DIM, k_pair, aug)
        ka_ref[0, hd] = out.astype(BF16)

    conv_b, conv_c, conv_u = (jnp.dot(h, wc_ref[:, t * BRANCH:(t + 1) * BRANCH], preferred_element_type=F32)
                              for t in range(3))
    z = conv_c * conv_u
    zh = jnp.where(first_of_seq, 0.0, z_tail[...])
    z_tail[...] = z[tm - SUBLANES:tm, :]
    row = lax.broadcasted_iota(jnp.int32, z.shape, 0)
    z1 = jnp.where(row == 0, zh[7:8], pltpu.roll(z, 1, axis=0))
    z2 = jnp.where(row == 0, zh[6:7], jnp.where(row == 1, zh[7:8], pltpu.roll(z, 2, axis=0)))
    cw = cw_ref[...]
    pn_ref[:, CONV_OFF:SWAK_OFF] = (conv_b * (cw[0:1] * z2 + cw[1:2] * z1 + cw[2:3] * z)).astype(BF16)

    pt_ref[...] = lax.dot_general(wt_ref[...], h, (((1,), (1,)), ((), ())),
                                  preferred_element_type=F32).astype(BF16)
    for lo in range(0, CONV_OFF, D_MODEL):
        pn_ref[:, lo:lo + D_MODEL] = jnp.dot(h, wn_ref[:, lo:lo + D_MODEL], preferred_element_type=F32).astype(BF16)
    pn_ref[:, SWAK_OFF:PN_COLS] = jnp.dot(h, wn_ref[:, CONV_OFF:], preferred_element_type=F32).astype(BF16)


def _in_proj(x, g, wn, wc, wk, wt, wfg, conv_w, forget_bias, seq, tm):
    n, d = x.shape
    tiles = seq // tm
    fb = jnp.pad(forget_bias, (0, LANES - FOX_HEADS)).reshape(1, LANES)
    return pl.pallas_call(
        functools.partial(_in_proj_kernel, tiles_per_seq=tiles),
        grid=(n // tm,),
        in_specs=[
            pl.BlockSpec((tm, d), lambda i: (i, 0)),
            _const_spec((1, d)), _const_spec(wn.shape), _const_spec(wc.shape), _const_spec(wk.shape),
            _const_spec(wt.shape), _const_spec(wfg.shape), _const_spec(conv_w.shape), _const_spec((1, LANES)),
        ],
        out_specs=[
            pl.BlockSpec((tm, PN_COLS), lambda i: (i, 0)),
            pl.BlockSpec((PT_ROWS, tm), lambda i: (0, i)),
            pl.BlockSpec((1, FOX_HEADS, tm, LANES), lambda i: (i // tiles, 0, i % tiles, 0)),
        ],
        out_shape=[jax.ShapeDtypeStruct((n, PN_COLS), BF16), jax.ShapeDtypeStruct((PT_ROWS, n), BF16),
                   jax.ShapeDtypeStruct((n // seq, FOX_HEADS, seq, LANES), BF16)],
        scratch_shapes=[pltpu.VMEM((SUBLANES, BRANCH), F32), pltpu.VMEM((SUBLANES, LANES), F32)],
        compiler_params=_params("arbitrary"),
        name="in_proj",
    )(x, g.reshape(1, d), wn, wc, wk, wt, wfg, conv_w, fb)


def _norm_proj_kernel(x_ref, g_ref, w_ref, o_ref):
    h = _rms(x_ref[...], g_ref[...]).astype(BF16)
    o_ref[...] = jnp.dot(h, w_ref[...], preferred_element_type=F32).astype(BF16)


def _norm_proj(x, g, w):
    n, d = x.shape
    cols = w.shape[1]
    return pl.pallas_call(
        _norm_proj_kernel,
        grid=(1,),
        in_specs=[_const_spec((n, d)), _const_spec((1, d)), _const_spec(w.shape)],
        out_specs=pl.BlockSpec((n, cols), lambda i: (0, 0)),
        out_shape=jax.ShapeDtypeStruct((n, cols), BF16),
        compiler_params=_params("arbitrary"),
        name="mem_proj",
    )(x, g.reshape(1, d), w)


def _fox_kernel(q_ref, qn_ref, k_ref, v_ref, o_ref, q_aug, v_aug, s_buf, mt_buf, p_buf, acc_ref, *, tq, tk, hp):
    i = pl.program_id(2)
    seq = v_ref.shape[1]
    cur = i % 2
    heads = tuple(range(hp))

    def build_queries(src_ref, qslot):
        qrow = lax.broadcasted_iota(jnp.int32, (LANES, tq), 0)
        for pr in range(hp // 2):
            qb = src_ref[pr * LANES:(pr + 1) * LANES, :].astype(F32)
            q_aug[qslot, 2 * pr] = jnp.where(qrow < HEAD_DIM, qb, _one_hot_rows((LANES, tq), 0, AUG_EVEN)).astype(BF16)
            q_aug[qslot, 2 * pr + 1] = jnp.where(qrow >= HEAD_DIM, qb,
                                                 _one_hot_rows((LANES, tq), 0, AUG_ODD)).astype(BF16)

    def scores(t, slot, qslot):
        off = pl.multiple_of(t * tk, tk)
        for hh in heads:
            s = jnp.dot(k_ref[0, hh, pl.ds(off, tk), :], q_aug[qslot, hh], preferred_element_type=F32)
            s_buf[hh, slot, :, 0:tq] = s
            mt_buf[hh, slot, 0:1, :] = jnp.max(s, axis=0, keepdims=True)

    def softmax(slot, ms, mask=None):
        out = []
        for hh in heads:
            s = s_buf[hh, slot, :, 0:tq]
            if mask is None:
                tile_max = mt_buf[hh, slot, 0:1, :]
            else:
                s = jnp.where(mask, s, NEG)
                tile_max = jnp.max(s, axis=0, keepdims=True)
            m_new = jnp.maximum(ms[hh], tile_max)
            p_buf[hh, slot, :, 0:tq] = jnp.exp2(s - m_new).astype(BF16)
            out.append((m_new, jnp.exp2(ms[hh] - m_new)))
        return tuple(o[0] for o in out), tuple(o[1] for o in out)

    def values(t, slot, alphas):
        off = pl.multiple_of(jnp.maximum(t, 0) * tk, tk)
        for hh in heads:
            pv = jnp.dot(v_aug[hh, :, pl.ds(off, tk)], p_buf[hh, slot, :, 0:tq], preferred_element_type=F32)
            acc_ref[hh] = alphas[hh] * acc_ref[hh] + pv

    @pl.when(i == 0)
    def _():
        ones_row = _one_hot_rows((V_ROWS - HEAD_DIM, seq), 0, (0,)).astype(BF16)
        for hh in heads:
            v_aug[hh, 0:HEAD_DIM, :] = v_ref[hh * HEAD_DIM:(hh + 1) * HEAD_DIM, :]
            v_aug[hh, HEAD_DIM:V_ROWS, :] = ones_row
            p_buf[hh, 1, :, 0:tq] = jnp.zeros((tk, tq), BF16)
            acc_ref[hh] = jnp.zeros(acc_ref.shape[1:], F32)
        build_queries(q_ref, 0)
        scores(0, 0, 0)

    def pair(u, carry):
        ms, alpha_prev = carry
        t = 2 * u
        scores(t + 1, 1, cur)
        ms, alpha0 = softmax(0, ms)
        values(t - 1, 1, alpha_prev)
        scores(t + 2, 0, cur)
        ms, alpha1 = softmax(1, ms)
        values(t, 0, alpha0)
        return ms, alpha1

    carry = (tuple(jnp.full((1, tq), NEG, F32) for _ in heads), tuple(jnp.ones((1, tq), F32) for _ in heads))
    ms, alpha_prev = lax.fori_loop(0, i, pair, carry)

    t = 2 * i
    key = lax.broadcasted_iota(jnp.int32, (tk, tq), 0)
    qry = lax.broadcasted_iota(jnp.int32, (tk, tq), 1)
    off_b = pl.multiple_of((t + 1) * tk, tk)
    tri = lax.broadcasted_iota(jnp.int32, (tk, tk), 0) <= lax.broadcasted_iota(jnp.int32, (tk, tk), 1)
    s_b = [jnp.where(tri, jnp.dot(k_ref[0, hh, pl.ds(off_b, tk), :], q_aug[cur, hh, :, tk:],
                                  preferred_element_type=F32), NEG) for hh in heads]
    ms, alpha0 = softmax(0, ms, key <= qry)
    build_queries(qn_ref, 1 - cur)
    scores(0, 0, 1 - cur)
    values(t - 1, 1, alpha_prev)
    values(t, 0, alpha0)
    outs = []
    for hh in heads:
        mt_buf[hh, 1, 0:1, :] = ms[hh]
        m_old = mt_buf[hh, 1, 0:1, tk:]
        m_new = jnp.maximum(m_old, jnp.max(s_b[hh], axis=0, keepdims=True))
        p_b = jnp.exp2(s_b[hh] - m_new).astype(BF16)
        pv = jnp.dot(v_aug[hh, :, pl.ds(off_b, tk)], p_b, preferred_element_type=F32)
        acc_ref[hh, :, tk:] = jnp.exp2(m_old - m_new) * acc_ref[hh, :, tk:] + pv
        acc = acc_ref[hh]
        outs.append(acc[:HEAD_DIM] / acc[HEAD_DIM:HEAD_DIM + 1])
    o_ref[...] = jnp.concatenate(outs, axis=0).T.astype(BF16)


def _fox_attention(pt, k_aug, b, s, tq, hp):
    tk = tq // 2
    per = s // tq
    rows = hp * HEAD_DIM
    qblk = FOXQ_ROW // rows
    vblk = FOXV_ROW // rows
    return pl.pallas_call(
        functools.partial(_fox_kernel, tq=tq, tk=tk, hp=hp),
        grid=(b, FOX_HEADS // hp, per),
        in_specs=[
            pl.BlockSpec((rows, tq), lambda bi, g, i: (qblk + g, bi * per + i)),
            pl.BlockSpec((rows, tq), lambda bi, g, i: (qblk + g, bi * per + jnp.minimum(i + 1, per - 1))),
            pl.BlockSpec((1, hp, s, LANES), lambda bi, g, i: (bi, g, 0, 0), pipeline_mode=pl.Buffered(1)),
            pl.BlockSpec((rows, s), lambda bi, g, i: (vblk + g, bi), pipeline_mode=pl.Buffered(1)),
        ],
        out_specs=pl.BlockSpec((tq, rows), lambda bi, g, i: (bi * per + i, g)),
        out_shape=jax.ShapeDtypeStruct((b * s, BRANCH), BF16),
        scratch_shapes=[
            pltpu.VMEM((2, hp, LANES, tq), BF16),
            pltpu.VMEM((hp, V_ROWS, s), BF16),
            pltpu.VMEM((hp, 2, tk, tq + PITCH_PAD), F32),
            pltpu.VMEM((hp, 2, SUBLANES, tq), F32),
            pltpu.VMEM((hp, 2, tk, tq + PITCH_PAD), BF16),
            pltpu.VMEM((hp, V_ROWS, tq), F32),
        ],
        compiler_params=_params("arbitrary", "arbitrary", "arbitrary", vmem_limit=FOX_VMEM_LIMIT_BYTES),
        name="fox_attention",
    )(pt, pt, k_aug, pt)


def _swa_bias_kernel(rb_ref, bucket_ref, win_ref, o_ref):
    h = pl.program_id(0)
    bucket = bucket_ref[...]
    bias = jnp.zeros(bucket.shape, F32)
    for b in range(N_BUCKETS):
        bias = jnp.where(bucket == b, rb_ref[b, h] * LOG2E, bias)
    o_ref[0, 0] = jnp.where(win_ref[0] > 0, bias, NEG)
    o_ref[1, 0] = jnp.where(win_ref[1] > 0, bias, NEG)


def _swa_bias_table(rel_bias):
    tq = np.arange(WINDOW)[None, :]
    sk = np.arange(2 * WINDOW)[:, None]
    dist = WINDOW + tq - sk
    n = jnp.maximum(jnp.asarray(dist, jnp.int32), 0)
    max_exact = N_BUCKETS // 2
    large = max_exact + (jnp.log(jnp.maximum(n, 1).astype(F32) / max_exact)
                         / math.log(WINDOW / max_exact) * (N_BUCKETS - max_exact)).astype(jnp.int32)
    bucket = jnp.where(n < max_exact, n, jnp.minimum(large, N_BUCKETS - 1))
    in_window = (dist >= 0) & (dist < WINDOW)
    win = np.stack([in_window, in_window & (sk >= WINDOW)]).astype(np.int32)
    full = lambda shape: pl.BlockSpec(shape, lambda h: (0,) * len(shape))
    return pl.pallas_call(
        _swa_bias_kernel,
        grid=(SWA_HEADS,),
        in_specs=[pl.BlockSpec(memory_space=pltpu.SMEM), full(bucket.shape), full(win.shape)],
        out_specs=pl.BlockSpec((2, 1, 2 * WINDOW, WINDOW), lambda h: (0, h // SWA_GROUP, 0, h % SWA_GROUP)),
        out_shape=jax.ShapeDtypeStruct((2, SWA_KV, 2 * WINDOW, SWA_GROUP * WINDOW), F32),
        compiler_params=_params("arbitrary"),
        name="swa_bias_table",
    )(rel_bias, bucket, jnp.asarray(win))


def _swa_kernel(sink_ref, q_ref, kp_ref, kc_ref, vp_ref, vc_ref, bias_ref, o_ref,
                k_buf, v_buf, s_buf, mt_buf, p_buf, dn_buf, *, nblk):
    gw = SWA_GROUP * WINDOW
    ts = nblk * WINDOW
    first = pl.program_id(1) == 0
    k_buf[0:WINDOW, :] = kp_ref[...]
    k_buf[WINDOW:WINDOW + ts, :] = kc_ref[...]
    v_buf[:, 0:WINDOW] = vp_ref[...]
    v_buf[:, WINDOW:WINDOW + ts] = vc_ref[...]
    glane = lax.broadcasted_iota(jnp.int32, (1, gw), 1) // WINDOW
    zeros = jnp.zeros((HEAD_DIM, gw), BF16)
    sinks = []
    for kv in range(SWA_KV):
        sink = jnp.zeros((1, gw), F32)
        for g in range(SWA_GROUP):
            sink = jnp.where(glane == g, sink_ref[kv * SWA_GROUP + g] * LOG2E, sink)
        sinks.append(sink)

    def scores(blk, slot):
        lo = pl.multiple_of(blk * WINDOW, WINDOW)
        sel = jnp.logical_and(first, blk == 0).astype(jnp.int32)
        keys = k_buf[pl.ds(lo, 2 * WINDOW), :]
        for kv in range(SWA_KV):
            qg = jnp.concatenate([q_ref[(kv * SWA_GROUP + g) * HEAD_DIM:(kv * SWA_GROUP + g + 1) * HEAD_DIM,
                                        pl.ds(lo, WINDOW)] for g in range(SWA_GROUP)], axis=1)
            q_pad = jnp.concatenate([qg, zeros] if kv == 0 else [zeros, qg], axis=0)
            s = jnp.dot(keys, q_pad, preferred_element_type=F32) + bias_ref[sel, kv]
            s_buf[kv, slot, :, 0:gw] = s
            mt_buf[kv, slot, 0:1, :] = jnp.max(s, axis=0, keepdims=True)

    def softmax(slot):
        for kv in range(SWA_KV):
            m = jnp.maximum(mt_buf[kv, slot, 0:1, :], sinks[kv])
            p = jnp.exp2(s_buf[kv, slot, :, 0:gw] - m)
            dn_buf[kv, slot, 0:1, :] = jnp.sum(p, axis=0, keepdims=True) + jnp.exp2(sinks[kv] - m)
            p_buf[kv, slot, :, 0:gw] = p.astype(BF16)

    def values(blk, slot):
        lo = pl.multiple_of(blk * WINDOW, WINDOW)
        for kv in range(SWA_KV):
            vb = v_buf[kv * HEAD_DIM:(kv + 1) * HEAD_DIM, pl.ds(lo, 2 * WINDOW)]
            o = jnp.dot(vb, p_buf[kv, slot, :, 0:gw], preferred_element_type=F32) / dn_buf[kv, slot, 0:1, :]
            for pr in range(SWA_GROUP // 2):
                two = jnp.concatenate([o[:, (2 * pr) * WINDOW:(2 * pr + 1) * WINDOW],
                                       o[:, (2 * pr + 1) * WINDOW:(2 * pr + 2) * WINDOW]], axis=0)
                col = (kv * (SWA_GROUP // 2) + pr) * LANES
                o_ref[pl.ds(lo, WINDOW), col:col + LANES] = two.T.astype(BF16)

    scores(jnp.int32(0), 0)
    scores(jnp.int32(1), 1)
    softmax(0)

    def pair(u, carry):
        blk = 2 * u + 1
        scores(blk + 1, 0)
        softmax(1)
        values(blk - 1, 0)
        scores(blk + 2, 1)
        softmax(0)
        values(blk, 1)
        return carry

    lax.fori_loop(0, nblk // 2 - 1, pair, 0)
    softmax(1)
    values(jnp.int32(nblk - 2), 0)
    values(jnp.int32(nblk - 1), 1)


def _swa_attention(sink, pt, pn, bias, b, s, nblk):
    ts = nblk * WINDOW
    per = s // ts
    perw = s // WINDOW
    qblk = SWAQ_ROW // BRANCH
    vblk = SWAV_ROW // LANES
    kblk = SWAK_OFF // LANES
    prev = lambda bi, i: bi * perw + jnp.maximum(i * nblk - 1, 0)
    return pl.pallas_call(
        functools.partial(_swa_kernel, nblk=nblk),
        grid=(b, per),
        in_specs=[
            pl.BlockSpec(memory_space=pltpu.SMEM),
            pl.BlockSpec((BRANCH, ts), lambda bi, i: (qblk, bi * per + i)),
            pl.BlockSpec((WINDOW, LANES), lambda bi, i: (prev(bi, i), kblk)),
            pl.BlockSpec((ts, LANES), lambda bi, i: (bi * per + i, kblk)),
            pl.BlockSpec((LANES, WINDOW), lambda bi, i: (vblk, prev(bi, i))),
            pl.BlockSpec((LANES, ts), lambda bi, i: (vblk, bi * per + i)),
            _const_spec(bias.shape),
        ],
        out_specs=pl.BlockSpec((ts, BRANCH), lambda bi, i: (bi * per + i, 0)),
        out_shape=jax.ShapeDtypeStruct((b * s, BRANCH), BF16),
        scratch_shapes=[
            pltpu.VMEM((ts + WINDOW, LANES), BF16),
            pltpu.VMEM((LANES, ts + WINDOW), BF16),
            pltpu.VMEM((SWA_KV, 2, 2 * WINDOW, SWA_GROUP * WINDOW + PITCH_PAD), F32),
            pltpu.VMEM((SWA_KV, 2, SUBLANES, SWA_GROUP * WINDOW), F32),
            pltpu.VMEM((SWA_KV, 2, 2 * WINDOW, SWA_GROUP * WINDOW + PITCH_PAD), BF16),
            pltpu.VMEM((SWA_KV, 2, SUBLANES, SWA_GROUP * WINDOW), F32),
        ],
        compiler_params=_params("arbitrary", "arbitrary"),
        name="swa_attention",
    )(sink, pt, pn, pn, pt, pt, bias)


def _merge_kernel(x_ref, gate_ref, yc_ref, yf_ref, ys_ref, wb_ref, wo_ref, o_ref, m_scr):
    branches = (yc_ref[...], yf_ref[...], ys_ref[...])
    for lo in range(0, D_MODEL, MERGE_CHUNK):
        gates = [jax.nn.sigmoid(gate_ref[:, b * D_MODEL + lo:b * D_MODEL + lo + MERGE_CHUNK].astype(F32))
                 for b in range(len(branches))]
        merged = None
        for b, y in enumerate(branches):
            term = gates[b] * jnp.dot(y, wb_ref[b, :, lo:lo + MERGE_CHUNK], preferred_element_type=F32)
            merged = term if merged is None else merged + term
        m_scr[:, lo:lo + MERGE_CHUNK] = merged.astype(BF16)
    o_ref[...] = x_ref[...] + jnp.dot(m_scr[...], wo_ref[...], preferred_element_type=F32)


def _merge(x, pn, y_fox, y_swa, w_branch, w_out, tm):
    n, d = x.shape
    return pl.pallas_call(
        _merge_kernel,
        grid=(n // tm,),
        in_specs=[
            pl.BlockSpec((tm, d), lambda i: (i, 0)),
            pl.BlockSpec((tm, 3 * D_MODEL), lambda i: (i, GATE_OFF // (3 * D_MODEL))),
            pl.BlockSpec((tm, BRANCH), lambda i: (i, CONV_OFF // BRANCH)),
            pl.BlockSpec((tm, BRANCH), lambda i: (i, 0)),
            pl.BlockSpec((tm, BRANCH), lambda i: (i, 0)),
            _const_spec(w_branch.shape), _const_spec(w_out.shape),
        ],
        out_specs=pl.BlockSpec((tm, d), lambda i: (i, 0)),
        out_shape=jax.ShapeDtypeStruct((n, d), F32),
        scratch_shapes=[pltpu.VMEM((tm, d), BF16)],
        compiler_params=_params("arbitrary"),
        name="conv_merge_out",
    )(x, pn, pn, y_fox, y_swa, w_branch, w_out)


def _xattn_kernel(x_ref, g_ref, wq_ref, kT_ref, v_ref, wo_ref, o_ref):
    x = x_ref[...]
    q = jnp.dot(_rms(x, g_ref[...]).astype(BF16), wq_ref[...], preferred_element_type=F32).astype(BF16)
    sls = [slice(h * X_HEAD_DIM, (h + 1) * X_HEAD_DIM) for h in range(X_HEADS)]
    scores = [jnp.dot(q[:, sl], kT_ref[0, sl, :], preferred_element_type=F32) for sl in sls]
    probs = [jnp.exp2(s - jnp.max(s, axis=-1, keepdims=True)) for s in scores]
    denoms = [jnp.sum(p, axis=-1, keepdims=True) for p in probs]
    outs = [jnp.dot(p.astype(BF16), v_ref[0, :, sl], preferred_element_type=F32) for p, sl in zip(probs, sls)]
    heads = [(o / d).astype(BF16) for o, d in zip(outs, denoms)]
    o_ref[...] = x + jnp.dot(jnp.concatenate(heads, axis=1), wo_ref[...], preferred_element_type=F32)


def _xattn(x, g, wq, kT, v, wo, seq, tm):
    n, d = x.shape
    tiles = seq // tm
    return pl.pallas_call(
        _xattn_kernel,
        grid=(n // tm,),
        in_specs=[
            pl.BlockSpec((tm, d), lambda i: (i, 0)),
            _const_spec((1, d)),
            _const_spec(wq.shape),
            pl.BlockSpec((1,) + kT.shape[1:], lambda i: (i // tiles, 0, 0)),
            pl.BlockSpec((1,) + v.shape[1:], lambda i: (i // tiles, 0, 0)),
            _const_spec(wo.shape),
        ],
        out_specs=pl.BlockSpec((tm, d), lambda i: (i, 0)),
        out_shape=jax.ShapeDtypeStruct((n, d), F32),
        compiler_params=_params("arbitrary"),
        name="cross_attention",
    )(x, g.reshape(1, d), wq, kT, v, wo)


def _swiglu_kernel(x_ref, g_ref, wg_ref, wu_ref, wd_ref, gf_ref, o_ref, *, final_norm):
    x = x_ref[...]
    xn = _rms(x, g_ref[...]).astype(BF16)
    d_ff = wg_ref.shape[1]
    y = x
    for lo in range(0, d_ff, FFN_CHUNK):
        gate = jnp.dot(xn, wg_ref[:, lo:lo + FFN_CHUNK], preferred_element_type=F32)
        up = jnp.dot(xn, wu_ref[:, lo:lo + FFN_CHUNK], preferred_element_type=F32)
        hidden = (gate * jax.nn.sigmoid(gate) * up).astype(BF16)
        y = y + jnp.dot(hidden, wd_ref[lo:lo + FFN_CHUNK, :], preferred_element_type=F32)
    o_ref[...] = _rms(y, gf_ref[...]) if final_norm else y


def _swiglu(x, g, wg, wu, wd, g_final, final_norm, tm):
    n, d = x.shape
    return pl.pallas_call(
        functools.partial(_swiglu_kernel, final_norm=final_norm),
        grid=(n // tm,),
        in_specs=[
            pl.BlockSpec((tm, d), lambda i: (i, 0)),
            _const_spec((1, d)),
            _const_spec(wg.shape), _const_spec(wu.shape), _const_spec(wd.shape),
            _const_spec((1, d)),
        ],
        out_specs=pl.BlockSpec((tm, d), lambda i: (i, 0)),
        out_shape=jax.ShapeDtypeStruct((n, d), F32),
        compiler_params=_params("arbitrary"),
        name="swiglu",
    )(x, g.reshape(1, d), wg, wu, wd, g_final.reshape(1, d))


def _pack_in_proj(w_in):
    sizes = [BRANCH] * 3 + [BRANCH] * 3 + [FOX_HEADS] + [BRANCH, SWA_KV * HEAD_DIM, SWA_KV * HEAD_DIM] + [3 * D_MODEL]
    offs = np.concatenate([[0], np.cumsum(sizes)])
    c_b, c_c, c_u, f_q, f_k, f_v, f_g, s_q, s_k, s_v, gates = (w_in[:, offs[t]:offs[t + 1]] for t in range(len(sizes)))
    qscale = HEAD_DIM ** -0.5 * LOG2E
    wn = jnp.concatenate([gates, s_k], axis=1)
    wc = jnp.concatenate([c_b, c_c, c_u], axis=1)
    wt = jnp.concatenate([f_q * qscale, f_v, s_q * qscale, s_v], axis=1).T
    wfg = jnp.pad(f_g, ((0, 0), (0, LANES - FOX_HEADS)))
    return wn.astype(BF16), wc.astype(BF16), f_k.astype(BF16), wt.astype(BF16), wfg.astype(BF16)


def kernel(x, mem, mix_norm_g, w_in, forget_bias, conv_w, sink, w_branch, w_mix_out, rel_bias,
           xattn_norm_g, mem_norm_g, w_xq, w_xkv, w_xo, ffn_norm_g, w_ffn_gate, w_ffn_up, w_ffn_down,
           final_norm_g):
    b, s, d = x.shape
    n = b * s
    depth = w_in.shape[0]
    assert d == D_MODEL and s % FOX_TQ == 0 and s % ROW_TILE == 0
    assert s % (SWA_BLOCKS_PER_STEP * WINDOW) == 0 and SWA_BLOCKS_PER_STEP % 2 == 0
    mem_len = mem.shape[1]
    xf = x.reshape(n, d)
    memf = mem.reshape(b * mem_len, d)
    swa_bias = _swa_bias_table(rel_bias)
    for l in range(depth):
        wn, wc, wk, wt, wfg = _pack_in_proj(w_in[l])
        pn, pt, k_aug = _in_proj(xf, mix_norm_g[l], wn, wc, wk, wt, wfg, conv_w[l], forget_bias[l],
                                 seq=s, tm=ROW_TILE)
        y_fox = _fox_attention(pt, k_aug, b, s, tq=FOX_TQ, hp=FOX_HEADS_PER_STEP)
        y_swa = _swa_attention(sink[l], pt, pn, swa_bias, b, s, nblk=SWA_BLOCKS_PER_STEP)
        xf = _merge(xf, pn, y_fox, y_swa, w_branch[l].astype(BF16), w_mix_out[l].astype(BF16), tm=ROW_TILE)

        kv = _norm_proj(memf, mem_norm_g[l], w_xkv[l].astype(BF16))
        kT = kv[:, :d].reshape(b, mem_len, d).transpose(0, 2, 1)
        v = kv[:, d:].reshape(b, mem_len, d)
        wq = (w_xq[l] * (X_HEAD_DIM ** -0.5 * LOG2E)).astype(BF16)
        xf = _xattn(xf, xattn_norm_g[l], wq, kT, v, w_xo[l].astype(BF16), seq=s, tm=ROW_TILE)

        xf = _swiglu(xf, ffn_norm_g[l], w_ffn_gate[l].astype(BF16), w_ffn_up[l].astype(BF16),
                     w_ffn_down[l].astype(BF16), final_norm_g, final_norm=(l == depth - 1), tm=ROW_TILE)
    return xf.reshape(b, s, d)
```

```python
import functools
import math

import jax
import jax.numpy as jnp
import numpy as np
from jax import lax
from jax.experimental import pallas as pl
from jax.experimental.pallas import tpu as pltpu

F32 = jnp.float32
BF16 = jnp.bfloat16

D_MODEL = 1024
HEAD_DIM = 64
BRANCH = 512
FOX_HEADS = 8
SWA_HEADS = 8
SWA_KV = 2
SWA_GROUP = SWA_HEADS // SWA_KV
WINDOW = 128
N_BUCKETS = 32
X_HEADS = 4
X_HEAD_DIM = D_MODEL // X_HEADS
RMS_EPS = 1e-6
NEG = -1e30
LOG2E = math.log2(math.e)

LANES = 128
SUBLANES = 8
VMEM_LIMIT_BYTES = 56 * 1024 * 1024
FOX_VMEM_LIMIT_BYTES = 58 * 1024 * 1024
PITCH_PAD = LANES
MERGE_CHUNK = 256
FFN_CHUNK = 256

ROW_TILE = 512
FOX_TQ = 1024
FOX_HEADS_PER_STEP = 4
SWA_BLOCKS_PER_STEP = 16

GATE_OFF = 0
CONV_OFF = 3 * D_MODEL
SWAK_OFF = CONV_OFF + BRANCH
PN_COLS = SWAK_OFF + SWA_KV * HEAD_DIM
FOXQ_ROW = 0
FOXV_ROW = BRANCH
SWAQ_ROW = 2 * BRANCH
SWAV_ROW = 3 * BRANCH
PT_ROWS = SWAV_ROW + SWA_KV * HEAD_DIM

AUG_STRIDE = 8
AUG_EVEN = (HEAD_DIM, HEAD_DIM + AUG_STRIDE, HEAD_DIM + 2 * AUG_STRIDE)
AUG_ODD = (0, AUG_STRIDE, 2 * AUG_STRIDE)
V_ROWS = 80


def _params(*sem, vmem_limit=VMEM_LIMIT_BYTES):
    return pltpu.CompilerParams(dimension_semantics=sem, vmem_limit_bytes=vmem_limit)


def _rms(x, g):
    return x * lax.rsqrt(jnp.mean(x * x, axis=-1, keepdims=True) + RMS_EPS) * g


def _const_spec(shape):
    nd = len(shape)
    return pl.BlockSpec(shape, lambda *_: (0,) * nd, pipeline_mode=pl.Buffered(1))


def _one_hot_rows(shape, axis, positions):
    idx = lax.broadcasted_iota(jnp.int32, shape, axis)
    hit = idx == positions[0]
    for p in positions[1:]:
        hit = hit | (idx == p)
    return jnp.where(hit, 1.0, 0.0)


def _in_proj_kernel(x_ref, g_ref, wn_ref, wc_ref, wk_ref, wt_ref, wfg_ref, cw_ref, fb_ref,
                    pn_ref, pt_ref, ka_ref, z_tail, c_tail, *, tiles_per_seq):
    tm = x_ref.shape[0]
    first_of_seq = pl.program_id(0) % tiles_per_seq == 0

    @pl.when(pl.program_id(0) == 0)
    def _():
        z_tail[...] = jnp.zeros(z_tail.shape, F32)
        c_tail[...] = jnp.zeros(c_tail.shape, F32)

    h = _rms(x_ref[...], g_ref[...]).astype(BF16)

    fg = jnp.dot(h, wfg_ref[...], preferred_element_type=F32)
    k_fox = jnp.dot(h, wk_ref[...], preferred_element_type=F32)
    lane = lax.broadcasted_iota(jnp.int32, (tm, LANES), 1)
    trow = lax.broadcasted_iota(jnp.int32, (tm, LANES), 0)
    c = jnp.where(lane < FOX_HEADS, jax.nn.log_sigmoid(fg + fb_ref[...]) * LOG2E, 0.0)
    d = 1
    while d < tm:
        c = c + jnp.where(trow >= d, pltpu.roll(c, d, axis=0), 0.0)
        d *= 2
    c = c + jnp.where(first_of_seq, 0.0, c_tail[0:1, :])
    c_tail[0:1, :] = c[tm - 1:tm, :]
    c1 = c.astype(BF16).astype(F32)
    r1 = c - c1
    c2 = r1.astype(BF16).astype(F32)
    c3 = (r1 - c2).astype(BF16).astype(F32)
    packed = -jnp.where(lane < AUG_STRIDE, c1,
                        jnp.where(lane < 2 * AUG_STRIDE, pltpu.roll(c2, AUG_STRIDE, axis=1),
                                  pltpu.roll(c3, 2 * AUG_STRIDE, axis=1)))
    keep_even = _one_hot_rows((1, LANES), 1, AUG_EVEN)
    keep_odd = _one_hot_rows((1, LANES), 1, AUG_ODD)
    for hd in range(FOX_HEADS):
        k_pair = k_fox[:, (hd // 2) * LANES:(hd // 2 + 1) * LANES]
        if hd % 2 == 0:
            aug = pltpu.roll(packed, (AUG_EVEN[0] - hd) % LANES, axis=1) * keep_even
            out = jnp.where(lane < HEAD_DIM, k_pair, aug)
        else:
            aug = pltpu.roll(packed, (AUG_ODD[0] - hd) % LANES, axis=1) * keep_odd
            out = jnp.where(lane >= HEAD_DIM, k_pair, aug)
        ka_ref[0, hd] = out.astype(BF16)

    conv_b, conv_c, conv_u = (jnp.dot(h, wc_ref[:, t * BRANCH:(t + 1) * BRANCH], preferred_element_type=F32)
                              for t in range(3))
    z = conv_c * conv_u
    zh = jnp.where(first_of_seq, 0.0, z_tail[...])
    z_tail[...] = z[tm - SUBLANES:tm, :]
    row = lax.broadcasted_iota(jnp.int32, z.shape, 0)
    z1 = jnp.where(row == 0, zh[7:8], pltpu.roll(z, 1, axis=0))
    z2 = jnp.where(row == 0, zh[6:7], jnp.where(row == 1, zh[7:8], pltpu.roll(z, 2, axis=0)))
    cw = cw_ref[...]
    pn_ref[:, CONV_OFF:SWAK_OFF] = (conv_b * (cw[0:1] * z2 + cw[1:2] * z1 + cw[2:3] * z)).astype(BF16)

    pt_ref[...] = lax.dot_general(wt_ref[...], h, (((1,), (1,)), ((), ())),
                                  preferred_element_type=F32).astype(BF16)
    for lo in range(0, CONV_OFF, D_MODEL):
        pn_ref[:, lo:lo + D_MODEL] = jnp.dot(h, wn_ref[:, lo:lo + D_MODEL], preferred_element_type=F32).astype(BF16)
    pn_ref[:, SWAK_OFF:PN_COLS] = jnp.dot(h, wn_ref[:, CONV_OFF:], preferred_element_type=F32).astype(BF16)


def _in_proj(x, g, wn, wc, wk, wt, wfg, conv_w, forget_bias, seq, tm):
    n, d = x.shape
    tiles = seq // tm
    fb = jnp.pad(forget_bias, (0, LANES - FOX_HEADS)).reshape(1, LANES)
    return pl.pallas_call(
        functools.partial(_in_proj_kernel, tiles_per_seq=tiles),
        grid=(n // tm,),
        in_specs=[
            pl.BlockSpec((tm, d), lambda i: (i, 0)),
            _const_spec((1, d)), _const_spec(wn.shape), _const_spec(wc.shape), _const_spec(wk.shape),
            _const_spec(wt.shape), _const_spec(wfg.shape), _const_spec(conv_w.shape), _const_spec((1, LANES)),
        ],
        out_specs=[
            pl.BlockSpec((tm, PN_COLS), lambda i: (i, 0)),
            pl.BlockSpec((PT_ROWS, tm), lambda i: (0, i)),
            pl.BlockSpec((1, FOX_HEADS, tm, LANES), lambda i: (i // tiles, 0, i % tiles, 0)),
        ],
        out_shape=[jax.ShapeDtypeStruct((n, PN_COLS), BF16), jax.ShapeDtypeStruct((PT_ROWS, n), BF16),
                   jax.ShapeDtypeStruct((n // seq, FOX_HEADS, seq, LANES), BF16)],
        scratch_shapes=[pltpu.VMEM((SUBLANES, BRANCH), F32), pltpu.VMEM((SUBLANES, LANES), F32)],
        compiler_params=_params("arbitrary"),
        name="in_proj",
    )(x, g.reshape(1, d), wn, wc, wk, wt, wfg, conv_w, fb)


def _norm_proj_kernel(x_ref, g_ref, w_ref, o_ref):
    h = _rms(x_ref[...], g_ref[...]).astype(BF16)
    o_ref[...] = jnp.dot(h, w_ref[...], preferred_element_type=F32).astype(BF16)


def _norm_proj(x, g, w):
    n, d = x.shape
    cols = w.shape[1]
    return pl.pallas_call(
        _norm_proj_kernel,
        grid=(1,),
        in_specs=[_const_spec((n, d)), _const_spec((1, d)), _const_spec(w.shape)],
        out_specs=pl.BlockSpec((n, cols), lambda i: (0, 0)),
        out_shape=jax.ShapeDtypeStruct((n, cols), BF16),
        compiler_params=_params("arbitrary"),
        name="mem_proj",
    )(x, g.reshape(1, d), w)


def _fox_kernel(q_ref, qn_ref, k_ref, v_ref, o_ref, q_aug, v_aug, s_buf, mt_buf, p_buf, acc_ref, *, tq, tk, hp):
    i = pl.program_id(2)
    seq = v_ref.shape[1]
    cur = i % 2
    heads = tuple(range(hp))

    def build_queries(src_ref, qslot):
        qrow = lax.broadcasted_iota(jnp.int32, (LANES, tq), 0)
        for pr in range(hp // 2):
            qb = src_ref[pr * LANES:(pr + 1) * LANES, :].astype(F32)
            q_aug[qslot, 2 * pr] = jnp.where(qrow < HEAD_DIM, qb, _one_hot_rows((LANES, tq), 0, AUG_EVEN)).astype(BF16)
            q_aug[qslot, 2 * pr + 1] = jnp.where(qrow >= HEAD_DIM, qb,
                                                 _one_hot_rows((LANES, tq), 0, AUG_ODD)).astype(BF16)

    def scores(t, slot, qslot):
        off = pl.multiple_of(t * tk, tk)
        for hh in heads:
            s = jnp.dot(k_ref[0, hh, pl.ds(off, tk), :], q_aug[qslot, hh], preferred_element_type=F32)
            s_buf[hh, slot, :, 0:tq] = s
            mt_buf[hh, slot, 0:1, :] = jnp.max(s, axis=0, keepdims=True)

    def softmax(slot, ms, mask=None):
        out = []
        for hh in heads:
            s = s_buf[hh, slot, :, 0:tq]
            if mask is None:
                tile_max = mt_buf[hh, slot, 0:1, :]
            else:
                s = jnp.where(mask, s, NEG)
                tile_max = jnp.max(s, axis=0, keepdims=True)
            m_new = jnp.maximum(ms[hh], tile_max)
            p_buf[hh, slot, :, 0:tq] = jnp.exp2(s - m_new).astype(BF16)
            out.append((m_new, jnp.exp2(ms[hh] - m_new)))
        return tuple(o[0] for o in out), tuple(o[1] for o in out)

    def values(t, slot, alphas):
        off = pl.multiple_of(jnp.maximum(t, 0) * tk, tk)
        for hh in heads:
            pv = jnp.dot(v_aug[hh, :, pl.ds(off, tk)], p_buf[hh, slot, :, 0:tq], preferred_element_type=F32)
            acc_ref[hh] = alphas[hh] * acc_ref[hh] + pv

    @pl.when(i == 0)
    def _():
        ones_row = _one_hot_rows((V_ROWS - HEAD_DIM, seq), 0, (0,)).astype(BF16)
        for hh in heads:
            v_aug[hh, 0:HEAD_DIM, :] = v_ref[hh * HEAD_DIM:(hh + 1) * HEAD_DIM, :]
            v_aug[hh, HEAD_DIM:V_ROWS, :] = ones_row
            p_buf[hh, 1, :, 0:tq] = jnp.zeros((tk, tq), BF16)
            acc_ref[hh] = jnp.zeros(acc_ref.shape[1:], F32)
        build_queries(q_ref, 0)
        scores(0, 0, 0)

    def pair(u, carry):
        ms, alpha_prev = carry
        t = 2 * u
        scores(t + 1, 1, cur)
        ms, alpha0 = softmax(0, ms)
        values(t - 1, 1, alpha_prev)
        scores(t + 2, 0, cur)
        ms, alpha1 = softmax(1, ms)
        values(t, 0, alpha0)
        return ms, alpha1

    carry = (tuple(jnp.full((1, tq), NEG, F32) for _ in heads), tuple(jnp.ones((1, tq), F32) for _ in heads))
    ms, alpha_prev = lax.fori_loop(0, i, pair, carry)

    t = 2 * i
    key = lax.broadcasted_iota(jnp.int32, (tk, tq), 0)
    qry = lax.broadcasted_iota(jnp.int32, (tk, tq), 1)
    off_b = pl.multiple_of((t + 1) * tk, tk)
    tri = lax.broadcasted_iota(jnp.int32, (tk, tk), 0) <= lax.broadcasted_iota(jnp.int32, (tk, tk), 1)
    s_b = [jnp.where(tri, jnp.dot(k_ref[0, hh, pl.ds(off_b, tk), :], q_aug[cur, hh, :, tk:],
                                  preferred_element_type=F32), NEG) for hh in heads]
    ms, alpha0 = softmax(0, ms, key <= qry)
    build_queries(qn_ref, 1 - cur)
    scores(0, 0, 1 - cur)
    values(t - 1, 1, alpha_prev)
    values(t, 0, alpha0)
    outs = []
    for hh in heads:
        mt_buf[hh, 1, 0:1, :] = ms[hh]
        m_old = mt_buf[hh, 1, 0:1, tk:]
        m_new = jnp.maximum(m_old, jnp.max(s_b[hh], axis=0, keepdims=True))
        p_b = jnp.exp2(s_b[hh] - m_new).astype(BF16)
        pv = jnp.dot(v_aug[hh, :, pl.ds(off_b, tk)], p_b, preferred_element_type=F32)
        acc_ref[hh, :, tk:] = jnp.exp2(m_old - m_new) * acc_ref[hh, :, tk:] + pv
        acc = acc_ref[hh]
        outs.append(acc[:HEAD_DIM] / acc[HEAD_DIM:HEAD_DIM + 1])
    o_ref[...] = jnp.concatenate(outs, axis=0).T.astype(BF16)


def _fox_attention(pt, k_aug, b, s, tq, hp):
    tk = tq // 2
    per = s // tq
    rows = hp * HEAD_DIM
    qblk = FOXQ_ROW // rows
    vblk = FOXV_ROW // rows
    return pl.pallas_call(
        functools.partial(_fox_kernel, tq=tq, tk=tk, hp=hp),
        grid=(b, FOX_HEADS // hp, per),
        in_specs=[
            pl.BlockSpec((rows, tq), lambda bi, g, i: (qblk + g, bi * per + i)),
            pl.BlockSpec((rows, tq), lambda bi, g, i: (qblk + g, bi * per + jnp.minimum(i + 1, per - 1))),
            pl.BlockSpec((1, hp, s, LANES), lambda bi, g, i: (bi, g, 0, 0), pipeline_mode=pl.Buffered(1)),
            pl.BlockSpec((rows, s), lambda bi, g, i: (vblk + g, bi), pipeline_mode=pl.Buffered(1)),
        ],
        out_specs=pl.BlockSpec((tq, rows), lambda bi, g, i: (bi * per + i, g)),
        out_shape=jax.ShapeDtypeStruct((b * s, BRANCH), BF16),
        scratch_shapes=[
            pltpu.VMEM((2, hp, LANES, tq), BF16),
            pltpu.VMEM((hp, V_ROWS, s), BF16),
            pltpu.VMEM((hp, 2, tk, tq + PITCH_PAD), F32),
            pltpu.VMEM((hp, 2, SUBLANES, tq), F32),
            pltpu.VMEM((hp, 2, tk, tq + PITCH_PAD), BF16),
            pltpu.VMEM((hp, V_ROWS, tq), F32),
        ],
        compiler_params=_params("arbitrary", "arbitrary", "arbitrary", vmem_limit=FOX_VMEM_LIMIT_BYTES),
        name="fox_attention",
    )(pt, pt, k_aug, pt)


def _swa_bias_kernel(rb_ref, bucket_ref, win_ref, o_ref):
    h = pl.program_id(0)
    bucket = bucket_ref[...]
    bias = jnp.zeros(bucket.shape, F32)
    for b in range(N_BUCKETS):
        bias = jnp.where(bucket == b, rb_ref[b, h] * LOG2E, bias)
    o_ref[0, 0] = jnp.where(win_ref[0] > 0, bias, NEG)
    o_ref[1, 0] = jnp.where(win_ref[1] > 0, bias, NEG)


def _swa_bias_table(rel_bias):
    tq = np.arange(WINDOW)[None, :]
    sk = np.arange(2 * WINDOW)[:, None]
    dist = WINDOW + tq - sk
    n = jnp.maximum(jnp.asarray(dist, jnp.int32), 0)
    max_exact = N_BUCKETS // 2
    large = max_exact + (jnp.log(jnp.maximum(n, 1).astype(F32) / max_exact)
                         / math.log(WINDOW / max_exact) * (N_BUCKETS - max_exact)).astype(jnp.int32)
    bucket = jnp.where(n < max_exact, n, jnp.minimum(large, N_BUCKETS - 1))
    in_window = (dist >= 0) & (dist < WINDOW)
    win = np.stack([in_window, in_window & (sk >= WINDOW)]).astype(np.int32)
    full = lambda shape: pl.BlockSpec(shape, lambda h: (0,) * len(shape))
    return pl.pallas_call(
        _swa_bias_kernel,
        grid=(SWA_HEADS,),
        in_specs=[pl.BlockSpec(memory_space=pltpu.SMEM), full(bucket.shape), full(win.shape)],
        out_specs=pl.BlockSpec((2, 1, 2 * WINDOW, WINDOW), lambda h: (0, h // SWA_GROUP, 0, h % SWA_GROUP)),
        out_shape=jax.ShapeDtypeStruct((2, SWA_KV, 2 * WINDOW, SWA_GROUP * WINDOW), F32),
        compiler_params=_params("arbitrary"),
        name="swa_bias_table",
    )(rel_bias, bucket, jnp.asarray(win))


def _swa_kernel(sink_ref, q_ref, kp_ref, kc_ref, vp_ref, vc_ref, bias_ref, o_ref,
                k_buf, v_buf, s_buf, mt_buf, p_buf, dn_buf, *, nblk):
    gw = SWA_GROUP * WINDOW
    ts = nblk * WINDOW
    first = pl.program_id(1) == 0
    k_buf[0:WINDOW, :] = kp_ref[...]
    k_buf[WINDOW:WINDOW + ts, :] = kc_ref[...]
    v_buf[:, 0:WINDOW] = vp_ref[...]
    v_buf[:, WINDOW:WINDOW + ts] = vc_ref[...]
    glane = lax.broadcasted_iota(jnp.int32, (1, gw), 1) // WINDOW
    zeros = jnp.zeros((HEAD_DIM, gw), BF16)
    sinks = []
    for kv in range(SWA_KV):
        sink = jnp.zeros((1, gw), F32)
        for g in range(SWA_GROUP):
            sink = jnp.where(glane == g, sink_ref[kv * SWA_GROUP + g] * LOG2E, sink)
        sinks.append(sink)

    def scores(blk, slot):
        lo = pl.multiple_of(blk * WINDOW, WINDOW)
        sel = jnp.logical_and(first, blk == 0).astype(jnp.int32)
        keys = k_buf[pl.ds(lo, 2 * WINDOW), :]
        for kv in range(SWA_KV):
            qg = jnp.concatenate([q_ref[(kv * SWA_GROUP + g) * HEAD_DIM:(kv * SWA_GROUP + g + 1) * HEAD_DIM,
                                        pl.ds(lo, WINDOW)] for g in range(SWA_GROUP)], axis=1)
            q_pad = jnp.concatenate([qg, zeros] if kv == 0 else [zeros, qg], axis=0)
            s = jnp.dot(keys, q_pad, preferred_element_type=F32) + bias_ref[sel, kv]
            s_buf[kv, slot, :, 0:gw] = s
            mt_buf[kv, slot, 0:1, :] = jnp.max(s, axis=0, keepdims=True)

    def softmax(slot):
        for kv in range(SWA_KV):
            m = jnp.maximum(mt_buf[kv, slot, 0:1, :], sinks[kv])
            p = jnp.exp2(s_buf[kv, slot, :, 0:gw] - m)
            dn_buf[kv, slot, 0:1, :] = jnp.sum(p, axis=0, keepdims=True) + jnp.exp2(sinks[kv] - m)
            p_buf[kv, slot, :, 0:gw] = p.astype(BF16)

    def values(blk, slot):
        lo = pl.multiple_of(blk * WINDOW, WINDOW)
        for kv in range(SWA_KV):
            vb = v_buf[kv * HEAD_DIM:(kv + 1) * HEAD_DIM, pl.ds(lo, 2 * WINDOW)]
            o = jnp.dot(vb, p_buf[kv, slot, :, 0:gw], preferred_element_type=F32) / dn_buf[kv, slot, 0:1, :]
            for pr in range(SWA_GROUP // 2):
                two = jnp.concatenate([o[:, (2 * pr) * WINDOW:(2 * pr + 1) * WINDOW],
                                       o[:, (2 * pr + 1) * WINDOW:(2 * pr + 2) * WINDOW]], axis=0)
                col = (kv * (SWA_GROUP // 2) + pr) * LANES
                o_ref[pl.ds(lo, WINDOW), col:col + LANES] = two.T.astype(BF16)

    scores(jnp.int32(0), 0)
    scores(jnp.int32(1), 1)
    softmax(0)

    def pair(u, carry):
        blk = 2 * u + 1
        scores(blk + 1, 0)
        softmax(1)
        values(blk - 1, 0)
        scores(blk + 2, 1)
        softmax(0)
        values(blk, 1)
        return carry

    lax.fori_loop(0, nblk // 2 - 1, pair, 0)
    softmax(1)
    values(jnp.int32(nblk - 2), 0)
    values(jnp.int32(nblk - 1), 1)


def _swa_attention(sink, pt, pn, bias, b, s, nblk):
    ts = nblk * WINDOW
    per = s // ts
    perw = s // WINDOW
    qblk = SWAQ_ROW // BRANCH
    vblk = SWAV_ROW // LANES
    kblk = SWAK_OFF // LANES
    prev = lambda bi, i: bi * perw + jnp.maximum(i * nblk - 1, 0)
    return pl.pallas_call(
        functools.partial(_swa_kernel, nblk=nblk),
        grid=(b, per),
        in_specs=[
            pl.BlockSpec(memory_space=pltpu.SMEM),
            pl.BlockSpec((BRANCH, ts), lambda bi, i: (qblk, bi * per + i)),
            pl.BlockSpec((WINDOW, LANES), lambda bi, i: (prev(bi, i), kblk)),
            pl.BlockSpec((ts, LANES), lambda bi, i: (bi * per + i, kblk)),
            pl.BlockSpec((LANES, WINDOW), lambda bi, i: (vblk, prev(bi, i))),
            pl.BlockSpec((LANES, ts), lambda bi, i: (vblk, bi * per + i)),
            _const_spec(bias.shape),
        ],
        out_specs=pl.BlockSpec((ts, BRANCH), lambda bi, i: (bi * per + i, 0)),
        out_shape=jax.ShapeDtypeStruct((b * s, BRANCH), BF16),
        scratch_shapes=[
            pltpu.VMEM((ts + WINDOW, LANES), BF16),
            pltpu.VMEM((LANES, ts + WINDOW), BF16),
            pltpu.VMEM((SWA_KV, 2, 2 * WINDOW, SWA_GROUP * WINDOW + PITCH_PAD), F32),
            pltpu.VMEM((SWA_KV, 2, SUBLANES, SWA_GROUP * WINDOW), F32),
            pltpu.VMEM((SWA_KV, 2, 2 * WINDOW, SWA_GROUP * WINDOW + PITCH_PAD), BF16),
            pltpu.VMEM((SWA_KV, 2, SUBLANES, SWA_GROUP * WINDOW), F32),
        ],
        compiler_params=_params("arbitrary", "arbitrary"),
        name="swa_attention",
    )(sink, pt, pn, pn, pt, pt, bias)


def _merge_tile(x, gate_ref, branches, wb_ref, wo_ref, m_scr):
    for lo in range(0, D_MODEL, MERGE_CHUNK):
        gates = [jax.nn.sigmoid(gate_ref[:, b * D_MODEL + lo:b * D_MODEL + lo + MERGE_CHUNK].astype(F32))
                 for b in range(len(branches))]
        merged = None
        for b, y in enumerate(branches):
            term = gates[b] * jnp.dot(y, wb_ref[b, :, lo:lo + MERGE_CHUNK], preferred_element_type=F32)
            merged = term if merged is None else merged + term
        m_scr[:, lo:lo + MERGE_CHUNK] = merged.astype(BF16)
    return x + jnp.dot(m_scr[...], wo_ref[...], preferred_element_type=F32)


def _xattn_tile(x, g_ref, wq_ref, kT_ref, v_ref, wo_ref):
    q = jnp.dot(_rms(x, g_ref[...]).astype(BF16), wq_ref[...], preferred_element_type=F32).astype(BF16)
    sls = [slice(h * X_HEAD_DIM, (h + 1) * X_HEAD_DIM) for h in range(X_HEADS)]
    scores = [jnp.dot(q[:, sl], kT_ref[0, sl, :], preferred_element_type=F32) for sl in sls]
    probs = [jnp.exp2(s - jnp.max(s, axis=-1, keepdims=True)) for s in scores]
    denoms = [jnp.sum(p, axis=-1, keepdims=True) for p in probs]
    outs = [jnp.dot(p.astype(BF16), v_ref[0, :, sl], preferred_element_type=F32) for p, sl in zip(probs, sls)]
    heads = [(o / d).astype(BF16) for o, d in zip(outs, denoms)]
    return x + jnp.dot(jnp.concatenate(heads, axis=1), wo_ref[...], preferred_element_type=F32)


def _merge_xattn_kernel(x_ref, gate_ref, yc_ref, yf_ref, ys_ref, wb_ref, wo_ref, g_ref, wq_ref, kT_ref, v_ref,
                        wxo_ref, o_ref, m_scr):
    x = _merge_tile(x_ref[...], gate_ref, (yc_ref[...], yf_ref[...], ys_ref[...]), wb_ref, wo_ref, m_scr)
    o_ref[...] = _xattn_tile(x, g_ref, wq_ref, kT_ref, v_ref, wxo_ref)


def _merge_xattn(x, pn, y_fox, y_swa, w_branch, w_out, g, wq, kT, v, wxo, seq, tm):
    n, d = x.shape
    tiles = seq // tm
    return pl.pallas_call(
        _merge_xattn_kernel,
        grid=(n // tm,),
        in_specs=[
            pl.BlockSpec((tm, d), lambda i: (i, 0)),
            pl.BlockSpec((tm, 3 * D_MODEL), lambda i: (i, GATE_OFF // (3 * D_MODEL))),
            pl.BlockSpec((tm, BRANCH), lambda i: (i, CONV_OFF // BRANCH)),
            pl.BlockSpec((tm, BRANCH), lambda i: (i, 0)),
            pl.BlockSpec((tm, BRANCH), lambda i: (i, 0)),
            _const_spec(w_branch.shape), _const_spec(w_out.shape),
            _const_spec((1, d)), _const_spec(wq.shape),
            pl.BlockSpec((1,) + kT.shape[1:], lambda i: (i // tiles, 0, 0)),
            pl.BlockSpec((1,) + v.shape[1:], lambda i: (i // tiles, 0, 0)),
            _const_spec(wxo.shape),
        ],
        out_specs=pl.BlockSpec((tm, d), lambda i: (i, 0)),
        out_shape=jax.ShapeDtypeStruct((n, d), F32),
        scratch_shapes=[pltpu.VMEM((tm, d), BF16)],
        compiler_params=_params("arbitrary"),
        name="merge_cross_attention",
    )(x, pn, pn, y_fox, y_swa, w_branch, w_out, g.reshape(1, d), wq, kT, v, wxo)


def _swiglu_kernel(x_ref, g_ref, wg_ref, wu_ref, wd_ref, gf_ref, o_ref, *, final_norm):
    x = x_ref[...]
    xn = _rms(x, g_ref[...]).astype(BF16)
    d_ff = wg_ref.shape[1]
    y = x
    for lo in range(0, d_ff, FFN_CHUNK):
        gate = jnp.dot(xn, wg_ref[:, lo:lo + FFN_CHUNK], preferred_element_type=F32)
        up = jnp.dot(xn, wu_ref[:, lo:lo + FFN_CHUNK], preferred_element_type=F32)
        hidden = (gate * jax.nn.sigmoid(gate) * up).astype(BF16)
        y = y + jnp.dot(hidden, wd_ref[lo:lo + FFN_CHUNK, :], preferred_element_type=F32)
    o_ref[...] = _rms(y, gf_ref[...]) if final_norm else y


def _swiglu(x, g, wg, wu, wd, g_final, final_norm, tm):
    n, d = x.shape
    return pl.pallas_call(
        functools.partial(_swiglu_kernel, final_norm=final_norm),
        grid=(n // tm,),
        in_specs=[
            pl.BlockSpec((tm, d), lambda i: (i, 0)),
            _const_spec((1, d)),
            _const_spec(wg.shape), _const_spec(wu.shape), _const_spec(wd.shape),
            _const_spec((1, d)),
        ],
        out_specs=pl.BlockSpec((tm, d), lambda i: (i, 0)),
        out_shape=jax.ShapeDtypeStruct((n, d), F32),
        compiler_params=_params("arbitrary"),
        name="swiglu",
    )(x, g.reshape(1, d), wg, wu, wd, g_final.reshape(1, d))


def _pack_in_proj(w_in):
    sizes = [BRANCH] * 3 + [BRANCH] * 3 + [FOX_HEADS] + [BRANCH, SWA_KV * HEAD_DIM, SWA_KV * HEAD_DIM] + [3 * D_MODEL]
    offs = np.concatenate([[0], np.cumsum(sizes)])
    c_b, c_c, c_u, f_q, f_k, f_v, f_g, s_q, s_k, s_v, gates = (w_in[:, offs[t]:offs[t + 1]] for t in range(len(sizes)))
    qscale = HEAD_DIM ** -0.5 * LOG2E
    wn = jnp.concatenate([gates, s_k], axis=1)
    wc = jnp.concatenate([c_b, c_c, c_u], axis=1)
    wt = jnp.concatenate([f_q * qscale, f_v, s_q * qscale, s_v], axis=1).T
    wfg = jnp.pad(f_g, ((0, 0), (0, LANES - FOX_HEADS)))
    return wn.astype(BF16), wc.astype(BF16), f_k.astype(BF16), wt.astype(BF16), wfg.astype(BF16)


def kernel(x, mem, mix_norm_g, w_in, forget_bias, conv_w, sink, w_branch, w_mix_out, rel_bias,
           xattn_norm_g, mem_norm_g, w_xq, w_xkv, w_xo, ffn_norm_g, w_ffn_gate, w_ffn_up, w_ffn_down,
           final_norm_g):
    b, s, d = x.shape
    n = b * s
    depth = w_in.shape[0]
    assert d == D_MODEL and s % FOX_TQ == 0 and s % ROW_TILE == 0
    assert s % (SWA_BLOCKS_PER_STEP * WINDOW) == 0 and SWA_BLOCKS_PER_STEP % 2 == 0
    mem_len = mem.shape[1]
    xf = x.reshape(n, d)
    memf = mem.reshape(b * mem_len, d)
    swa_bias = _swa_bias_table(rel_bias)
    for l in range(depth):
        wn, wc, wk, wt, wfg = _pack_in_proj(w_in[l])
        pn, pt, k_aug = _in_proj(xf, mix_norm_g[l], wn, wc, wk, wt, wfg, conv_w[l], forget_bias[l],
                                 seq=s, tm=ROW_TILE)
        y_fox = _fox_attention(pt, k_aug, b, s, tq=FOX_TQ, hp=FOX_HEADS_PER_STEP)
        y_swa = _swa_attention(sink[l], pt, pn, swa_bias, b, s, nblk=SWA_BLOCKS_PER_STEP)
        kv = _norm_proj(memf, mem_norm_g[l], w_xkv[l].astype(BF16))
        kT = kv[:, :d].reshape(b, mem_len, d).transpose(0, 2, 1)
        v = kv[:, d:].reshape(b, mem_len, d)
        wq = (w_xq[l] * (X_HEAD_DIM ** -0.5 * LOG2E)).astype(BF16)
        xf = _merge_xattn(xf, pn, y_fox, y_swa, w_branch[l].astype(BF16), w_mix_out[l].astype(BF16),
                          xattn_norm_g[l], wq, kT, v, w_xo[l].astype(BF16), seq=s, tm=ROW_TILE)

        xf = _swiglu(xf, ffn_norm_g[l], w_ffn_gate[l].astype(BF16), w_ffn_up[l].astype(BF16),
                     w_ffn_down[l].astype(BF16), final_norm_g, final_norm=(l == depth - 1), tm=ROW_TILE)
    return xf.reshape(b, s, d)
```

```python
import functools
import math

import jax
import jax.numpy as jnp
import numpy as np
from jax import lax
from jax.experimental import pallas as pl
from jax.experimental.pallas import tpu as pltpu

F32 = jnp.float32
BF16 = jnp.bfloat16

D_MODEL = 1024
HEAD_DIM = 64
BRANCH = 512
FOX_HEADS = 8
SWA_HEADS = 8
SWA_KV = 2
SWA_GROUP = SWA_HEADS // SWA_KV
WINDOW = 128
N_BUCKETS = 32
X_HEADS = 4
X_HEAD_DIM = D_MODEL // X_HEADS
RMS_EPS = 1e-6
NEG = -1e30
LOG2E = math.log2(math.e)

LANES = 128
SUBLANES = 8
VMEM_LIMIT_BYTES = 56 * 1024 * 1024
FOX_VMEM_LIMIT_BYTES = 58 * 1024 * 1024
TAIL_VMEM_LIMIT_BYTES = 58 * 1024 * 1024
PITCH_PAD = LANES
MERGE_CHUNK = 256
FFN_CHUNK = 256

ROW_TILE = 512
FOX_TQ = 1024
FOX_HEADS_PER_STEP = 4
SWA_BLOCKS_PER_STEP = 16

GATE_OFF = 0
CONV_OFF = 3 * D_MODEL
SWAK_OFF = CONV_OFF + BRANCH
PN_COLS = SWAK_OFF + SWA_KV * HEAD_DIM
FOXQ_ROW = 0
FOXV_ROW = BRANCH
SWAQ_ROW = 2 * BRANCH
SWAV_ROW = 3 * BRANCH
PT_ROWS = SWAV_ROW + SWA_KV * HEAD_DIM

AUG_STRIDE = 8
AUG_EVEN = (HEAD_DIM, HEAD_DIM + AUG_STRIDE, HEAD_DIM + 2 * AUG_STRIDE)
AUG_ODD = (0, AUG_STRIDE, 2 * AUG_STRIDE)
V_ROWS = 80


def _params(*sem, vmem_limit=VMEM_LIMIT_BYTES):
    return pltpu.CompilerParams(dimension_semantics=sem, vmem_limit_bytes=vmem_limit)


def _rms(x, g):
    return x * lax.rsqrt(jnp.mean(x * x, axis=-1, keepdims=True) + RMS_EPS) * g


def _const_spec(shape):
    nd = len(shape)
    return pl.BlockSpec(shape, lambda *_: (0,) * nd, pipeline_mode=pl.Buffered(1))


def _one_hot_rows(shape, axis, positions):
    idx = lax.broadcasted_iota(jnp.int32, shape, axis)
    hit = idx == positions[0]
    for p in positions[1:]:
        hit = hit | (idx == p)
    return jnp.where(hit, 1.0, 0.0)


def _in_proj_kernel(x_ref, g_ref, wn_ref, wc_ref, wk_ref, wt_ref, wfg_ref, cw_ref, fb_ref,
                    pn_ref, pt_ref, ka_ref, z_tail, c_tail, *, tiles_per_seq):
    tm = x_ref.shape[0]
    first_of_seq = pl.program_id(0) % tiles_per_seq == 0

    @pl.when(pl.program_id(0) == 0)
    def _():
        z_tail[...] = jnp.zeros(z_tail.shape, F32)
        c_tail[...] = jnp.zeros(c_tail.shape, F32)

    h = _rms(x_ref[...], g_ref[...]).astype(BF16)

    fg = jnp.dot(h, wfg_ref[...], preferred_element_type=F32)
    k_fox = jnp.dot(h, wk_ref[...], preferred_element_type=F32)
    lane = lax.broadcasted_iota(jnp.int32, (tm, LANES), 1)
    trow = lax.broadcasted_iota(jnp.int32, (tm, LANES), 0)
    c = jnp.where(lane < FOX_HEADS, jax.nn.log_sigmoid(fg + fb_ref[...]) * LOG2E, 0.0)
    d = 1
    while d < tm:
        c = c + jnp.where(trow >= d, pltpu.roll(c, d, axis=0), 0.0)
        d *= 2
    c = c + jnp.where(first_of_seq, 0.0, c_tail[0:1, :])
    c_tail[0:1, :] = c[tm - 1:tm, :]
    c1 = c.astype(BF16).astype(F32)
    r1 = c - c1
    c2 = r1.astype(BF16).astype(F32)
    c3 = (r1 - c2).astype(BF16).astype(F32)
    packed = -jnp.where(lane < AUG_STRIDE, c1,
                        jnp.where(lane < 2 * AUG_STRIDE, pltpu.roll(c2, AUG_STRIDE, axis=1),
                                  pltpu.roll(c3, 2 * AUG_STRIDE, axis=1)))
    keep_even = _one_hot_rows((1, LANES), 1, AUG_EVEN)
    keep_odd = _one_hot_rows((1, LANES), 1, AUG_ODD)
    for hd in range(FOX_HEADS):
        k_pair = k_fox[:, (hd // 2) * LANES:(hd // 2 + 1) * LANES]
        if hd % 2 == 0:
            aug = pltpu.roll(packed, (AUG_EVEN[0] - hd) % LANES, axis=1) * keep_even
            out = jnp.where(lane < HEAD_DIM, k_pair, aug)
        else:
            aug = pltpu.roll(packed, (AUG_ODD[0] - hd) % LANES, axis=1) * keep_odd
            out = jnp.where(lane >= HEAD_DIM, k_pair, aug)
        ka_ref[0, hd] = out.astype(BF16)

    conv_b, conv_c, conv_u = (jnp.dot(h, wc_ref[:, t * BRANCH:(t + 1) * BRANCH], preferred_element_type=F32)
                              for t in range(3))
    z = conv_c * conv_u
    zh = jnp.where(first_of_seq, 0.0, z_tail[...])
    z_tail[...] = z[tm - SUBLANES:tm, :]
    row = lax.broadcasted_iota(jnp.int32, z.shape, 0)
    z1 = jnp.where(row == 0, zh[7:8], pltpu.roll(z, 1, axis=0))
    z2 = jnp.where(row == 0, zh[6:7], jnp.where(row == 1, zh[7:8], pltpu.roll(z, 2, axis=0)))
    cw = cw_ref[...]
    pn_ref[:, CONV_OFF:SWAK_OFF] = (conv_b * (cw[0:1] * z2 + cw[1:2] * z1 + cw[2:3] * z)).astype(BF16)

    pt_ref[...] = lax.dot_general(wt_ref[...], h, (((1,), (1,)), ((), ())),
                                  preferred_element_type=F32).astype(BF16)
    for lo in range(0, CONV_OFF, D_MODEL):
        pn_ref[:, lo:lo + D_MODEL] = jnp.dot(h, wn_ref[:, lo:lo + D_MODEL], preferred_element_type=F32).astype(BF16)
    pn_ref[:, SWAK_OFF:PN_COLS] = jnp.dot(h, wn_ref[:, CONV_OFF:], preferred_element_type=F32).astype(BF16)


def _in_proj(x, g, wn, wc, wk, wt, wfg, conv_w, forget_bias, seq, tm):
    n, d = x.shape
    tiles = seq // tm
    fb = jnp.pad(forget_bias, (0, LANES - FOX_HEADS)).reshape(1, LANES)
    return pl.pallas_call(
        functools.partial(_in_proj_kernel, tiles_per_seq=tiles),
        grid=(n // tm,),
        in_specs=[
            pl.BlockSpec((tm, d), lambda i: (i, 0)),
            _const_spec((1, d)), _const_spec(wn.shape), _const_spec(wc.shape), _const_spec(wk.shape),
            _const_spec(wt.shape), _const_spec(wfg.shape), _const_spec(conv_w.shape), _const_spec((1, LANES)),
        ],
        out_specs=[
            pl.BlockSpec((tm, PN_COLS), lambda i: (i, 0)),
            pl.BlockSpec((PT_ROWS, tm), lambda i: (0, i)),
            pl.BlockSpec((1, FOX_HEADS, tm, LANES), lambda i: (i // tiles, 0, i % tiles, 0)),
        ],
        out_shape=[jax.ShapeDtypeStruct((n, PN_COLS), BF16), jax.ShapeDtypeStruct((PT_ROWS, n), BF16),
                   jax.ShapeDtypeStruct((n // seq, FOX_HEADS, seq, LANES), BF16)],
        scratch_shapes=[pltpu.VMEM((SUBLANES, BRANCH), F32), pltpu.VMEM((SUBLANES, LANES), F32)],
        compiler_params=_params("arbitrary"),
        name="in_proj",
    )(x, g.reshape(1, d), wn, wc, wk, wt, wfg, conv_w, fb)


def _norm_proj_kernel(x_ref, g_ref, w_ref, o_ref):
    h = _rms(x_ref[...], g_ref[...]).astype(BF16)
    o_ref[...] = jnp.dot(h, w_ref[...], preferred_element_type=F32).astype(BF16)


def _norm_proj(x, g, w):
    n, d = x.shape
    cols = w.shape[1]
    return pl.pallas_call(
        _norm_proj_kernel,
        grid=(1,),
        in_specs=[_const_spec((n, d)), _const_spec((1, d)), _const_spec(w.shape)],
        out_specs=pl.BlockSpec((n, cols), lambda i: (0, 0)),
        out_shape=jax.ShapeDtypeStruct((n, cols), BF16),
        compiler_params=_params("arbitrary"),
        name="mem_proj",
    )(x, g.reshape(1, d), w)


def _fox_kernel(q_ref, qn_ref, k_ref, v_ref, o_ref, q_aug, v_aug, s_buf, mt_buf, p_buf, acc_ref, *, tq, tk, hp):
    i = pl.program_id(2)
    seq = v_ref.shape[1]
    cur = i % 2
    heads = tuple(range(hp))

    def build_queries(src_ref, qslot):
        qrow = lax.broadcasted_iota(jnp.int32, (LANES, tq), 0)
        for pr in range(hp // 2):
            qb = src_ref[pr * LANES:(pr + 1) * LANES, :].astype(F32)
            q_aug[qslot, 2 * pr] = jnp.where(qrow < HEAD_DIM, qb, _one_hot_rows((LANES, tq), 0, AUG_EVEN)).astype(BF16)
            q_aug[qslot, 2 * pr + 1] = jnp.where(qrow >= HEAD_DIM, qb,
                                                 _one_hot_rows((LANES, tq), 0, AUG_ODD)).astype(BF16)

    def scores(t, slot, qslot):
        off = pl.multiple_of(t * tk, tk)
        for hh in heads:
            s = jnp.dot(k_ref[0, hh, pl.ds(off, tk), :], q_aug[qslot, hh], preferred_element_type=F32)
            s_buf[hh, slot, :, 0:tq] = s
            mt_buf[hh, slot, 0:1, :] = jnp.max(s, axis=0, keepdims=True)

    def softmax(slot, ms, mask=None):
        out = []
        for hh in heads:
            s = s_buf[hh, slot, :, 0:tq]
            if mask is None:
                tile_max = mt_buf[hh, slot, 0:1, :]
            else:
                s = jnp.where(mask, s, NEG)
                tile_max = jnp.max(s, axis=0, keepdims=True)
            m_new = jnp.maximum(ms[hh], tile_max)
            p_buf[hh, slot, :, 0:tq] = jnp.exp2(s - m_new).astype(BF16)
            out.append((m_new, jnp.exp2(ms[hh] - m_new)))
        return tuple(o[0] for o in out), tuple(o[1] for o in out)

    def values(t, slot, alphas):
        off = pl.multiple_of(jnp.maximum(t, 0) * tk, tk)
        for hh in heads:
            pv = jnp.dot(v_aug[hh, :, pl.ds(off, tk)], p_buf[hh, slot, :, 0:tq], preferred_element_type=F32)
            acc_ref[hh] = alphas[hh] * acc_ref[hh] + pv

    @pl.when(i == 0)
    def _():
        ones_row = _one_hot_rows((V_ROWS - HEAD_DIM, seq), 0, (0,)).astype(BF16)
        for hh in heads:
            v_aug[hh, 0:HEAD_DIM, :] = v_ref[hh * HEAD_DIM:(hh + 1) * HEAD_DIM, :]
            v_aug[hh, HEAD_DIM:V_ROWS, :] = ones_row
            p_buf[hh, 1, :, 0:tq] = jnp.zeros((tk, tq), BF16)
            acc_ref[hh] = jnp.zeros(acc_ref.shape[1:], F32)
        build_queries(q_ref, 0)
        scores(0, 0, 0)

    def pair(u, carry):
        ms, alpha_prev = carry
        t = 2 * u
        scores(t + 1, 1, cur)
        ms, alpha0 = softmax(0, ms)
        values(t - 1, 1, alpha_prev)
        scores(t + 2, 0, cur)
        ms, alpha1 = softmax(1, ms)
        values(t, 0, alpha0)
        return ms, alpha1

    carry = (tuple(jnp.full((1, tq), NEG, F32) for _ in heads), tuple(jnp.ones((1, tq), F32) for _ in heads))
    ms, alpha_prev = lax.fori_loop(0, i, pair, carry)

    t = 2 * i
    key = lax.broadcasted_iota(jnp.int32, (tk, tq), 0)
    qry = lax.broadcasted_iota(jnp.int32, (tk, tq), 1)
    off_b = pl.multiple_of((t + 1) * tk, tk)
    tri = lax.broadcasted_iota(jnp.int32, (tk, tk), 0) <= lax.broadcasted_iota(jnp.int32, (tk, tk), 1)
    s_b = [jnp.where(tri, jnp.dot(k_ref[0, hh, pl.ds(off_b, tk), :], q_aug[cur, hh, :, tk:],
                                  preferred_element_type=F32), NEG) for hh in heads]
    ms, alpha0 = softmax(0, ms, key <= qry)
    build_queries(qn_ref, 1 - cur)
    scores(0, 0, 1 - cur)
    values(t - 1, 1, alpha_prev)
    values(t, 0, alpha0)
    outs = []
    for hh in heads:
        mt_buf[hh, 1, 0:1, :] = ms[hh]
        m_old = mt_buf[hh, 1, 0:1, tk:]
        m_new = jnp.maximum(m_old, jnp.max(s_b[hh], axis=0, keepdims=True))
        p_b = jnp.exp2(s_b[hh] - m_new).astype(BF16)
        pv = jnp.dot(v_aug[hh, :, pl.ds(off_b, tk)], p_b, preferred_element_type=F32)
        acc_ref[hh, :, tk:] = jnp.exp2(m_old - m_new) * acc_ref[hh, :, tk:] + pv
        acc = acc_ref[hh]
        outs.append(acc[:HEAD_DIM] / acc[HEAD_DIM:HEAD_DIM + 1])
    o_ref[...] = jnp.concatenate(outs, axis=0).T.astype(BF16)


def _fox_attention(pt, k_aug, b, s, tq, hp):
    tk = tq // 2
    per = s // tq
    rows = hp * HEAD_DIM
    qblk = FOXQ_ROW // rows
    vblk = FOXV_ROW // rows
    return pl.pallas_call(
        functools.partial(_fox_kernel, tq=tq, tk=tk, hp=hp),
        grid=(b, FOX_HEADS // hp, per),
        in_specs=[
            pl.BlockSpec((rows, tq), lambda bi, g, i: (qblk + g, bi * per + i)),
            pl.BlockSpec((rows, tq), lambda bi, g, i: (qblk + g, bi * per + jnp.minimum(i + 1, per - 1))),
            pl.BlockSpec((1, hp, s, LANES), lambda bi, g, i: (bi, g, 0, 0), pipeline_mode=pl.Buffered(1)),
            pl.BlockSpec((rows, s), lambda bi, g, i: (vblk + g, bi), pipeline_mode=pl.Buffered(1)),
        ],
        out_specs=pl.BlockSpec((tq, rows), lambda bi, g, i: (bi * per + i, g)),
        out_shape=jax.ShapeDtypeStruct((b * s, BRANCH), BF16),
        scratch_shapes=[
            pltpu.VMEM((2, hp, LANES, tq), BF16),
            pltpu.VMEM((hp, V_ROWS, s), BF16),
            pltpu.VMEM((hp, 2, tk, tq + PITCH_PAD), F32),
            pltpu.VMEM((hp, 2, SUBLANES, tq), F32),
            pltpu.VMEM((hp, 2, tk, tq + PITCH_PAD), BF16),
            pltpu.VMEM((hp, V_ROWS, tq), F32),
        ],
        compiler_params=_params("arbitrary", "arbitrary", "arbitrary", vmem_limit=FOX_VMEM_LIMIT_BYTES),
        name="fox_attention",
    )(pt, pt, k_aug, pt)


def _swa_bias_kernel(rb_ref, bucket_ref, win_ref, o_ref):
    h = pl.program_id(0)
    bucket = bucket_ref[...]
    bias = jnp.zeros(bucket.shape, F32)
    for b in range(N_BUCKETS):
        bias = jnp.where(bucket == b, rb_ref[b, h] * LOG2E, bias)
    o_ref[0, 0] = jnp.where(win_ref[0] > 0, bias, NEG)
    o_ref[1, 0] = jnp.where(win_ref[1] > 0, bias, NEG)


def _swa_bias_table(rel_bias):
    tq = np.arange(WINDOW)[None, :]
    sk = np.arange(2 * WINDOW)[:, None]
    dist = WINDOW + tq - sk
    n = jnp.maximum(jnp.asarray(dist, jnp.int32), 0)
    max_exact = N_BUCKETS // 2
    large = max_exact + (jnp.log(jnp.maximum(n, 1).astype(F32) / max_exact)
                         / math.log(WINDOW / max_exact) * (N_BUCKETS - max_exact)).astype(jnp.int32)
    bucket = jnp.where(n < max_exact, n, jnp.minimum(large, N_BUCKETS - 1))
    in_window = (dist >= 0) & (dist < WINDOW)
    win = np.stack([in_window, in_window & (sk >= WINDOW)]).astype(np.int32)
    full = lambda shape: pl.BlockSpec(shape, lambda h: (0,) * len(shape))
    return pl.pallas_call(
        _swa_bias_kernel,
        grid=(SWA_HEADS,),
        in_specs=[pl.BlockSpec(memory_space=pltpu.SMEM), full(bucket.shape), full(win.shape)],
        out_specs=pl.BlockSpec((2, 1, 2 * WINDOW, WINDOW), lambda h: (0, h // SWA_GROUP, 0, h % SWA_GROUP)),
        out_shape=jax.ShapeDtypeStruct((2, SWA_KV, 2 * WINDOW, SWA_GROUP * WINDOW), F32),
        compiler_params=_params("arbitrary"),
        name="swa_bias_table",
    )(rel_bias, bucket, jnp.asarray(win))


def _swa_kernel(sink_ref, q_ref, kp_ref, kc_ref, vp_ref, vc_ref, bias_ref, o_ref,
                k_buf, v_buf, s_buf, mt_buf, p_buf, dn_buf, *, nblk):
    gw = SWA_GROUP * WINDOW
    ts = nblk * WINDOW
    first = pl.program_id(1) == 0
    k_buf[0:WINDOW, :] = kp_ref[...]
    k_buf[WINDOW:WINDOW + ts, :] = kc_ref[...]
    v_buf[:, 0:WINDOW] = vp_ref[...]
    v_buf[:, WINDOW:WINDOW + ts] = vc_ref[...]
    glane = lax.broadcasted_iota(jnp.int32, (1, gw), 1) // WINDOW
    zeros = jnp.zeros((HEAD_DIM, gw), BF16)
    sinks = []
    for kv in range(SWA_KV):
        sink = jnp.zeros((1, gw), F32)
        for g in range(SWA_GROUP):
            sink = jnp.where(glane == g, sink_ref[kv * SWA_GROUP + g] * LOG2E, sink)
        sinks.append(sink)

    def scores(blk, slot):
        lo = pl.multiple_of(blk * WINDOW, WINDOW)
        sel = jnp.logical_and(first, blk == 0).astype(jnp.int32)
        keys = k_buf[pl.ds(lo, 2 * WINDOW), :]
        for kv in range(SWA_KV):
            qg = jnp.concatenate([q_ref[(kv * SWA_GROUP + g) * HEAD_DIM:(kv * SWA_GROUP + g + 1) * HEAD_DIM,
                                        pl.ds(lo, WINDOW)] for g in range(SWA_GROUP)], axis=1)
            q_pad = jnp.concatenate([qg, zeros] if kv == 0 else [zeros, qg], axis=0)
            s = jnp.dot(keys, q_pad, preferred_element_type=F32) + bias_ref[sel, kv]
            s_buf[kv, slot, :, 0:gw] = s
            mt_buf[kv, slot, 0:1, :] = jnp.max(s, axis=0, keepdims=True)

    def softmax(slot):
        for kv in range(SWA_KV):
            m = jnp.maximum(mt_buf[kv, slot, 0:1, :], sinks[kv])
            p = jnp.exp2(s_buf[kv, slot, :, 0:gw] - m)
            dn_buf[kv, slot, 0:1, :] = jnp.sum(p, axis=0, keepdims=True) + jnp.exp2(sinks[kv] - m)
            p_buf[kv, slot, :, 0:gw] = p.astype(BF16)

    def values(blk, slot):
        lo = pl.multiple_of(blk * WINDOW, WINDOW)
        for kv in range(SWA_KV):
            vb = v_buf[kv * HEAD_DIM:(kv + 1) * HEAD_DIM, pl.ds(lo, 2 * WINDOW)]
            o = jnp.dot(vb, p_buf[kv, slot, :, 0:gw], preferred_element_type=F32) / dn_buf[kv, slot, 0:1, :]
            for pr in range(SWA_GROUP // 2):
                two = jnp.concatenate([o[:, (2 * pr) * WINDOW:(2 * pr + 1) * WINDOW],
                                       o[:, (2 * pr + 1) * WINDOW:(2 * pr + 2) * WINDOW]], axis=0)
                col = (kv * (SWA_GROUP // 2) + pr) * LANES
                o_ref[pl.ds(lo, WINDOW), col:col + LANES] = two.T.astype(BF16)

    scores(jnp.int32(0), 0)
    scores(jnp.int32(1), 1)
    softmax(0)

    def pair(u, carry):
        blk = 2 * u + 1
        scores(blk + 1, 0)
        softmax(1)
        values(blk - 1, 0)
        scores(blk + 2, 1)
        softmax(0)
        values(blk, 1)
        return carry

    lax.fori_loop(0, nblk // 2 - 1, pair, 0)
    softmax(1)
    values(jnp.int32(nblk - 2), 0)
    values(jnp.int32(nblk - 1), 1)


def _swa_attention(sink, pt, pn, bias, b, s, nblk):
    ts = nblk * WINDOW
    per = s // ts
    perw = s // WINDOW
    qblk = SWAQ_ROW // BRANCH
    vblk = SWAV_ROW // LANES
    kblk = SWAK_OFF // LANES
    prev = lambda bi, i: bi * perw + jnp.maximum(i * nblk - 1, 0)
    return pl.pallas_call(
        functools.partial(_swa_kernel, nblk=nblk),
        grid=(b, per),
        in_specs=[
            pl.BlockSpec(memory_space=pltpu.SMEM),
            pl.BlockSpec((BRANCH, ts), lambda bi, i: (qblk, bi * per + i)),
            pl.BlockSpec((WINDOW, LANES), lambda bi, i: (prev(bi, i), kblk)),
            pl.BlockSpec((ts, LANES), lambda bi, i: (bi * per + i, kblk)),
            pl.BlockSpec((LANES, WINDOW), lambda bi, i: (vblk, prev(bi, i))),
            pl.BlockSpec((LANES, ts), lambda bi, i: (vblk, bi * per + i)),
            _const_spec(bias.shape),
        ],
        out_specs=pl.BlockSpec((ts, BRANCH), lambda bi, i: (bi * per + i, 0)),
        out_shape=jax.ShapeDtypeStruct((b * s, BRANCH), BF16),
        scratch_shapes=[
            pltpu.VMEM((ts + WINDOW, LANES), BF16),
            pltpu.VMEM((LANES, ts + WINDOW), BF16),
            pltpu.VMEM((SWA_KV, 2, 2 * WINDOW, SWA_GROUP * WINDOW + PITCH_PAD), F32),
            pltpu.VMEM((SWA_KV, 2, SUBLANES, SWA_GROUP * WINDOW), F32),
            pltpu.VMEM((SWA_KV, 2, 2 * WINDOW, SWA_GROUP * WINDOW + PITCH_PAD), BF16),
            pltpu.VMEM((SWA_KV, 2, SUBLANES, SWA_GROUP * WINDOW), F32),
        ],
        compiler_params=_params("arbitrary", "arbitrary"),
        name="swa_attention",
    )(sink, pt, pn, pn, pt, pt, bias)


def _merge_tile(x, gate_ref, branches, wb_ref, wo_ref, m_scr):
    for lo in range(0, D_MODEL, MERGE_CHUNK):
        gates = [jax.nn.sigmoid(gate_ref[:, b * D_MODEL + lo:b * D_MODEL + lo + MERGE_CHUNK].astype(F32))
                 for b in range(len(branches))]
        merged = None
        for b, y in enumerate(branches):
            term = gates[b] * jnp.dot(y, wb_ref[b, :, lo:lo + MERGE_CHUNK], preferred_element_type=F32)
            merged = term if merged is None else merged + term
        m_scr[:, lo:lo + MERGE_CHUNK] = merged.astype(BF16)
    return x + jnp.dot(m_scr[...], wo_ref[...], preferred_element_type=F32)


def _xattn_tile(x, g_ref, wq_ref, kT_ref, v_ref, wo_ref):
    q = jnp.dot(_rms(x, g_ref[...]).astype(BF16), wq_ref[...], preferred_element_type=F32).astype(BF16)
    sls = [slice(h * X_HEAD_DIM, (h + 1) * X_HEAD_DIM) for h in range(X_HEADS)]
    scores = [jnp.dot(q[:, sl], kT_ref[0, sl, :], preferred_element_type=F32) for sl in sls]
    probs = [jnp.exp2(s - jnp.max(s, axis=-1, keepdims=True)) for s in scores]
    denoms = [jnp.sum(p, axis=-1, keepdims=True) for p in probs]
    outs = [jnp.dot(p.astype(BF16), v_ref[0, :, sl], preferred_element_type=F32) for p, sl in zip(probs, sls)]
    heads = [(o / d).astype(BF16) for o, d in zip(outs, denoms)]
    return x + jnp.dot(jnp.concatenate(heads, axis=1), wo_ref[...], preferred_element_type=F32)


def _swiglu_tile(x, g_ref, wg_ref, wu_ref, wd_ref):
    xn = _rms(x, g_ref[...]).astype(BF16)
    y = x
    for lo in range(0, wg_ref.shape[1], FFN_CHUNK):
        gate = jnp.dot(xn, wg_ref[:, lo:lo + FFN_CHUNK], preferred_element_type=F32)
        up = jnp.dot(xn, wu_ref[:, lo:lo + FFN_CHUNK], preferred_element_type=F32)
        hidden = (gate * jax.nn.sigmoid(gate) * up).astype(BF16)
        y = y + jnp.dot(hidden, wd_ref[lo:lo + FFN_CHUNK, :], preferred_element_type=F32)
    return y


def _layer_tail_kernel(x_ref, gate_ref, yc_ref, yf_ref, ys_ref, wb_ref, wo_ref, gx_ref, wq_ref, kT_ref, v_ref,
                       wxo_ref, gf_ref, wg_ref, wu_ref, wd_ref, gl_ref, o_ref, m_scr, *, final_norm):
    x = _merge_tile(x_ref[...], gate_ref, (yc_ref[...], yf_ref[...], ys_ref[...]), wb_ref, wo_ref, m_scr)
    x = _xattn_tile(x, gx_ref, wq_ref, kT_ref, v_ref, wxo_ref)
    x = _swiglu_tile(x, gf_ref, wg_ref, wu_ref, wd_ref)
    o_ref[...] = _rms(x, gl_ref[...]) if final_norm else x


def _layer_tail(x, pn, y_fox, y_swa, w_branch, w_out, gx, wq, kT, v, wxo, gf, wg, wu, wd, g_last, final_norm,
                seq, tm):
    n, d = x.shape
    tiles = seq // tm
    vec = _const_spec((1, d))
    return pl.pallas_call(
        functools.partial(_layer_tail_kernel, final_norm=final_norm),
        grid=(n // tm,),
        in_specs=[
            pl.BlockSpec((tm, d), lambda i: (i, 0)),
            pl.BlockSpec((tm, 3 * D_MODEL), lambda i: (i, GATE_OFF // (3 * D_MODEL))),
            pl.BlockSpec((tm, BRANCH), lambda i: (i, CONV_OFF // BRANCH)),
            pl.BlockSpec((tm, BRANCH), lambda i: (i, 0)),
            pl.BlockSpec((tm, BRANCH), lambda i: (i, 0)),
            _const_spec(w_branch.shape), _const_spec(w_out.shape),
            vec, _const_spec(wq.shape),
            pl.BlockSpec((1,) + kT.shape[1:], lambda i: (i // tiles, 0, 0)),
            pl.BlockSpec((1,) + v.shape[1:], lambda i: (i // tiles, 0, 0)),
            _const_spec(wxo.shape),
            vec, _const_spec(wg.shape), _const_spec(wu.shape), _const_spec(wd.shape), vec,
        ],
        out_specs=pl.BlockSpec((tm, d), lambda i: (i, 0)),
        out_shape=jax.ShapeDtypeStruct((n, d), F32),
        scratch_shapes=[pltpu.VMEM((tm, d), BF16)],
        compiler_params=_params("arbitrary", vmem_limit=TAIL_VMEM_LIMIT_BYTES),
        name="layer_tail",
    )(x, pn, pn, y_fox, y_swa, w_branch, w_out, gx.reshape(1, d), wq, kT, v, wxo,
      gf.reshape(1, d), wg, wu, wd, g_last.reshape(1, d))


def _pack_in_proj(w_in):
    sizes = [BRANCH] * 3 + [BRANCH] * 3 + [FOX_HEADS] + [BRANCH, SWA_KV * HEAD_DIM, SWA_KV * HEAD_DIM] + [3 * D_MODEL]
    offs = np.concatenate([[0], np.cumsum(sizes)])
    c_b, c_c, c_u, f_q, f_k, f_v, f_g, s_q, s_k, s_v, gates = (w_in[:, offs[t]:offs[t + 1]] for t in range(len(sizes)))
    qscale = HEAD_DIM ** -0.5 * LOG2E
    wn = jnp.concatenate([gates, s_k], axis=1)
    wc = jnp.concatenate([c_b, c_c, c_u], axis=1)
    wt = jnp.concatenate([f_q * qscale, f_v, s_q * qscale, s_v], axis=1).T
    wfg = jnp.pad(f_g, ((0, 0), (0, LANES - FOX_HEADS)))
    return wn.astype(BF16), wc.astype(BF16), f_k.astype(BF16), wt.astype(BF16), wfg.astype(BF16)


def kernel(x, mem, mix_norm_g, w_in, forget_bias, conv_w, sink, w_branch, w_mix_out, rel_bias,
           xattn_norm_g, mem_norm_g, w_xq, w_xkv, w_xo, ffn_norm_g, w_ffn_gate, w_ffn_up, w_ffn_down,
           final_norm_g):
    b, s, d = x.shape
    n = b * s
    depth = w_in.shape[0]
    assert d == D_MODEL and s % FOX_TQ == 0 and s % ROW_TILE == 0
    assert s % (SWA_BLOCKS_PER_STEP * WINDOW) == 0 and SWA_BLOCKS_PER_STEP % 2 == 0
    mem_len = mem.shape[1]
    xf = x.reshape(n, d)
    memf = mem.reshape(b * mem_len, d)
    swa_bias = _swa_bias_table(rel_bias)
    for l in range(depth):
        wn, wc, wk, wt, wfg = _pack_in_proj(w_in[l])
        pn, pt, k_aug = _in_proj(xf, mix_norm_g[l], wn, wc, wk, wt, wfg, conv_w[l], forget_bias[l],
                                 seq=s, tm=ROW_TILE)
        y_fox = _fox_attention(pt, k_aug, b, s, tq=FOX_TQ, hp=FOX_HEADS_PER_STEP)
        y_swa = _swa_attention(sink[l], pt, pn, swa_bias, b, s, nblk=SWA_BLOCKS_PER_STEP)
        kv = _norm_proj(memf, mem_norm_g[l], w_xkv[l].astype(BF16))
        kT = kv[:, :d].reshape(b, mem_len, d).transpose(0, 2, 1)
        v = kv[:, d:].reshape(b, mem_len, d)
        wq = (w_xq[l] * (X_HEAD_DIM ** -0.5 * LOG2E)).astype(BF16)
        xf = _layer_tail(xf, pn, y_fox, y_swa, w_branch[l].astype(BF16), w_mix_out[l].astype(BF16),
                         xattn_norm_g[l], wq, kT, v, w_xo[l].astype(BF16),
                         ffn_norm_g[l], w_ffn_gate[l].astype(BF16), w_ffn_up[l].astype(BF16),
                         w_ffn_down[l].astype(BF16), final_norm_g, final_norm=(l == depth - 1),
                         seq=s, tm=ROW_TILE)
    return xf.reshape(b, s, d)
```

```python
import functools
import math

import jax
import jax.numpy as jnp
import numpy as np
from jax import lax
from jax.experimental import pallas as pl
from jax.experimental.pallas import tpu as pltpu

F32 = jnp.float32
BF16 = jnp.bfloat16

D_MODEL = 1024
HEAD_DIM = 64
BRANCH = 512
FOX_HEADS = 8
SWA_HEADS = 8
SWA_KV = 2
SWA_GROUP = SWA_HEADS // SWA_KV
WINDOW = 128
N_BUCKETS = 32
X_HEADS = 4
X_HEAD_DIM = D_MODEL // X_HEADS
RMS_EPS = 1e-6
NEG = -1e30
LOG2E = math.log2(math.e)

LANES = 128
SUBLANES = 8
VMEM_LIMIT_BYTES = 56 * 1024 * 1024
FOX_VMEM_LIMIT_BYTES = 58 * 1024 * 1024
TAIL_VMEM_LIMIT_BYTES = 58 * 1024 * 1024
PITCH_PAD = LANES
MERGE_CHUNK = 256
FFN_CHUNK = 256

ROW_TILE = 512
FOX_TQ = 1024
FOX_HEADS_PER_STEP = 4
SWA_BLOCKS_PER_STEP = 16

GATE_OFF = 0
CONV_OFF = 3 * D_MODEL
SWAK_OFF = CONV_OFF + BRANCH
PN_COLS = SWAK_OFF + SWA_KV * HEAD_DIM
FOXQ_ROW = 0
FOXV_ROW = BRANCH
SWAQ_ROW = 2 * BRANCH
SWAV_ROW = 3 * BRANCH
PT_ROWS = SWAV_ROW + SWA_KV * HEAD_DIM

AUG_STRIDE = 8
AUG_EVEN = (HEAD_DIM, HEAD_DIM + AUG_STRIDE, HEAD_DIM + 2 * AUG_STRIDE)
AUG_ODD = (0, AUG_STRIDE, 2 * AUG_STRIDE)
V_ROWS = 80


def _params(*sem, vmem_limit=VMEM_LIMIT_BYTES):
    return pltpu.CompilerParams(dimension_semantics=sem, vmem_limit_bytes=vmem_limit)


def _rms(x, g):
    return x * lax.rsqrt(jnp.mean(x * x, axis=-1, keepdims=True) + RMS_EPS) * g


def _const_spec(shape):
    nd = len(shape)
    return pl.BlockSpec(shape, lambda *_: (0,) * nd, pipeline_mode=pl.Buffered(1))


def _one_hot_rows(shape, axis, positions):
    idx = lax.broadcasted_iota(jnp.int32, shape, axis)
    hit = idx == positions[0]
    for p in positions[1:]:
        hit = hit | (idx == p)
    return jnp.where(hit, 1.0, 0.0)


def _in_proj_kernel(x_ref, g_ref, wn_ref, wc_ref, wk_ref, wt_ref, wfg_ref, cw_ref, fb_ref,
                    pn_ref, pt_ref, ka_ref, z_tail, c_tail, *, tiles_per_seq):
    tm = x_ref.shape[0]
    first_of_seq = pl.program_id(0) % tiles_per_seq == 0

    @pl.when(pl.program_id(0) == 0)
    def _():
        z_tail[...] = jnp.zeros(z_tail.shape, F32)
        c_tail[...] = jnp.zeros(c_tail.shape, F32)

    h = _rms(x_ref[...], g_ref[...]).astype(BF16)

    fg = jnp.dot(h, wfg_ref[...], preferred_element_type=F32)
    k_fox = jnp.dot(h, wk_ref[...], preferred_element_type=F32)
    lane = lax.broadcasted_iota(jnp.int32, (tm, LANES), 1)
    trow = lax.broadcasted_iota(jnp.int32, (tm, LANES), 0)
    c = jnp.where(lane < FOX_HEADS, jax.nn.log_sigmoid(fg + fb_ref[...]) * LOG2E, 0.0)
    d = 1
    while d < tm:
        c = c + jnp.where(trow >= d, pltpu.roll(c, d, axis=0), 0.0)
        d *= 2
    c = c + jnp.where(first_of_seq, 0.0, c_tail[0:1, :])
    c_tail[0:1, :] = c[tm - 1:tm, :]
    c1 = c.astype(BF16).astype(F32)
    r1 = c - c1
    c2 = r1.astype(BF16).astype(F32)
    c3 = (r1 - c2).astype(BF16).astype(F32)
    packed = -jnp.where(lane < AUG_STRIDE, c1,
                        jnp.where(lane < 2 * AUG_STRIDE, pltpu.roll(c2, AUG_STRIDE, axis=1),
                                  pltpu.roll(c3, 2 * AUG_STRIDE, axis=1)))
    keep_even = _one_hot_rows((1, LANES), 1, AUG_EVEN)
    keep_odd = _one_hot_rows((1, LANES), 1, AUG_ODD)
    for hd in range(FOX_HEADS):
        k_pair = k_fox[:, (hd // 2) * LANES:(hd // 2 + 1) * LANES]
        if hd % 2 == 0:
            aug = pltpu.roll(packed, (AUG_EVEN[0] - hd) % LANES, axis=1) * keep_even
            out = jnp.where(lane < HEAD_DIM, k_pair, aug)
        else:
            aug = pltpu.roll(packed, (AUG_ODD[0] - hd) % LANES, axis=1) * keep_odd
            out = jnp.where(lane >= HEAD_DIM, k_pair, aug)
        ka_ref[0, hd] = out.astype(BF16)

    conv_b, conv_c, conv_u = (jnp.dot(h, wc_ref[:, t * BRANCH:(t + 1) * BRANCH], preferred_element_type=F32)
                              for t in range(3))
    z = conv_c * conv_u
    zh = jnp.where(first_of_seq, 0.0, z_tail[...])
    z_tail[...] = z[tm - SUBLANES:tm, :]
    row = lax.broadcasted_iota(jnp.int32, z.shape, 0)
    z1 = jnp.where(row == 0, zh[7:8], pltpu.roll(z, 1, axis=0))
    z2 = jnp.where(row == 0, zh[6:7], jnp.where(row == 1, zh[7:8], pltpu.roll(z, 2, axis=0)))
    cw = cw_ref[...]
    pn_ref[:, CONV_OFF:SWAK_OFF] = (conv_b * (cw[0:1] * z2 + cw[1:2] * z1 + cw[2:3] * z)).astype(BF16)

    for lo in range(0, CONV_OFF, D_MODEL):
        pn_ref[:, lo:lo + D_MODEL] = jnp.dot(h, wn_ref[:, lo:lo + D_MODEL], preferred_element_type=F32).astype(BF16)
    pn_ref[:, SWAK_OFF:PN_COLS] = jnp.dot(h, wn_ref[:, CONV_OFF:], preferred_element_type=F32).astype(BF16)
    pt_ref[...] = lax.dot_general(wt_ref[...], h, (((1,), (1,)), ((), ())),
                                  preferred_element_type=F32).astype(BF16)


def _in_proj(x, g, wn, wc, wk, wt, wfg, conv_w, forget_bias, seq, tm):
    n, d = x.shape
    tiles = seq // tm
    fb = jnp.pad(forget_bias, (0, LANES - FOX_HEADS)).reshape(1, LANES)
    return pl.pallas_call(
        functools.partial(_in_proj_kernel, tiles_per_seq=tiles),
        grid=(n // tm,),
        in_specs=[
            pl.BlockSpec((tm, d), lambda i: (i, 0)),
            _const_spec((1, d)), _const_spec(wn.shape), _const_spec(wc.shape), _const_spec(wk.shape),
            _const_spec(wt.shape), _const_spec(wfg.shape), _const_spec(conv_w.shape), _const_spec((1, LANES)),
        ],
        out_specs=[
            pl.BlockSpec((tm, PN_COLS), lambda i: (i, 0)),
            pl.BlockSpec((PT_ROWS, tm), lambda i: (0, i)),
            pl.BlockSpec((1, FOX_HEADS, tm, LANES), lambda i: (i // tiles, 0, i % tiles, 0)),
        ],
        out_shape=[jax.ShapeDtypeStruct((n, PN_COLS), BF16), jax.ShapeDtypeStruct((PT_ROWS, n), BF16),
                   jax.ShapeDtypeStruct((n // seq, FOX_HEADS, seq, LANES), BF16)],
        scratch_shapes=[pltpu.VMEM((SUBLANES, BRANCH), F32), pltpu.VMEM((SUBLANES, LANES), F32)],
        compiler_params=_params("arbitrary"),
        name="in_proj",
    )(x, g.reshape(1, d), wn, wc, wk, wt, wfg, conv_w, fb)


def _norm_proj_kernel(x_ref, g_ref, w_ref, o_ref):
    h = _rms(x_ref[...], g_ref[...]).astype(BF16)
    o_ref[...] = jnp.dot(h, w_ref[...], preferred_element_type=F32).astype(BF16)


def _norm_proj(x, g, w):
    n, d = x.shape
    cols = w.shape[1]
    return pl.pallas_call(
        _norm_proj_kernel,
        grid=(1,),
        in_specs=[_const_spec((n, d)), _const_spec((1, d)), _const_spec(w.shape)],
        out_specs=pl.BlockSpec((n, cols), lambda i: (0, 0)),
        out_shape=jax.ShapeDtypeStruct((n, cols), BF16),
        compiler_params=_params("arbitrary"),
        name="mem_proj",
    )(x, g.reshape(1, d), w)


def _fox_kernel(q_ref, qn_ref, k_ref, v_ref, o_ref, q_aug, v_aug, s_buf, mt_buf, p_buf, acc_ref, *, tq, tk, hp):
    i = pl.program_id(2)
    seq = v_ref.shape[1]
    cur = i % 2
    heads = tuple(range(hp))

    def build_queries(src_ref, qslot):
        qrow = lax.broadcasted_iota(jnp.int32, (LANES, tq), 0)
        for pr in range(hp // 2):
            qb = src_ref[pr * LANES:(pr + 1) * LANES, :].astype(F32)
            q_aug[qslot, 2 * pr] = jnp.where(qrow < HEAD_DIM, qb, _one_hot_rows((LANES, tq), 0, AUG_EVEN)).astype(BF16)
            q_aug[qslot, 2 * pr + 1] = jnp.where(qrow >= HEAD_DIM, qb,
                                                 _one_hot_rows((LANES, tq), 0, AUG_ODD)).astype(BF16)

    def scores(t, slot, qslot):
        off = pl.multiple_of(t * tk, tk)
        for hh in heads:
            s = jnp.dot(k_ref[0, hh, pl.ds(off, tk), :], q_aug[qslot, hh], preferred_element_type=F32)
            s_buf[hh, slot, :, 0:tq] = s
            mt_buf[hh, slot, 0:1, :] = jnp.max(s, axis=0, keepdims=True)

    def softmax(slot, ms, mask=None):
        out = []
        for hh in heads:
            s = s_buf[hh, slot, :, 0:tq]
            if mask is None:
                tile_max = mt_buf[hh, slot, 0:1, :]
            else:
                s = jnp.where(mask, s, NEG)
                tile_max = jnp.max(s, axis=0, keepdims=True)
            m_new = jnp.maximum(ms[hh], tile_max)
            p_buf[hh, slot, :, 0:tq] = jnp.exp2(s - m_new).astype(BF16)
            out.append((m_new, jnp.exp2(ms[hh] - m_new)))
        return tuple(o[0] for o in out), tuple(o[1] for o in out)

    def values(t, slot, alphas):
        off = pl.multiple_of(jnp.maximum(t, 0) * tk, tk)
        for hh in heads:
            pv = jnp.dot(v_aug[hh, :, pl.ds(off, tk)], p_buf[hh, slot, :, 0:tq], preferred_element_type=F32)
            acc_ref[hh] = alphas[hh] * acc_ref[hh] + pv

    @pl.when(i == 0)
    def _():
        ones_row = _one_hot_rows((V_ROWS - HEAD_DIM, seq), 0, (0,)).astype(BF16)
        for hh in heads:
            v_aug[hh, 0:HEAD_DIM, :] = v_ref[hh * HEAD_DIM:(hh + 1) * HEAD_DIM, :]
            v_aug[hh, HEAD_DIM:V_ROWS, :] = ones_row
            p_buf[hh, 1, :, 0:tq] = jnp.zeros((tk, tq), BF16)
            acc_ref[hh] = jnp.zeros(acc_ref.shape[1:], F32)
        build_queries(q_ref, 0)
        scores(0, 0, 0)

    def pair(u, carry):
        ms, alpha_prev = carry
        t = 2 * u
        scores(t + 1, 1, cur)
        ms, alpha0 = softmax(0, ms)
        values(t - 1, 1, alpha_prev)
        scores(t + 2, 0, cur)
        ms, alpha1 = softmax(1, ms)
        values(t, 0, alpha0)
        return ms, alpha1

    carry = (tuple(jnp.full((1, tq), NEG, F32) for _ in heads), tuple(jnp.ones((1, tq), F32) for _ in heads))
    ms, alpha_prev = lax.fori_loop(0, i, pair, carry)

    t = 2 * i
    key = lax.broadcasted_iota(jnp.int32, (tk, tq), 0)
    qry = lax.broadcasted_iota(jnp.int32, (tk, tq), 1)
    off_b = pl.multiple_of((t + 1) * tk, tk)
    tri = lax.broadcasted_iota(jnp.int32, (tk, tk), 0) <= lax.broadcasted_iota(jnp.int32, (tk, tk), 1)
    s_b = [jnp.where(tri, jnp.dot(k_ref[0, hh, pl.ds(off_b, tk), :], q_aug[cur, hh, :, tk:],
                                  preferred_element_type=F32), NEG) for hh in heads]
    ms, alpha0 = softmax(0, ms, key <= qry)
    build_queries(qn_ref, 1 - cur)
    scores(0, 0, 1 - cur)
    values(t - 1, 1, alpha_prev)
    values(t, 0, alpha0)
    outs = []
    for hh in heads:
        mt_buf[hh, 1, 0:1, :] = ms[hh]
        m_old = mt_buf[hh, 1, 0:1, tk:]
        m_new = jnp.maximum(m_old, jnp.max(s_b[hh], axis=0, keepdims=True))
        p_b = jnp.exp2(s_b[hh] - m_new).astype(BF16)
        pv = jnp.dot(v_aug[hh, :, pl.ds(off_b, tk)], p_b, preferred_element_type=F32)
        acc_ref[hh, :, tk:] = jnp.exp2(m_old - m_new) * acc_ref[hh, :, tk:] + pv
        acc = acc_ref[hh]
        outs.append(acc[:HEAD_DIM] / acc[HEAD_DIM:HEAD_DIM + 1])
    o_ref[...] = jnp.concatenate(outs, axis=0).T.astype(BF16)


def _fox_attention(pt, k_aug, b, s, tq, hp):
    tk = tq // 2
    per = s // tq
    rows = hp * HEAD_DIM
    qblk = FOXQ_ROW // rows
    vblk = FOXV_ROW // rows
    return pl.pallas_call(
        functools.partial(_fox_kernel, tq=tq, tk=tk, hp=hp),
        grid=(b, FOX_HEADS // hp, per),
        in_specs=[
            pl.BlockSpec((rows, tq), lambda bi, g, i: (qblk + g, bi * per + i)),
            pl.BlockSpec((rows, tq), lambda bi, g, i: (qblk + g, bi * per + jnp.minimum(i + 1, per - 1))),
            pl.BlockSpec((1, hp, s, LANES), lambda bi, g, i: (bi, g, 0, 0), pipeline_mode=pl.Buffered(1)),
            pl.BlockSpec((rows, s), lambda bi, g, i: (vblk + g, bi), pipeline_mode=pl.Buffered(1)),
        ],
        out_specs=pl.BlockSpec((tq, rows), lambda bi, g, i: (bi * per + i, g)),
        out_shape=jax.ShapeDtypeStruct((b * s, BRANCH), BF16),
        scratch_shapes=[
            pltpu.VMEM((2, hp, LANES, tq), BF16),
            pltpu.VMEM((hp, V_ROWS, s), BF16),
            pltpu.VMEM((hp, 2, tk, tq + PITCH_PAD), F32),
            pltpu.VMEM((hp, 2, SUBLANES, tq), F32),
            pltpu.VMEM((hp, 2, tk, tq + PITCH_PAD), BF16),
            pltpu.VMEM((hp, V_ROWS, tq), F32),
        ],
        compiler_params=_params("arbitrary", "arbitrary", "arbitrary", vmem_limit=FOX_VMEM_LIMIT_BYTES),
        name="fox_attention",
    )(pt, pt, k_aug, pt)


def _swa_bias_kernel(rb_ref, bucket_ref, win_ref, o_ref):
    h = pl.program_id(0)
    bucket = bucket_ref[...]
    bias = jnp.zeros(bucket.shape, F32)
    for b in range(N_BUCKETS):
        bias = jnp.where(bucket == b, rb_ref[b, h] * LOG2E, bias)
    o_ref[0, 0] = jnp.where(win_ref[0] > 0, bias, NEG)
    o_ref[1, 0] = jnp.where(win_ref[1] > 0, bias, NEG)


def _swa_bias_table(rel_bias):
    tq = np.arange(WINDOW)[None, :]
    sk = np.arange(2 * WINDOW)[:, None]
    dist = WINDOW + tq - sk
    n = jnp.maximum(jnp.asarray(dist, jnp.int32), 0)
    max_exact = N_BUCKETS // 2
    large = max_exact + (jnp.log(jnp.maximum(n, 1).astype(F32) / max_exact)
                         / math.log(WINDOW / max_exact) * (N_BUCKETS - max_exact)).astype(jnp.int32)
    bucket = jnp.where(n < max_exact, n, jnp.minimum(large, N_BUCKETS - 1))
    in_window = (dist >= 0) & (dist < WINDOW)
    win = np.stack([in_window, in_window & (sk >= WINDOW)]).astype(np.int32)
    full = lambda shape: pl.BlockSpec(shape, lambda h: (0,) * len(shape))
    return pl.pallas_call(
        _swa_bias_kernel,
        grid=(SWA_HEADS,),
        in_specs=[pl.BlockSpec(memory_space=pltpu.SMEM), full(bucket.shape), full(win.shape)],
        out_specs=pl.BlockSpec((2, 1, 2 * WINDOW, WINDOW), lambda h: (0, h // SWA_GROUP, 0, h % SWA_GROUP)),
        out_shape=jax.ShapeDtypeStruct((2, SWA_KV, 2 * WINDOW, SWA_GROUP * WINDOW), F32),
        compiler_params=_params("arbitrary"),
        name="swa_bias_table",
    )(rel_bias, bucket, jnp.asarray(win))


def _swa_kernel(sink_ref, q_ref, kp_ref, kc_ref, vp_ref, vc_ref, bias_ref, o_ref,
                k_buf, v_buf, s_buf, mt_buf, p_buf, dn_buf, *, nblk):
    gw = SWA_GROUP * WINDOW
    ts = nblk * WINDOW
    first = pl.program_id(1) == 0
    k_buf[0:WINDOW, :] = kp_ref[...]
    k_buf[WINDOW:WINDOW + ts, :] = kc_ref[...]
    v_buf[:, 0:WINDOW] = vp_ref[...]
    v_buf[:, WINDOW:WINDOW + ts] = vc_ref[...]
    glane = lax.broadcasted_iota(jnp.int32, (1, gw), 1) // WINDOW
    zeros = jnp.zeros((HEAD_DIM, gw), BF16)
    sinks = []
    for kv in range(SWA_KV):
        sink = jnp.zeros((1, gw), F32)
        for g in range(SWA_GROUP):
            sink = jnp.where(glane == g, sink_ref[kv * SWA_GROUP + g] * LOG2E, sink)
        sinks.append(sink)

    def scores(blk, slot):
        lo = pl.multiple_of(blk * WINDOW, WINDOW)
        sel = jnp.logical_and(first, blk == 0).astype(jnp.int32)
        keys = k_buf[pl.ds(lo, 2 * WINDOW), :]
        for kv in range(SWA_KV):
            qg = jnp.concatenate([q_ref[(kv * SWA_GROUP + g) * HEAD_DIM:(kv * SWA_GROUP + g + 1) * HEAD_DIM,
                                        pl.ds(lo, WINDOW)] for g in range(SWA_GROUP)], axis=1)
            q_pad = jnp.concatenate([qg, zeros] if kv == 0 else [zeros, qg], axis=0)
            s = jnp.dot(keys, q_pad, preferred_element_type=F32) + bias_ref[sel, kv]
            s_buf[kv, slot, :, 0:gw] = s
            mt_buf[kv, slot, 0:1, :] = jnp.max(s, axis=0, keepdims=True)

    def softmax(slot):
        for kv in range(SWA_KV):
            m = jnp.maximum(mt_buf[kv, slot, 0:1, :], sinks[kv])
            p = jnp.exp2(s_buf[kv, slot, :, 0:gw] - m)
            dn_buf[kv, slot, 0:1, :] = jnp.sum(p, axis=0, keepdims=True) + jnp.exp2(sinks[kv] - m)
            p_buf[kv, slot, :, 0:gw] = p.astype(BF16)

    def values(blk, slot):
        lo = pl.multiple_of(blk * WINDOW, WINDOW)
        for kv in range(SWA_KV):
            vb = v_buf[kv * HEAD_DIM:(kv + 1) * HEAD_DIM, pl.ds(lo, 2 * WINDOW)]
            o = jnp.dot(vb, p_buf[kv, slot, :, 0:gw], preferred_element_type=F32) / dn_buf[kv, slot, 0:1, :]
            for pr in range(SWA_GROUP // 2):
                two = jnp.concatenate([o[:, (2 * pr) * WINDOW:(2 * pr + 1) * WINDOW],
                                       o[:, (2 * pr + 1) * WINDOW:(2 * pr + 2) * WINDOW]], axis=0)
                col = (kv * (SWA_GROUP // 2) + pr) * LANES
                o_ref[pl.ds(lo, WINDOW), col:col + LANES] = two.T.astype(BF16)

    scores(jnp.int32(0), 0)
    scores(jnp.int32(1), 1)
    softmax(0)

    def pair(u, carry):
        blk = 2 * u + 1
        scores(blk + 1, 0)
        softmax(1)
        values(blk - 1, 0)
        scores(blk + 2, 1)
        softmax(0)
        values(blk, 1)
        return carry

    lax.fori_loop(0, nblk // 2 - 1, pair, 0)
    softmax(1)
    values(jnp.int32(nblk - 2), 0)
    values(jnp.int32(nblk - 1), 1)


def _swa_attention(sink, pt, pn, bias, b, s, nblk):
    ts = nblk * WINDOW
    per = s // ts
    perw = s // WINDOW
    qblk = SWAQ_ROW // BRANCH
    vblk = SWAV_ROW // LANES
    kblk = SWAK_OFF // LANES
    prev = lambda bi, i: bi * perw + jnp.maximum(i * nblk - 1, 0)
    return pl.pallas_call(
        functools.partial(_swa_kernel, nblk=nblk),
        grid=(b, per),
        in_specs=[
            pl.BlockSpec(memory_space=pltpu.SMEM),
            pl.BlockSpec((BRANCH, ts), lambda bi, i: (qblk, bi * per + i)),
            pl.BlockSpec((WINDOW, LANES), lambda bi, i: (prev(bi, i), kblk)),
            pl.BlockSpec((ts, LANES), lambda bi, i: (bi * per + i, kblk)),
            pl.BlockSpec((LANES, WINDOW), lambda bi, i: (vblk, prev(bi, i))),
            pl.BlockSpec((LANES, ts), lambda bi, i: (vblk, bi * per + i)),
            _const_spec(bias.shape),
        ],
        out_specs=pl.BlockSpec((ts, BRANCH), lambda bi, i: (bi * per + i, 0)),
        out_shape=jax.ShapeDtypeStruct((b * s, BRANCH), BF16),
        scratch_shapes=[
            pltpu.VMEM((ts + WINDOW, LANES), BF16),
            pltpu.VMEM((LANES, ts + WINDOW), BF16),
            pltpu.VMEM((SWA_KV, 2, 2 * WINDOW, SWA_GROUP * WINDOW + PITCH_PAD), F32),
            pltpu.VMEM((SWA_KV, 2, SUBLANES, SWA_GROUP * WINDOW), F32),
            pltpu.VMEM((SWA_KV, 2, 2 * WINDOW, SWA_GROUP * WINDOW + PITCH_PAD), BF16),
            pltpu.VMEM((SWA_KV, 2, SUBLANES, SWA_GROUP * WINDOW), F32),
        ],
        compiler_params=_params("arbitrary", "arbitrary"),
        name="swa_attention",
    )(sink, pt, pn, pn, pt, pt, bias)


def _merge_tile(x, gate_ref, branches, wb_ref, wo_ref, m_scr):
    for lo in range(0, D_MODEL, MERGE_CHUNK):
        gates = [jax.nn.sigmoid(gate_ref[:, b * D_MODEL + lo:b * D_MODEL + lo + MERGE_CHUNK].astype(F32))
                 for b in range(len(branches))]
        merged = None
        for b, y in enumerate(branches):
            term = gates[b] * jnp.dot(y, wb_ref[b, :, lo:lo + MERGE_CHUNK], preferred_element_type=F32)
            merged = term if merged is None else merged + term
        m_scr[:, lo:lo + MERGE_CHUNK] = merged.astype(BF16)
    return x + jnp.dot(m_scr[...], wo_ref[...], preferred_element_type=F32)


def _xattn_tile(x, g_ref, wq_ref, kT_ref, v_ref, wo_ref):
    q = jnp.dot(_rms(x, g_ref[...]).astype(BF16), wq_ref[...], preferred_element_type=F32).astype(BF16)
    sls = [slice(h * X_HEAD_DIM, (h + 1) * X_HEAD_DIM) for h in range(X_HEADS)]
    scores = [jnp.dot(q[:, sl], kT_ref[0, sl, :], preferred_element_type=F32) for sl in sls]
    probs = [jnp.exp2(s - jnp.max(s, axis=-1, keepdims=True)) for s in scores]
    denoms = [jnp.sum(p, axis=-1, keepdims=True) for p in probs]
    outs = [jnp.dot(p.astype(BF16), v_ref[0, :, sl], preferred_element_type=F32) for p, sl in zip(probs, sls)]
    heads = [(o / d).astype(BF16) for o, d in zip(outs, denoms)]
    return x + jnp.dot(jnp.concatenate(heads, axis=1), wo_ref[...], preferred_element_type=F32)


def _swiglu_tile(x, g_ref, wg_ref, wu_ref, wd_ref):
    xn = _rms(x, g_ref[...]).astype(BF16)
    y = x
    for lo in range(0, wg_ref.shape[1], FFN_CHUNK):
        gate = jnp.dot(xn, wg_ref[:, lo:lo + FFN_CHUNK], preferred_element_type=F32)
        up = jnp.dot(xn, wu_ref[:, lo:lo + FFN_CHUNK], preferred_element_type=F32)
        hidden = (gate * jax.nn.sigmoid(gate) * up).astype(BF16)
        y = y + jnp.dot(hidden, wd_ref[lo:lo + FFN_CHUNK, :], preferred_element_type=F32)
    return y


def _layer_tail_kernel(x_ref, gate_ref, yc_ref, yf_ref, ys_ref, wb_ref, wo_ref, gx_ref, wq_ref, kT_ref, v_ref,
                       wxo_ref, gf_ref, wg_ref, wu_ref, wd_ref, gl_ref, o_ref, m_scr, *, final_norm):
    x = _merge_tile(x_ref[...], gate_ref, (yc_ref[...], yf_ref[...], ys_ref[...]), wb_ref, wo_ref, m_scr)
    x = _xattn_tile(x, gx_ref, wq_ref, kT_ref, v_ref, wxo_ref)
    x = _swiglu_tile(x, gf_ref, wg_ref, wu_ref, wd_ref)
    o_ref[...] = _rms(x, gl_ref[...]) if final_norm else x


def _layer_tail(x, pn, y_fox, y_swa, w_branch, w_out, gx, wq, kT, v, wxo, gf, wg, wu, wd, g_last, final_norm,
                seq, tm):
    n, d = x.shape
    tiles = seq // tm
    vec = _const_spec((1, d))
    return pl.pallas_call(
        functools.partial(_layer_tail_kernel, final_norm=final_norm),
        grid=(n // tm,),
        in_specs=[
            pl.BlockSpec((tm, d), lambda i: (i, 0)),
            pl.BlockSpec((tm, 3 * D_MODEL), lambda i: (i, GATE_OFF // (3 * D_MODEL))),
            pl.BlockSpec((tm, BRANCH), lambda i: (i, CONV_OFF // BRANCH)),
            pl.BlockSpec((tm, BRANCH), lambda i: (i, 0)),
            pl.BlockSpec((tm, BRANCH), lambda i: (i, 0)),
            _const_spec(w_branch.shape), _const_spec(w_out.shape),
            vec, _const_spec(wq.shape),
            pl.BlockSpec((1,) + kT.shape[1:], lambda i: (i // tiles, 0, 0)),
            pl.BlockSpec((1,) + v.shape[1:], lambda i: (i // tiles, 0, 0)),
            _const_spec(wxo.shape),
            vec, _const_spec(wg.shape), _const_spec(wu.shape), _const_spec(wd.shape), vec,
        ],
        out_specs=pl.BlockSpec((tm, d), lambda i: (i, 0)),
        out_shape=jax.ShapeDtypeStruct((n, d), F32),
        scratch_shapes=[pltpu.VMEM((tm, d), BF16)],
        compiler_params=_params("arbitrary", vmem_limit=TAIL_VMEM_LIMIT_BYTES),
        name="layer_tail",
    )(x, pn, pn, y_fox, y_swa, w_branch, w_out, gx.reshape(1, d), wq, kT, v, wxo,
      gf.reshape(1, d), wg, wu, wd, g_last.reshape(1, d))


def _pack_in_proj(w_in):
    sizes = [BRANCH] * 3 + [BRANCH] * 3 + [FOX_HEADS] + [BRANCH, SWA_KV * HEAD_DIM, SWA_KV * HEAD_DIM] + [3 * D_MODEL]
    offs = np.concatenate([[0], np.cumsum(sizes)])
    c_b, c_c, c_u, f_q, f_k, f_v, f_g, s_q, s_k, s_v, gates = (w_in[:, offs[t]:offs[t + 1]] for t in range(len(sizes)))
    qscale = HEAD_DIM ** -0.5 * LOG2E
    wn = jnp.concatenate([gates, s_k], axis=1)
    wc = jnp.concatenate([c_b, c_c, c_u], axis=1)
    wt = jnp.concatenate([f_q * qscale, f_v, s_q * qscale, s_v], axis=1).T
    wfg = jnp.pad(f_g, ((0, 0), (0, LANES - FOX_HEADS)))
    return wn.astype(BF16), wc.astype(BF16), f_k.astype(BF16), wt.astype(BF16), wfg.astype(BF16)


def kernel(x, mem, mix_norm_g, w_in, forget_bias, conv_w, sink, w_branch, w_mix_out, rel_bias,
           xattn_norm_g, mem_norm_g, w_xq, w_xkv, w_xo, ffn_norm_g, w_ffn_gate, w_ffn_up, w_ffn_down,
           final_norm_g):
    b, s, d = x.shape
    n = b * s
    depth = w_in.shape[0]
    assert d == D_MODEL and s % FOX_TQ == 0 and s % ROW_TILE == 0
    assert s % (SWA_BLOCKS_PER_STEP * WINDOW) == 0 and SWA_BLOCKS_PER_STEP % 2 == 0
    mem_len = mem.shape[1]
    xf = x.reshape(n, d)
    memf = mem.reshape(b * mem_len, d)
    swa_bias = _swa_bias_table(rel_bias)
    for l in range(depth):
        wn, wc, wk, wt, wfg = _pack_in_proj(w_in[l])
        pn, pt, k_aug = _in_proj(xf, mix_norm_g[l], wn, wc, wk, wt, wfg, conv_w[l], forget_bias[l],
                                 seq=s, tm=ROW_TILE)
        y_fox = _fox_attention(pt, k_aug, b, s, tq=FOX_TQ, hp=FOX_HEADS_PER_STEP)
        y_swa = _swa_attention(sink[l], pt, pn, swa_bias, b, s, nblk=SWA_BLOCKS_PER_STEP)
        kv = _norm_proj(memf, mem_norm_g[l], w_xkv[l].astype(BF16))
        kT = kv[:, :d].reshape(b, mem_len, d).transpose(0, 2, 1)
        v = kv[:, d:].reshape(b, mem_len, d)
        wq = (w_xq[l] * (X_HEAD_DIM ** -0.5 * LOG2E)).astype(BF16)
        xf = _layer_tail(xf, pn, y_fox, y_swa, w_branch[l].astype(BF16), w_mix_out[l].astype(BF16),
                         xattn_norm_g[l], wq, kT, v, w_xo[l].astype(BF16),
                         ffn_norm_g[l], w_ffn_gate[l].astype(BF16), w_ffn_up[l].astype(BF16),
                         w_ffn_down[l].astype(BF16), final_norm_g, final_norm=(l == depth - 1),
                         seq=s, tm=ROW_TILE)
    return xf.reshape(b, s, d)
```

```python
import functools
import math

import jax
import jax.numpy as jnp
import numpy as np
from jax import lax
from jax.experimental import pallas as pl
from jax.experimental.pallas import tpu as pltpu

F32 = jnp.float32
BF16 = jnp.bfloat16

D_MODEL = 1024
HEAD_DIM = 64
BRANCH = 512
FOX_HEADS = 8
SWA_HEADS = 8
SWA_KV = 2
SWA_GROUP = SWA_HEADS // SWA_KV
WINDOW = 128
N_BUCKETS = 32
X_HEADS = 4
X_HEAD_DIM = D_MODEL // X_HEADS
RMS_EPS = 1e-6
NEG = -1e30
LOG2E = math.log2(math.e)

LANES = 128
SUBLANES = 8
VMEM_LIMIT_BYTES = 56 * 1024 * 1024
FOX_VMEM_LIMIT_BYTES = 58 * 1024 * 1024
TAIL_VMEM_LIMIT_BYTES = 58 * 1024 * 1024
PITCH_PAD = LANES
MERGE_CHUNK = 256
FFN_CHUNK = 256

ROW_TILE = 512
FOX_TQ = 1024
FOX_HEADS_PER_STEP = 4
SWA_BLOCKS_PER_STEP = 16

GATE_OFF = 0
CONV_OFF = 3 * D_MODEL
SWAK_OFF = CONV_OFF + BRANCH
PN_COLS = SWAK_OFF + SWA_KV * HEAD_DIM
FOXQ_ROW = 0
FOXV_ROW = BRANCH
SWAQ_ROW = 2 * BRANCH
SWAV_ROW = 3 * BRANCH
PT_ROWS = SWAV_ROW + SWA_KV * HEAD_DIM

AUG_STRIDE = 8
AUG_EVEN = (HEAD_DIM, HEAD_DIM + AUG_STRIDE, HEAD_DIM + 2 * AUG_STRIDE)
AUG_ODD = (0, AUG_STRIDE, 2 * AUG_STRIDE)
V_ROWS = 80


def _params(*sem, vmem_limit=VMEM_LIMIT_BYTES):
    return pltpu.CompilerParams(dimension_semantics=sem, vmem_limit_bytes=vmem_limit)


def _rms(x, g):
    return x * lax.rsqrt(jnp.mean(x * x, axis=-1, keepdims=True) + RMS_EPS) * g


def _const_spec(shape):
    nd = len(shape)
    return pl.BlockSpec(shape, lambda *_: (0,) * nd, pipeline_mode=pl.Buffered(1))


def _one_hot_rows(shape, axis, positions):
    idx = lax.broadcasted_iota(jnp.int32, shape, axis)
    hit = idx == positions[0]
    for p in positions[1:]:
        hit = hit | (idx == p)
    return jnp.where(hit, 1.0, 0.0)


def _in_proj_kernel(x_ref, g_ref, wn_ref, wc_ref, wk_ref, wt_ref, wfg_ref, cw_ref, fb_ref,
                    pn_ref, pt_ref, ka_ref, z_tail, c_tail, *, tiles_per_seq):
    tm = x_ref.shape[0]
    first_of_seq = pl.program_id(0) % tiles_per_seq == 0

    @pl.when(pl.program_id(0) == 0)
    def _():
        z_tail[...] = jnp.zeros(z_tail.shape, F32)
        c_tail[...] = jnp.zeros(c_tail.shape, F32)

    h = _rms(x_ref[...], g_ref[...]).astype(BF16)

    fg = jnp.dot(h, wfg_ref[...], preferred_element_type=F32)
    k_fox = jnp.dot(h, wk_ref[...], preferred_element_type=F32)
    lane = lax.broadcasted_iota(jnp.int32, (tm, LANES), 1)
    trow = lax.broadcasted_iota(jnp.int32, (tm, LANES), 0)
    c = jnp.where(lane < FOX_HEADS, jax.nn.log_sigmoid(fg + fb_ref[...]) * LOG2E, 0.0)
    d = 1
    while d < tm:
        c = c + jnp.where(trow >= d, pltpu.roll(c, d, axis=0), 0.0)
        d *= 2
    c = c + jnp.where(first_of_seq, 0.0, c_tail[0:1, :])
    c_tail[0:1, :] = c[tm - 1:tm, :]
    c1 = c.astype(BF16).astype(F32)
    r1 = c - c1
    c2 = r1.astype(BF16).astype(F32)
    c3 = (r1 - c2).astype(BF16).astype(F32)
    packed = -jnp.where(lane < AUG_STRIDE, c1,
                        jnp.where(lane < 2 * AUG_STRIDE, pltpu.roll(c2, AUG_STRIDE, axis=1),
                                  pltpu.roll(c3, 2 * AUG_STRIDE, axis=1)))
    keep_even = _one_hot_rows((1, LANES), 1, AUG_EVEN)
    keep_odd = _one_hot_rows((1, LANES), 1, AUG_ODD)
    for hd in range(FOX_HEADS):
        k_pair = k_fox[:, (hd // 2) * LANES:(hd // 2 + 1) * LANES]
        if hd % 2 == 0:
            aug = pltpu.roll(packed, (AUG_EVEN[0] - hd) % LANES, axis=1) * keep_even
            out = jnp.where(lane < HEAD_DIM, k_pair, aug)
        else:
            aug = pltpu.roll(packed, (AUG_ODD[0] - hd) % LANES, axis=1) * keep_odd
            out = jnp.where(lane >= HEAD_DIM, k_pair, aug)
        ka_ref[0, hd] = out.astype(BF16)

    conv_b, conv_c, conv_u = (jnp.dot(h, wc_ref[:, t * BRANCH:(t + 1) * BRANCH], preferred_element_type=F32)
                              for t in range(3))
    z = conv_c * conv_u
    zh = jnp.where(first_of_seq, 0.0, z_tail[...])
    z_tail[...] = z[tm - SUBLANES:tm, :]
    row = lax.broadcasted_iota(jnp.int32, z.shape, 0)
    z1 = jnp.where(row == 0, zh[7:8], pltpu.roll(z, 1, axis=0))
    z2 = jnp.where(row == 0, zh[6:7], jnp.where(row == 1, zh[7:8], pltpu.roll(z, 2, axis=0)))
    cw = cw_ref[...]
    pn_ref[:, CONV_OFF:SWAK_OFF] = (conv_b * (cw[0:1] * z2 + cw[1:2] * z1 + cw[2:3] * z)).astype(BF16)

    for lo in range(0, CONV_OFF, D_MODEL):
        pn_ref[:, lo:lo + D_MODEL] = jnp.dot(h, wn_ref[:, lo:lo + D_MODEL], preferred_element_type=F32).astype(BF16)
    pn_ref[:, SWAK_OFF:PN_COLS] = jnp.dot(h, wn_ref[:, CONV_OFF:], preferred_element_type=F32).astype(BF16)
    pt_ref[...] = lax.dot_general(wt_ref[...], h, (((1,), (1,)), ((), ())),
                                  preferred_element_type=F32).astype(BF16)


def _in_proj(x, g, wn, wc, wk, wt, wfg, conv_w, forget_bias, seq, tm):
    n, d = x.shape
    tiles = seq // tm
    fb = jnp.pad(forget_bias, (0, LANES - FOX_HEADS)).reshape(1, LANES)
    return pl.pallas_call(
        functools.partial(_in_proj_kernel, tiles_per_seq=tiles),
        grid=(n // tm,),
        in_specs=[
            pl.BlockSpec((tm, d), lambda i: (i, 0)),
            _const_spec((1, d)), _const_spec(wn.shape), _const_spec(wc.shape), _const_spec(wk.shape),
            _const_spec(wt.shape), _const_spec(wfg.shape), _const_spec(conv_w.shape), _const_spec((1, LANES)),
        ],
        out_specs=[
            pl.BlockSpec((tm, PN_COLS), lambda i: (i, 0)),
            pl.BlockSpec((PT_ROWS, tm), lambda i: (0, i)),
            pl.BlockSpec((1, FOX_HEADS, tm, LANES), lambda i: (i // tiles, 0, i % tiles, 0)),
        ],
        out_shape=[jax.ShapeDtypeStruct((n, PN_COLS), BF16), jax.ShapeDtypeStruct((PT_ROWS, n), BF16),
                   jax.ShapeDtypeStruct((n // seq, FOX_HEADS, seq, LANES), BF16)],
        scratch_shapes=[pltpu.VMEM((SUBLANES, BRANCH), F32), pltpu.VMEM((SUBLANES, LANES), F32)],
        compiler_params=_params("arbitrary"),
        name="in_proj",
    )(x, g.reshape(1, d), wn, wc, wk, wt, wfg, conv_w, fb)


def _norm_proj_kernel(x_ref, g_ref, w_ref, o_ref):
    h = _rms(x_ref[...], g_ref[...]).astype(BF16)
    o_ref[...] = jnp.dot(h, w_ref[...], preferred_element_type=F32).astype(BF16)


def _norm_proj(x, g, w):
    n, d = x.shape
    cols = w.shape[1]
    return pl.pallas_call(
        _norm_proj_kernel,
        grid=(1,),
        in_specs=[_const_spec((n, d)), _const_spec((1, d)), _const_spec(w.shape)],
        out_specs=pl.BlockSpec((n, cols), lambda i: (0, 0)),
        out_shape=jax.ShapeDtypeStruct((n, cols), BF16),
        compiler_params=_params("arbitrary"),
        name="mem_proj",
    )(x, g.reshape(1, d), w)


def _fox_kernel(q_ref, qn_ref, k_ref, v_ref, o_ref, q_aug, v_aug, s_buf, mt_buf, p_buf, acc_ref, *, tq, tk, hp):
    i = pl.program_id(2)
    seq = v_ref.shape[1]
    cur = i % 2
    heads = tuple(range(hp))

    def build_queries(src_ref, qslot):
        qrow = lax.broadcasted_iota(jnp.int32, (LANES, tq), 0)
        for pr in range(hp // 2):
            qb = src_ref[pr * LANES:(pr + 1) * LANES, :].astype(F32)
            q_aug[qslot, 2 * pr] = jnp.where(qrow < HEAD_DIM, qb, _one_hot_rows((LANES, tq), 0, AUG_EVEN)).astype(BF16)
            q_aug[qslot, 2 * pr + 1] = jnp.where(qrow >= HEAD_DIM, qb,
                                                 _one_hot_rows((LANES, tq), 0, AUG_ODD)).astype(BF16)

    def scores(t, slot, qslot):
        off = pl.multiple_of(t * tk, tk)
        for hh in heads:
            s = jnp.dot(k_ref[0, hh, pl.ds(off, tk), :], q_aug[qslot, hh], preferred_element_type=F32)
            s_buf[hh, slot, :, 0:tq] = s
            mt_buf[hh, slot, 0:1, :] = jnp.max(s, axis=0, keepdims=True)

    def softmax(slot, ms, mask=None):
        out = []
        for hh in heads:
            s = s_buf[hh, slot, :, 0:tq]
            if mask is None:
                tile_max = mt_buf[hh, slot, 0:1, :]
            else:
                s = jnp.where(mask, s, NEG)
                tile_max = jnp.max(s, axis=0, keepdims=True)
            m_new = jnp.maximum(ms[hh], tile_max)
            p_buf[hh, slot, :, 0:tq] = jnp.exp2(s - m_new).astype(BF16)
            out.append((m_new, jnp.exp2(ms[hh] - m_new)))
        return tuple(o[0] for o in out), tuple(o[1] for o in out)

    def values(t, slot, alphas):
        off = pl.multiple_of(jnp.maximum(t, 0) * tk, tk)
        for hh in heads:
            pv = jnp.dot(v_aug[hh, :, pl.ds(off, tk)], p_buf[hh, slot, :, 0:tq], preferred_element_type=F32)
            acc_ref[hh] = alphas[hh] * acc_ref[hh] + pv

    @pl.when(i == 0)
    def _():
        ones_row = _one_hot_rows((V_ROWS - HEAD_DIM, seq), 0, (0,)).astype(BF16)
        for hh in heads:
            v_aug[hh, 0:HEAD_DIM, :] = v_ref[hh * HEAD_DIM:(hh + 1) * HEAD_DIM, :]
            v_aug[hh, HEAD_DIM:V_ROWS, :] = ones_row
            p_buf[hh, 1, :, 0:tq] = jnp.zeros((tk, tq), BF16)
            acc_ref[hh] = jnp.zeros(acc_ref.shape[1:], F32)
        build_queries(q_ref, 0)
        scores(0, 0, 0)

    def pair(u, carry):
        ms, alpha_prev = carry
        t = 2 * u
        scores(t + 1, 1, cur)
        ms, alpha0 = softmax(0, ms)
        values(t - 1, 1, alpha_prev)
        scores(t + 2, 0, cur)
        ms, alpha1 = softmax(1, ms)
        values(t, 0, alpha0)
        return ms, alpha1

    carry = (tuple(jnp.full((1, tq), NEG, F32) for _ in heads), tuple(jnp.ones((1, tq), F32) for _ in heads))
    ms, alpha_prev = lax.fori_loop(0, i, pair, carry)

    t = 2 * i
    key = lax.broadcasted_iota(jnp.int32, (tk, tq), 0)
    qry = lax.broadcasted_iota(jnp.int32, (tk, tq), 1)
    off_b = pl.multiple_of((t + 1) * tk, tk)
    tri = lax.broadcasted_iota(jnp.int32, (tk, tk), 0) <= lax.broadcasted_iota(jnp.int32, (tk, tk), 1)
    s_b = [jnp.where(tri, jnp.dot(k_ref[0, hh, pl.ds(off_b, tk), :], q_aug[cur, hh, :, tk:],
                                  preferred_element_type=F32), NEG) for hh in heads]
    ms, alpha0 = softmax(0, ms, key <= qry)
    build_queries(qn_ref, 1 - cur)
    scores(0, 0, 1 - cur)
    values(t - 1, 1, alpha_prev)
    values(t, 0, alpha0)
    outs = []
    for hh in heads:
        mt_buf[hh, 1, 0:1, :] = ms[hh]
        m_old = mt_buf[hh, 1, 0:1, tk:]
        m_new = jnp.maximum(m_old, jnp.max(s_b[hh], axis=0, keepdims=True))
        p_b = jnp.exp2(s_b[hh] - m_new).astype(BF16)
        pv = jnp.dot(v_aug[hh, :, pl.ds(off_b, tk)], p_b, preferred_element_type=F32)
        acc_ref[hh, :, tk:] = jnp.exp2(m_old - m_new) * acc_ref[hh, :, tk:] + pv
        acc = acc_ref[hh]
        outs.append(acc[:HEAD_DIM] / acc[HEAD_DIM:HEAD_DIM + 1])
    o_ref[...] = jnp.concatenate(outs, axis=0).T.astype(BF16)


def _fox_attention(pt, k_aug, b, s, tq, hp):
    tk = tq // 2
    per = s // tq
    rows = hp * HEAD_DIM
    qblk = FOXQ_ROW // rows
    vblk = FOXV_ROW // rows
    return pl.pallas_call(
        functools.partial(_fox_kernel, tq=tq, tk=tk, hp=hp),
        grid=(b, FOX_HEADS // hp, per),
        in_specs=[
            pl.BlockSpec((rows, tq), lambda bi, g, i: (qblk + g, bi * per + i)),
            pl.BlockSpec((rows, tq), lambda bi, g, i: (qblk + g, bi * per + jnp.minimum(i + 1, per - 1))),
            pl.BlockSpec((1, hp, s, LANES), lambda bi, g, i: (bi, g, 0, 0), pipeline_mode=pl.Buffered(1)),
            pl.BlockSpec((rows, s), lambda bi, g, i: (vblk + g, bi), pipeline_mode=pl.Buffered(1)),
        ],
        out_specs=pl.BlockSpec((tq, rows), lambda bi, g, i: (bi * per + i, g)),
        out_shape=jax.ShapeDtypeStruct((b * s, BRANCH), BF16),
        scratch_shapes=[
            pltpu.VMEM((2, hp, LANES, tq), BF16),
            pltpu.VMEM((hp, V_ROWS, s), BF16),
            pltpu.VMEM((hp, 2, tk, tq + PITCH_PAD), F32),
            pltpu.VMEM((hp, 2, SUBLANES, tq), F32),
            pltpu.VMEM((hp, 2, tk, tq + PITCH_PAD), BF16),
            pltpu.VMEM((hp, V_ROWS, tq), F32),
        ],
        compiler_params=_params("arbitrary", "arbitrary", "arbitrary", vmem_limit=FOX_VMEM_LIMIT_BYTES),
        name="fox_attention",
    )(pt, pt, k_aug, pt)


def _swa_bias_kernel(rb_ref, bucket_ref, win_ref, o_ref):
    h = pl.program_id(0)
    bucket = bucket_ref[...]
    bias = jnp.zeros(bucket.shape, F32)
    for b in range(N_BUCKETS):
        bias = jnp.where(bucket == b, rb_ref[b, h] * LOG2E, bias)
    o_ref[0, 0] = jnp.where(win_ref[0] > 0, bias, NEG)
    o_ref[1, 0] = jnp.where(win_ref[1] > 0, bias, NEG)


def _swa_bias_table(rel_bias):
    tq = np.arange(WINDOW)[None, :]
    sk = np.arange(2 * WINDOW)[:, None]
    dist = WINDOW + tq - sk
    n = jnp.maximum(jnp.asarray(dist, jnp.int32), 0)
    max_exact = N_BUCKETS // 2
    large = max_exact + (jnp.log(jnp.maximum(n, 1).astype(F32) / max_exact)
                         / math.log(WINDOW / max_exact) * (N_BUCKETS - max_exact)).astype(jnp.int32)
    bucket = jnp.where(n < max_exact, n, jnp.minimum(large, N_BUCKETS - 1))
    in_window = (dist >= 0) & (dist < WINDOW)
    win = np.stack([in_window, in_window & (sk >= WINDOW)]).astype(np.int32)
    full = lambda shape: pl.BlockSpec(shape, lambda h: (0,) * len(shape))
    return pl.pallas_call(
        _swa_bias_kernel,
        grid=(SWA_HEADS,),
        in_specs=[pl.BlockSpec(memory_space=pltpu.SMEM), full(bucket.shape), full(win.shape)],
        out_specs=pl.BlockSpec((2, 1, 2 * WINDOW, WINDOW), lambda h: (0, h // SWA_GROUP, 0, h % SWA_GROUP)),
        out_shape=jax.ShapeDtypeStruct((2, SWA_KV, 2 * WINDOW, SWA_GROUP * WINDOW), F32),
        compiler_params=_params("arbitrary"),
        name="swa_bias_table",
    )(rel_bias, bucket, jnp.asarray(win))


def _swa_kernel(sink_ref, q_ref, kp_ref, kc_ref, vp_ref, vc_ref, bias_ref, o_ref,
                k_buf, v_buf, s_buf, mt_buf, p_buf, dn_buf, *, nblk):
    gw = SWA_GROUP * WINDOW
    ts = nblk * WINDOW
    first = pl.program_id(1) == 0
    k_buf[0:WINDOW, :] = kp_ref[...]
    k_buf[WINDOW:WINDOW + ts, :] = kc_ref[...]
    ones_row = _one_hot_rows((V_ROWS - HEAD_DIM, ts + WINDOW), 0, (0,)).astype(BF16)
    for kv in range(SWA_KV):
        rows = slice(kv * HEAD_DIM, (kv + 1) * HEAD_DIM)
        v_buf[kv, 0:HEAD_DIM, 0:WINDOW] = vp_ref[rows, :]
        v_buf[kv, 0:HEAD_DIM, WINDOW:WINDOW + ts] = vc_ref[rows, :]
        v_buf[kv, HEAD_DIM:V_ROWS, :] = ones_row
    glane = lax.broadcasted_iota(jnp.int32, (1, gw), 1) // WINDOW
    zeros = jnp.zeros((HEAD_DIM, gw), BF16)
    sinks = []
    for kv in range(SWA_KV):
        sink = jnp.zeros((1, gw), F32)
        for g in range(SWA_GROUP):
            sink = jnp.where(glane == g, sink_ref[kv * SWA_GROUP + g] * LOG2E, sink)
        sinks.append(sink)

    def scores(blk, slot):
        lo = pl.multiple_of(blk * WINDOW, WINDOW)
        sel = jnp.logical_and(first, blk == 0).astype(jnp.int32)
        keys = k_buf[pl.ds(lo, 2 * WINDOW), :]
        for kv in range(SWA_KV):
            qg = jnp.concatenate([q_ref[(kv * SWA_GROUP + g) * HEAD_DIM:(kv * SWA_GROUP + g + 1) * HEAD_DIM,
                                        pl.ds(lo, WINDOW)] for g in range(SWA_GROUP)], axis=1)
            q_pad = jnp.concatenate([qg, zeros] if kv == 0 else [zeros, qg], axis=0)
            s = jnp.dot(keys, q_pad, preferred_element_type=F32) + bias_ref[sel, kv]
            s_buf[kv, slot, :, 0:gw] = s
            mt_buf[kv, slot, 0:1, :] = jnp.max(s, axis=0, keepdims=True)

    def softmax(slot):
        for kv in range(SWA_KV):
            m = jnp.maximum(mt_buf[kv, slot, 0:1, :], sinks[kv])
            p = jnp.exp2(s_buf[kv, slot, :, 0:gw] - m)
            dn_buf[kv, slot, 0:1, :] = jnp.exp2(sinks[kv] - m)
            p_buf[kv, slot, :, 0:gw] = p.astype(BF16)

    def values(blk, slot):
        lo = pl.multiple_of(blk * WINDOW, WINDOW)
        for kv in range(SWA_KV):
            pv = jnp.dot(v_buf[kv, :, pl.ds(lo, 2 * WINDOW)], p_buf[kv, slot, :, 0:gw],
                         preferred_element_type=F32)
            o = pv[:HEAD_DIM] / (pv[HEAD_DIM:HEAD_DIM + 1] + dn_buf[kv, slot, 0:1, :])
            for pr in range(SWA_GROUP // 2):
                two = jnp.concatenate([o[:, (2 * pr) * WINDOW:(2 * pr + 1) * WINDOW],
                                       o[:, (2 * pr + 1) * WINDOW:(2 * pr + 2) * WINDOW]], axis=0)
                col = (kv * (SWA_GROUP // 2) + pr) * LANES
                o_ref[pl.ds(lo, WINDOW), col:col + LANES] = two.T.astype(BF16)

    scores(jnp.int32(0), 0)
    scores(jnp.int32(1), 1)
    softmax(0)

    def pair(u, carry):
        blk = 2 * u + 1
        scores(blk + 1, 0)
        softmax(1)
        values(blk - 1, 0)
        scores(blk + 2, 1)
        softmax(0)
        values(blk, 1)
        return carry

    lax.fori_loop(0, nblk // 2 - 1, pair, 0)
    softmax(1)
    values(jnp.int32(nblk - 2), 0)
    values(jnp.int32(nblk - 1), 1)


def _swa_attention(sink, pt, pn, bias, b, s, nblk):
    ts = nblk * WINDOW
    per = s // ts
    perw = s // WINDOW
    qblk = SWAQ_ROW // BRANCH
    vblk = SWAV_ROW // LANES
    kblk = SWAK_OFF // LANES
    prev = lambda bi, i: bi * perw + jnp.maximum(i * nblk - 1, 0)
    return pl.pallas_call(
        functools.partial(_swa_kernel, nblk=nblk),
        grid=(b, per),
        in_specs=[
            pl.BlockSpec(memory_space=pltpu.SMEM),
            pl.BlockSpec((BRANCH, ts), lambda bi, i: (qblk, bi * per + i)),
            pl.BlockSpec((WINDOW, LANES), lambda bi, i: (prev(bi, i), kblk)),
            pl.BlockSpec((ts, LANES), lambda bi, i: (bi * per + i, kblk)),
            pl.BlockSpec((LANES, WINDOW), lambda bi, i: (vblk, prev(bi, i))),
            pl.BlockSpec((LANES, ts), lambda bi, i: (vblk, bi * per + i)),
            _const_spec(bias.shape),
        ],
        out_specs=pl.BlockSpec((ts, BRANCH), lambda bi, i: (bi * per + i, 0)),
        out_shape=jax.ShapeDtypeStruct((b * s, BRANCH), BF16),
        scratch_shapes=[
            pltpu.VMEM((ts + WINDOW, LANES), BF16),
            pltpu.VMEM((SWA_KV, V_ROWS, ts + WINDOW), BF16),
            pltpu.VMEM((SWA_KV, 2, 2 * WINDOW, SWA_GROUP * WINDOW + PITCH_PAD), F32),
            pltpu.VMEM((SWA_KV, 2, SUBLANES, SWA_GROUP * WINDOW), F32),
            pltpu.VMEM((SWA_KV, 2, 2 * WINDOW, SWA_GROUP * WINDOW + PITCH_PAD), BF16),
            pltpu.VMEM((SWA_KV, 2, SUBLANES, SWA_GROUP * WINDOW), F32),
        ],
        compiler_params=_params("arbitrary", "arbitrary"),
        name="swa_attention",
    )(sink, pt, pn, pn, pt, pt, bias)


def _merge_tile(x, gate_ref, branches, wb_ref, wo_ref, m_scr):
    for lo in range(0, D_MODEL, MERGE_CHUNK):
        gates = [jax.nn.sigmoid(gate_ref[:, b * D_MODEL + lo:b * D_MODEL + lo + MERGE_CHUNK].astype(F32))
                 for b in range(len(branches))]
        merged = None
        for b, y in enumerate(branches):
            term = gates[b] * jnp.dot(y, wb_ref[b, :, lo:lo + MERGE_CHUNK], preferred_element_type=F32)
            merged = term if merged is None else merged + term
        m_scr[:, lo:lo + MERGE_CHUNK] = merged.astype(BF16)
    return x + jnp.dot(m_scr[...], wo_ref[...], preferred_element_type=F32)


def _xattn_tile(x, g_ref, wq_ref, kT_ref, v_ref, wo_ref):
    q = jnp.dot(_rms(x, g_ref[...]).astype(BF16), wq_ref[...], preferred_element_type=F32).astype(BF16)
    sls = [slice(h * X_HEAD_DIM, (h + 1) * X_HEAD_DIM) for h in range(X_HEADS)]
    scores = [jnp.dot(q[:, sl], kT_ref[0, sl, :], preferred_element_type=F32) for sl in sls]
    probs = [jnp.exp2(s - jnp.max(s, axis=-1, keepdims=True)) for s in scores]
    denoms = [jnp.sum(p, axis=-1, keepdims=True) for p in probs]
    outs = [jnp.dot(p.astype(BF16), v_ref[0, :, sl], preferred_element_type=F32) for p, sl in zip(probs, sls)]
    heads = [(o / d).astype(BF16) for o, d in zip(outs, denoms)]
    return x + jnp.dot(jnp.concatenate(heads, axis=1), wo_ref[...], preferred_element_type=F32)


def _swiglu_tile(x, g_ref, wg_ref, wu_ref, wd_ref):
    xn = _rms(x, g_ref[...]).astype(BF16)
    y = x
    for lo in range(0, wg_ref.shape[1], FFN_CHUNK):
        gate = jnp.dot(xn, wg_ref[:, lo:lo + FFN_CHUNK], preferred_element_type=F32)
        up = jnp.dot(xn, wu_ref[:, lo:lo + FFN_CHUNK], preferred_element_type=F32)
        hidden = (gate * jax.nn.sigmoid(gate) * up).astype(BF16)
        y = y + jnp.dot(hidden, wd_ref[lo:lo + FFN_CHUNK, :], preferred_element_type=F32)
    return y


def _layer_tail_kernel(x_ref, gate_ref, yc_ref, yf_ref, ys_ref, wb_ref, wo_ref, gx_ref, wq_ref, kT_ref, v_ref,
                       wxo_ref, gf_ref, wg_ref, wu_ref, wd_ref, gl_ref, o_ref, m_scr, *, final_norm):
    x = _merge_tile(x_ref[...], gate_ref, (yc_ref[...], yf_ref[...], ys_ref[...]), wb_ref, wo_ref, m_scr)
    x = _xattn_tile(x, gx_ref, wq_ref, kT_ref, v_ref, wxo_ref)
    x = _swiglu_tile(x, gf_ref, wg_ref, wu_ref, wd_ref)
    o_ref[...] = _rms(x, gl_ref[...]) if final_norm else x


def _layer_tail(x, pn, y_fox, y_swa, w_branch, w_out, gx, wq, kT, v, wxo, gf, wg, wu, wd, g_last, final_norm,
                seq, tm):
    n, d = x.shape
    tiles = seq // tm
    vec = _const_spec((1, d))
    return pl.pallas_call(
        functools.partial(_layer_tail_kernel, final_norm=final_norm),
        grid=(n // tm,),
        in_specs=[
            pl.BlockSpec((tm, d), lambda i: (i, 0)),
            pl.BlockSpec((tm, 3 * D_MODEL), lambda i: (i, GATE_OFF // (3 * D_MODEL))),
            pl.BlockSpec((tm, BRANCH), lambda i: (i, CONV_OFF // BRANCH)),
            pl.BlockSpec((tm, BRANCH), lambda i: (i, 0)),
            pl.BlockSpec((tm, BRANCH), lambda i: (i, 0)),
            _const_spec(w_branch.shape), _const_spec(w_out.shape),
            vec, _const_spec(wq.shape),
            pl.BlockSpec((1,) + kT.shape[1:], lambda i: (i // tiles, 0, 0)),
            pl.BlockSpec((1,) + v.shape[1:], lambda i: (i // tiles, 0, 0)),
            _const_spec(wxo.shape),
            vec, _const_spec(wg.shape), _const_spec(wu.shape), _const_spec(wd.shape), vec,
        ],
        out_specs=pl.BlockSpec((tm, d), lambda i: (i, 0)),
        out_shape=jax.ShapeDtypeStruct((n, d), F32),
        scratch_shapes=[pltpu.VMEM((tm, d), BF16)],
        compiler_params=_params("arbitrary", vmem_limit=TAIL_VMEM_LIMIT_BYTES),
        name="layer_tail",
    )(x, pn, pn, y_fox, y_swa, w_branch, w_out, gx.reshape(1, d), wq, kT, v, wxo,
      gf.reshape(1, d), wg, wu, wd, g_last.reshape(1, d))


def _pack_in_proj(w_in):
    sizes = [BRANCH] * 3 + [BRANCH] * 3 + [FOX_HEADS] + [BRANCH, SWA_KV * HEAD_DIM, SWA_KV * HEAD_DIM] + [3 * D_MODEL]
    offs = np.concatenate([[0], np.cumsum(sizes)])
    c_b, c_c, c_u, f_q, f_k, f_v, f_g, s_q, s_k, s_v, gates = (w_in[:, offs[t]:offs[t + 1]] for t in range(len(sizes)))
    qscale = HEAD_DIM ** -0.5 * LOG2E
    wn = jnp.concatenate([gates, s_k], axis=1)
    wc = jnp.concatenate([c_b, c_c, c_u], axis=1)
    wt = jnp.concatenate([f_q * qscale, f_v, s_q * qscale, s_v], axis=1).T
    wfg = jnp.pad(f_g, ((0, 0), (0, LANES - FOX_HEADS)))
    return wn.astype(BF16), wc.astype(BF16), f_k.astype(BF16), wt.astype(BF16), wfg.astype(BF16)


def kernel(x, mem, mix_norm_g, w_in, forget_bias, conv_w, sink, w_branch, w_mix_out, rel_bias,
           xattn_norm_g, mem_norm_g, w_xq, w_xkv, w_xo, ffn_norm_g, w_ffn_gate, w_ffn_up, w_ffn_down,
           final_norm_g):
    b, s, d = x.shape
    n = b * s
    depth = w_in.shape[0]
    assert d == D_MODEL and s % FOX_TQ == 0 and s % ROW_TILE == 0
    assert s % (SWA_BLOCKS_PER_STEP * WINDOW) == 0 and SWA_BLOCKS_PER_STEP % 2 == 0
    mem_len = mem.shape[1]
    xf = x.reshape(n, d)
    memf = mem.reshape(b * mem_len, d)
    swa_bias = _swa_bias_table(rel_bias)
    for l in range(depth):
        wn, wc, wk, wt, wfg = _pack_in_proj(w_in[l])
        pn, pt, k_aug = _in_proj(xf, mix_norm_g[l], wn, wc, wk, wt, wfg, conv_w[l], forget_bias[l],
                                 seq=s, tm=ROW_TILE)
        y_fox = _fox_attention(pt, k_aug, b, s, tq=FOX_TQ, hp=FOX_HEADS_PER_STEP)
        y_swa = _swa_attention(sink[l], pt, pn, swa_bias, b, s, nblk=SWA_BLOCKS_PER_STEP)
        kv = _norm_proj(memf, mem_norm_g[l], w_xkv[l].astype(BF16))
        kT = kv[:, :d].reshape(b, mem_len, d).transpose(0, 2, 1)
        v = kv[:, d:].reshape(b, mem_len, d)
        wq = (w_xq[l] * (X_HEAD_DIM ** -0.5 * LOG2E)).astype(BF16)
        xf = _layer_tail(xf, pn, y_fox, y_swa, w_branch[l].astype(BF16), w_mix_out[l].astype(BF16),
                         xattn_norm_g[l], wq, kT, v, w_xo[l].astype(BF16),
                         ffn_norm_g[l], w_ffn_gate[l].astype(BF16), w_ffn_up[l].astype(BF16),
                         w_ffn_down[l].astype(BF16), final_norm_g, final_norm=(l == depth - 1),
                         seq=s, tm=ROW_TILE)
    return xf.reshape(b, s, d)
```

```python
import functools
import math

import jax
import jax.numpy as jnp
import numpy as np
from jax import lax
from jax.experimental import pallas as pl
from jax.experimental.pallas import tpu as pltpu

F32 = jnp.float32
BF16 = jnp.bfloat16

D_MODEL = 1024
HEAD_DIM = 64
BRANCH = 512
FOX_HEADS = 8
SWA_HEADS = 8
SWA_KV = 2
SWA_GROUP = SWA_HEADS // SWA_KV
WINDOW = 128
N_BUCKETS = 32
X_HEADS = 4
X_HEAD_DIM = D_MODEL // X_HEADS
RMS_EPS = 1e-6
NEG = -1e30
LOG2E = math.log2(math.e)

LANES = 128
SUBLANES = 8
VMEM_LIMIT_BYTES = 56 * 1024 * 1024
FOX_VMEM_LIMIT_BYTES = 58 * 1024 * 1024
TAIL_VMEM_LIMIT_BYTES = 58 * 1024 * 1024
PITCH_PAD = LANES
MERGE_CHUNK = 256
FFN_CHUNK = 256

ROW_TILE = 512
FOX_TQ = 1024
FOX_HEADS_PER_STEP = 4
SWA_BLOCKS_PER_STEP = 16

GATE_OFF = 0
CONV_OFF = 3 * D_MODEL
SWAK_OFF = CONV_OFF + BRANCH
PN_COLS = SWAK_OFF + SWA_KV * HEAD_DIM
FOXQ_ROW = 0
FOXV_ROW = BRANCH
SWAQ_ROW = 2 * BRANCH
SWAV_ROW = 3 * BRANCH
PT_ROWS = SWAV_ROW + SWA_KV * HEAD_DIM

AUG_STRIDE = 8
AUG_EVEN = (HEAD_DIM, HEAD_DIM + AUG_STRIDE, HEAD_DIM + 2 * AUG_STRIDE)
AUG_ODD = (0, AUG_STRIDE, 2 * AUG_STRIDE)
V_ROWS = 80
SKIP_BITS = 160.0
SKIP_SLACK = 1.05


def _params(*sem, vmem_limit=VMEM_LIMIT_BYTES):
    return pltpu.CompilerParams(dimension_semantics=sem, vmem_limit_bytes=vmem_limit)


def _rms(x, g):
    return x * lax.rsqrt(jnp.mean(x * x, axis=-1, keepdims=True) + RMS_EPS) * g


def _const_spec(shape):
    nd = len(shape)
    return pl.BlockSpec(shape, lambda *_: (0,) * nd, pipeline_mode=pl.Buffered(1))


def _one_hot_rows(shape, axis, positions):
    idx = lax.broadcasted_iota(jnp.int32, shape, axis)
    hit = idx == positions[0]
    for p in positions[1:]:
        hit = hit | (idx == p)
    return jnp.where(hit, 1.0, 0.0)


def _in_proj_kernel(x_ref, g_ref, wn_ref, wc_ref, wk_ref, wt_ref, wfg_ref, cw_ref, fb_ref,
                    pn_ref, pt_ref, ka_ref, st_ref, z_tail, c_tail, *, tiles_per_seq):
    tm = x_ref.shape[0]
    first_of_seq = pl.program_id(0) % tiles_per_seq == 0

    @pl.when(pl.program_id(0) == 0)
    def _():
        z_tail[...] = jnp.zeros(z_tail.shape, F32)
        c_tail[...] = jnp.zeros(c_tail.shape, F32)

    h = _rms(x_ref[...], g_ref[...]).astype(BF16)

    fg = jnp.dot(h, wfg_ref[...], preferred_element_type=F32)
    k_fox = jnp.dot(h, wk_ref[...], preferred_element_type=F32)
    lane = lax.broadcasted_iota(jnp.int32, (tm, LANES), 1)
    trow = lax.broadcasted_iota(jnp.int32, (tm, LANES), 0)
    c = jnp.where(lane < FOX_HEADS, jax.nn.log_sigmoid(fg + fb_ref[...]) * LOG2E, 0.0)
    d = 1
    while d < tm:
        c = c + jnp.where(trow >= d, pltpu.roll(c, d, axis=0), 0.0)
        d *= 2
    c = c + jnp.where(first_of_seq, 0.0, c_tail[0:1, :])
    c_tail[0:1, :] = c[tm - 1:tm, :]
    head_of_col = lax.broadcasted_iota(jnp.int32, (BRANCH, LANES), 0) // HEAD_DIM
    head_sel = jnp.where(head_of_col == lax.broadcasted_iota(jnp.int32, (BRANCH, LANES), 1), 1.0, 0.0).astype(BF16)
    k_sq = jnp.dot((k_fox * k_fox).astype(BF16), head_sel, preferred_element_type=F32)
    st_ref[...] = jnp.zeros(st_ref.shape, F32)
    st_ref[0:1, :] = c[tm - 1:tm, :]
    st_ref[1:2, :] = jnp.max(k_sq, axis=0, keepdims=True)
    c1 = c.astype(BF16).astype(F32)
    r1 = c - c1
    c2 = r1.astype(BF16).astype(F32)
    c3 = (r1 - c2).astype(BF16).astype(F32)
    packed = -jnp.where(lane < AUG_STRIDE, c1,
                        jnp.where(lane < 2 * AUG_STRIDE, pltpu.roll(c2, AUG_STRIDE, axis=1),
                                  pltpu.roll(c3, 2 * AUG_STRIDE, axis=1)))
    keep_even = _one_hot_rows((1, LANES), 1, AUG_EVEN)
    keep_odd = _one_hot_rows((1, LANES), 1, AUG_ODD)
    for hd in range(FOX_HEADS):
        k_pair = k_fox[:, (hd // 2) * LANES:(hd // 2 + 1) * LANES]
        if hd % 2 == 0:
            aug = pltpu.roll(packed, (AUG_EVEN[0] - hd) % LANES, axis=1) * keep_even
            out = jnp.where(lane < HEAD_DIM, k_pair, aug)
        else:
            aug = pltpu.roll(packed, (AUG_ODD[0] - hd) % LANES, axis=1) * keep_odd
            out = jnp.where(lane >= HEAD_DIM, k_pair, aug)
        ka_ref[0, hd] = out.astype(BF16)

    conv_b, conv_c, conv_u = (jnp.dot(h, wc_ref[:, t * BRANCH:(t + 1) * BRANCH], preferred_element_type=F32)
                              for t in range(3))
    z = conv_c * conv_u
    zh = jnp.where(first_of_seq, 0.0, z_tail[...])
    z_tail[...] = z[tm - SUBLANES:tm, :]
    row = lax.broadcasted_iota(jnp.int32, z.shape, 0)
    z1 = jnp.where(row == 0, zh[7:8], pltpu.roll(z, 1, axis=0))
    z2 = jnp.where(row == 0, zh[6:7], jnp.where(row == 1, zh[7:8], pltpu.roll(z, 2, axis=0)))
    cw = cw_ref[...]
    pn_ref[:, CONV_OFF:SWAK_OFF] = (conv_b * (cw[0:1] * z2 + cw[1:2] * z1 + cw[2:3] * z)).astype(BF16)

    for lo in range(0, CONV_OFF, D_MODEL):
        pn_ref[:, lo:lo + D_MODEL] = jnp.dot(h, wn_ref[:, lo:lo + D_MODEL], preferred_element_type=F32).astype(BF16)
    pn_ref[:, SWAK_OFF:PN_COLS] = jnp.dot(h, wn_ref[:, CONV_OFF:], preferred_element_type=F32).astype(BF16)
    pt = lax.dot_general(wt_ref[...], h, (((1,), (1,)), ((), ())), preferred_element_type=F32)
    pt_ref[...] = pt.astype(BF16)
    for hd in range(FOX_HEADS):
        q_h = pt[FOXQ_ROW + hd * HEAD_DIM:FOXQ_ROW + (hd + 1) * HEAD_DIM, :]
        q_sq = jnp.sum(q_h * q_h, axis=0, keepdims=True)
        st_ref[hd:hd + 1, FOX_HEADS:FOX_HEADS + 1] = jnp.max(q_sq, axis=1, keepdims=True)


def _in_proj(x, g, wn, wc, wk, wt, wfg, conv_w, forget_bias, seq, tm):
    n, d = x.shape
    tiles = seq // tm
    fb = jnp.pad(forget_bias, (0, LANES - FOX_HEADS)).reshape(1, LANES)
    return pl.pallas_call(
        functools.partial(_in_proj_kernel, tiles_per_seq=tiles),
        grid=(n // tm,),
        in_specs=[
            pl.BlockSpec((tm, d), lambda i: (i, 0)),
            _const_spec((1, d)), _const_spec(wn.shape), _const_spec(wc.shape), _const_spec(wk.shape),
            _const_spec(wt.shape), _const_spec(wfg.shape), _const_spec(conv_w.shape), _const_spec((1, LANES)),
        ],
        out_specs=[
            pl.BlockSpec((tm, PN_COLS), lambda i: (i, 0)),
            pl.BlockSpec((PT_ROWS, tm), lambda i: (0, i)),
            pl.BlockSpec((1, FOX_HEADS, tm, LANES), lambda i: (i // tiles, 0, i % tiles, 0)),
            pl.BlockSpec((SUBLANES, LANES), lambda i: (i, 0)),
        ],
        out_shape=[jax.ShapeDtypeStruct((n, PN_COLS), BF16), jax.ShapeDtypeStruct((PT_ROWS, n), BF16),
                   jax.ShapeDtypeStruct((n // seq, FOX_HEADS, seq, LANES), BF16),
                   jax.ShapeDtypeStruct((n // tm * SUBLANES, LANES), F32)],
        scratch_shapes=[pltpu.VMEM((SUBLANES, BRANCH), F32), pltpu.VMEM((SUBLANES, LANES), F32)],
        compiler_params=_params("arbitrary"),
        name="in_proj",
    )(x, g.reshape(1, d), wn, wc, wk, wt, wfg, conv_w, fb)


def _norm_proj_kernel(x_ref, g_ref, w_ref, o_ref):
    h = _rms(x_ref[...], g_ref[...]).astype(BF16)
    o_ref[...] = jnp.dot(h, w_ref[...], preferred_element_type=F32).astype(BF16)


def _norm_proj(x, g, w):
    n, d = x.shape
    cols = w.shape[1]
    return pl.pallas_call(
        _norm_proj_kernel,
        grid=(1,),
        in_specs=[_const_spec((n, d)), _const_spec((1, d)), _const_spec(w.shape)],
        out_specs=pl.BlockSpec((n, cols), lambda i: (0, 0)),
        out_shape=jax.ShapeDtypeStruct((n, cols), BF16),
        compiler_params=_params("arbitrary"),
        name="mem_proj",
    )(x, g.reshape(1, d), w)


def _fox_kernel(st_ref, q_ref, qn_ref, k_ref, v_ref, o_ref, q_aug, v_aug, s_buf, mt_buf, p_buf, acc_ref, first_ref,
                *, tq, tk, hp):
    bi = pl.program_id(0)
    grp = pl.program_id(1)
    i = pl.program_id(2)
    seq = v_ref.shape[1]
    cur = i % 2
    heads = tuple(range(hp))

    def first_live_pair(qi):
        c_q = jnp.maximum(2 * qi - 1, 0)
        first = qi
        for hh in heads:
            h = grp * hp + hh
            q_sq = jnp.maximum(st_ref[bi, 2 * qi, 2, h], st_ref[bi, 2 * qi + 1, 2, h])
            k_sq = st_ref[bi, 0, 1, h]
            bound_sq = (4.0 * SKIP_SLACK * SKIP_SLACK) * q_sq * k_sq

            def count(u, n):
                room = -SKIP_BITS - (st_ref[bi, c_q, 0, h] - st_ref[bi, 2 * u + 1, 0, h])
                dead = jnp.logical_and(room > 0.0, bound_sq < room * room)
                return n + dead.astype(jnp.int32)

            first = jnp.minimum(first, lax.fori_loop(0, qi, count, jnp.int32(0)))
        return first

    def build_queries(src_ref, qslot):
        qrow = lax.broadcasted_iota(jnp.int32, (LANES, tq), 0)
        for pr in range(hp // 2):
            qb = src_ref[pr * LANES:(pr + 1) * LANES, :].astype(F32)
            q_aug[qslot, 2 * pr] = jnp.where(qrow < HEAD_DIM, qb, _one_hot_rows((LANES, tq), 0, AUG_EVEN)).astype(BF16)
            q_aug[qslot, 2 * pr + 1] = jnp.where(qrow >= HEAD_DIM, qb,
                                                 _one_hot_rows((LANES, tq), 0, AUG_ODD)).astype(BF16)

    def scores(t, slot, qslot):
        off = pl.multiple_of(t * tk, tk)
        for hh in heads:
            s = jnp.dot(k_ref[0, hh, pl.ds(off, tk), :], q_aug[qslot, hh], preferred_element_type=F32)
            s_buf[hh, slot, :, 0:tq] = s
            mt_buf[hh, slot, 0:1, :] = jnp.max(s, axis=0, keepdims=True)

    def softmax(slot, ms, mask=None):
        out = []
        for hh in heads:
            s = s_buf[hh, slot, :, 0:tq]
            if mask is None:
                tile_max = mt_buf[hh, slot, 0:1, :]
            else:
                s = jnp.where(mask, s, NEG)
                tile_max = jnp.max(s, axis=0, keepdims=True)
            m_new = jnp.maximum(ms[hh], tile_max)
            p_buf[hh, slot, :, 0:tq] = jnp.exp2(s - m_new).astype(BF16)
            out.append((m_new, jnp.exp2(ms[hh] - m_new)))
        return tuple(o[0] for o in out), tuple(o[1] for o in out)

    def values(t, slot, alphas):
        off = pl.multiple_of(jnp.maximum(t, 0) * tk, tk)
        for hh in heads:
            pv = jnp.dot(v_aug[hh, :, pl.ds(off, tk)], p_buf[hh, slot, :, 0:tq], preferred_element_type=F32)
            acc_ref[hh] = alphas[hh] * acc_ref[hh] + pv

    @pl.when(i == 0)
    def _():
        ones_row = _one_hot_rows((V_ROWS - HEAD_DIM, seq), 0, (0,)).astype(BF16)
        for hh in heads:
            v_aug[hh, 0:HEAD_DIM, :] = v_ref[hh * HEAD_DIM:(hh + 1) * HEAD_DIM, :]
            v_aug[hh, HEAD_DIM:V_ROWS, :] = ones_row
            p_buf[hh, 1, :, 0:tq] = jnp.zeros((tk, tq), BF16)
            acc_ref[hh] = jnp.zeros(acc_ref.shape[1:], F32)
        build_queries(q_ref, 0)
        scores(0, 0, 0)

    def pair(u, carry):
        ms, alpha_prev = carry
        t = 2 * u
        scores(t + 1, 1, cur)
        ms, alpha0 = softmax(0, ms)
        values(t - 1, 1, alpha_prev)
        scores(t + 2, 0, cur)
        ms, alpha1 = softmax(1, ms)
        values(t, 0, alpha0)
        return ms, alpha1

    carry = (tuple(jnp.full((1, tq), NEG, F32) for _ in heads), tuple(jnp.ones((1, tq), F32) for _ in heads))
    first_now = jnp.where(i == 0, 0, first_ref[0])
    first_next = first_live_pair(jnp.minimum(i + 1, seq // tq - 1))
    ms, alpha_prev = lax.fori_loop(first_now, i, pair, carry)

    t = 2 * i
    key = lax.broadcasted_iota(jnp.int32, (tk, tq), 0)
    qry = lax.broadcasted_iota(jnp.int32, (tk, tq), 1)
    off_b = pl.multiple_of((t + 1) * tk, tk)
    tri = lax.broadcasted_iota(jnp.int32, (tk, tk), 0) <= lax.broadcasted_iota(jnp.int32, (tk, tk), 1)
    s_b = [jnp.where(tri, jnp.dot(k_ref[0, hh, pl.ds(off_b, tk), :], q_aug[cur, hh, :, tk:],
                                  preferred_element_type=F32), NEG) for hh in heads]
    ms, alpha0 = softmax(0, ms, key <= qry)
    build_queries(qn_ref, 1 - cur)
    first_ref[0] = first_next
    scores(2 * first_next, 0, 1 - cur)
    values(t - 1, 1, alpha_prev)
    values(t, 0, alpha0)
    outs = []
    for hh in heads:
        mt_buf[hh, 1, 0:1, :] = ms[hh]
        m_old = mt_buf[hh, 1, 0:1, tk:]
        m_new = jnp.maximum(m_old, jnp.max(s_b[hh], axis=0, keepdims=True))
        p_b = jnp.exp2(s_b[hh] - m_new).astype(BF16)
        pv = jnp.dot(v_aug[hh, :, pl.ds(off_b, tk)], p_b, preferred_element_type=F32)
        acc_ref[hh, :, tk:] = jnp.exp2(m_old - m_new) * acc_ref[hh, :, tk:] + pv
        acc = acc_ref[hh]
        outs.append(acc[:HEAD_DIM] / acc[HEAD_DIM:HEAD_DIM + 1])
    o_ref[...] = jnp.concatenate(outs, axis=0).T.astype(BF16)


def _fox_attention(stats, pt, k_aug, b, s, tq, hp):
    tk = tq // 2
    per = s // tq
    rows = hp * HEAD_DIM
    qblk = FOXQ_ROW // rows
    vblk = FOXV_ROW // rows
    return pl.pallas_call(
        functools.partial(_fox_kernel, tq=tq, tk=tk, hp=hp),
        grid=(b, FOX_HEADS // hp, per),
        in_specs=[
            pl.BlockSpec(memory_space=pltpu.SMEM),
            pl.BlockSpec((rows, tq), lambda bi, g, i: (qblk + g, bi * per + i)),
            pl.BlockSpec((rows, tq), lambda bi, g, i: (qblk + g, bi * per + jnp.minimum(i + 1, per - 1))),
            pl.BlockSpec((1, hp, s, LANES), lambda bi, g, i: (bi, g, 0, 0), pipeline_mode=pl.Buffered(1)),
            pl.BlockSpec((rows, s), lambda bi, g, i: (vblk + g, bi), pipeline_mode=pl.Buffered(1)),
        ],
        out_specs=pl.BlockSpec((tq, rows), lambda bi, g, i: (bi * per + i, g)),
        out_shape=jax.ShapeDtypeStruct((b * s, BRANCH), BF16),
        scratch_shapes=[
            pltpu.VMEM((2, hp, LANES, tq), BF16),
            pltpu.VMEM((hp, V_ROWS, s), BF16),
            pltpu.VMEM((hp, 2, tk, tq + PITCH_PAD), F32),
            pltpu.VMEM((hp, 2, SUBLANES, tq), F32),
            pltpu.VMEM((hp, 2, tk, tq + PITCH_PAD), BF16),
            pltpu.VMEM((hp, V_ROWS, tq), F32),
            pltpu.SMEM((1,), jnp.int32),
        ],
        compiler_params=_params("arbitrary", "arbitrary", "arbitrary", vmem_limit=FOX_VMEM_LIMIT_BYTES),
        name="fox_attention",
    )(stats, pt, pt, k_aug, pt)


def _swa_bias_kernel(rb_ref, bucket_ref, win_ref, o_ref):
    h = pl.program_id(0)
    bucket = bucket_ref[...]
    bias = jnp.zeros(bucket.shape, F32)
    for b in range(N_BUCKETS):
        bias = jnp.where(bucket == b, rb_ref[b, h] * LOG2E, bias)
    o_ref[0, 0] = jnp.where(win_ref[0] > 0, bias, NEG)
    o_ref[1, 0] = jnp.where(win_ref[1] > 0, bias, NEG)


def _swa_bias_table(rel_bias):
    tq = np.arange(WINDOW)[None, :]
    sk = np.arange(2 * WINDOW)[:, None]
    dist = WINDOW + tq - sk
    n = jnp.maximum(jnp.asarray(dist, jnp.int32), 0)
    max_exact = N_BUCKETS // 2
    large = max_exact + (jnp.log(jnp.maximum(n, 1).astype(F32) / max_exact)
                         / math.log(WINDOW / max_exact) * (N_BUCKETS - max_exact)).astype(jnp.int32)
    bucket = jnp.where(n < max_exact, n, jnp.minimum(large, N_BUCKETS - 1))
    in_window = (dist >= 0) & (dist < WINDOW)
    win = np.stack([in_window, in_window & (sk >= WINDOW)]).astype(np.int32)
    full = lambda shape: pl.BlockSpec(shape, lambda h: (0,) * len(shape))
    return pl.pallas_call(
        _swa_bias_kernel,
        grid=(SWA_HEADS,),
        in_specs=[pl.BlockSpec(memory_space=pltpu.SMEM), full(bucket.shape), full(win.shape)],
        out_specs=pl.BlockSpec((2, 1, 2 * WINDOW, WINDOW), lambda h: (0, h // SWA_GROUP, 0, h % SWA_GROUP)),
        out_shape=jax.ShapeDtypeStruct((2, SWA_KV, 2 * WINDOW, SWA_GROUP * WINDOW), F32),
        compiler_params=_params("arbitrary"),
        name="swa_bias_table",
    )(rel_bias, bucket, jnp.asarray(win))


def _swa_kernel(sink_ref, q_ref, kp_ref, kc_ref, vp_ref, vc_ref, bias_ref, o_ref,
                k_buf, v_buf, s_buf, mt_buf, p_buf, dn_buf, *, nblk):
    gw = SWA_GROUP * WINDOW
    ts = nblk * WINDOW
    first = pl.program_id(1) == 0
    k_buf[0:WINDOW, :] = kp_ref[...]
    k_buf[WINDOW:WINDOW + ts, :] = kc_ref[...]
    ones_row = _one_hot_rows((V_ROWS - HEAD_DIM, ts + WINDOW), 0, (0,)).astype(BF16)
    for kv in range(SWA_KV):
        rows = slice(kv * HEAD_DIM, (kv + 1) * HEAD_DIM)
        v_buf[kv, 0:HEAD_DIM, 0:WINDOW] = vp_ref[rows, :]
        v_buf[kv, 0:HEAD_DIM, WINDOW:WINDOW + ts] = vc_ref[rows, :]
        v_buf[kv, HEAD_DIM:V_ROWS, :] = ones_row
    glane = lax.broadcasted_iota(jnp.int32, (1, gw), 1) // WINDOW
    zeros = jnp.zeros((HEAD_DIM, gw), BF16)
    sinks = []
    for kv in range(SWA_KV):
        sink = jnp.zeros((1, gw), F32)
        for g in range(SWA_GROUP):
            sink = jnp.where(glane == g, sink_ref[kv * SWA_GROUP + g] * LOG2E, sink)
        sinks.append(sink)

    def scores(blk, slot):
        lo = pl.multiple_of(blk * WINDOW, WINDOW)
        sel = jnp.logical_and(first, blk == 0).astype(jnp.int32)
        keys = k_buf[pl.ds(lo, 2 * WINDOW), :]
        for kv in range(SWA_KV):
            qg = jnp.concatenate([q_ref[(kv * SWA_GROUP + g) * HEAD_DIM:(kv * SWA_GROUP + g + 1) * HEAD_DIM,
                                        pl.ds(lo, WINDOW)] for g in range(SWA_GROUP)], axis=1)
            q_pad = jnp.concatenate([qg, zeros] if kv == 0 else [zeros, qg], axis=0)
            s = jnp.dot(keys, q_pad, preferred_element_type=F32) + bias_ref[sel, kv]
            s_buf[kv, slot, :, 0:gw] = s
            mt_buf[kv, slot, 0:1, :] = jnp.max(s, axis=0, keepdims=True)

    def softmax(slot):
        for kv in range(SWA_KV):
            m = jnp.maximum(mt_buf[kv, slot, 0:1, :], sinks[kv])
            p = jnp.exp2(s_buf[kv, slot, :, 0:gw] - m)
            dn_buf[kv, slot, 0:1, :] = jnp.exp2(sinks[kv] - m)
            p_buf[kv, slot, :, 0:gw] = p.astype(BF16)

    def values(blk, slot):
        lo = pl.multiple_of(blk * WINDOW, WINDOW)
        for kv in range(SWA_KV):
            pv = jnp.dot(v_buf[kv, :, pl.ds(lo, 2 * WINDOW)], p_buf[kv, slot, :, 0:gw],
                         preferred_element_type=F32)
            o = pv[:HEAD_DIM] / (pv[HEAD_DIM:HEAD_DIM + 1] + dn_buf[kv, slot, 0:1, :])
            for pr in range(SWA_GROUP // 2):
                two = jnp.concatenate([o[:, (2 * pr) * WINDOW:(2 * pr + 1) * WINDOW],
                                       o[:, (2 * pr + 1) * WINDOW:(2 * pr + 2) * WINDOW]], axis=0)
                col = (kv * (SWA_GROUP // 2) + pr) * LANES
                o_ref[pl.ds(lo, WINDOW), col:col + LANES] = two.T.astype(BF16)

    scores(jnp.int32(0), 0)
    scores(jnp.int32(1), 1)
    softmax(0)

    def pair(u, carry):
        blk = 2 * u + 1
        scores(blk + 1, 0)
        softmax(1)
        values(blk - 1, 0)
        scores(blk + 2, 1)
        softmax(0)
        values(blk, 1)
        return carry

    lax.fori_loop(0, nblk // 2 - 1, pair, 0)
    softmax(1)
    values(jnp.int32(nblk - 2), 0)
    values(jnp.int32(nblk - 1), 1)


def _swa_attention(sink, pt, pn, bias, b, s, nblk):
    ts = nblk * WINDOW
    per = s // ts
    perw = s // WINDOW
    qblk = SWAQ_ROW // BRANCH
    vblk = SWAV_ROW // LANES
    kblk = SWAK_OFF // LANES
    prev = lambda bi, i: bi * perw + jnp.maximum(i * nblk - 1, 0)
    return pl.pallas_call(
        functools.partial(_swa_kernel, nblk=nblk),
        grid=(b, per),
        in_specs=[
            pl.BlockSpec(memory_space=pltpu.SMEM),
            pl.BlockSpec((BRANCH, ts), lambda bi, i: (qblk, bi * per + i)),
            pl.BlockSpec((WINDOW, LANES), lambda bi, i: (prev(bi, i), kblk)),
            pl.BlockSpec((ts, LANES), lambda bi, i: (bi * per + i, kblk)),
            pl.BlockSpec((LANES, WINDOW), lambda bi, i: (vblk, prev(bi, i))),
            pl.BlockSpec((LANES, ts), lambda bi, i: (vblk, bi * per + i)),
            _const_spec(bias.shape),
        ],
        out_specs=pl.BlockSpec((ts, BRANCH), lambda bi, i: (bi * per + i, 0)),
        out_shape=jax.ShapeDtypeStruct((b * s, BRANCH), BF16),
        scratch_shapes=[
            pltpu.VMEM((ts + WINDOW, LANES), BF16),
            pltpu.VMEM((SWA_KV, V_ROWS, ts + WINDOW), BF16),
            pltpu.VMEM((SWA_KV, 2, 2 * WINDOW, SWA_GROUP * WINDOW + PITCH_PAD), F32),
            pltpu.VMEM((SWA_KV, 2, SUBLANES, SWA_GROUP * WINDOW), F32),
            pltpu.VMEM((SWA_KV, 2, 2 * WINDOW, SWA_GROUP * WINDOW + PITCH_PAD), BF16),
            pltpu.VMEM((SWA_KV, 2, SUBLANES, SWA_GROUP * WINDOW), F32),
        ],
        compiler_params=_params("arbitrary", "arbitrary"),
        name="swa_attention",
    )(sink, pt, pn, pn, pt, pt, bias)


def _merge_tile(x, gate_ref, branches, wb_ref, wo_ref, m_scr):
    for lo in range(0, D_MODEL, MERGE_CHUNK):
        gates = [jax.nn.sigmoid(gate_ref[:, b * D_MODEL + lo:b * D_MODEL + lo + MERGE_CHUNK].astype(F32))
                 for b in range(len(branches))]
        merged = None
        for b, y in enumerate(branches):
            term = gates[b] * jnp.dot(y, wb_ref[b, :, lo:lo + MERGE_CHUNK], preferred_element_type=F32)
            merged = term if merged is None else merged + term
        m_scr[:, lo:lo + MERGE_CHUNK] = merged.astype(BF16)
    return x + jnp.dot(m_scr[...], wo_ref[...], preferred_element_type=F32)


def _xattn_tile(x, g_ref, wq_ref, kT_ref, v_ref, wo_ref):
    q = jnp.dot(_rms(x, g_ref[...]).astype(BF16), wq_ref[...], preferred_element_type=F32).astype(BF16)
    sls = [slice(h * X_HEAD_DIM, (h + 1) * X_HEAD_DIM) for h in range(X_HEADS)]
    scores = [jnp.dot(q[:, sl], kT_ref[0, sl, :], preferred_element_type=F32) for sl in sls]
    probs = [jnp.exp2(s - jnp.max(s, axis=-1, keepdims=True)) for s in scores]
    denoms = [jnp.sum(p, axis=-1, keepdims=True) for p in probs]
    outs = [jnp.dot(p.astype(BF16), v_ref[0, :, sl], preferred_element_type=F32) for p, sl in zip(probs, sls)]
    heads = [(o / d).astype(BF16) for o, d in zip(outs, denoms)]
    return x + jnp.dot(jnp.concatenate(heads, axis=1), wo_ref[...], preferred_element_type=F32)


def _swiglu_tile(x, g_ref, wg_ref, wu_ref, wd_ref):
    xn = _rms(x, g_ref[...]).astype(BF16)
    y = x
    for lo in range(0, wg_ref.shape[1], FFN_CHUNK):
        gate = jnp.dot(xn, wg_ref[:, lo:lo + FFN_CHUNK], preferred_element_type=F32)
        up = jnp.dot(xn, wu_ref[:, lo:lo + FFN_CHUNK], preferred_element_type=F32)
        hidden = (gate * jax.nn.sigmoid(gate) * up).astype(BF16)
        y = y + jnp.dot(hidden, wd_ref[lo:lo + FFN_CHUNK, :], preferred_element_type=F32)
    return y


def _layer_tail_kernel(x_ref, gate_ref, yc_ref, yf_ref, ys_ref, wb_ref, wo_ref, gx_ref, wq_ref, kT_ref, v_ref,
                       wxo_ref, gf_ref, wg_ref, wu_ref, wd_ref, gl_ref, o_ref, m_scr, *, final_norm):
    x = _merge_tile(x_ref[...], gate_ref, (yc_ref[...], yf_ref[...], ys_ref[...]), wb_ref, wo_ref, m_scr)
    x = _xattn_tile(x, gx_ref, wq_ref, kT_ref, v_ref, wxo_ref)
    x = _swiglu_tile(x, gf_ref, wg_ref, wu_ref, wd_ref)
    o_ref[...] = _rms(x, gl_ref[...]) if final_norm else x


def _layer_tail(x, pn, y_fox, y_swa, w_branch, w_out, gx, wq, kT, v, wxo, gf, wg, wu, wd, g_last, final_norm,
                seq, tm):
    n, d = x.shape
    tiles = seq // tm
    vec = _const_spec((1, d))
    return pl.pallas_call(
        functools.partial(_layer_tail_kernel, final_norm=final_norm),
        grid=(n // tm,),
        in_specs=[
            pl.BlockSpec((tm, d), lambda i: (i, 0)),
            pl.BlockSpec((tm, 3 * D_MODEL), lambda i: (i, GATE_OFF // (3 * D_MODEL))),
            pl.BlockSpec((tm, BRANCH), lambda i: (i, CONV_OFF // BRANCH)),
            pl.BlockSpec((tm, BRANCH), lambda i: (i, 0)),
            pl.BlockSpec((tm, BRANCH), lambda i: (i, 0)),
            _const_spec(w_branch.shape), _const_spec(w_out.shape),
            vec, _const_spec(wq.shape),
            pl.BlockSpec((1,) + kT.shape[1:], lambda i: (i // tiles, 0, 0)),
            pl.BlockSpec((1,) + v.shape[1:], lambda i: (i // tiles, 0, 0)),
            _const_spec(wxo.shape),
            vec, _const_spec(wg.shape), _const_spec(wu.shape), _const_spec(wd.shape), vec,
        ],
        out_specs=pl.BlockSpec((tm, d), lambda i: (i, 0)),
        out_shape=jax.ShapeDtypeStruct((n, d), F32),
        scratch_shapes=[pltpu.VMEM((tm, d), BF16)],
        compiler_params=_params("arbitrary", vmem_limit=TAIL_VMEM_LIMIT_BYTES),
        name="layer_tail",
    )(x, pn, pn, y_fox, y_swa, w_branch, w_out, gx.reshape(1, d), wq, kT, v, wxo,
      gf.reshape(1, d), wg, wu, wd, g_last.reshape(1, d))


def _permute_heads(w, perm, axis):
    shape = w.shape
    split = shape[:axis] + (FOX_HEADS, HEAD_DIM) + shape[axis + 1:]
    return jnp.take(w.reshape(split), perm, axis=axis).reshape(shape)


def _pack_in_proj(w_in, head_perm):
    sizes = [BRANCH] * 3 + [BRANCH] * 3 + [FOX_HEADS] + [BRANCH, SWA_KV * HEAD_DIM, SWA_KV * HEAD_DIM] + [3 * D_MODEL]
    offs = np.concatenate([[0], np.cumsum(sizes)])
    c_b, c_c, c_u, f_q, f_k, f_v, f_g, s_q, s_k, s_v, gates = (w_in[:, offs[t]:offs[t + 1]] for t in range(len(sizes)))
    f_q, f_k, f_v = (_permute_heads(w, head_perm, 1) for w in (f_q, f_k, f_v))
    f_g = jnp.take(f_g, head_perm, axis=1)
    qscale = HEAD_DIM ** -0.5 * LOG2E
    wn = jnp.concatenate([gates, s_k], axis=1)
    wc = jnp.concatenate([c_b, c_c, c_u], axis=1)
    wt = jnp.concatenate([f_q * qscale, f_v, s_q * qscale, s_v], axis=1).T
    wfg = jnp.pad(f_g, ((0, 0), (0, LANES - FOX_HEADS)))
    return wn.astype(BF16), wc.astype(BF16), f_k.astype(BF16), wt.astype(BF16), wfg.astype(BF16)


def kernel(x, mem, mix_norm_g, w_in, forget_bias, conv_w, sink, w_branch, w_mix_out, rel_bias,
           xattn_norm_g, mem_norm_g, w_xq, w_xkv, w_xo, ffn_norm_g, w_ffn_gate, w_ffn_up, w_ffn_down,
           final_norm_g):
    b, s, d = x.shape
    n = b * s
    depth = w_in.shape[0]
    assert d == D_MODEL and s % FOX_TQ == 0 and s % ROW_TILE == 0 and FOX_TQ == 2 * ROW_TILE
    assert s % (SWA_BLOCKS_PER_STEP * WINDOW) == 0 and SWA_BLOCKS_PER_STEP % 2 == 0
    mem_len = mem.shape[1]
    xf = x.reshape(n, d)
    memf = mem.reshape(b * mem_len, d)
    swa_bias = _swa_bias_table(rel_bias)
    for l in range(depth):
        head_perm = jnp.argsort(forget_bias[l])
        wn, wc, wk, wt, wfg = _pack_in_proj(w_in[l], head_perm)
        w_br = w_branch[l].at[1].set(_permute_heads(w_branch[l, 1], head_perm, 0)).astype(BF16)
        pn, pt, k_aug, st = _in_proj(xf, mix_norm_g[l], wn, wc, wk, wt, wfg, conv_w[l],
                                     jnp.take(forget_bias[l], head_perm), seq=s, tm=ROW_TILE)
        st = st.reshape(b, s // ROW_TILE, SUBLANES, LANES)
        k_sq = jnp.broadcast_to(jnp.max(st[:, :, 1:2, 0:FOX_HEADS], axis=1, keepdims=True),
                                (b, s // ROW_TILE, 1, FOX_HEADS))
        stats = jnp.concatenate([st[:, :, 0:1, 0:FOX_HEADS], k_sq, st[:, :, None, 0:FOX_HEADS, FOX_HEADS]], axis=2)
        y_fox = _fox_attention(stats, pt, k_aug, b, s, tq=FOX_TQ, hp=FOX_HEADS_PER_STEP)
        y_swa = _swa_attention(sink[l], pt, pn, swa_bias, b, s, nblk=SWA_BLOCKS_PER_STEP)
        kv = _norm_proj(memf, mem_norm_g[l], w_xkv[l].astype(BF16))
        kT = kv[:, :d].reshape(b, mem_len, d).transpose(0, 2, 1)
        v = kv[:, d:].reshape(b, mem_len, d)
        wq = (w_xq[l] * (X_HEAD_DIM ** -0.5 * LOG2E)).astype(BF16)
        xf = _layer_tail(xf, pn, y_fox, y_swa, w_br, w_mix_out[l].astype(BF16),
                         xattn_norm_g[l], wq, kT, v, w_xo[l].astype(BF16),
                         ffn_norm_g[l], w_ffn_gate[l].astype(BF16), w_ffn_up[l].astype(BF16),
                         w_ffn_down[l].astype(BF16), final_norm_g, final_norm=(l == depth - 1),
                         seq=s, tm=ROW_TILE)
    return xf.reshape(b, s, d)
```

```python
import functools
import math

import jax
import jax.numpy as jnp
import numpy as np
from jax import lax
from jax.experimental import pallas as pl
from jax.experimental.pallas import tpu as pltpu

F32 = jnp.float32
BF16 = jnp.bfloat16

D_MODEL = 1024
HEAD_DIM = 64
BRANCH = 512
FOX_HEADS = 8
SWA_HEADS = 8
SWA_KV = 2
SWA_GROUP = SWA_HEADS // SWA_KV
WINDOW = 128
N_BUCKETS = 32
X_HEADS = 4
X_HEAD_DIM = D_MODEL // X_HEADS
RMS_EPS = 1e-6
NEG = -1e30
LOG2E = math.log2(math.e)

LANES = 128
SUBLANES = 8
VMEM_LIMIT_BYTES = 56 * 1024 * 1024
FOX_VMEM_LIMIT_BYTES = 58 * 1024 * 1024
TAIL_VMEM_LIMIT_BYTES = 58 * 1024 * 1024
PITCH_PAD = LANES
MERGE_CHUNK = 256
FFN_CHUNK = 256

ROW_TILE = 512
FOX_TQ = 1024
FOX_HEADS_PER_STEP = 4
SWA_BLOCKS_PER_STEP = 16

GATE_OFF = 0
CONV_OFF = 3 * D_MODEL
SWAK_OFF = CONV_OFF + BRANCH
PN_COLS = SWAK_OFF + SWA_KV * HEAD_DIM
FOXQ_ROW = 0
FOXV_ROW = BRANCH
SWAQ_ROW = 2 * BRANCH
SWAV_ROW = 3 * BRANCH
PT_ROWS = SWAV_ROW + SWA_KV * HEAD_DIM

AUG_STRIDE = 8
AUG_EVEN = (HEAD_DIM, HEAD_DIM + AUG_STRIDE, HEAD_DIM + 2 * AUG_STRIDE)
AUG_ODD = (0, AUG_STRIDE, 2 * AUG_STRIDE)
V_ROWS = 80
SKIP_BITS = 160.0
SKIP_SLACK = 1.05


def _params(*sem, vmem_limit=VMEM_LIMIT_BYTES):
    return pltpu.CompilerParams(dimension_semantics=sem, vmem_limit_bytes=vmem_limit)


def _rms(x, g):
    return x * lax.rsqrt(jnp.mean(x * x, axis=-1, keepdims=True) + RMS_EPS) * g


def _const_spec(shape):
    nd = len(shape)
    return pl.BlockSpec(shape, lambda *_: (0,) * nd, pipeline_mode=pl.Buffered(1))


def _one_hot_rows(shape, axis, positions):
    idx = lax.broadcasted_iota(jnp.int32, shape, axis)
    hit = idx == positions[0]
    for p in positions[1:]:
        hit = hit | (idx == p)
    return jnp.where(hit, 1.0, 0.0)


def _in_proj_kernel(x_ref, g_ref, wn_ref, wc_ref, wk_ref, wt_ref, wfg_ref, cw_ref, fb_ref,
                    pn_ref, pt_ref, ka_ref, st_ref, z_tail, c_tail, *, tiles_per_seq):
    tm = x_ref.shape[0]
    first_of_seq = pl.program_id(0) % tiles_per_seq == 0

    @pl.when(pl.program_id(0) == 0)
    def _():
        z_tail[...] = jnp.zeros(z_tail.shape, F32)
        c_tail[...] = jnp.zeros(c_tail.shape, F32)

    h = _rms(x_ref[...], g_ref[...]).astype(BF16)

    fg = jnp.dot(h, wfg_ref[...], preferred_element_type=F32)
    k_fox = jnp.dot(h, wk_ref[...], preferred_element_type=F32)
    lane = lax.broadcasted_iota(jnp.int32, (tm, LANES), 1)
    trow = lax.broadcasted_iota(jnp.int32, (tm, LANES), 0)
    c = jnp.where(lane < FOX_HEADS, jax.nn.log_sigmoid(fg + fb_ref[...]) * LOG2E, 0.0)
    d = 1
    while d < tm:
        c = c + jnp.where(trow >= d, pltpu.roll(c, d, axis=0), 0.0)
        d *= 2
    c = c + jnp.where(first_of_seq, 0.0, c_tail[0:1, :])
    c_tail[0:1, :] = c[tm - 1:tm, :]
    head_of_col = lax.broadcasted_iota(jnp.int32, (BRANCH, LANES), 0) // HEAD_DIM
    head_sel = jnp.where(head_of_col == lax.broadcasted_iota(jnp.int32, (BRANCH, LANES), 1), 1.0, 0.0).astype(BF16)
    k_sq = jnp.dot((k_fox * k_fox).astype(BF16), head_sel, preferred_element_type=F32)
    st_ref[...] = jnp.zeros(st_ref.shape, F32)
    st_ref[0:1, :] = c[tm - 1:tm, :]
    st_ref[1:2, :] = jnp.max(k_sq, axis=0, keepdims=True)
    c1 = c.astype(BF16).astype(F32)
    r1 = c - c1
    c2 = r1.astype(BF16).astype(F32)
    c3 = (r1 - c2).astype(BF16).astype(F32)
    packed = -jnp.where(lane < AUG_STRIDE, c1,
                        jnp.where(lane < 2 * AUG_STRIDE, pltpu.roll(c2, AUG_STRIDE, axis=1),
                                  pltpu.roll(c3, 2 * AUG_STRIDE, axis=1)))
    keep_even = _one_hot_rows((1, LANES), 1, AUG_EVEN)
    keep_odd = _one_hot_rows((1, LANES), 1, AUG_ODD)
    for hd in range(FOX_HEADS):
        k_pair = k_fox[:, (hd // 2) * LANES:(hd // 2 + 1) * LANES]
        if hd % 2 == 0:
            aug = pltpu.roll(packed, (AUG_EVEN[0] - hd) % LANES, axis=1) * keep_even
            out = jnp.where(lane < HEAD_DIM, k_pair, aug)
        else:
            aug = pltpu.roll(packed, (AUG_ODD[0] - hd) % LANES, axis=1) * keep_odd
            out = jnp.where(lane >= HEAD_DIM, k_pair, aug)
        ka_ref[0, hd] = out.astype(BF16)

    conv_b, conv_c, conv_u = (jnp.dot(h, wc_ref[:, t * BRANCH:(t + 1) * BRANCH], preferred_element_type=F32)
                              for t in range(3))
    z = conv_c * conv_u
    zh = jnp.where(first_of_seq, 0.0, z_tail[...])
    z_tail[...] = z[tm - SUBLANES:tm, :]
    row = lax.broadcasted_iota(jnp.int32, z.shape, 0)
    z1 = jnp.where(row == 0, zh[7:8], pltpu.roll(z, 1, axis=0))
    z2 = jnp.where(row == 0, zh[6:7], jnp.where(row == 1, zh[7:8], pltpu.roll(z, 2, axis=0)))
    cw = cw_ref[...]
    pn_ref[:, CONV_OFF:SWAK_OFF] = (conv_b * (cw[0:1] * z2 + cw[1:2] * z1 + cw[2:3] * z)).astype(BF16)

    for lo in range(0, CONV_OFF, D_MODEL):
        pn_ref[:, lo:lo + D_MODEL] = jnp.dot(h, wn_ref[:, lo:lo + D_MODEL], preferred_element_type=F32).astype(BF16)
    pn_ref[:, SWAK_OFF:PN_COLS] = jnp.dot(h, wn_ref[:, CONV_OFF:], preferred_element_type=F32).astype(BF16)
    pt = lax.dot_general(wt_ref[...], h, (((1,), (1,)), ((), ())), preferred_element_type=F32)
    pt_ref[...] = pt.astype(BF16)
    for hd in range(FOX_HEADS):
        q_h = pt[FOXQ_ROW + hd * HEAD_DIM:FOXQ_ROW + (hd + 1) * HEAD_DIM, :]
        q_sq = jnp.sum(q_h * q_h, axis=0, keepdims=True)
        st_ref[hd:hd + 1, FOX_HEADS:FOX_HEADS + 1] = jnp.max(q_sq, axis=1, keepdims=True)


def _in_proj(x, g, wn, wc, wk, wt, wfg, conv_w, forget_bias, seq, tm):
    n, d = x.shape
    tiles = seq // tm
    fb = jnp.pad(forget_bias, (0, LANES - FOX_HEADS)).reshape(1, LANES)
    return pl.pallas_call(
        functools.partial(_in_proj_kernel, tiles_per_seq=tiles),
        grid=(n // tm,),
        in_specs=[
            pl.BlockSpec((tm, d), lambda i: (i, 0)),
            _const_spec((1, d)), _const_spec(wn.shape), _const_spec(wc.shape), _const_spec(wk.shape),
            _const_spec(wt.shape), _const_spec(wfg.shape), _const_spec(conv_w.shape), _const_spec((1, LANES)),
        ],
        out_specs=[
            pl.BlockSpec((tm, PN_COLS), lambda i: (i, 0)),
            pl.BlockSpec((PT_ROWS, tm), lambda i: (0, i)),
            pl.BlockSpec((1, FOX_HEADS, tm, LANES), lambda i: (i // tiles, 0, i % tiles, 0)),
            pl.BlockSpec((SUBLANES, LANES), lambda i: (i, 0)),
        ],
        out_shape=[jax.ShapeDtypeStruct((n, PN_COLS), BF16), jax.ShapeDtypeStruct((PT_ROWS, n), BF16),
                   jax.ShapeDtypeStruct((n // seq, FOX_HEADS, seq, LANES), BF16),
                   jax.ShapeDtypeStruct((n // tm * SUBLANES, LANES), F32)],
        scratch_shapes=[pltpu.VMEM((SUBLANES, BRANCH), F32), pltpu.VMEM((SUBLANES, LANES), F32)],
        compiler_params=_params("arbitrary"),
        name="in_proj",
    )(x, g.reshape(1, d), wn, wc, wk, wt, wfg, conv_w, fb)


def _norm_proj_kernel(x_ref, g_ref, w_ref, o_ref):
    h = _rms(x_ref[...], g_ref[...]).astype(BF16)
    o_ref[...] = jnp.dot(h, w_ref[...], preferred_element_type=F32).astype(BF16)


def _norm_proj(x, g, w):
    n, d = x.shape
    cols = w.shape[1]
    return pl.pallas_call(
        _norm_proj_kernel,
        grid=(1,),
        in_specs=[_const_spec((n, d)), _const_spec((1, d)), _const_spec(w.shape)],
        out_specs=pl.BlockSpec((n, cols), lambda i: (0, 0)),
        out_shape=jax.ShapeDtypeStruct((n, cols), BF16),
        compiler_params=_params("arbitrary"),
        name="mem_proj",
    )(x, g.reshape(1, d), w)


def _fox_kernel(st_ref, q_ref, qn_ref, k_ref, v_ref, o_ref, q_aug, v_aug, s_buf, mt_buf, p_buf, acc_ref, first_ref,
                *, tq, tk, hp):
    bi = pl.program_id(0)
    grp = pl.program_id(1)
    i = pl.program_id(2)
    seq = v_ref.shape[1]
    cur = i % 2
    heads = tuple(range(hp))

    def first_live_pair(qi):
        c_q = jnp.maximum(2 * qi - 1, 0)
        first = qi
        for hh in heads:
            h = grp * hp + hh
            q_sq = jnp.maximum(st_ref[bi, 2 * qi, 2, h], st_ref[bi, 2 * qi + 1, 2, h])
            k_sq = st_ref[bi, 0, 1, h]
            bound_sq = (4.0 * SKIP_SLACK * SKIP_SLACK) * q_sq * k_sq

            def count(u, n):
                room = -SKIP_BITS - (st_ref[bi, c_q, 0, h] - st_ref[bi, 2 * u + 1, 0, h])
                dead = jnp.logical_and(room > 0.0, bound_sq < room * room)
                return n + dead.astype(jnp.int32)

            first = jnp.minimum(first, lax.fori_loop(0, qi, count, jnp.int32(0)))
        return first

    def build_queries(src_ref, qslot):
        qrow = lax.broadcasted_iota(jnp.int32, (LANES, tq), 0)
        for pr in range(hp // 2):
            qb = src_ref[pr * LANES:(pr + 1) * LANES, :].astype(F32)
            q_aug[qslot, 2 * pr] = jnp.where(qrow < HEAD_DIM, qb, _one_hot_rows((LANES, tq), 0, AUG_EVEN)).astype(BF16)
            q_aug[qslot, 2 * pr + 1] = jnp.where(qrow >= HEAD_DIM, qb,
                                                 _one_hot_rows((LANES, tq), 0, AUG_ODD)).astype(BF16)

    def scores(t, slot, qslot):
        off = pl.multiple_of(t * tk, tk)
        for hh in heads:
            s = jnp.dot(k_ref[0, hh, pl.ds(off, tk), :], q_aug[qslot, hh], preferred_element_type=F32)
            s_buf[hh, slot, :, 0:tq] = s
            mt_buf[hh, slot, 0:1, :] = jnp.max(s, axis=0, keepdims=True)

    def softmax(slot, ms, mask=None):
        out = []
        for hh in heads:
            s = s_buf[hh, slot, :, 0:tq]
            if mask is None:
                tile_max = mt_buf[hh, slot, 0:1, :]
            else:
                s = jnp.where(mask, s, NEG)
                tile_max = jnp.max(s, axis=0, keepdims=True)
            m_new = jnp.maximum(ms[hh], tile_max)
            p_buf[hh, slot, :, 0:tq] = jnp.exp2(s - m_new).astype(BF16)
            out.append((m_new, jnp.exp2(ms[hh] - m_new)))
        return tuple(o[0] for o in out), tuple(o[1] for o in out)

    def values(t, slot, alphas):
        off = pl.multiple_of(jnp.maximum(t, 0) * tk, tk)
        for hh in heads:
            pv = jnp.dot(v_aug[hh, :, pl.ds(off, tk)], p_buf[hh, slot, :, 0:tq], preferred_element_type=F32)
            acc_ref[hh] = alphas[hh] * acc_ref[hh] + pv

    @pl.when(i == 0)
    def _():
        ones_row = _one_hot_rows((V_ROWS - HEAD_DIM, seq), 0, (0,)).astype(BF16)
        for hh in heads:
            v_aug[hh, 0:HEAD_DIM, :] = v_ref[hh * HEAD_DIM:(hh + 1) * HEAD_DIM, :]
            v_aug[hh, HEAD_DIM:V_ROWS, :] = ones_row
            p_buf[hh, 1, :, 0:tq] = jnp.zeros((tk, tq), BF16)
            acc_ref[hh] = jnp.zeros(acc_ref.shape[1:], F32)
        build_queries(q_ref, 0)
        scores(0, 0, 0)

    def pair(u, carry):
        ms, alpha_prev = carry
        t = 2 * u
        scores(t + 1, 1, cur)
        ms, alpha0 = softmax(0, ms)
        values(t - 1, 1, alpha_prev)
        scores(t + 2, 0, cur)
        ms, alpha1 = softmax(1, ms)
        values(t, 0, alpha0)
        return ms, alpha1

    carry = (tuple(jnp.full((1, tq), NEG, F32) for _ in heads), tuple(jnp.ones((1, tq), F32) for _ in heads))
    first_now = jnp.where(i == 0, 0, first_ref[0])
    first_next = first_live_pair(jnp.minimum(i + 1, seq // tq - 1))
    ms, alpha_prev = lax.fori_loop(first_now, i, pair, carry)

    t = 2 * i
    key = lax.broadcasted_iota(jnp.int32, (tk, tq), 0)
    qry = lax.broadcasted_iota(jnp.int32, (tk, tq), 1)
    off_b = pl.multiple_of((t + 1) * tk, tk)
    tri = lax.broadcasted_iota(jnp.int32, (tk, tk), 0) <= lax.broadcasted_iota(jnp.int32, (tk, tk), 1)
    s_b = [jnp.where(tri, jnp.dot(k_ref[0, hh, pl.ds(off_b, tk), :], q_aug[cur, hh, :, tk:],
                                  preferred_element_type=F32), NEG) for hh in heads]
    ms, alpha0 = softmax(0, ms, key <= qry)
    build_queries(qn_ref, 1 - cur)
    first_ref[0] = first_next
    scores(2 * first_next, 0, 1 - cur)
    values(t - 1, 1, alpha_prev)
    values(t, 0, alpha0)
    outs = []
    for hh in heads:
        mt_buf[hh, 1, 0:1, :] = ms[hh]
        m_old = mt_buf[hh, 1, 0:1, tk:]
        m_new = jnp.maximum(m_old, jnp.max(s_b[hh], axis=0, keepdims=True))
        p_b = jnp.exp2(s_b[hh] - m_new).astype(BF16)
        pv = jnp.dot(v_aug[hh, :, pl.ds(off_b, tk)], p_b, preferred_element_type=F32)
        acc_ref[hh, :, tk:] = jnp.exp2(m_old - m_new) * acc_ref[hh, :, tk:] + pv
        acc = acc_ref[hh]
        outs.append(acc[:HEAD_DIM] / acc[HEAD_DIM:HEAD_DIM + 1])
    o_ref[...] = jnp.concatenate(outs, axis=0).T.astype(BF16)


def _fox_attention(stats, pt, k_aug, b, s, tq, hp):
    tk = tq // 2
    per = s // tq
    rows = hp * HEAD_DIM
    qblk = FOXQ_ROW // rows
    vblk = FOXV_ROW // rows
    return pl.pallas_call(
        functools.partial(_fox_kernel, tq=tq, tk=tk, hp=hp),
        grid=(b, FOX_HEADS // hp, per),
        in_specs=[
            pl.BlockSpec(memory_space=pltpu.SMEM),
            pl.BlockSpec((rows, tq), lambda bi, g, i: (qblk + g, bi * per + i)),
            pl.BlockSpec((rows, tq), lambda bi, g, i: (qblk + g, bi * per + jnp.minimum(i + 1, per - 1))),
            pl.BlockSpec((1, hp, s, LANES), lambda bi, g, i: (bi, g, 0, 0), pipeline_mode=pl.Buffered(1)),
            pl.BlockSpec((rows, s), lambda bi, g, i: (vblk + g, bi), pipeline_mode=pl.Buffered(1)),
        ],
        out_specs=pl.BlockSpec((tq, rows), lambda bi, g, i: (bi * per + i, g)),
        out_shape=jax.ShapeDtypeStruct((b * s, BRANCH), BF16),
        scratch_shapes=[
            pltpu.VMEM((2, hp, LANES, tq), BF16),
            pltpu.VMEM((hp, V_ROWS, s), BF16),
            pltpu.VMEM((hp, 2, tk, tq + PITCH_PAD), F32),
            pltpu.VMEM((hp, 2, SUBLANES, tq), F32),
            pltpu.VMEM((hp, 2, tk, tq + PITCH_PAD), BF16),
            pltpu.VMEM((hp, V_ROWS, tq), F32),
            pltpu.SMEM((1,), jnp.int32),
        ],
        compiler_params=_params("arbitrary", "arbitrary", "arbitrary", vmem_limit=FOX_VMEM_LIMIT_BYTES),
        name="fox_attention",
    )(stats, pt, pt, k_aug, pt)


def _swa_bias_kernel(rb_ref, bucket_ref, win_ref, o_ref):
    h = pl.program_id(0)
    bucket = bucket_ref[...]
    bias = jnp.zeros(bucket.shape, F32)
    for b in range(N_BUCKETS):
        bias = jnp.where(bucket == b, rb_ref[b, h] * LOG2E, bias)
    o_ref[0, 0] = jnp.where(win_ref[0] > 0, bias, NEG)
    o_ref[1, 0] = jnp.where(win_ref[1] > 0, bias, NEG)


def _swa_bias_table(rel_bias):
    tq = np.arange(WINDOW)[None, :]
    sk = np.arange(2 * WINDOW)[:, None]
    dist = WINDOW + tq - sk
    n = jnp.maximum(jnp.asarray(dist, jnp.int32), 0)
    max_exact = N_BUCKETS // 2
    large = max_exact + (jnp.log(jnp.maximum(n, 1).astype(F32) / max_exact)
                         / math.log(WINDOW / max_exact) * (N_BUCKETS - max_exact)).astype(jnp.int32)
    bucket = jnp.where(n < max_exact, n, jnp.minimum(large, N_BUCKETS - 1))
    in_window = (dist >= 0) & (dist < WINDOW)
    win = np.stack([in_window, in_window & (sk >= WINDOW)]).astype(np.int32)
    full = lambda shape: pl.BlockSpec(shape, lambda h: (0,) * len(shape))
    return pl.pallas_call(
        _swa_bias_kernel,
        grid=(SWA_HEADS,),
        in_specs=[pl.BlockSpec(memory_space=pltpu.SMEM), full(bucket.shape), full(win.shape)],
        out_specs=pl.BlockSpec((2, 1, 2 * WINDOW, WINDOW), lambda h: (0, h // SWA_GROUP, 0, h % SWA_GROUP)),
        out_shape=jax.ShapeDtypeStruct((2, SWA_KV, 2 * WINDOW, SWA_GROUP * WINDOW), F32),
        compiler_params=_params("arbitrary"),
        name="swa_bias_table",
    )(rel_bias, bucket, jnp.asarray(win))


def _swa_kernel(sink_ref, q_ref, kp_ref, kc_ref, vp_ref, vc_ref, bias_ref, o_ref,
                k_buf, v_buf, s_buf, mt_buf, p_buf, dn_buf, *, nblk):
    gw = SWA_GROUP * WINDOW
    ts = nblk * WINDOW
    first = pl.program_id(1) == 0
    k_buf[0:WINDOW, :] = kp_ref[...]
    k_buf[WINDOW:WINDOW + ts, :] = kc_ref[...]
    ones_row = _one_hot_rows((V_ROWS - HEAD_DIM, ts + WINDOW), 0, (0,)).astype(BF16)
    for kv in range(SWA_KV):
        rows = slice(kv * HEAD_DIM, (kv + 1) * HEAD_DIM)
        v_buf[kv, 0:HEAD_DIM, 0:WINDOW] = vp_ref[rows, :]
        v_buf[kv, 0:HEAD_DIM, WINDOW:WINDOW + ts] = vc_ref[rows, :]
        v_buf[kv, HEAD_DIM:V_ROWS, :] = ones_row
    glane = lax.broadcasted_iota(jnp.int32, (1, gw), 1) // WINDOW
    zeros = jnp.zeros((HEAD_DIM, gw), BF16)
    sinks = []
    for kv in range(SWA_KV):
        sink = jnp.zeros((1, gw), F32)
        for g in range(SWA_GROUP):
            sink = jnp.where(glane == g, sink_ref[kv * SWA_GROUP + g] * LOG2E, sink)
        sinks.append(sink)

    def scores(blk, slot):
        lo = pl.multiple_of(blk * WINDOW, WINDOW)
        sel = jnp.logical_and(first, blk == 0).astype(jnp.int32)
        keys = k_buf[pl.ds(lo, 2 * WINDOW), :]
        for kv in range(SWA_KV):
            qg = jnp.concatenate([q_ref[(kv * SWA_GROUP + g) * HEAD_DIM:(kv * SWA_GROUP + g + 1) * HEAD_DIM,
                                        pl.ds(lo, WINDOW)] for g in range(SWA_GROUP)], axis=1)
            q_pad = jnp.concatenate([qg, zeros] if kv == 0 else [zeros, qg], axis=0)
            s = jnp.dot(keys, q_pad, preferred_element_type=F32) + bias_ref[sel, kv]
            s_buf[kv, slot, :, 0:gw] = s
            mt_buf[kv, slot, 0:1, :] = jnp.max(s, axis=0, keepdims=True)

    def softmax(slot):
        for kv in range(SWA_KV):
            m = jnp.maximum(mt_buf[kv, slot, 0:1, :], sinks[kv])
            p = jnp.exp2(s_buf[kv, slot, :, 0:gw] - m)
            dn_buf[kv, slot, 0:1, :] = jnp.exp2(sinks[kv] - m)
            p_buf[kv, slot, :, 0:gw] = p.astype(BF16)

    def values(blk, slot):
        lo = pl.multiple_of(blk * WINDOW, WINDOW)
        for kv in range(SWA_KV):
            pv = jnp.dot(v_buf[kv, :, pl.ds(lo, 2 * WINDOW)], p_buf[kv, slot, :, 0:gw],
                         preferred_element_type=F32)
            o = pv[:HEAD_DIM] / (pv[HEAD_DIM:HEAD_DIM + 1] + dn_buf[kv, slot, 0:1, :])
            for pr in range(SWA_GROUP // 2):
                two = jnp.concatenate([o[:, (2 * pr) * WINDOW:(2 * pr + 1) * WINDOW],
                                       o[:, (2 * pr + 1) * WINDOW:(2 * pr + 2) * WINDOW]], axis=0)
                col = (kv * (SWA_GROUP // 2) + pr) * LANES
                o_ref[pl.ds(lo, WINDOW), col:col + LANES] = two.T.astype(BF16)

    scores(jnp.int32(0), 0)
    scores(jnp.int32(1), 1)
    softmax(0)

    def pair(u, carry):
        blk = 2 * u + 1
        scores(blk + 1, 0)
        softmax(1)
        values(blk - 1, 0)
        scores(blk + 2, 1)
        softmax(0)
        values(blk, 1)
        return carry

    lax.fori_loop(0, nblk // 2 - 1, pair, 0)
    softmax(1)
    values(jnp.int32(nblk - 2), 0)
    values(jnp.int32(nblk - 1), 1)


def _swa_attention(sink, pt, pn, bias, b, s, nblk):
    ts = nblk * WINDOW
    per = s // ts
    perw = s // WINDOW
    qblk = SWAQ_ROW // BRANCH
    vblk = SWAV_ROW // LANES
    kblk = SWAK_OFF // LANES
    prev = lambda bi, i: bi * perw + jnp.maximum(i * nblk - 1, 0)
    return pl.pallas_call(
        functools.partial(_swa_kernel, nblk=nblk),
        grid=(b, per),
        in_specs=[
            pl.BlockSpec(memory_space=pltpu.SMEM),
            pl.BlockSpec((BRANCH, ts), lambda bi, i: (qblk, bi * per + i)),
            pl.BlockSpec((WINDOW, LANES), lambda bi, i: (prev(bi, i), kblk)),
            pl.BlockSpec((ts, LANES), lambda bi, i: (bi * per + i, kblk)),
            pl.BlockSpec((LANES, WINDOW), lambda bi, i: (vblk, prev(bi, i))),
            pl.BlockSpec((LANES, ts), lambda bi, i: (vblk, bi * per + i)),
            _const_spec(bias.shape),
        ],
        out_specs=pl.BlockSpec((ts, BRANCH), lambda bi, i: (bi * per + i, 0)),
        out_shape=jax.ShapeDtypeStruct((b * s, BRANCH), BF16),
        scratch_shapes=[
            pltpu.VMEM((ts + WINDOW, LANES), BF16),
            pltpu.VMEM((SWA_KV, V_ROWS, ts + WINDOW), BF16),
            pltpu.VMEM((SWA_KV, 2, 2 * WINDOW, SWA_GROUP * WINDOW + PITCH_PAD), F32),
            pltpu.VMEM((SWA_KV, 2, SUBLANES, SWA_GROUP * WINDOW), F32),
            pltpu.VMEM((SWA_KV, 2, 2 * WINDOW, SWA_GROUP * WINDOW + PITCH_PAD), BF16),
            pltpu.VMEM((SWA_KV, 2, SUBLANES, SWA_GROUP * WINDOW), F32),
        ],
        compiler_params=_params("arbitrary", "arbitrary"),
        name="swa_attention",
    )(sink, pt, pn, pn, pt, pt, bias)


def _merge_tile(x, gate_ref, branches, wb_ref, wo_ref, m_scr):
    for lo in range(0, D_MODEL, MERGE_CHUNK):
        gates = [jax.nn.sigmoid(gate_ref[:, b * D_MODEL + lo:b * D_MODEL + lo + MERGE_CHUNK].astype(F32))
                 for b in range(len(branches))]
        merged = None
        for b, y in enumerate(branches):
            term = gates[b] * jnp.dot(y, wb_ref[b, :, lo:lo + MERGE_CHUNK], preferred_element_type=F32)
            merged = term if merged is None else merged + term
        m_scr[:, lo:lo + MERGE_CHUNK] = merged.astype(BF16)
    return x + jnp.dot(m_scr[...], wo_ref[...], preferred_element_type=F32)


def _xattn_tile(x, g_ref, wq_ref, kT_ref, v_ref, wo_ref):
    q = jnp.dot(_rms(x, g_ref[...]).astype(BF16), wq_ref[...], preferred_element_type=F32).astype(BF16)
    sls = [slice(h * X_HEAD_DIM, (h + 1) * X_HEAD_DIM) for h in range(X_HEADS)]
    scores = [jnp.dot(q[:, sl], kT_ref[0, sl, :], preferred_element_type=F32) for sl in sls]
    probs = [jnp.exp2(s - jnp.max(s, axis=-1, keepdims=True)) for s in scores]
    denoms = [jnp.sum(p, axis=-1, keepdims=True) for p in probs]
    outs = [jnp.dot(p.astype(BF16), v_ref[0, :, sl], preferred_element_type=F32) for p, sl in zip(probs, sls)]
    heads = [(o / d).astype(BF16) for o, d in zip(outs, denoms)]
    return x + jnp.dot(jnp.concatenate(heads, axis=1), wo_ref[...], preferred_element_type=F32)


def _swiglu_tile(x, g_ref, wg_ref, wu_ref, wd_ref):
    xn = _rms(x, g_ref[...]).astype(BF16)
    y = x
    for lo in range(0, wg_ref.shape[1], FFN_CHUNK):
        gate = jnp.dot(xn, wg_ref[:, lo:lo + FFN_CHUNK], preferred_element_type=F32)
        up = jnp.dot(xn, wu_ref[:, lo:lo + FFN_CHUNK], preferred_element_type=F32)
        hidden = (gate * jax.nn.sigmoid(gate) * up).astype(BF16)
        y = y + jnp.dot(hidden, wd_ref[lo:lo + FFN_CHUNK, :], preferred_element_type=F32)
    return y


def _layer_tail_kernel(x_ref, gate_ref, yc_ref, yf_ref, ys_ref, wb_ref, wo_ref, gx_ref, wq_ref, kT_ref, v_ref,
                       wxo_ref, gf_ref, wg_ref, wu_ref, wd_ref, gl_ref, o_ref, m_scr, *, final_norm):
    x = _merge_tile(x_ref[...], gate_ref, (yc_ref[...], yf_ref[...], ys_ref[...]), wb_ref, wo_ref, m_scr)
    x = _xattn_tile(x, gx_ref, wq_ref, kT_ref, v_ref, wxo_ref)
    x = _swiglu_tile(x, gf_ref, wg_ref, wu_ref, wd_ref)
    o_ref[...] = _rms(x, gl_ref[...]) if final_norm else x


def _layer_tail(x, pn, y_fox, y_swa, w_branch, w_out, gx, wq, kT, v, wxo, gf, wg, wu, wd, g_last, final_norm,
                seq, tm):
    n, d = x.shape
    tiles = seq // tm
    vec = _const_spec((1, d))
    return pl.pallas_call(
        functools.partial(_layer_tail_kernel, final_norm=final_norm),
        grid=(n // tm,),
        in_specs=[
            pl.BlockSpec((tm, d), lambda i: (i, 0)),
            pl.BlockSpec((tm, 3 * D_MODEL), lambda i: (i, GATE_OFF // (3 * D_MODEL))),
            pl.BlockSpec((tm, BRANCH), lambda i: (i, CONV_OFF // BRANCH)),
            pl.BlockSpec((tm, BRANCH), lambda i: (i, 0)),
            pl.BlockSpec((tm, BRANCH), lambda i: (i, 0)),
            _const_spec(w_branch.shape), _const_spec(w_out.shape),
            vec, _const_spec(wq.shape),
            pl.BlockSpec((1,) + kT.shape[1:], lambda i: (i // tiles, 0, 0)),
            pl.BlockSpec((1,) + v.shape[1:], lambda i: (i // tiles, 0, 0)),
            _const_spec(wxo.shape),
            vec, _const_spec(wg.shape), _const_spec(wu.shape), _const_spec(wd.shape), vec,
        ],
        out_specs=pl.BlockSpec((tm, d), lambda i: (i, 0)),
        out_shape=jax.ShapeDtypeStruct((n, d), F32),
        scratch_shapes=[pltpu.VMEM((tm, d), BF16)],
        compiler_params=_params("arbitrary", vmem_limit=TAIL_VMEM_LIMIT_BYTES),
        name="layer_tail",
    )(x, pn, pn, y_fox, y_swa, w_branch, w_out, gx.reshape(1, d), wq, kT, v, wxo,
      gf.reshape(1, d), wg, wu, wd, g_last.reshape(1, d))


def _permute_heads(w, perm, axis):
    shape = w.shape
    split = shape[:axis] + (FOX_HEADS, HEAD_DIM) + shape[axis + 1:]
    return jnp.take(w.reshape(split), perm, axis=axis).reshape(shape)


def _pack_in_proj(w_in, w_in_bf16, l, head_perm):
    sizes = [3 * BRANCH] + [BRANCH] * 3 + [FOX_HEADS] + [BRANCH, SWA_KV * HEAD_DIM, SWA_KV * HEAD_DIM] + [3 * D_MODEL]
    offs = np.concatenate([[0], np.cumsum(sizes)])
    f_q, f_v, s_q, s_v = (w_in[l, :, offs[t]:offs[t + 1]] for t in (1, 3, 5, 7))
    wc, f_k, f_g, s_k, gates = (w_in_bf16[l, :, offs[t]:offs[t + 1]] for t in (0, 2, 4, 6, 8))
    qscale = HEAD_DIM ** -0.5 * LOG2E
    f_q, s_q = (f_q * qscale).astype(BF16), (s_q * qscale).astype(BF16)
    f_v, s_v = f_v.astype(BF16), s_v.astype(BF16)
    f_q, f_k, f_v = (_permute_heads(w, head_perm, 1) for w in (f_q, f_k, f_v))
    f_g = jnp.take(f_g, head_perm, axis=1)
    wn = jnp.concatenate([gates, s_k], axis=1)
    wt = jnp.concatenate([f_q, f_v, s_q, s_v], axis=1).T
    wfg = jnp.pad(f_g, ((0, 0), (0, LANES - FOX_HEADS)))
    return wn, wc, f_k, wt, wfg


def kernel(x, mem, mix_norm_g, w_in, forget_bias, conv_w, sink, w_branch, w_mix_out, rel_bias,
           xattn_norm_g, mem_norm_g, w_xq, w_xkv, w_xo, ffn_norm_g, w_ffn_gate, w_ffn_up, w_ffn_down,
           final_norm_g):
    b, s, d = x.shape
    n = b * s
    depth = w_in.shape[0]
    assert d == D_MODEL and s % FOX_TQ == 0 and s % ROW_TILE == 0 and FOX_TQ == 2 * ROW_TILE
    assert s % (SWA_BLOCKS_PER_STEP * WINDOW) == 0 and SWA_BLOCKS_PER_STEP % 2 == 0
    mem_len = mem.shape[1]
    xf = x.reshape(n, d)
    memf = mem.reshape(b * mem_len, d)
    swa_bias = _swa_bias_table(rel_bias)
    w_in_bf16 = w_in.astype(BF16)
    w_branch_bf16 = w_branch.astype(BF16)
    for l in range(depth):
        head_perm = jnp.argsort(forget_bias[l])
        wn, wc, wk, wt, wfg = _pack_in_proj(w_in, w_in_bf16, l, head_perm)
        w_br = w_branch_bf16[l].at[1].set(_permute_heads(w_branch_bf16[l, 1], head_perm, 0))
        pn, pt, k_aug, st = _in_proj(xf, mix_norm_g[l], wn, wc, wk, wt, wfg, conv_w[l],
                                     jnp.take(forget_bias[l], head_perm), seq=s, tm=ROW_TILE)
        st = st.reshape(b, s // ROW_TILE, SUBLANES, LANES)
        k_sq = jnp.broadcast_to(jnp.max(st[:, :, 1:2, 0:FOX_HEADS], axis=1, keepdims=True),
                                (b, s // ROW_TILE, 1, FOX_HEADS))
        stats = jnp.concatenate([st[:, :, 0:1, 0:FOX_HEADS], k_sq, st[:, :, None, 0:FOX_HEADS, FOX_HEADS]], axis=2)
        y_fox = _fox_attention(stats, pt, k_aug, b, s, tq=FOX_TQ, hp=FOX_HEADS_PER_STEP)
        y_swa = _swa_attention(sink[l], pt, pn, swa_bias, b, s, nblk=SWA_BLOCKS_PER_STEP)
        kv = _norm_proj(memf, mem_norm_g[l], w_xkv[l].astype(BF16))
        kT = kv[:, :d].reshape(b, mem_len, d).transpose(0, 2, 1)
        v = kv[:, d:].reshape(b, mem_len, d)
        wq = (w_xq[l] * (X_HEAD_DIM ** -0.5 * LOG2E)).astype(BF16)
        xf = _layer_tail(xf, pn, y_fox, y_swa, w_br, w_mix_out[l].astype(BF16),
                         xattn_norm_g[l], wq, kT, v, w_xo[l].astype(BF16),
                         ffn_norm_g[l], w_ffn_gate[l].astype(BF16), w_ffn_up[l].astype(BF16),
                         w_ffn_down[l].astype(BF16), final_norm_g, final_norm=(l == depth - 1),
                         seq=s, tm=ROW_TILE)
    return xf.reshape(b, s, d)
```

```python
import functools
import math

import jax
import jax.numpy as jnp
import numpy as np
from jax import lax
from jax.experimental import pallas as pl
from jax.experimental.pallas import tpu as pltpu

F32 = jnp.float32
BF16 = jnp.bfloat16

D_MODEL = 1024
HEAD_DIM = 64
BRANCH = 512
FOX_HEADS = 8
SWA_HEADS = 8
SWA_KV = 2
SWA_GROUP = SWA_HEADS // SWA_KV
WINDOW = 128
N_BUCKETS = 32
X_HEADS = 4
X_HEAD_DIM = D_MODEL // X_HEADS
RMS_EPS = 1e-6
NEG = -1e30
LOG2E = math.log2(math.e)

LANES = 128
SUBLANES = 8
VMEM_LIMIT_BYTES = 56 * 1024 * 1024
FOX_VMEM_LIMIT_BYTES = 58 * 1024 * 1024
TAIL_VMEM_LIMIT_BYTES = 58 * 1024 * 1024
PITCH_PAD = LANES
MERGE_CHUNK = 256
FFN_CHUNK = 256

ROW_TILE = 512
FOX_TQ = 1024
FOX_HEADS_PER_STEP = 4
SWA_BLOCKS_PER_STEP = 16

GATE_OFF = 0
CONV_OFF = 3 * D_MODEL
SWAK_OFF = CONV_OFF + BRANCH
PN_COLS = SWAK_OFF + SWA_KV * HEAD_DIM
FOXQ_ROW = 0
FOXV_ROW = BRANCH
SWAQ_ROW = 2 * BRANCH
SWAV_ROW = 3 * BRANCH
PT_ROWS = SWAV_ROW + SWA_KV * HEAD_DIM

AUG_STRIDE = 8
AUG_EVEN = (HEAD_DIM, HEAD_DIM + AUG_STRIDE, HEAD_DIM + 2 * AUG_STRIDE)
AUG_ODD = (0, AUG_STRIDE, 2 * AUG_STRIDE)
V_ROWS = 80
SKIP_BITS = 160.0
SKIP_SLACK = 1.05


def _params(*sem, vmem_limit=VMEM_LIMIT_BYTES):
    return pltpu.CompilerParams(dimension_semantics=sem, vmem_limit_bytes=vmem_limit)


def _rms(x, g):
    return x * lax.rsqrt(jnp.mean(x * x, axis=-1, keepdims=True) + RMS_EPS) * g


def _const_spec(shape):
    nd = len(shape)
    return pl.BlockSpec(shape, lambda *_: (0,) * nd, pipeline_mode=pl.Buffered(1))


def _layer_spec(stacked_shape, layer):
    return pl.BlockSpec((None,) + tuple(stacked_shape[1:]), lambda *_: (layer, 0, 0), pipeline_mode=pl.Buffered(1))


def _one_hot_rows(shape, axis, positions):
    idx = lax.broadcasted_iota(jnp.int32, shape, axis)
    hit = idx == positions[0]
    for p in positions[1:]:
        hit = hit | (idx == p)
    return jnp.where(hit, 1.0, 0.0)


def _in_proj_kernel(x_ref, g_ref, wn_ref, wc_ref, wk_ref, wt_ref, wfg_ref, cw_ref, fb_ref,
                    pn_ref, pt_ref, ka_ref, st_ref, z_tail, c_tail, *, tiles_per_seq):
    tm = x_ref.shape[0]
    first_of_seq = pl.program_id(0) % tiles_per_seq == 0

    @pl.when(pl.program_id(0) == 0)
    def _():
        z_tail[...] = jnp.zeros(z_tail.shape, F32)
        c_tail[...] = jnp.zeros(c_tail.shape, F32)

    h = _rms(x_ref[...], g_ref[...]).astype(BF16)

    fg = jnp.dot(h, wfg_ref[...], preferred_element_type=F32)
    k_fox = jnp.dot(h, wk_ref[...], preferred_element_type=F32)
    lane = lax.broadcasted_iota(jnp.int32, (tm, LANES), 1)
    trow = lax.broadcasted_iota(jnp.int32, (tm, LANES), 0)
    c = jnp.where(lane < FOX_HEADS, jax.nn.log_sigmoid(fg + fb_ref[...]) * LOG2E, 0.0)
    d = 1
    while d < tm:
        c = c + jnp.where(trow >= d, pltpu.roll(c, d, axis=0), 0.0)
        d *= 2
    c = c + jnp.where(first_of_seq, 0.0, c_tail[0:1, :])
    c_tail[0:1, :] = c[tm - 1:tm, :]
    head_of_col = lax.broadcasted_iota(jnp.int32, (BRANCH, LANES), 0) // HEAD_DIM
    head_sel = jnp.where(head_of_col == lax.broadcasted_iota(jnp.int32, (BRANCH, LANES), 1), 1.0, 0.0).astype(BF16)
    k_sq = jnp.dot((k_fox * k_fox).astype(BF16), head_sel, preferred_element_type=F32)
    st_ref[...] = jnp.zeros(st_ref.shape, F32)
    st_ref[0:1, :] = c[tm - 1:tm, :]
    st_ref[1:2, :] = jnp.max(k_sq, axis=0, keepdims=True)
    c1 = c.astype(BF16).astype(F32)
    r1 = c - c1
    c2 = r1.astype(BF16).astype(F32)
    c3 = (r1 - c2).astype(BF16).astype(F32)
    packed = -jnp.where(lane < AUG_STRIDE, c1,
                        jnp.where(lane < 2 * AUG_STRIDE, pltpu.roll(c2, AUG_STRIDE, axis=1),
                                  pltpu.roll(c3, 2 * AUG_STRIDE, axis=1)))
    keep_even = _one_hot_rows((1, LANES), 1, AUG_EVEN)
    keep_odd = _one_hot_rows((1, LANES), 1, AUG_ODD)
    for hd in range(FOX_HEADS):
        k_pair = k_fox[:, (hd // 2) * LANES:(hd // 2 + 1) * LANES]
        if hd % 2 == 0:
            aug = pltpu.roll(packed, (AUG_EVEN[0] - hd) % LANES, axis=1) * keep_even
            out = jnp.where(lane < HEAD_DIM, k_pair, aug)
        else:
            aug = pltpu.roll(packed, (AUG_ODD[0] - hd) % LANES, axis=1) * keep_odd
            out = jnp.where(lane >= HEAD_DIM, k_pair, aug)
        ka_ref[0, hd] = out.astype(BF16)

    conv_b, conv_c, conv_u = (jnp.dot(h, wc_ref[:, t * BRANCH:(t + 1) * BRANCH], preferred_element_type=F32)
                              for t in range(3))
    z = conv_c * conv_u
    zh = jnp.where(first_of_seq, 0.0, z_tail[...])
    z_tail[...] = z[tm - SUBLANES:tm, :]
    row = lax.broadcasted_iota(jnp.int32, z.shape, 0)
    z1 = jnp.where(row == 0, zh[7:8], pltpu.roll(z, 1, axis=0))
    z2 = jnp.where(row == 0, zh[6:7], jnp.where(row == 1, zh[7:8], pltpu.roll(z, 2, axis=0)))
    cw = cw_ref[...]
    pn_ref[:, CONV_OFF:SWAK_OFF] = (conv_b * (cw[0:1] * z2 + cw[1:2] * z1 + cw[2:3] * z)).astype(BF16)

    for lo in range(0, CONV_OFF, D_MODEL):
        pn_ref[:, lo:lo + D_MODEL] = jnp.dot(h, wn_ref[:, lo:lo + D_MODEL], preferred_element_type=F32).astype(BF16)
    pn_ref[:, SWAK_OFF:PN_COLS] = jnp.dot(h, wn_ref[:, CONV_OFF:], preferred_element_type=F32).astype(BF16)
    pt = lax.dot_general(wt_ref[...], h, (((1,), (1,)), ((), ())), preferred_element_type=F32)
    pt_ref[...] = pt.astype(BF16)
    for hd in range(FOX_HEADS):
        q_h = pt[FOXQ_ROW + hd * HEAD_DIM:FOXQ_ROW + (hd + 1) * HEAD_DIM, :]
        q_sq = jnp.sum(q_h * q_h, axis=0, keepdims=True)
        st_ref[hd:hd + 1, FOX_HEADS:FOX_HEADS + 1] = jnp.max(q_sq, axis=1, keepdims=True)


def _in_proj(x, g, wn, wc, wk, wt, wfg, conv_w, forget_bias, seq, tm):
    n, d = x.shape
    tiles = seq // tm
    fb = jnp.pad(forget_bias, (0, LANES - FOX_HEADS)).reshape(1, LANES)
    return pl.pallas_call(
        functools.partial(_in_proj_kernel, tiles_per_seq=tiles),
        grid=(n // tm,),
        in_specs=[
            pl.BlockSpec((tm, d), lambda i: (i, 0)),
            _const_spec((1, d)), _const_spec(wn.shape), _const_spec(wc.shape), _const_spec(wk.shape),
            _const_spec(wt.shape), _const_spec(wfg.shape), _const_spec(conv_w.shape), _const_spec((1, LANES)),
        ],
        out_specs=[
            pl.BlockSpec((tm, PN_COLS), lambda i: (i, 0)),
            pl.BlockSpec((PT_ROWS, tm), lambda i: (0, i)),
            pl.BlockSpec((1, FOX_HEADS, tm, LANES), lambda i: (i // tiles, 0, i % tiles, 0)),
            pl.BlockSpec((SUBLANES, LANES), lambda i: (i, 0)),
        ],
        out_shape=[jax.ShapeDtypeStruct((n, PN_COLS), BF16), jax.ShapeDtypeStruct((PT_ROWS, n), BF16),
                   jax.ShapeDtypeStruct((n // seq, FOX_HEADS, seq, LANES), BF16),
                   jax.ShapeDtypeStruct((n // tm * SUBLANES, LANES), F32)],
        scratch_shapes=[pltpu.VMEM((SUBLANES, BRANCH), F32), pltpu.VMEM((SUBLANES, LANES), F32)],
        compiler_params=_params("arbitrary"),
        name="in_proj",
    )(x, g.reshape(1, d), wn, wc, wk, wt, wfg, conv_w, fb)


def _norm_proj_kernel(x_ref, g_ref, w_ref, o_ref):
    h = _rms(x_ref[...], g_ref[...]).astype(BF16)
    o_ref[...] = jnp.dot(h, w_ref[...], preferred_element_type=F32).astype(BF16)


def _norm_proj(x, g, w, layer):
    n, d = x.shape
    cols = w.shape[2]
    return pl.pallas_call(
        _norm_proj_kernel,
        grid=(1,),
        in_specs=[_const_spec((n, d)), _const_spec((1, d)), _layer_spec(w.shape, layer)],
        out_specs=pl.BlockSpec((n, cols), lambda i: (0, 0)),
        out_shape=jax.ShapeDtypeStruct((n, cols), BF16),
        compiler_params=_params("arbitrary"),
        name="mem_proj",
    )(x, g.reshape(1, d), w)


def _fox_kernel(st_ref, q_ref, qn_ref, k_ref, v_ref, o_ref, q_aug, v_aug, s_buf, mt_buf, p_buf, acc_ref, first_ref,
                *, tq, tk, hp):
    bi = pl.program_id(0)
    grp = pl.program_id(1)
    i = pl.program_id(2)
    seq = v_ref.shape[1]
    cur = i % 2
    heads = tuple(range(hp))

    def first_live_pair(qi):
        c_q = jnp.maximum(2 * qi - 1, 0)
        first = qi
        for hh in heads:
            h = grp * hp + hh
            q_sq = jnp.maximum(st_ref[bi, 2 * qi, 2, h], st_ref[bi, 2 * qi + 1, 2, h])
            k_sq = st_ref[bi, 0, 1, h]
            bound_sq = (4.0 * SKIP_SLACK * SKIP_SLACK) * q_sq * k_sq

            def count(u, n):
                room = -SKIP_BITS - (st_ref[bi, c_q, 0, h] - st_ref[bi, 2 * u + 1, 0, h])
                dead = jnp.logical_and(room > 0.0, bound_sq < room * room)
                return n + dead.astype(jnp.int32)

            first = jnp.minimum(first, lax.fori_loop(0, qi, count, jnp.int32(0)))
        return first

    def build_queries(src_ref, qslot):
        qrow = lax.broadcasted_iota(jnp.int32, (LANES, tq), 0)
        for pr in range(hp // 2):
            qb = src_ref[pr * LANES:(pr + 1) * LANES, :].astype(F32)
            q_aug[qslot, 2 * pr] = jnp.where(qrow < HEAD_DIM, qb, _one_hot_rows((LANES, tq), 0, AUG_EVEN)).astype(BF16)
            q_aug[qslot, 2 * pr + 1] = jnp.where(qrow >= HEAD_DIM, qb,
                                                 _one_hot_rows((LANES, tq), 0, AUG_ODD)).astype(BF16)

    def scores(t, slot, qslot):
        off = pl.multiple_of(t * tk, tk)
        for hh in heads:
            s = jnp.dot(k_ref[0, hh, pl.ds(off, tk), :], q_aug[qslot, hh], preferred_element_type=F32)
            s_buf[hh, slot, :, 0:tq] = s
            mt_buf[hh, slot, 0:1, :] = jnp.max(s, axis=0, keepdims=True)

    def softmax(slot, ms, mask=None):
        out = []
        for hh in heads:
            s = s_buf[hh, slot, :, 0:tq]
            if mask is None:
                tile_max = mt_buf[hh, slot, 0:1, :]
            else:
                s = jnp.where(mask, s, NEG)
                tile_max = jnp.max(s, axis=0, keepdims=True)
            m_new = jnp.maximum(ms[hh], tile_max)
            p_buf[hh, slot, :, 0:tq] = jnp.exp2(s - m_new).astype(BF16)
            out.append((m_new, jnp.exp2(ms[hh] - m_new)))
        return tuple(o[0] for o in out), tuple(o[1] for o in out)

    def values(t, slot, alphas):
        off = pl.multiple_of(jnp.maximum(t, 0) * tk, tk)
        for hh in heads:
            pv = jnp.dot(v_aug[hh, :, pl.ds(off, tk)], p_buf[hh, slot, :, 0:tq], preferred_element_type=F32)
            acc_ref[hh] = alphas[hh] * acc_ref[hh] + pv

    @pl.when(i == 0)
    def _():
        ones_row = _one_hot_rows((V_ROWS - HEAD_DIM, seq), 0, (0,)).astype(BF16)
        for hh in heads:
            v_aug[hh, 0:HEAD_DIM, :] = v_ref[hh * HEAD_DIM:(hh + 1) * HEAD_DIM, :]
            v_aug[hh, HEAD_DIM:V_ROWS, :] = ones_row
            p_buf[hh, 1, :, 0:tq] = jnp.zeros((tk, tq), BF16)
            acc_ref[hh] = jnp.zeros(acc_ref.shape[1:], F32)
        build_queries(q_ref, 0)
        scores(0, 0, 0)

    def pair(u, carry):
        ms, alpha_prev = carry
        t = 2 * u
        scores(t + 1, 1, cur)
        ms, alpha0 = softmax(0, ms)
        values(t - 1, 1, alpha_prev)
        scores(t + 2, 0, cur)
        ms, alpha1 = softmax(1, ms)
        values(t, 0, alpha0)
        return ms, alpha1

    carry = (tuple(jnp.full((1, tq), NEG, F32) for _ in heads), tuple(jnp.ones((1, tq), F32) for _ in heads))
    first_now = jnp.where(i == 0, 0, first_ref[0])
    first_next = first_live_pair(jnp.minimum(i + 1, seq // tq - 1))
    ms, alpha_prev = lax.fori_loop(first_now, i, pair, carry)

    t = 2 * i
    key = lax.broadcasted_iota(jnp.int32, (tk, tq), 0)
    qry = lax.broadcasted_iota(jnp.int32, (tk, tq), 1)
    off_b = pl.multiple_of((t + 1) * tk, tk)
    tri = lax.broadcasted_iota(jnp.int32, (tk, tk), 0) <= lax.broadcasted_iota(jnp.int32, (tk, tk), 1)
    s_b = [jnp.where(tri, jnp.dot(k_ref[0, hh, pl.ds(off_b, tk), :], q_aug[cur, hh, :, tk:],
                                  preferred_element_type=F32), NEG) for hh in heads]
    ms, alpha0 = softmax(0, ms, key <= qry)
    build_queries(qn_ref, 1 - cur)
    first_ref[0] = first_next
    scores(2 * first_next, 0, 1 - cur)
    values(t - 1, 1, alpha_prev)
    values(t, 0, alpha0)
    outs = []
    for hh in heads:
        mt_buf[hh, 1, 0:1, :] = ms[hh]
        m_old = mt_buf[hh, 1, 0:1, tk:]
        m_new = jnp.maximum(m_old, jnp.max(s_b[hh], axis=0, keepdims=True))
        p_b = jnp.exp2(s_b[hh] - m_new).astype(BF16)
        pv = jnp.dot(v_aug[hh, :, pl.ds(off_b, tk)], p_b, preferred_element_type=F32)
        acc_ref[hh, :, tk:] = jnp.exp2(m_old - m_new) * acc_ref[hh, :, tk:] + pv
        acc = acc_ref[hh]
        outs.append(acc[:HEAD_DIM] / acc[HEAD_DIM:HEAD_DIM + 1])
    o_ref[...] = jnp.concatenate(outs, axis=0).T.astype(BF16)


def _fox_attention(stats, pt, k_aug, b, s, tq, hp):
    tk = tq // 2
    per = s // tq
    rows = hp * HEAD_DIM
    qblk = FOXQ_ROW // rows
    vblk = FOXV_ROW // rows
    return pl.pallas_call(
        functools.partial(_fox_kernel, tq=tq, tk=tk, hp=hp),
        grid=(b, FOX_HEADS // hp, per),
        in_specs=[
            pl.BlockSpec(memory_space=pltpu.SMEM),
            pl.BlockSpec((rows, tq), lambda bi, g, i: (qblk + g, bi * per + i)),
            pl.BlockSpec((rows, tq), lambda bi, g, i: (qblk + g, bi * per + jnp.minimum(i + 1, per - 1))),
            pl.BlockSpec((1, hp, s, LANES), lambda bi, g, i: (bi, g, 0, 0), pipeline_mode=pl.Buffered(1)),
            pl.BlockSpec((rows, s), lambda bi, g, i: (vblk + g, bi), pipeline_mode=pl.Buffered(1)),
        ],
        out_specs=pl.BlockSpec((tq, rows), lambda bi, g, i: (bi * per + i, g)),
        out_shape=jax.ShapeDtypeStruct((b * s, BRANCH), BF16),
        scratch_shapes=[
            pltpu.VMEM((2, hp, LANES, tq), BF16),
            pltpu.VMEM((hp, V_ROWS, s), BF16),
            pltpu.VMEM((hp, 2, tk, tq + PITCH_PAD), F32),
            pltpu.VMEM((hp, 2, SUBLANES, tq), F32),
            pltpu.VMEM((hp, 2, tk, tq + PITCH_PAD), BF16),
            pltpu.VMEM((hp, V_ROWS, tq), F32),
            pltpu.SMEM((1,), jnp.int32),
        ],
        compiler_params=_params("arbitrary", "arbitrary", "arbitrary", vmem_limit=FOX_VMEM_LIMIT_BYTES),
        name="fox_attention",
    )(stats, pt, pt, k_aug, pt)


def _swa_bias_kernel(rb_ref, bucket_ref, win_ref, o_ref):
    h = pl.program_id(0)
    bucket = bucket_ref[...]
    bias = jnp.zeros(bucket.shape, F32)
    for b in range(N_BUCKETS):
        bias = jnp.where(bucket == b, rb_ref[b, h] * LOG2E, bias)
    o_ref[0, 0] = jnp.where(win_ref[0] > 0, bias, NEG)
    o_ref[1, 0] = jnp.where(win_ref[1] > 0, bias, NEG)


def _swa_bias_table(rel_bias):
    tq = np.arange(WINDOW)[None, :]
    sk = np.arange(2 * WINDOW)[:, None]
    dist = WINDOW + tq - sk
    n = jnp.maximum(jnp.asarray(dist, jnp.int32), 0)
    max_exact = N_BUCKETS // 2
    large = max_exact + (jnp.log(jnp.maximum(n, 1).astype(F32) / max_exact)
                         / math.log(WINDOW / max_exact) * (N_BUCKETS - max_exact)).astype(jnp.int32)
    bucket = jnp.where(n < max_exact, n, jnp.minimum(large, N_BUCKETS - 1))
    in_window = (dist >= 0) & (dist < WINDOW)
    win = np.stack([in_window, in_window & (sk >= WINDOW)]).astype(np.int32)
    full = lambda shape: pl.BlockSpec(shape, lambda h: (0,) * len(shape))
    return pl.pallas_call(
        _swa_bias_kernel,
        grid=(SWA_HEADS,),
        in_specs=[pl.BlockSpec(memory_space=pltpu.SMEM), full(bucket.shape), full(win.shape)],
        out_specs=pl.BlockSpec((2, 1, 2 * WINDOW, WINDOW), lambda h: (0, h // SWA_GROUP, 0, h % SWA_GROUP)),
        out_shape=jax.ShapeDtypeStruct((2, SWA_KV, 2 * WINDOW, SWA_GROUP * WINDOW), F32),
        compiler_params=_params("arbitrary"),
        name="swa_bias_table",
    )(rel_bias, bucket, jnp.asarray(win))


def _swa_kernel(sink_ref, q_ref, kp_ref, kc_ref, vp_ref, vc_ref, bias_ref, o_ref,
                k_buf, v_buf, s_buf, mt_buf, p_buf, dn_buf, *, nblk):
    gw = SWA_GROUP * WINDOW
    ts = nblk * WINDOW
    first = pl.program_id(1) == 0
    k_buf[0:WINDOW, :] = kp_ref[...]
    k_buf[WINDOW:WINDOW + ts, :] = kc_ref[...]
    ones_row = _one_hot_rows((V_ROWS - HEAD_DIM, ts + WINDOW), 0, (0,)).astype(BF16)
    for kv in range(SWA_KV):
        rows = slice(kv * HEAD_DIM, (kv + 1) * HEAD_DIM)
        v_buf[kv, 0:HEAD_DIM, 0:WINDOW] = vp_ref[rows, :]
        v_buf[kv, 0:HEAD_DIM, WINDOW:WINDOW + ts] = vc_ref[rows, :]
        v_buf[kv, HEAD_DIM:V_ROWS, :] = ones_row
    glane = lax.broadcasted_iota(jnp.int32, (1, gw), 1) // WINDOW
    zeros = jnp.zeros((HEAD_DIM, gw), BF16)
    sinks = []
    for kv in range(SWA_KV):
        sink = jnp.zeros((1, gw), F32)
        for g in range(SWA_GROUP):
            sink = jnp.where(glane == g, sink_ref[kv * SWA_GROUP + g] * LOG2E, sink)
        sinks.append(sink)

    def scores(blk, slot):
        lo = pl.multiple_of(blk * WINDOW, WINDOW)
        sel = jnp.logical_and(first, blk == 0).astype(jnp.int32)
        keys = k_buf[pl.ds(lo, 2 * WINDOW), :]
        for kv in range(SWA_KV):
            qg = jnp.concatenate([q_ref[(kv * SWA_GROUP + g) * HEAD_DIM:(kv * SWA_GROUP + g + 1) * HEAD_DIM,
                                        pl.ds(lo, WINDOW)] for g in range(SWA_GROUP)], axis=1)
            q_pad = jnp.concatenate([qg, zeros] if kv == 0 else [zeros, qg], axis=0)
            s = jnp.dot(keys, q_pad, preferred_element_type=F32) + bias_ref[sel, kv]
            s_buf[kv, slot, :, 0:gw] = s
            mt_buf[kv, slot, 0:1, :] = jnp.max(s, axis=0, keepdims=True)

    def softmax(slot):
        for kv in range(SWA_KV):
            m = jnp.maximum(mt_buf[kv, slot, 0:1, :], sinks[kv])
            p = jnp.exp2(s_buf[kv, slot, :, 0:gw] - m)
            dn_buf[kv, slot, 0:1, :] = jnp.exp2(sinks[kv] - m)
            p_buf[kv, slot, :, 0:gw] = p.astype(BF16)

    def values(blk, slot):
        lo = pl.multiple_of(blk * WINDOW, WINDOW)
        for kv in range(SWA_KV):
            pv = jnp.dot(v_buf[kv, :, pl.ds(lo, 2 * WINDOW)], p_buf[kv, slot, :, 0:gw],
                         preferred_element_type=F32)
            o = pv[:HEAD_DIM] / (pv[HEAD_DIM:HEAD_DIM + 1] + dn_buf[kv, slot, 0:1, :])
            for pr in range(SWA_GROUP // 2):
                two = jnp.concatenate([o[:, (2 * pr) * WINDOW:(2 * pr + 1) * WINDOW],
                                       o[:, (2 * pr + 1) * WINDOW:(2 * pr + 2) * WINDOW]], axis=0)
                col = (kv * (SWA_GROUP // 2) + pr) * LANES
                o_ref[pl.ds(lo, WINDOW), col:col + LANES] = two.T.astype(BF16)

    scores(jnp.int32(0), 0)
    scores(jnp.int32(1), 1)
    softmax(0)

    def pair(u, carry):
        blk = 2 * u + 1
        scores(blk + 1, 0)
        softmax(1)
        values(blk - 1, 0)
        scores(blk + 2, 1)
        softmax(0)
        values(blk, 1)
        return carry

    lax.fori_loop(0, nblk // 2 - 1, pair, 0)
    softmax(1)
    values(jnp.int32(nblk - 2), 0)
    values(jnp.int32(nblk - 1), 1)


def _swa_attention(sink, pt, pn, bias, b, s, nblk):
    ts = nblk * WINDOW
    per = s // ts
    perw = s // WINDOW
    qblk = SWAQ_ROW // BRANCH
    vblk = SWAV_ROW // LANES
    kblk = SWAK_OFF // LANES
    prev = lambda bi, i: bi * perw + jnp.maximum(i * nblk - 1, 0)
    return pl.pallas_call(
        functools.partial(_swa_kernel, nblk=nblk),
        grid=(b, per),
        in_specs=[
            pl.BlockSpec(memory_space=pltpu.SMEM),
            pl.BlockSpec((BRANCH, ts), lambda bi, i: (qblk, bi * per + i)),
            pl.BlockSpec((WINDOW, LANES), lambda bi, i: (prev(bi, i), kblk)),
            pl.BlockSpec((ts, LANES), lambda bi, i: (bi * per + i, kblk)),
            pl.BlockSpec((LANES, WINDOW), lambda bi, i: (vblk, prev(bi, i))),
            pl.BlockSpec((LANES, ts), lambda bi, i: (vblk, bi * per + i)),
            _const_spec(bias.shape),
        ],
        out_specs=pl.BlockSpec((ts, BRANCH), lambda bi, i: (bi * per + i, 0)),
        out_shape=jax.ShapeDtypeStruct((b * s, BRANCH), BF16),
        scratch_shapes=[
            pltpu.VMEM((ts + WINDOW, LANES), BF16),
            pltpu.VMEM((SWA_KV, V_ROWS, ts + WINDOW), BF16),
            pltpu.VMEM((SWA_KV, 2, 2 * WINDOW, SWA_GROUP * WINDOW + PITCH_PAD), F32),
            pltpu.VMEM((SWA_KV, 2, SUBLANES, SWA_GROUP * WINDOW), F32),
            pltpu.VMEM((SWA_KV, 2, 2 * WINDOW, SWA_GROUP * WINDOW + PITCH_PAD), BF16),
            pltpu.VMEM((SWA_KV, 2, SUBLANES, SWA_GROUP * WINDOW), F32),
        ],
        compiler_params=_params("arbitrary", "arbitrary"),
        name="swa_attention",
    )(sink, pt, pn, pn, pt, pt, bias)


def _merge_tile(x, gate_ref, branches, wb_ref, wo_ref, m_scr):
    for lo in range(0, D_MODEL, MERGE_CHUNK):
        gates = [jax.nn.sigmoid(gate_ref[:, b * D_MODEL + lo:b * D_MODEL + lo + MERGE_CHUNK].astype(F32))
                 for b in range(len(branches))]
        merged = None
        for b, y in enumerate(branches):
            term = gates[b] * jnp.dot(y, wb_ref[b, :, lo:lo + MERGE_CHUNK], preferred_element_type=F32)
            merged = term if merged is None else merged + term
        m_scr[:, lo:lo + MERGE_CHUNK] = merged.astype(BF16)
    return x + jnp.dot(m_scr[...], wo_ref[...], preferred_element_type=F32)


def _xattn_tile(x, g_ref, wq_ref, kT_ref, v_ref, wo_ref):
    q = jnp.dot(_rms(x, g_ref[...]).astype(BF16), wq_ref[...], preferred_element_type=F32).astype(BF16)
    sls = [slice(h * X_HEAD_DIM, (h + 1) * X_HEAD_DIM) for h in range(X_HEADS)]
    scores = [jnp.dot(q[:, sl], kT_ref[0, sl, :], preferred_element_type=F32) for sl in sls]
    probs = [jnp.exp2(s - jnp.max(s, axis=-1, keepdims=True)) for s in scores]
    denoms = [jnp.sum(p, axis=-1, keepdims=True) for p in probs]
    outs = [jnp.dot(p.astype(BF16), v_ref[0, :, sl], preferred_element_type=F32) for p, sl in zip(probs, sls)]
    heads = [(o / d).astype(BF16) for o, d in zip(outs, denoms)]
    return x + jnp.dot(jnp.concatenate(heads, axis=1), wo_ref[...], preferred_element_type=F32)


def _swiglu_tile(x, g_ref, wg_ref, wu_ref, wd_ref):
    xn = _rms(x, g_ref[...]).astype(BF16)
    y = x
    for lo in range(0, wg_ref.shape[1], FFN_CHUNK):
        gate = jnp.dot(xn, wg_ref[:, lo:lo + FFN_CHUNK], preferred_element_type=F32)
        up = jnp.dot(xn, wu_ref[:, lo:lo + FFN_CHUNK], preferred_element_type=F32)
        hidden = (gate * jax.nn.sigmoid(gate) * up).astype(BF16)
        y = y + jnp.dot(hidden, wd_ref[lo:lo + FFN_CHUNK, :], preferred_element_type=F32)
    return y


def _layer_tail_kernel(x_ref, gate_ref, yc_ref, yf_ref, ys_ref, wb_ref, wo_ref, gx_ref, wq_ref, kT_ref, v_ref,
                       wxo_ref, gf_ref, wg_ref, wu_ref, wd_ref, gl_ref, o_ref, m_scr, *, final_norm):
    x = _merge_tile(x_ref[...], gate_ref, (yc_ref[...], yf_ref[...], ys_ref[...]), wb_ref, wo_ref, m_scr)
    x = _xattn_tile(x, gx_ref, wq_ref, kT_ref, v_ref, wxo_ref)
    x = _swiglu_tile(x, gf_ref, wg_ref, wu_ref, wd_ref)
    o_ref[...] = _rms(x, gl_ref[...]) if final_norm else x


def _layer_tail(x, pn, y_fox, y_swa, w_branch, w_out, gx, wq, kT, v, wxo, gf, wg, wu, wd, g_last, final_norm,
                layer, seq, tm):
    n, d = x.shape
    tiles = seq // tm
    vec = _const_spec((1, d))
    return pl.pallas_call(
        functools.partial(_layer_tail_kernel, final_norm=final_norm),
        grid=(n // tm,),
        in_specs=[
            pl.BlockSpec((tm, d), lambda i: (i, 0)),
            pl.BlockSpec((tm, 3 * D_MODEL), lambda i: (i, GATE_OFF // (3 * D_MODEL))),
            pl.BlockSpec((tm, BRANCH), lambda i: (i, CONV_OFF // BRANCH)),
            pl.BlockSpec((tm, BRANCH), lambda i: (i, 0)),
            pl.BlockSpec((tm, BRANCH), lambda i: (i, 0)),
            _const_spec(w_branch.shape), _layer_spec(w_out.shape, layer),
            vec, _layer_spec(wq.shape, layer),
            pl.BlockSpec((1,) + kT.shape[1:], lambda i: (i // tiles, 0, 0)),
            pl.BlockSpec((1,) + v.shape[1:], lambda i: (i // tiles, 0, 0)),
            _layer_spec(wxo.shape, layer),
            vec, _layer_spec(wg.shape, layer), _layer_spec(wu.shape, layer), _layer_spec(wd.shape, layer), vec,
        ],
        out_specs=pl.BlockSpec((tm, d), lambda i: (i, 0)),
        out_shape=jax.ShapeDtypeStruct((n, d), F32),
        scratch_shapes=[pltpu.VMEM((tm, d), BF16)],
        compiler_params=_params("arbitrary", vmem_limit=TAIL_VMEM_LIMIT_BYTES),
        name="layer_tail",
    )(x, pn, pn, y_fox, y_swa, w_branch, w_out, gx.reshape(1, d), wq, kT, v, wxo,
      gf.reshape(1, d), wg, wu, wd, g_last.reshape(1, d))


def _permute_heads(w, perm, axis):
    shape = w.shape
    split = shape[:axis] + (FOX_HEADS, HEAD_DIM) + shape[axis + 1:]
    return jnp.take(w.reshape(split), perm, axis=axis).reshape(shape)


def _pack_in_proj(w_in, w_in_bf16, l, head_perm):
    sizes = [3 * BRANCH] + [BRANCH] * 3 + [FOX_HEADS] + [BRANCH, SWA_KV * HEAD_DIM, SWA_KV * HEAD_DIM] + [3 * D_MODEL]
    offs = np.concatenate([[0], np.cumsum(sizes)])
    f_q, f_v, s_q, s_v = (w_in[l, :, offs[t]:offs[t + 1]] for t in (1, 3, 5, 7))
    wc, f_k, f_g, s_k, gates = (w_in_bf16[l, :, offs[t]:offs[t + 1]] for t in (0, 2, 4, 6, 8))
    qscale = HEAD_DIM ** -0.5 * LOG2E
    f_q, s_q = (f_q * qscale).astype(BF16), (s_q * qscale).astype(BF16)
    f_v, s_v = f_v.astype(BF16), s_v.astype(BF16)
    f_q, f_k, f_v = (_permute_heads(w, head_perm, 1) for w in (f_q, f_k, f_v))
    f_g = jnp.take(f_g, head_perm, axis=1)
    wn = jnp.concatenate([gates, s_k], axis=1)
    wt = jnp.concatenate([f_q, f_v, s_q, s_v], axis=1).T
    wfg = jnp.pad(f_g, ((0, 0), (0, LANES - FOX_HEADS)))
    return wn, wc, f_k, wt, wfg


def kernel(x, mem, mix_norm_g, w_in, forget_bias, conv_w, sink, w_branch, w_mix_out, rel_bias,
           xattn_norm_g, mem_norm_g, w_xq, w_xkv, w_xo, ffn_norm_g, w_ffn_gate, w_ffn_up, w_ffn_down,
           final_norm_g):
    b, s, d = x.shape
    n = b * s
    depth = w_in.shape[0]
    assert d == D_MODEL and s % FOX_TQ == 0 and s % ROW_TILE == 0 and FOX_TQ == 2 * ROW_TILE
    assert s % (SWA_BLOCKS_PER_STEP * WINDOW) == 0 and SWA_BLOCKS_PER_STEP % 2 == 0
    mem_len = mem.shape[1]
    xf = x.reshape(n, d)
    memf = mem.reshape(b * mem_len, d)
    swa_bias = _swa_bias_table(rel_bias)
    w_in_bf16 = w_in.astype(BF16)
    w_branch_bf16 = w_branch.astype(BF16)
    w_out, wxkv, wxo = w_mix_out.astype(BF16), w_xkv.astype(BF16), w_xo.astype(BF16)
    wq = (w_xq * (X_HEAD_DIM ** -0.5 * LOG2E)).astype(BF16)
    wg, wu, wd = w_ffn_gate.astype(BF16), w_ffn_up.astype(BF16), w_ffn_down.astype(BF16)
    for l in range(depth):
        head_perm = jnp.argsort(forget_bias[l])
        wn, wc, wk, wt, wfg = _pack_in_proj(w_in, w_in_bf16, l, head_perm)
        w_br = w_branch_bf16[l].at[1].set(_permute_heads(w_branch_bf16[l, 1], head_perm, 0))
        pn, pt, k_aug, st = _in_proj(xf, mix_norm_g[l], wn, wc, wk, wt, wfg, conv_w[l],
                                     jnp.take(forget_bias[l], head_perm), seq=s, tm=ROW_TILE)
        st = st.reshape(b, s // ROW_TILE, SUBLANES, LANES)
        k_sq = jnp.broadcast_to(jnp.max(st[:, :, 1:2, 0:FOX_HEADS], axis=1, keepdims=True),
                                (b, s // ROW_TILE, 1, FOX_HEADS))
        stats = jnp.concatenate([st[:, :, 0:1, 0:FOX_HEADS], k_sq, st[:, :, None, 0:FOX_HEADS, FOX_HEADS]], axis=2)
        y_fox = _fox_attention(stats, pt, k_aug, b, s, tq=FOX_TQ, hp=FOX_HEADS_PER_STEP)
        y_swa = _swa_attention(sink[l], pt, pn, swa_bias, b, s, nblk=SWA_BLOCKS_PER_STEP)
        kv = _norm_proj(memf, mem_norm_g[l], wxkv, l)
        kT = kv[:, :d].reshape(b, mem_len, d).transpose(0, 2, 1)
        v = kv[:, d:].reshape(b, mem_len, d)
        xf = _layer_tail(xf, pn, y_fox, y_swa, w_br, w_out, xattn_norm_g[l], wq, kT, v, wxo,
                         ffn_norm_g[l], wg, wu, wd, final_norm_g, final_norm=(l == depth - 1),
                         layer=l, seq=s, tm=ROW_TILE)
    return xf.reshape(b, s, d)
```

```python
import functools
import math

import jax
import jax.numpy as jnp
import numpy as np
from jax import lax
from jax.experimental import pallas as pl
from jax.experimental.pallas import tpu as pltpu

F32 = jnp.float32
BF16 = jnp.bfloat16

D_MODEL = 1024
HEAD_DIM = 64
BRANCH = 512
FOX_HEADS = 8
SWA_HEADS = 8
SWA_KV = 2
SWA_GROUP = SWA_HEADS // SWA_KV
WINDOW = 128
N_BUCKETS = 32
X_HEADS = 4
X_HEAD_DIM = D_MODEL // X_HEADS
RMS_EPS = 1e-6
NEG = -1e30
LOG2E = math.log2(math.e)

LANES = 128
SUBLANES = 8
VMEM_LIMIT_BYTES = 56 * 1024 * 1024
FOX_VMEM_LIMIT_BYTES = 58 * 1024 * 1024
TAIL_VMEM_LIMIT_BYTES = 58 * 1024 * 1024
PITCH_PAD = LANES
MERGE_CHUNK = 256
FFN_CHUNK = 256

ROW_TILE = 512
FOX_TQ = 1024
FOX_HEADS_PER_STEP = 4
SWA_BLOCKS_PER_STEP = 16

GATE_OFF = 0
CONV_OFF = 3 * D_MODEL
SWAK_OFF = CONV_OFF + BRANCH
PN_COLS = SWAK_OFF + SWA_KV * HEAD_DIM
FOXQ_ROW = 0
FOXV_ROW = BRANCH
SWAQ_ROW = 2 * BRANCH
SWAV_ROW = 3 * BRANCH
PT_ROWS = SWAV_ROW + SWA_KV * HEAD_DIM

AUG_STRIDE = 8
AUG_EVEN = (HEAD_DIM, HEAD_DIM + AUG_STRIDE, HEAD_DIM + 2 * AUG_STRIDE)
AUG_ODD = (0, AUG_STRIDE, 2 * AUG_STRIDE)
V_ROWS = 80
SKIP_BITS = 160.0
SKIP_SLACK = 1.05


def _params(*sem, vmem_limit=VMEM_LIMIT_BYTES):
    return pltpu.CompilerParams(dimension_semantics=sem, vmem_limit_bytes=vmem_limit)


def _rms(x, g):
    return x * lax.rsqrt(jnp.mean(x * x, axis=-1, keepdims=True) + RMS_EPS) * g


def _const_spec(shape):
    nd = len(shape)
    return pl.BlockSpec(shape, lambda *_: (0,) * nd, pipeline_mode=pl.Buffered(1))


def _layer_spec(stacked_shape, layer):
    return pl.BlockSpec((None,) + tuple(stacked_shape[1:]), lambda *_: (layer, 0, 0), pipeline_mode=pl.Buffered(1))


def _one_hot_rows(shape, axis, positions):
    idx = lax.broadcasted_iota(jnp.int32, shape, axis)
    hit = idx == positions[0]
    for p in positions[1:]:
        hit = hit | (idx == p)
    return jnp.where(hit, 1.0, 0.0)


def _in_proj_kernel(x_ref, g_ref, wn_ref, wc_ref, wk_ref, wt_ref, wfg_ref, cw_ref, fb_ref,
                    pn_ref, pt_ref, ka_ref, st_ref, z_tail, c_tail, *, tiles_per_seq):
    tm = x_ref.shape[0]
    first_of_seq = pl.program_id(0) % tiles_per_seq == 0

    @pl.when(pl.program_id(0) == 0)
    def _():
        z_tail[...] = jnp.zeros(z_tail.shape, F32)
        c_tail[...] = jnp.zeros(c_tail.shape, F32)

    h = _rms(x_ref[...], g_ref[...]).astype(BF16)

    fg = jnp.dot(h, wfg_ref[...], preferred_element_type=F32)
    k_fox = jnp.dot(h, wk_ref[...], preferred_element_type=F32)
    lane = lax.broadcasted_iota(jnp.int32, (tm, LANES), 1)
    trow = lax.broadcasted_iota(jnp.int32, (tm, LANES), 0)
    c = jnp.where(lane < FOX_HEADS, jax.nn.log_sigmoid(fg + fb_ref[...]) * LOG2E, 0.0)
    d = 1
    while d < tm:
        c = c + jnp.where(trow >= d, pltpu.roll(c, d, axis=0), 0.0)
        d *= 2
    c = c + jnp.where(first_of_seq, 0.0, c_tail[0:1, :])
    c_tail[0:1, :] = c[tm - 1:tm, :]
    head_of_col = lax.broadcasted_iota(jnp.int32, (BRANCH, LANES), 0) // HEAD_DIM
    head_sel = jnp.where(head_of_col == lax.broadcasted_iota(jnp.int32, (BRANCH, LANES), 1), 1.0, 0.0).astype(BF16)
    k_sq = jnp.dot((k_fox * k_fox).astype(BF16), head_sel, preferred_element_type=F32)
    st_ref[...] = jnp.zeros(st_ref.shape, F32)
    st_ref[0:1, :] = c[tm - 1:tm, :]
    st_ref[1:2, :] = jnp.max(k_sq, axis=0, keepdims=True)
    c1 = c.astype(BF16).astype(F32)
    r1 = c - c1
    c2 = r1.astype(BF16).astype(F32)
    c3 = (r1 - c2).astype(BF16).astype(F32)
    packed = -jnp.where(lane < AUG_STRIDE, c1,
                        jnp.where(lane < 2 * AUG_STRIDE, pltpu.roll(c2, AUG_STRIDE, axis=1),
                                  pltpu.roll(c3, 2 * AUG_STRIDE, axis=1)))
    keep_even = _one_hot_rows((1, LANES), 1, AUG_EVEN)
    keep_odd = _one_hot_rows((1, LANES), 1, AUG_ODD)
    for hd in range(FOX_HEADS):
        k_pair = k_fox[:, (hd // 2) * LANES:(hd // 2 + 1) * LANES]
        if hd % 2 == 0:
            aug = pltpu.roll(packed, (AUG_EVEN[0] - hd) % LANES, axis=1) * keep_even
            out = jnp.where(lane < HEAD_DIM, k_pair, aug)
        else:
            aug = pltpu.roll(packed, (AUG_ODD[0] - hd) % LANES, axis=1) * keep_odd
            out = jnp.where(lane >= HEAD_DIM, k_pair, aug)
        ka_ref[0, hd] = out.astype(BF16)

    conv_b, conv_c, conv_u = (jnp.dot(h, wc_ref[:, t * BRANCH:(t + 1) * BRANCH], preferred_element_type=F32)
                              for t in range(3))
    z = conv_c * conv_u
    zh = jnp.where(first_of_seq, 0.0, z_tail[...])
    z_tail[...] = z[tm - SUBLANES:tm, :]
    row = lax.broadcasted_iota(jnp.int32, z.shape, 0)
    z1 = jnp.where(row == 0, zh[7:8], pltpu.roll(z, 1, axis=0))
    z2 = jnp.where(row == 0, zh[6:7], jnp.where(row == 1, zh[7:8], pltpu.roll(z, 2, axis=0)))
    cw = cw_ref[...]
    pn_ref[:, CONV_OFF:SWAK_OFF] = (conv_b * (cw[0:1] * z2 + cw[1:2] * z1 + cw[2:3] * z)).astype(BF16)

    for lo in range(0, CONV_OFF, D_MODEL):
        pn_ref[:, lo:lo + D_MODEL] = jnp.dot(h, wn_ref[:, lo:lo + D_MODEL], preferred_element_type=F32).astype(BF16)
    pn_ref[:, SWAK_OFF:PN_COLS] = jnp.dot(h, wn_ref[:, CONV_OFF:], preferred_element_type=F32).astype(BF16)
    pt = lax.dot_general(wt_ref[...], h, (((1,), (1,)), ((), ())), preferred_element_type=F32)
    pt_ref[...] = pt.astype(BF16)
    for hd in range(FOX_HEADS):
        q_h = pt[FOXQ_ROW + hd * HEAD_DIM:FOXQ_ROW + (hd + 1) * HEAD_DIM, :]
        q_sq = jnp.sum(q_h * q_h, axis=0, keepdims=True)
        st_ref[hd:hd + 1, FOX_HEADS:FOX_HEADS + 1] = jnp.max(q_sq, axis=1, keepdims=True)


def _in_proj(x, g, wn, wc, wk, wt, wfg, conv_w, forget_bias, seq, tm):
    n, d = x.shape
    tiles = seq // tm
    fb = jnp.pad(forget_bias, (0, LANES - FOX_HEADS)).reshape(1, LANES)
    return pl.pallas_call(
        functools.partial(_in_proj_kernel, tiles_per_seq=tiles),
        grid=(n // tm,),
        in_specs=[
            pl.BlockSpec((tm, d), lambda i: (i, 0)),
            _const_spec((1, d)), _const_spec(wn.shape), _const_spec(wc.shape), _const_spec(wk.shape),
            _const_spec(wt.shape), _const_spec(wfg.shape), _const_spec(conv_w.shape), _const_spec((1, LANES)),
        ],
        out_specs=[
            pl.BlockSpec((tm, PN_COLS), lambda i: (i, 0)),
            pl.BlockSpec((PT_ROWS, tm), lambda i: (0, i)),
            pl.BlockSpec((1, FOX_HEADS, tm, LANES), lambda i: (i // tiles, 0, i % tiles, 0)),
            pl.BlockSpec((SUBLANES, LANES), lambda i: (i, 0)),
        ],
        out_shape=[jax.ShapeDtypeStruct((n, PN_COLS), BF16), jax.ShapeDtypeStruct((PT_ROWS, n), BF16),
                   jax.ShapeDtypeStruct((n // seq, FOX_HEADS, seq, LANES), BF16),
                   jax.ShapeDtypeStruct((n // tm * SUBLANES, LANES), F32)],
        scratch_shapes=[pltpu.VMEM((SUBLANES, BRANCH), F32), pltpu.VMEM((SUBLANES, LANES), F32)],
        compiler_params=_params("arbitrary"),
        name="in_proj",
    )(x, g.reshape(1, d), wn, wc, wk, wt, wfg, conv_w, fb)


def _norm_proj_kernel(x_ref, g_ref, w_ref, o_ref):
    h = _rms(x_ref[...], g_ref[...]).astype(BF16)
    o_ref[...] = jnp.dot(h, w_ref[...], preferred_element_type=F32).astype(BF16)


def _norm_proj(x, g, w, layer):
    n, d = x.shape
    cols = w.shape[2]
    return pl.pallas_call(
        _norm_proj_kernel,
        grid=(1,),
        in_specs=[_const_spec((n, d)), _const_spec((1, d)), _layer_spec(w.shape, layer)],
        out_specs=pl.BlockSpec((n, cols), lambda i: (0, 0)),
        out_shape=jax.ShapeDtypeStruct((n, cols), BF16),
        compiler_params=_params("arbitrary"),
        name="mem_proj",
    )(x, g.reshape(1, d), w)


def _fox_kernel(st_ref, q_ref, qn_ref, k0_ref, kn_ref, v_ref, o_ref, q_aug, k_all, v_aug, s_buf, mt_buf, p_buf,
                acc_ref, first_ref, *, tq, tk, hp):
    bi = pl.program_id(0)
    grp = pl.program_id(1)
    i = pl.program_id(2)
    seq = k_all.shape[1]
    cur = i % 2
    heads = tuple(range(hp))

    def first_live_pair(qi):
        c_q = jnp.maximum(2 * qi - 1, 0)
        first = qi
        for hh in heads:
            h = grp * hp + hh
            q_sq = jnp.maximum(st_ref[bi, 2 * qi, 2, h], st_ref[bi, 2 * qi + 1, 2, h])
            k_sq = st_ref[bi, 0, 1, h]
            bound_sq = (4.0 * SKIP_SLACK * SKIP_SLACK) * q_sq * k_sq

            def count(u, n):
                room = -SKIP_BITS - (st_ref[bi, c_q, 0, h] - st_ref[bi, 2 * u + 1, 0, h])
                dead = jnp.logical_and(room > 0.0, bound_sq < room * room)
                return n + dead.astype(jnp.int32)

            first = jnp.minimum(first, lax.fori_loop(0, qi, count, jnp.int32(0)))
        return first

    def build_queries(src_ref, qslot):
        qrow = lax.broadcasted_iota(jnp.int32, (LANES, tq), 0)
        for pr in range(hp // 2):
            qb = src_ref[pr * LANES:(pr + 1) * LANES, :].astype(F32)
            q_aug[qslot, 2 * pr] = jnp.where(qrow < HEAD_DIM, qb, _one_hot_rows((LANES, tq), 0, AUG_EVEN)).astype(BF16)
            q_aug[qslot, 2 * pr + 1] = jnp.where(qrow >= HEAD_DIM, qb,
                                                 _one_hot_rows((LANES, tq), 0, AUG_ODD)).astype(BF16)

    def scores(t, slot, qslot):
        off = pl.multiple_of(t * tk, tk)
        for hh in heads:
            s = jnp.dot(k_all[hh, pl.ds(off, tk), :], q_aug[qslot, hh], preferred_element_type=F32)
            s_buf[hh, slot, :, 0:tq] = s
            mt_buf[hh, slot, 0:1, :] = jnp.max(s, axis=0, keepdims=True)

    def softmax(slot, ms, mask=None):
        out = []
        for hh in heads:
            s = s_buf[hh, slot, :, 0:tq]
            if mask is None:
                tile_max = mt_buf[hh, slot, 0:1, :]
            else:
                s = jnp.where(mask, s, NEG)
                tile_max = jnp.max(s, axis=0, keepdims=True)
            m_new = jnp.maximum(ms[hh], tile_max)
            p_buf[hh, slot, :, 0:tq] = jnp.exp2(s - m_new).astype(BF16)
            out.append((m_new, jnp.exp2(ms[hh] - m_new)))
        return tuple(o[0] for o in out), tuple(o[1] for o in out)

    def values(t, slot, alphas):
        off = pl.multiple_of(jnp.maximum(t, 0) * tk, tk)
        for hh in heads:
            pv = jnp.dot(v_aug[hh, :, pl.ds(off, tk)], p_buf[hh, slot, :, 0:tq], preferred_element_type=F32)
            acc_ref[hh] = alphas[hh] * acc_ref[hh] + pv

    v_off = pl.multiple_of(i * tq, tq)
    ones_row = _one_hot_rows((V_ROWS - HEAD_DIM, tq), 0, (0,)).astype(BF16)
    for hh in heads:
        v_aug[hh, 0:HEAD_DIM, pl.ds(v_off, tq)] = v_ref[hh * HEAD_DIM:(hh + 1) * HEAD_DIM, :]
        v_aug[hh, HEAD_DIM:V_ROWS, pl.ds(v_off, tq)] = ones_row

    @pl.when(i == 0)
    def _():
        for hh in heads:
            k_all[hh, 0:tq, :] = k0_ref[0, hh]
            p_buf[hh, 1, :, 0:tq] = jnp.zeros((tk, tq), BF16)
            acc_ref[hh] = jnp.zeros(acc_ref.shape[1:], F32)
        build_queries(q_ref, 0)
        scores(0, 0, 0)

    def pair(u, carry):
        ms, alpha_prev = carry
        t = 2 * u
        scores(t + 1, 1, cur)
        ms, alpha0 = softmax(0, ms)
        values(t - 1, 1, alpha_prev)
        scores(t + 2, 0, cur)
        ms, alpha1 = softmax(1, ms)
        values(t, 0, alpha0)
        return ms, alpha1

    carry = (tuple(jnp.full((1, tq), NEG, F32) for _ in heads), tuple(jnp.ones((1, tq), F32) for _ in heads))
    first_now = jnp.where(i == 0, 0, first_ref[0])
    first_next = first_live_pair(jnp.minimum(i + 1, seq // tq - 1))
    ms, alpha_prev = lax.fori_loop(first_now, i, pair, carry)

    t = 2 * i
    key = lax.broadcasted_iota(jnp.int32, (tk, tq), 0)
    qry = lax.broadcasted_iota(jnp.int32, (tk, tq), 1)
    off_b = pl.multiple_of((t + 1) * tk, tk)
    tri = lax.broadcasted_iota(jnp.int32, (tk, tk), 0) <= lax.broadcasted_iota(jnp.int32, (tk, tk), 1)
    s_b = [jnp.where(tri, jnp.dot(k_all[hh, pl.ds(off_b, tk), :], q_aug[cur, hh, :, tk:],
                                  preferred_element_type=F32), NEG) for hh in heads]
    ms, alpha0 = softmax(0, ms, key <= qry)
    kn_off = pl.multiple_of(jnp.minimum(i + 1, seq // tq - 1) * tq, tq)
    for hh in heads:
        k_all[hh, pl.ds(kn_off, tq), :] = kn_ref[0, hh]
    build_queries(qn_ref, 1 - cur)
    first_ref[0] = first_next
    scores(2 * first_next, 0, 1 - cur)
    values(t - 1, 1, alpha_prev)
    values(t, 0, alpha0)
    outs = []
    for hh in heads:
        mt_buf[hh, 1, 0:1, :] = ms[hh]
        m_old = mt_buf[hh, 1, 0:1, tk:]
        m_new = jnp.maximum(m_old, jnp.max(s_b[hh], axis=0, keepdims=True))
        p_b = jnp.exp2(s_b[hh] - m_new).astype(BF16)
        pv = jnp.dot(v_aug[hh, :, pl.ds(off_b, tk)], p_b, preferred_element_type=F32)
        acc_ref[hh, :, tk:] = jnp.exp2(m_old - m_new) * acc_ref[hh, :, tk:] + pv
        acc = acc_ref[hh]
        outs.append(acc[:HEAD_DIM] / acc[HEAD_DIM:HEAD_DIM + 1])
    o_ref[...] = jnp.concatenate(outs, axis=0).T.astype(BF16)


def _fox_attention(stats, pt, k_aug, b, s, tq, hp):
    tk = tq // 2
    per = s // tq
    rows = hp * HEAD_DIM
    qblk = FOXQ_ROW // rows
    vblk = FOXV_ROW // rows
    return pl.pallas_call(
        functools.partial(_fox_kernel, tq=tq, tk=tk, hp=hp),
        grid=(b, FOX_HEADS // hp, per),
        in_specs=[
            pl.BlockSpec(memory_space=pltpu.SMEM),
            pl.BlockSpec((rows, tq), lambda bi, g, i: (qblk + g, bi * per + i)),
            pl.BlockSpec((rows, tq), lambda bi, g, i: (qblk + g, bi * per + jnp.minimum(i + 1, per - 1))),
            pl.BlockSpec((1, hp, tq, LANES), lambda bi, g, i: (bi, g, 0, 0)),
            pl.BlockSpec((1, hp, tq, LANES), lambda bi, g, i: (bi, g, jnp.minimum(i + 1, per - 1), 0)),
            pl.BlockSpec((rows, tq), lambda bi, g, i: (vblk + g, bi * per + i)),
        ],
        out_specs=pl.BlockSpec((tq, rows), lambda bi, g, i: (bi * per + i, g)),
        out_shape=jax.ShapeDtypeStruct((b * s, BRANCH), BF16),
        scratch_shapes=[
            pltpu.VMEM((2, hp, LANES, tq), BF16),
            pltpu.VMEM((hp, s, LANES), BF16),
            pltpu.VMEM((hp, V_ROWS, s), BF16),
            pltpu.VMEM((hp, 2, tk, tq + PITCH_PAD), F32),
            pltpu.VMEM((hp, 2, SUBLANES, tq), F32),
            pltpu.VMEM((hp, 2, tk, tq + PITCH_PAD), BF16),
            pltpu.VMEM((hp, V_ROWS, tq), F32),
            pltpu.SMEM((1,), jnp.int32),
        ],
        compiler_params=_params("arbitrary", "arbitrary", "arbitrary", vmem_limit=FOX_VMEM_LIMIT_BYTES),
        name="fox_attention",
    )(stats, pt, pt, k_aug, k_aug, pt)


def _swa_bias_kernel(rb_ref, bucket_ref, win_ref, o_ref):
    h = pl.program_id(0)
    bucket = bucket_ref[...]
    bias = jnp.zeros(bucket.shape, F32)
    for b in range(N_BUCKETS):
        bias = jnp.where(bucket == b, rb_ref[b, h] * LOG2E, bias)
    o_ref[0, 0] = jnp.where(win_ref[0] > 0, bias, NEG)
    o_ref[1, 0] = jnp.where(win_ref[1] > 0, bias, NEG)


def _swa_bias_table(rel_bias):
    tq = np.arange(WINDOW)[None, :]
    sk = np.arange(2 * WINDOW)[:, None]
    dist = WINDOW + tq - sk
    n = jnp.maximum(jnp.asarray(dist, jnp.int32), 0)
    max_exact = N_BUCKETS // 2
    large = max_exact + (jnp.log(jnp.maximum(n, 1).astype(F32) / max_exact)
                         / math.log(WINDOW / max_exact) * (N_BUCKETS - max_exact)).astype(jnp.int32)
    bucket = jnp.where(n < max_exact, n, jnp.minimum(large, N_BUCKETS - 1))
    in_window = (dist >= 0) & (dist < WINDOW)
    win = np.stack([in_window, in_window & (sk >= WINDOW)]).astype(np.int32)
    full = lambda shape: pl.BlockSpec(shape, lambda h: (0,) * len(shape))
    return pl.pallas_call(
        _swa_bias_kernel,
        grid=(SWA_HEADS,),
        in_specs=[pl.BlockSpec(memory_space=pltpu.SMEM), full(bucket.shape), full(win.shape)],
        out_specs=pl.BlockSpec((2, 1, 2 * WINDOW, WINDOW), lambda h: (0, h // SWA_GROUP, 0, h % SWA_GROUP)),
        out_shape=jax.ShapeDtypeStruct((2, SWA_KV, 2 * WINDOW, SWA_GROUP * WINDOW), F32),
        compiler_params=_params("arbitrary"),
        name="swa_bias_table",
    )(rel_bias, bucket, jnp.asarray(win))


def _swa_kernel(sink_ref, q_ref, kp_ref, kc_ref, vp_ref, vc_ref, bias_ref, o_ref,
                k_buf, v_buf, s_buf, mt_buf, p_buf, dn_buf, *, nblk):
    gw = SWA_GROUP * WINDOW
    ts = nblk * WINDOW
    first = pl.program_id(1) == 0
    k_buf[0:WINDOW, :] = kp_ref[...]
    k_buf[WINDOW:WINDOW + ts, :] = kc_ref[...]
    ones_row = _one_hot_rows((V_ROWS - HEAD_DIM, ts + WINDOW), 0, (0,)).astype(BF16)
    for kv in range(SWA_KV):
        rows = slice(kv * HEAD_DIM, (kv + 1) * HEAD_DIM)
        v_buf[kv, 0:HEAD_DIM, 0:WINDOW] = vp_ref[rows, :]
        v_buf[kv, 0:HEAD_DIM, WINDOW:WINDOW + ts] = vc_ref[rows, :]
        v_buf[kv, HEAD_DIM:V_ROWS, :] = ones_row
    glane = lax.broadcasted_iota(jnp.int32, (1, gw), 1) // WINDOW
    zeros = jnp.zeros((HEAD_DIM, gw), BF16)
    sinks = []
    for kv in range(SWA_KV):
        sink = jnp.zeros((1, gw), F32)
        for g in range(SWA_GROUP):
            sink = jnp.where(glane == g, sink_ref[kv * SWA_GROUP + g] * LOG2E, sink)
        sinks.append(sink)

    def scores(blk, slot):
        lo = pl.multiple_of(blk * WINDOW, WINDOW)
        sel = jnp.logical_and(first, blk == 0).astype(jnp.int32)
        keys = k_buf[pl.ds(lo, 2 * WINDOW), :]
        for kv in range(SWA_KV):
            qg = jnp.concatenate([q_ref[(kv * SWA_GROUP + g) * HEAD_DIM:(kv * SWA_GROUP + g + 1) * HEAD_DIM,
                                        pl.ds(lo, WINDOW)] for g in range(SWA_GROUP)], axis=1)
            q_pad = jnp.concatenate([qg, zeros] if kv == 0 else [zeros, qg], axis=0)
            s = jnp.dot(keys, q_pad, preferred_element_type=F32) + bias_ref[sel, kv]
            s_buf[kv, slot, :, 0:gw] = s
            mt_buf[kv, slot, 0:1, :] = jnp.max(s, axis=0, keepdims=True)

    def softmax(slot):
        for kv in range(SWA_KV):
            m = jnp.maximum(mt_buf[kv, slot, 0:1, :], sinks[kv])
            p = jnp.exp2(s_buf[kv, slot, :, 0:gw] - m)
            dn_buf[kv, slot, 0:1, :] = jnp.exp2(sinks[kv] - m)
            p_buf[kv, slot, :, 0:gw] = p.astype(BF16)

    def values(blk, slot):
        lo = pl.multiple_of(blk * WINDOW, WINDOW)
        for kv in range(SWA_KV):
            pv = jnp.dot(v_buf[kv, :, pl.ds(lo, 2 * WINDOW)], p_buf[kv, slot, :, 0:gw],
                         preferred_element_type=F32)
            o = pv[:HEAD_DIM] / (pv[HEAD_DIM:HEAD_DIM + 1] + dn_buf[kv, slot, 0:1, :])
            for pr in range(SWA_GROUP // 2):
                two = jnp.concatenate([o[:, (2 * pr) * WINDOW:(2 * pr + 1) * WINDOW],
                                       o[:, (2 * pr + 1) * WINDOW:(2 * pr + 2) * WINDOW]], axis=0)
                col = (kv * (SWA_GROUP // 2) + pr) * LANES
                o_ref[pl.ds(lo, WINDOW), col:col + LANES] = two.T.astype(BF16)

    scores(jnp.int32(0), 0)
    scores(jnp.int32(1), 1)
    softmax(0)

    def pair(u, carry):
        blk = 2 * u + 1
        scores(blk + 1, 0)
        softmax(1)
        values(blk - 1, 0)
        scores(blk + 2, 1)
        softmax(0)
        values(blk, 1)
        return carry

    lax.fori_loop(0, nblk // 2 - 1, pair, 0)
    softmax(1)
    values(jnp.int32(nblk - 2), 0)
    values(jnp.int32(nblk - 1), 1)


def _swa_attention(sink, pt, pn, bias, b, s, nblk):
    ts = nblk * WINDOW
    per = s // ts
    perw = s // WINDOW
    qblk = SWAQ_ROW // BRANCH
    vblk = SWAV_ROW // LANES
    kblk = SWAK_OFF // LANES
    prev = lambda bi, i: bi * perw + jnp.maximum(i * nblk - 1, 0)
    return pl.pallas_call(
        functools.partial(_swa_kernel, nblk=nblk),
        grid=(b, per),
        in_specs=[
            pl.BlockSpec(memory_space=pltpu.SMEM),
            pl.BlockSpec((BRANCH, ts), lambda bi, i: (qblk, bi * per + i)),
            pl.BlockSpec((WINDOW, LANES), lambda bi, i: (prev(bi, i), kblk)),
            pl.BlockSpec((ts, LANES), lambda bi, i: (bi * per + i, kblk)),
            pl.BlockSpec((LANES, WINDOW), lambda bi, i: (vblk, prev(bi, i))),
            pl.BlockSpec((LANES, ts), lambda bi, i: (vblk, bi * per + i)),
            _const_spec(bias.shape),
        ],
        out_specs=pl.BlockSpec((ts, BRANCH), lambda bi, i: (bi * per + i, 0)),
        out_shape=jax.ShapeDtypeStruct((b * s, BRANCH), BF16),
        scratch_shapes=[
            pltpu.VMEM((ts + WINDOW, LANES), BF16),
            pltpu.VMEM((SWA_KV, V_ROWS, ts + WINDOW), BF16),
            pltpu.VMEM((SWA_KV, 2, 2 * WINDOW, SWA_GROUP * WINDOW + PITCH_PAD), F32),
            pltpu.VMEM((SWA_KV, 2, SUBLANES, SWA_GROUP * WINDOW), F32),
            pltpu.VMEM((SWA_KV, 2, 2 * WINDOW, SWA_GROUP * WINDOW + PITCH_PAD), BF16),
            pltpu.VMEM((SWA_KV, 2, SUBLANES, SWA_GROUP * WINDOW), F32),
        ],
        compiler_params=_params("arbitrary", "arbitrary"),
        name="swa_attention",
    )(sink, pt, pn, pn, pt, pt, bias)


def _merge_tile(x, gate_ref, branches, wb_ref, wo_ref, m_scr):
    for lo in range(0, D_MODEL, MERGE_CHUNK):
        gates = [jax.nn.sigmoid(gate_ref[:, b * D_MODEL + lo:b * D_MODEL + lo + MERGE_CHUNK].astype(F32))
                 for b in range(len(branches))]
        merged = None
        for b, y in enumerate(branches):
            term = gates[b] * jnp.dot(y, wb_ref[b, :, lo:lo + MERGE_CHUNK], preferred_element_type=F32)
            merged = term if merged is None else merged + term
        m_scr[:, lo:lo + MERGE_CHUNK] = merged.astype(BF16)
    return x + jnp.dot(m_scr[...], wo_ref[...], preferred_element_type=F32)


def _xattn_tile(x, g_ref, wq_ref, kT_ref, v_ref, wo_ref):
    q = jnp.dot(_rms(x, g_ref[...]).astype(BF16), wq_ref[...], preferred_element_type=F32).astype(BF16)
    sls = [slice(h * X_HEAD_DIM, (h + 1) * X_HEAD_DIM) for h in range(X_HEADS)]
    scores = [jnp.dot(q[:, sl], kT_ref[0, sl, :], preferred_element_type=F32) for sl in sls]
    probs = [jnp.exp2(s - jnp.max(s, axis=-1, keepdims=True)) for s in scores]
    denoms = [jnp.sum(p, axis=-1, keepdims=True) for p in probs]
    outs = [jnp.dot(p.astype(BF16), v_ref[0, :, sl], preferred_element_type=F32) for p, sl in zip(probs, sls)]
    heads = [(o / d).astype(BF16) for o, d in zip(outs, denoms)]
    return x + jnp.dot(jnp.concatenate(heads, axis=1), wo_ref[...], preferred_element_type=F32)


def _swiglu_tile(x, g_ref, wg_ref, wu_ref, wd_ref):
    xn = _rms(x, g_ref[...]).astype(BF16)
    y = x
    for lo in range(0, wg_ref.shape[1], FFN_CHUNK):
        gate = jnp.dot(xn, wg_ref[:, lo:lo + FFN_CHUNK], preferred_element_type=F32)
        up = jnp.dot(xn, wu_ref[:, lo:lo + FFN_CHUNK], preferred_element_type=F32)
        hidden = (gate * jax.nn.sigmoid(gate) * up).astype(BF16)
        y = y + jnp.dot(hidden, wd_ref[lo:lo + FFN_CHUNK, :], preferred_element_type=F32)
    return y


def _layer_tail_kernel(x_ref, gate_ref, yc_ref, yf_ref, ys_ref, wb_ref, wo_ref, gx_ref, wq_ref, kT_ref, v_ref,
                       wxo_ref, gf_ref, wg_ref, wu_ref, wd_ref, gl_ref, o_ref, m_scr, *, final_norm):
    x = _merge_tile(x_ref[...], gate_ref, (yc_ref[...], yf_ref[...], ys_ref[...]), wb_ref, wo_ref, m_scr)
    x = _xattn_tile(x, gx_ref, wq_ref, kT_ref, v_ref, wxo_ref)
    x = _swiglu_tile(x, gf_ref, wg_ref, wu_ref, wd_ref)
    o_ref[...] = _rms(x, gl_ref[...]) if final_norm else x


def _layer_tail(x, pn, y_fox, y_swa, w_branch, w_out, gx, wq, kT, v, wxo, gf, wg, wu, wd, g_last, final_norm,
                layer, seq, tm):
    n, d = x.shape
    tiles = seq // tm
    vec = _const_spec((1, d))
    return pl.pallas_call(
        functools.partial(_layer_tail_kernel, final_norm=final_norm),
        grid=(n // tm,),
        in_specs=[
            pl.BlockSpec((tm, d), lambda i: (i, 0)),
            pl.BlockSpec((tm, 3 * D_MODEL), lambda i: (i, GATE_OFF // (3 * D_MODEL))),
            pl.BlockSpec((tm, BRANCH), lambda i: (i, CONV_OFF // BRANCH)),
            pl.BlockSpec((tm, BRANCH), lambda i: (i, 0)),
            pl.BlockSpec((tm, BRANCH), lambda i: (i, 0)),
            _const_spec(w_branch.shape), _layer_spec(w_out.shape, layer),
            vec, _layer_spec(wq.shape, layer),
            pl.BlockSpec((1,) + kT.shape[1:], lambda i: (i // tiles, 0, 0)),
            pl.BlockSpec((1,) + v.shape[1:], lambda i: (i // tiles, 0, 0)),
            _layer_spec(wxo.shape, layer),
            vec, _layer_spec(wg.shape, layer), _layer_spec(wu.shape, layer), _layer_spec(wd.shape, layer), vec,
        ],
        out_specs=pl.BlockSpec((tm, d), lambda i: (i, 0)),
        out_shape=jax.ShapeDtypeStruct((n, d), F32),
        scratch_shapes=[pltpu.VMEM((tm, d), BF16)],
        compiler_params=_params("arbitrary", vmem_limit=TAIL_VMEM_LIMIT_BYTES),
        name="layer_tail",
    )(x, pn, pn, y_fox, y_swa, w_branch, w_out, gx.reshape(1, d), wq, kT, v, wxo,
      gf.reshape(1, d), wg, wu, wd, g_last.reshape(1, d))


def _permute_heads(w, perm, axis):
    shape = w.shape
    split = shape[:axis] + (FOX_HEADS, HEAD_DIM) + shape[axis + 1:]
    return jnp.take(w.reshape(split), perm, axis=axis).reshape(shape)


def _pack_in_proj(w_in, w_in_bf16, l, head_perm):
    sizes = [3 * BRANCH] + [BRANCH] * 3 + [FOX_HEADS] + [BRANCH, SWA_KV * HEAD_DIM, SWA_KV * HEAD_DIM] + [3 * D_MODEL]
    offs = np.concatenate([[0], np.cumsum(sizes)])
    f_q, f_v, s_q, s_v = (w_in[l, :, offs[t]:offs[t + 1]] for t in (1, 3, 5, 7))
    wc, f_k, f_g, s_k, gates = (w_in_bf16[l, :, offs[t]:offs[t + 1]] for t in (0, 2, 4, 6, 8))
    qscale = HEAD_DIM ** -0.5 * LOG2E
    f_q, s_q = (f_q * qscale).astype(BF16), (s_q * qscale).astype(BF16)
    f_v, s_v = f_v.astype(BF16), s_v.astype(BF16)
    f_q, f_k, f_v = (_permute_heads(w, head_perm, 1) for w in (f_q, f_k, f_v))
    f_g = jnp.take(f_g, head_perm, axis=1)
    wn = jnp.concatenate([gates, s_k], axis=1)
    wt = jnp.concatenate([f_q, f_v, s_q, s_v], axis=1).T
    wfg = jnp.pad(f_g, ((0, 0), (0, LANES - FOX_HEADS)))
    return wn, wc, f_k, wt, wfg


def kernel(x, mem, mix_norm_g, w_in, forget_bias, conv_w, sink, w_branch, w_mix_out, rel_bias,
           xattn_norm_g, mem_norm_g, w_xq, w_xkv, w_xo, ffn_norm_g, w_ffn_gate, w_ffn_up, w_ffn_down,
           final_norm_g):
    b, s, d = x.shape
    n = b * s
    depth = w_in.shape[0]
    assert d == D_MODEL and s % FOX_TQ == 0 and s % ROW_TILE == 0 and FOX_TQ == 2 * ROW_TILE
    assert s % (SWA_BLOCKS_PER_STEP * WINDOW) == 0 and SWA_BLOCKS_PER_STEP % 2 == 0
    mem_len = mem.shape[1]
    xf = x.reshape(n, d)
    memf = mem.reshape(b * mem_len, d)
    swa_bias = _swa_bias_table(rel_bias)
    w_in_bf16 = w_in.astype(BF16)
    w_branch_bf16 = w_branch.astype(BF16)
    w_out, wxkv, wxo = w_mix_out.astype(BF16), w_xkv.astype(BF16), w_xo.astype(BF16)
    wq = (w_xq * (X_HEAD_DIM ** -0.5 * LOG2E)).astype(BF16)
    wg, wu, wd = w_ffn_gate.astype(BF16), w_ffn_up.astype(BF16), w_ffn_down.astype(BF16)
    for l in range(depth):
        head_perm = jnp.argsort(forget_bias[l])
        wn, wc, wk, wt, wfg = _pack_in_proj(w_in, w_in_bf16, l, head_perm)
        w_br = w_branch_bf16[l].at[1].set(_permute_heads(w_branch_bf16[l, 1], head_perm, 0))
        pn, pt, k_aug, st = _in_proj(xf, mix_norm_g[l], wn, wc, wk, wt, wfg, conv_w[l],
                                     jnp.take(forget_bias[l], head_perm), seq=s, tm=ROW_TILE)
        st = st.reshape(b, s // ROW_TILE, SUBLANES, LANES)
        k_sq = jnp.broadcast_to(jnp.max(st[:, :, 1:2, 0:FOX_HEADS], axis=1, keepdims=True),
                                (b, s // ROW_TILE, 1, FOX_HEADS))
        stats = jnp.concatenate([st[:, :, 0:1, 0:FOX_HEADS], k_sq, st[:, :, None, 0:FOX_HEADS, FOX_HEADS]], axis=2)
        y_fox = _fox_attention(stats, pt, k_aug, b, s, tq=FOX_TQ, hp=FOX_HEADS_PER_STEP)
        y_swa = _swa_attention(sink[l], pt, pn, swa_bias, b, s, nblk=SWA_BLOCKS_PER_STEP)
        kv = _norm_proj(memf, mem_norm_g[l], wxkv, l)
        kT = kv[:, :d].reshape(b, mem_len, d).transpose(0, 2, 1)
        v = kv[:, d:].reshape(b, mem_len, d)
        xf = _layer_tail(xf, pn, y_fox, y_swa, w_br, w_out, xattn_norm_g[l], wq, kT, v, wxo,
                         ffn_norm_g[l], wg, wu, wd, final_norm_g, final_norm=(l == depth - 1),
                         layer=l, seq=s, tm=ROW_TILE)
    return xf.reshape(b, s, d)
```

```python
import functools
import math

import jax
import jax.numpy as jnp
import numpy as np
from jax import lax
from jax.experimental import pallas as pl
from jax.experimental.pallas import tpu as pltpu

F32 = jnp.float32
BF16 = jnp.bfloat16

D_MODEL = 1024
HEAD_DIM = 64
BRANCH = 512
FOX_HEADS = 8
SWA_HEADS = 8
SWA_KV = 2
SWA_GROUP = SWA_HEADS // SWA_KV
WINDOW = 128
N_BUCKETS = 32
X_HEADS = 4
X_HEAD_DIM = D_MODEL // X_HEADS
RMS_EPS = 1e-6
NEG = -1e30
LOG2E = math.log2(math.e)

LANES = 128
SUBLANES = 8
VMEM_LIMIT_BYTES = 56 * 1024 * 1024
FOX_VMEM_LIMIT_BYTES = 58 * 1024 * 1024
TAIL_VMEM_LIMIT_BYTES = 58 * 1024 * 1024
PITCH_PAD = LANES
MERGE_CHUNK = 256
FFN_CHUNK = 256

ROW_TILE = 512
FOX_TQ = 1024
FOX_HEADS_PER_STEP = 4
SWA_BLOCKS_PER_STEP = 16

GATE_OFF = 0
CONV_OFF = 3 * D_MODEL
SWAK_OFF = CONV_OFF + BRANCH
PN_COLS = SWAK_OFF + SWA_KV * HEAD_DIM
FOXQ_ROW = 0
FOXV_ROW = BRANCH
SWAQ_ROW = 2 * BRANCH
SWAV_ROW = 3 * BRANCH
PT_ROWS = SWAV_ROW + SWA_KV * HEAD_DIM

AUG_STRIDE = 8
AUG_EVEN = (HEAD_DIM, HEAD_DIM + AUG_STRIDE, HEAD_DIM + 2 * AUG_STRIDE)
AUG_ODD = (0, AUG_STRIDE, 2 * AUG_STRIDE)
V_ROWS = 80
SKIP_BITS = 160.0
SKIP_SLACK = 1.05


def _params(*sem, vmem_limit=VMEM_LIMIT_BYTES):
    return pltpu.CompilerParams(dimension_semantics=sem, vmem_limit_bytes=vmem_limit)


def _rms(x, g):
    return x * lax.rsqrt(jnp.mean(x * x, axis=-1, keepdims=True) + RMS_EPS) * g


def _const_spec(shape):
    nd = len(shape)
    return pl.BlockSpec(shape, lambda *_: (0,) * nd, pipeline_mode=pl.Buffered(1))


def _layer_spec(stacked_shape, layer):
    return pl.BlockSpec((None,) + tuple(stacked_shape[1:]), lambda *_: (layer, 0, 0), pipeline_mode=pl.Buffered(1))


def _one_hot_rows(shape, axis, positions):
    idx = lax.broadcasted_iota(jnp.int32, shape, axis)
    hit = idx == positions[0]
    for p in positions[1:]:
        hit = hit | (idx == p)
    return jnp.where(hit, 1.0, 0.0)


def _in_proj_kernel(x_ref, g_ref, wn_ref, wc_ref, wk_ref, wt_ref, wfg_ref, cw_ref, fb_ref,
                    pn_ref, pt_ref, ka_ref, st_ref, z_tail, c_tail, *, tiles_per_seq):
    tm = x_ref.shape[0]
    first_of_seq = pl.program_id(0) % tiles_per_seq == 0

    @pl.when(pl.program_id(0) == 0)
    def _():
        z_tail[...] = jnp.zeros(z_tail.shape, F32)
        c_tail[...] = jnp.zeros(c_tail.shape, F32)

    h = _rms(x_ref[...], g_ref[...]).astype(BF16)

    fg_swak = jnp.dot(h, wfg_ref[...], preferred_element_type=F32)
    fg = fg_swak[:, 0:LANES]
    pn_ref[:, SWAK_OFF:PN_COLS] = fg_swak[:, LANES:].astype(BF16)
    k_fox = jnp.dot(h, wk_ref[...], preferred_element_type=F32)
    lane = lax.broadcasted_iota(jnp.int32, (tm, LANES), 1)
    trow = lax.broadcasted_iota(jnp.int32, (tm, LANES), 0)
    c = jnp.where(lane < FOX_HEADS, jax.nn.log_sigmoid(fg + fb_ref[...]) * LOG2E, 0.0)
    d = 1
    while d < tm:
        c = c + jnp.where(trow >= d, pltpu.roll(c, d, axis=0), 0.0)
        d *= 2
    c = c + jnp.where(first_of_seq, 0.0, c_tail[0:1, :])
    c_tail[0:1, :] = c[tm - 1:tm, :]
    head_of_col = lax.broadcasted_iota(jnp.int32, (BRANCH, LANES), 0) // HEAD_DIM
    head_sel = jnp.where(head_of_col == lax.broadcasted_iota(jnp.int32, (BRANCH, LANES), 1), 1.0, 0.0).astype(BF16)
    k_sq = jnp.dot((k_fox * k_fox).astype(BF16), head_sel, preferred_element_type=F32)
    st_ref[...] = jnp.zeros(st_ref.shape, F32)
    st_ref[0:1, :] = c[tm - 1:tm, :]
    st_ref[1:2, :] = jnp.max(k_sq, axis=0, keepdims=True)
    c1 = c.astype(BF16).astype(F32)
    r1 = c - c1
    c2 = r1.astype(BF16).astype(F32)
    c3 = (r1 - c2).astype(BF16).astype(F32)
    packed = -jnp.where(lane < AUG_STRIDE, c1,
                        jnp.where(lane < 2 * AUG_STRIDE, pltpu.roll(c2, AUG_STRIDE, axis=1),
                                  pltpu.roll(c3, 2 * AUG_STRIDE, axis=1)))
    keep_even = _one_hot_rows((1, LANES), 1, AUG_EVEN)
    keep_odd = _one_hot_rows((1, LANES), 1, AUG_ODD)
    for hd in range(FOX_HEADS):
        k_pair = k_fox[:, (hd // 2) * LANES:(hd // 2 + 1) * LANES]
        if hd % 2 == 0:
            aug = pltpu.roll(packed, (AUG_EVEN[0] - hd) % LANES, axis=1) * keep_even
            out = jnp.where(lane < HEAD_DIM, k_pair, aug)
        else:
            aug = pltpu.roll(packed, (AUG_ODD[0] - hd) % LANES, axis=1) * keep_odd
            out = jnp.where(lane >= HEAD_DIM, k_pair, aug)
        ka_ref[0, hd] = out.astype(BF16)

    conv_b, conv_c, conv_u = (jnp.dot(h, wc_ref[:, t * BRANCH:(t + 1) * BRANCH], preferred_element_type=F32)
                              for t in range(3))
    z = conv_c * conv_u
    zh = jnp.where(first_of_seq, 0.0, z_tail[...])
    z_tail[...] = z[tm - SUBLANES:tm, :]
    row = lax.broadcasted_iota(jnp.int32, z.shape, 0)
    z1 = jnp.where(row == 0, zh[7:8], pltpu.roll(z, 1, axis=0))
    z2 = jnp.where(row == 0, zh[6:7], jnp.where(row == 1, zh[7:8], pltpu.roll(z, 2, axis=0)))
    cw = cw_ref[...]
    pn_ref[:, CONV_OFF:SWAK_OFF] = (conv_b * (cw[0:1] * z2 + cw[1:2] * z1 + cw[2:3] * z)).astype(BF16)

    for lo in range(0, CONV_OFF, D_MODEL):
        pn_ref[:, lo:lo + D_MODEL] = jnp.dot(h, wn_ref[:, lo:lo + D_MODEL], preferred_element_type=F32).astype(BF16)
    pt = lax.dot_general(wt_ref[...], h, (((1,), (1,)), ((), ())), preferred_element_type=F32)
    pt_ref[...] = pt.astype(BF16)
    for hd in range(FOX_HEADS):
        q_h = pt[FOXQ_ROW + hd * HEAD_DIM:FOXQ_ROW + (hd + 1) * HEAD_DIM, :]
        q_sq = jnp.sum(q_h * q_h, axis=0, keepdims=True)
        st_ref[hd:hd + 1, FOX_HEADS:FOX_HEADS + 1] = jnp.max(q_sq, axis=1, keepdims=True)


def _in_proj(x, g, wn, wc, wk, wt, wfg, conv_w, forget_bias, seq, tm):
    n, d = x.shape
    tiles = seq // tm
    fb = jnp.pad(forget_bias, (0, LANES - FOX_HEADS)).reshape(1, LANES)
    return pl.pallas_call(
        functools.partial(_in_proj_kernel, tiles_per_seq=tiles),
        grid=(n // tm,),
        in_specs=[
            pl.BlockSpec((tm, d), lambda i: (i, 0)),
            _const_spec((1, d)), _const_spec(wn.shape), _const_spec(wc.shape), _const_spec(wk.shape),
            _const_spec(wt.shape), _const_spec(wfg.shape), _const_spec(conv_w.shape), _const_spec((1, LANES)),
        ],
        out_specs=[
            pl.BlockSpec((tm, PN_COLS), lambda i: (i, 0)),
            pl.BlockSpec((PT_ROWS, tm), lambda i: (0, i)),
            pl.BlockSpec((1, FOX_HEADS, tm, LANES), lambda i: (i // tiles, 0, i % tiles, 0)),
            pl.BlockSpec((SUBLANES, LANES), lambda i: (i, 0)),
        ],
        out_shape=[jax.ShapeDtypeStruct((n, PN_COLS), BF16), jax.ShapeDtypeStruct((PT_ROWS, n), BF16),
                   jax.ShapeDtypeStruct((n // seq, FOX_HEADS, seq, LANES), BF16),
                   jax.ShapeDtypeStruct((n // tm * SUBLANES, LANES), F32)],
        scratch_shapes=[pltpu.VMEM((SUBLANES, BRANCH), F32), pltpu.VMEM((SUBLANES, LANES), F32)],
        compiler_params=_params("arbitrary"),
        name="in_proj",
    )(x, g.reshape(1, d), wn, wc, wk, wt, wfg, conv_w, fb)


def _norm_proj_kernel(x_ref, g_ref, w_ref, o_ref):
    h = _rms(x_ref[...], g_ref[...]).astype(BF16)
    o_ref[...] = jnp.dot(h, w_ref[...], preferred_element_type=F32).astype(BF16)


def _norm_proj(x, g, w, layer):
    n, d = x.shape
    cols = w.shape[2]
    return pl.pallas_call(
        _norm_proj_kernel,
        grid=(1,),
        in_specs=[_const_spec((n, d)), _const_spec((1, d)), _layer_spec(w.shape, layer)],
        out_specs=pl.BlockSpec((n, cols), lambda i: (0, 0)),
        out_shape=jax.ShapeDtypeStruct((n, cols), BF16),
        compiler_params=_params("arbitrary"),
        name="mem_proj",
    )(x, g.reshape(1, d), w)


def _fox_kernel(st_ref, q_ref, qn_ref, k0_ref, kn_ref, v_ref, o_ref, q_aug, k_all, v_aug, s_buf, mt_buf, p_buf,
                acc_ref, first_ref, *, tq, tk, hp):
    bi = pl.program_id(0)
    grp = pl.program_id(1)
    i = pl.program_id(2)
    seq = k_all.shape[1]
    cur = i % 2
    heads = tuple(range(hp))

    def first_live_pair(qi):
        c_q = jnp.maximum(2 * qi - 1, 0)
        first = qi
        for hh in heads:
            h = grp * hp + hh
            q_sq = jnp.maximum(st_ref[bi, 2 * qi, 2, h], st_ref[bi, 2 * qi + 1, 2, h])
            k_sq = st_ref[bi, 0, 1, h]
            bound_sq = (4.0 * SKIP_SLACK * SKIP_SLACK) * q_sq * k_sq

            def count(u, n):
                room = -SKIP_BITS - (st_ref[bi, c_q, 0, h] - st_ref[bi, 2 * u + 1, 0, h])
                dead = jnp.logical_and(room > 0.0, bound_sq < room * room)
                return n + dead.astype(jnp.int32)

            first = jnp.minimum(first, lax.fori_loop(0, qi, count, jnp.int32(0)))
        return first

    def build_queries(src_ref, qslot):
        qrow = lax.broadcasted_iota(jnp.int32, (LANES, tq), 0)
        for pr in range(hp // 2):
            qb = src_ref[pr * LANES:(pr + 1) * LANES, :].astype(F32)
            q_aug[qslot, 2 * pr] = jnp.where(qrow < HEAD_DIM, qb, _one_hot_rows((LANES, tq), 0, AUG_EVEN)).astype(BF16)
            q_aug[qslot, 2 * pr + 1] = jnp.where(qrow >= HEAD_DIM, qb,
                                                 _one_hot_rows((LANES, tq), 0, AUG_ODD)).astype(BF16)

    def scores(t, slot, qslot):
        off = pl.multiple_of(t * tk, tk)
        for hh in heads:
            s = jnp.dot(k_all[hh, pl.ds(off, tk), :], q_aug[qslot, hh], preferred_element_type=F32)
            s_buf[hh, slot, :, 0:tq] = s
            mt_buf[hh, slot, 0:1, :] = jnp.max(s, axis=0, keepdims=True)

    def softmax(slot, ms, mask=None):
        out = []
        for hh in heads:
            s = s_buf[hh, slot, :, 0:tq]
            if mask is None:
                tile_max = mt_buf[hh, slot, 0:1, :]
            else:
                s = jnp.where(mask, s, NEG)
                tile_max = jnp.max(s, axis=0, keepdims=True)
            m_new = jnp.maximum(ms[hh], tile_max)
            p_buf[hh, slot, :, 0:tq] = jnp.exp2(s - m_new).astype(BF16)
            out.append((m_new, jnp.exp2(ms[hh] - m_new)))
        return tuple(o[0] for o in out), tuple(o[1] for o in out)

    def values(t, slot, alphas):
        off = pl.multiple_of(jnp.maximum(t, 0) * tk, tk)
        for hh in heads:
            pv = jnp.dot(v_aug[hh, :, pl.ds(off, tk)], p_buf[hh, slot, :, 0:tq], preferred_element_type=F32)
            acc_ref[hh] = alphas[hh] * acc_ref[hh] + pv

    v_off = pl.multiple_of(i * tq, tq)
    ones_row = _one_hot_rows((V_ROWS - HEAD_DIM, tq), 0, (0,)).astype(BF16)
    for hh in heads:
        v_aug[hh, 0:HEAD_DIM, pl.ds(v_off, tq)] = v_ref[hh * HEAD_DIM:(hh + 1) * HEAD_DIM, :]
        v_aug[hh, HEAD_DIM:V_ROWS, pl.ds(v_off, tq)] = ones_row

    @pl.when(i == 0)
    def _():
        for hh in heads:
            k_all[hh, 0:tq, :] = k0_ref[0, hh]
            p_buf[hh, 1, :, 0:tq] = jnp.zeros((tk, tq), BF16)
            acc_ref[hh] = jnp.zeros(acc_ref.shape[1:], F32)
        build_queries(q_ref, 0)
        scores(0, 0, 0)

    def pair(u, carry):
        ms, alpha_prev = carry
        t = 2 * u
        scores(t + 1, 1, cur)
        ms, alpha0 = softmax(0, ms)
        values(t - 1, 1, alpha_prev)
        scores(t + 2, 0, cur)
        ms, alpha1 = softmax(1, ms)
        values(t, 0, alpha0)
        return ms, alpha1

    carry = (tuple(jnp.full((1, tq), NEG, F32) for _ in heads), tuple(jnp.ones((1, tq), F32) for _ in heads))
    first_now = jnp.where(i == 0, 0, first_ref[0])
    first_next = first_live_pair(jnp.minimum(i + 1, seq // tq - 1))
    ms, alpha_prev = lax.fori_loop(first_now, i, pair, carry)

    t = 2 * i
    key = lax.broadcasted_iota(jnp.int32, (tk, tq), 0)
    qry = lax.broadcasted_iota(jnp.int32, (tk, tq), 1)
    off_b = pl.multiple_of((t + 1) * tk, tk)
    tri = lax.broadcasted_iota(jnp.int32, (tk, tk), 0) <= lax.broadcasted_iota(jnp.int32, (tk, tk), 1)
    s_b = [jnp.where(tri, jnp.dot(k_all[hh, pl.ds(off_b, tk), :], q_aug[cur, hh, :, tk:],
                                  preferred_element_type=F32), NEG) for hh in heads]
    ms, alpha0 = softmax(0, ms, key <= qry)
    kn_off = pl.multiple_of(jnp.minimum(i + 1, seq // tq - 1) * tq, tq)
    for hh in heads:
        k_all[hh, pl.ds(kn_off, tq), :] = kn_ref[0, hh]
    build_queries(qn_ref, 1 - cur)
    first_ref[0] = first_next
    scores(2 * first_next, 0, 1 - cur)
    values(t - 1, 1, alpha_prev)
    values(t, 0, alpha0)
    outs = []
    for hh in heads:
        mt_buf[hh, 1, 0:1, :] = ms[hh]
        m_old = mt_buf[hh, 1, 0:1, tk:]
        m_new = jnp.maximum(m_old, jnp.max(s_b[hh], axis=0, keepdims=True))
        p_b = jnp.exp2(s_b[hh] - m_new).astype(BF16)
        pv = jnp.dot(v_aug[hh, :, pl.ds(off_b, tk)], p_b, preferred_element_type=F32)
        acc_ref[hh, :, tk:] = jnp.exp2(m_old - m_new) * acc_ref[hh, :, tk:] + pv
        acc = acc_ref[hh]
        outs.append(acc[:HEAD_DIM] / acc[HEAD_DIM:HEAD_DIM + 1])
    o_ref[...] = jnp.concatenate(outs, axis=0).T.astype(BF16)


def _fox_attention(stats, pt, k_aug, b, s, tq, hp):
    tk = tq // 2
    per = s // tq
    rows = hp * HEAD_DIM
    qblk = FOXQ_ROW // rows
    vblk = FOXV_ROW // rows
    return pl.pallas_call(
        functools.partial(_fox_kernel, tq=tq, tk=tk, hp=hp),
        grid=(b, FOX_HEADS // hp, per),
        in_specs=[
            pl.BlockSpec(memory_space=pltpu.SMEM),
            pl.BlockSpec((rows, tq), lambda bi, g, i: (qblk + g, bi * per + i)),
            pl.BlockSpec((rows, tq), lambda bi, g, i: (qblk + g, bi * per + jnp.minimum(i + 1, per - 1))),
            pl.BlockSpec((1, hp, tq, LANES), lambda bi, g, i: (bi, g, 0, 0)),
            pl.BlockSpec((1, hp, tq, LANES), lambda bi, g, i: (bi, g, jnp.minimum(i + 1, per - 1), 0)),
            pl.BlockSpec((rows, tq), lambda bi, g, i: (vblk + g, bi * per + i)),
        ],
        out_specs=pl.BlockSpec((tq, rows), lambda bi, g, i: (bi * per + i, g)),
        out_shape=jax.ShapeDtypeStruct((b * s, BRANCH), BF16),
        scratch_shapes=[
            pltpu.VMEM((2, hp, LANES, tq), BF16),
            pltpu.VMEM((hp, s, LANES), BF16),
            pltpu.VMEM((hp, V_ROWS, s), BF16),
            pltpu.VMEM((hp, 2, tk, tq + PITCH_PAD), F32),
            pltpu.VMEM((hp, 2, SUBLANES, tq), F32),
            pltpu.VMEM((hp, 2, tk, tq + PITCH_PAD), BF16),
            pltpu.VMEM((hp, V_ROWS, tq), F32),
            pltpu.SMEM((1,), jnp.int32),
        ],
        compiler_params=_params("arbitrary", "arbitrary", "arbitrary", vmem_limit=FOX_VMEM_LIMIT_BYTES),
        name="fox_attention",
    )(stats, pt, pt, k_aug, k_aug, pt)


def _swa_bias_kernel(rb_ref, bucket_ref, win_ref, o_ref):
    h = pl.program_id(0)
    bucket = bucket_ref[...]
    bias = jnp.zeros(bucket.shape, F32)
    for b in range(N_BUCKETS):
        bias = jnp.where(bucket == b, rb_ref[b, h] * LOG2E, bias)
    o_ref[0, 0] = jnp.where(win_ref[0] > 0, bias, NEG)
    o_ref[1, 0] = jnp.where(win_ref[1] > 0, bias, NEG)


def _swa_bias_table(rel_bias):
    tq = np.arange(WINDOW)[None, :]
    sk = np.arange(2 * WINDOW)[:, None]
    dist = WINDOW + tq - sk
    n = jnp.maximum(jnp.asarray(dist, jnp.int32), 0)
    max_exact = N_BUCKETS // 2
    large = max_exact + (jnp.log(jnp.maximum(n, 1).astype(F32) / max_exact)
                         / math.log(WINDOW / max_exact) * (N_BUCKETS - max_exact)).astype(jnp.int32)
    bucket = jnp.where(n < max_exact, n, jnp.minimum(large, N_BUCKETS - 1))
    in_window = (dist >= 0) & (dist < WINDOW)
    win = np.stack([in_window, in_window & (sk >= WINDOW)]).astype(np.int32)
    full = lambda shape: pl.BlockSpec(shape, lambda h: (0,) * len(shape))
    return pl.pallas_call(
        _swa_bias_kernel,
        grid=(SWA_HEADS,),
        in_specs=[pl.BlockSpec(memory_space=pltpu.SMEM), full(bucket.shape), full(win.shape)],
        out_specs=pl.BlockSpec((2, 1, 2 * WINDOW, WINDOW), lambda h: (0, h // SWA_GROUP, 0, h % SWA_GROUP)),
        out_shape=jax.ShapeDtypeStruct((2, SWA_KV, 2 * WINDOW, SWA_GROUP * WINDOW), F32),
        compiler_params=_params("arbitrary"),
        name="swa_bias_table",
    )(rel_bias, bucket, jnp.asarray(win))


def _swa_kernel(sink_ref, q_ref, kp_ref, kc_ref, vp_ref, vc_ref, bias_ref, o_ref,
                k_buf, v_buf, s_buf, mt_buf, p_buf, dn_buf, *, nblk):
    gw = SWA_GROUP * WINDOW
    ts = nblk * WINDOW
    first = pl.program_id(1) == 0
    k_buf[0:WINDOW, :] = kp_ref[...]
    k_buf[WINDOW:WINDOW + ts, :] = kc_ref[...]
    ones_row = _one_hot_rows((V_ROWS - HEAD_DIM, ts + WINDOW), 0, (0,)).astype(BF16)
    for kv in range(SWA_KV):
        rows = slice(kv * HEAD_DIM, (kv + 1) * HEAD_DIM)
        v_buf[kv, 0:HEAD_DIM, 0:WINDOW] = vp_ref[rows, :]
        v_buf[kv, 0:HEAD_DIM, WINDOW:WINDOW + ts] = vc_ref[rows, :]
        v_buf[kv, HEAD_DIM:V_ROWS, :] = ones_row
    glane = lax.broadcasted_iota(jnp.int32, (1, gw), 1) // WINDOW
    zeros = jnp.zeros((HEAD_DIM, gw), BF16)
    sinks = []
    for kv in range(SWA_KV):
        sink = jnp.zeros((1, gw), F32)
        for g in range(SWA_GROUP):
            sink = jnp.where(glane == g, sink_ref[kv * SWA_GROUP + g] * LOG2E, sink)
        sinks.append(sink)

    def scores(blk, slot):
        lo = pl.multiple_of(blk * WINDOW, WINDOW)
        sel = jnp.logical_and(first, blk == 0).astype(jnp.int32)
        keys = k_buf[pl.ds(lo, 2 * WINDOW), :]
        for kv in range(SWA_KV):
            qg = jnp.concatenate([q_ref[(kv * SWA_GROUP + g) * HEAD_DIM:(kv * SWA_GROUP + g + 1) * HEAD_DIM,
                                        pl.ds(lo, WINDOW)] for g in range(SWA_GROUP)], axis=1)
            q_pad = jnp.concatenate([qg, zeros] if kv == 0 else [zeros, qg], axis=0)
            s = jnp.dot(keys, q_pad, preferred_element_type=F32) + bias_ref[sel, kv]
            s_buf[kv, slot, :, 0:gw] = s
            mt_buf[kv, slot, 0:1, :] = jnp.max(s, axis=0, keepdims=True)

    def softmax(slot):
        for kv in range(SWA_KV):
            m = jnp.maximum(mt_buf[kv, slot, 0:1, :], sinks[kv])
            p = jnp.exp2(s_buf[kv, slot, :, 0:gw] - m)
            dn_buf[kv, slot, 0:1, :] = jnp.exp2(sinks[kv] - m)
            p_buf[kv, slot, :, 0:gw] = p.astype(BF16)

    def values(blk, slot):
        lo = pl.multiple_of(blk * WINDOW, WINDOW)
        for kv in range(SWA_KV):
            pv = jnp.dot(v_buf[kv, :, pl.ds(lo, 2 * WINDOW)], p_buf[kv, slot, :, 0:gw],
                         preferred_element_type=F32)
            o = pv[:HEAD_DIM] / (pv[HEAD_DIM:HEAD_DIM + 1] + dn_buf[kv, slot, 0:1, :])
            for pr in range(SWA_GROUP // 2):
                two = jnp.concatenate([o[:, (2 * pr) * WINDOW:(2 * pr + 1) * WINDOW],
                                       o[:, (2 * pr + 1) * WINDOW:(2 * pr + 2) * WINDOW]], axis=0)
                col = (kv * (SWA_GROUP // 2) + pr) * LANES
                o_ref[pl.ds(lo, WINDOW), col:col + LANES] = two.T.astype(BF16)

    scores(jnp.int32(0), 0)
    scores(jnp.int32(1), 1)
    softmax(0)

    def pair(u, carry):
        blk = 2 * u + 1
        scores(blk + 1, 0)
        softmax(1)
        values(blk - 1, 0)
        scores(blk + 2, 1)
        softmax(0)
        values(blk, 1)
        return carry

    lax.fori_loop(0, nblk // 2 - 1, pair, 0)
    softmax(1)
    values(jnp.int32(nblk - 2), 0)
    values(jnp.int32(nblk - 1), 1)


def _swa_attention(sink, pt, pn, bias, b, s, nblk):
    ts = nblk * WINDOW
    per = s // ts
    perw = s // WINDOW
    qblk = SWAQ_ROW // BRANCH
    vblk = SWAV_ROW // LANES
    kblk = SWAK_OFF // LANES
    prev = lambda bi, i: bi * perw + jnp.maximum(i * nblk - 1, 0)
    return pl.pallas_call(
        functools.partial(_swa_kernel, nblk=nblk),
        grid=(b, per),
        in_specs=[
            pl.BlockSpec(memory_space=pltpu.SMEM),
            pl.BlockSpec((BRANCH, ts), lambda bi, i: (qblk, bi * per + i)),
            pl.BlockSpec((WINDOW, LANES), lambda bi, i: (prev(bi, i), kblk)),
            pl.BlockSpec((ts, LANES), lambda bi, i: (bi * per + i, kblk)),
            pl.BlockSpec((LANES, WINDOW), lambda bi, i: (vblk, prev(bi, i))),
            pl.BlockSpec((LANES, ts), lambda bi, i: (vblk, bi * per + i)),
            _const_spec(bias.shape),
        ],
        out_specs=pl.BlockSpec((ts, BRANCH), lambda bi, i: (bi * per + i, 0)),
        out_shape=jax.ShapeDtypeStruct((b * s, BRANCH), BF16),
        scratch_shapes=[
            pltpu.VMEM((ts + WINDOW, LANES), BF16),
            pltpu.VMEM((SWA_KV, V_ROWS, ts + WINDOW), BF16),
            pltpu.VMEM((SWA_KV, 2, 2 * WINDOW, SWA_GROUP * WINDOW + PITCH_PAD), F32),
            pltpu.VMEM((SWA_KV, 2, SUBLANES, SWA_GROUP * WINDOW), F32),
            pltpu.VMEM((SWA_KV, 2, 2 * WINDOW, SWA_GROUP * WINDOW + PITCH_PAD), BF16),
            pltpu.VMEM((SWA_KV, 2, SUBLANES, SWA_GROUP * WINDOW), F32),
        ],
        compiler_params=_params("arbitrary", "arbitrary"),
        name="swa_attention",
    )(sink, pt, pn, pn, pt, pt, bias)


def _merge_tile(x, gate_ref, branches, wb_ref, wo_ref, m_scr):
    for lo in range(0, D_MODEL, MERGE_CHUNK):
        gates = [jax.nn.sigmoid(gate_ref[:, b * D_MODEL + lo:b * D_MODEL + lo + MERGE_CHUNK].astype(F32))
                 for b in range(len(branches))]
        merged = None
        for b, y in enumerate(branches):
            term = gates[b] * jnp.dot(y, wb_ref[b, :, lo:lo + MERGE_CHUNK], preferred_element_type=F32)
            merged = term if merged is None else merged + term
        m_scr[:, lo:lo + MERGE_CHUNK] = merged.astype(BF16)
    return x + jnp.dot(m_scr[...], wo_ref[...], preferred_element_type=F32)


def _xattn_tile(x, g_ref, wq_ref, kT_ref, v_ref, wo_ref):
    q = jnp.dot(_rms(x, g_ref[...]).astype(BF16), wq_ref[...], preferred_element_type=F32).astype(BF16)
    sls = [slice(h * X_HEAD_DIM, (h + 1) * X_HEAD_DIM) for h in range(X_HEADS)]
    scores = [jnp.dot(q[:, sl], kT_ref[0, sl, :], preferred_element_type=F32) for sl in sls]
    probs = [jnp.exp2(s - jnp.max(s, axis=-1, keepdims=True)) for s in scores]
    denoms = [jnp.sum(p, axis=-1, keepdims=True) for p in probs]
    outs = [jnp.dot(p.astype(BF16), v_ref[0, :, sl], preferred_element_type=F32) for p, sl in zip(probs, sls)]
    heads = [(o / d).astype(BF16) for o, d in zip(outs, denoms)]
    return x + jnp.dot(jnp.concatenate(heads, axis=1), wo_ref[...], preferred_element_type=F32)


def _swiglu_tile(x, g_ref, wg_ref, wu_ref, wd_ref):
    xn = _rms(x, g_ref[...]).astype(BF16)
    y = x
    for lo in range(0, wg_ref.shape[1], FFN_CHUNK):
        gate = jnp.dot(xn, wg_ref[:, lo:lo + FFN_CHUNK], preferred_element_type=F32)
        up = jnp.dot(xn, wu_ref[:, lo:lo + FFN_CHUNK], preferred_element_type=F32)
        hidden = (gate * jax.nn.sigmoid(gate) * up).astype(BF16)
        y = y + jnp.dot(hidden, wd_ref[lo:lo + FFN_CHUNK, :], preferred_element_type=F32)
    return y


def _layer_tail_kernel(x_ref, gate_ref, yc_ref, yf_ref, ys_ref, wb_ref, wo_ref, gx_ref, wq_ref, kT_ref, v_ref,
                       wxo_ref, gf_ref, wg_ref, wu_ref, wd_ref, gl_ref, o_ref, m_scr, *, final_norm):
    x = _merge_tile(x_ref[...], gate_ref, (yc_ref[...], yf_ref[...], ys_ref[...]), wb_ref, wo_ref, m_scr)
    x = _xattn_tile(x, gx_ref, wq_ref, kT_ref, v_ref, wxo_ref)
    x = _swiglu_tile(x, gf_ref, wg_ref, wu_ref, wd_ref)
    o_ref[...] = _rms(x, gl_ref[...]) if final_norm else x


def _layer_tail(x, pn, y_fox, y_swa, w_branch, w_out, gx, wq, kT, v, wxo, gf, wg, wu, wd, g_last, final_norm,
                layer, seq, tm):
    n, d = x.shape
    tiles = seq // tm
    vec = _const_spec((1, d))
    return pl.pallas_call(
        functools.partial(_layer_tail_kernel, final_norm=final_norm),
        grid=(n // tm,),
        in_specs=[
            pl.BlockSpec((tm, d), lambda i: (i, 0)),
            pl.BlockSpec((tm, 3 * D_MODEL), lambda i: (i, GATE_OFF // (3 * D_MODEL))),
            pl.BlockSpec((tm, BRANCH), lambda i: (i, CONV_OFF // BRANCH)),
            pl.BlockSpec((tm, BRANCH), lambda i: (i, 0)),
            pl.BlockSpec((tm, BRANCH), lambda i: (i, 0)),
            _const_spec(w_branch.shape), _layer_spec(w_out.shape, layer),
            vec, _layer_spec(wq.shape, layer),
            pl.BlockSpec((1,) + kT.shape[1:], lambda i: (i // tiles, 0, 0)),
            pl.BlockSpec((1,) + v.shape[1:], lambda i: (i // tiles, 0, 0)),
            _layer_spec(wxo.shape, layer),
            vec, _layer_spec(wg.shape, layer), _layer_spec(wu.shape, layer), _layer_spec(wd.shape, layer), vec,
        ],
        out_specs=pl.BlockSpec((tm, d), lambda i: (i, 0)),
        out_shape=jax.ShapeDtypeStruct((n, d), F32),
        scratch_shapes=[pltpu.VMEM((tm, d), BF16)],
        compiler_params=_params("arbitrary", vmem_limit=TAIL_VMEM_LIMIT_BYTES),
        name="layer_tail",
    )(x, pn, pn, y_fox, y_swa, w_branch, w_out, gx.reshape(1, d), wq, kT, v, wxo,
      gf.reshape(1, d), wg, wu, wd, g_last.reshape(1, d))


def _permute_heads(w, perm, axis):
    shape = w.shape
    split = shape[:axis] + (FOX_HEADS, HEAD_DIM) + shape[axis + 1:]
    return jnp.take(w.reshape(split), perm, axis=axis).reshape(shape)


def _pack_in_proj(w_in, w_in_bf16, l, head_perm):
    sizes = [3 * BRANCH] + [BRANCH] * 3 + [FOX_HEADS] + [BRANCH, SWA_KV * HEAD_DIM, SWA_KV * HEAD_DIM] + [3 * D_MODEL]
    offs = np.concatenate([[0], np.cumsum(sizes)])
    f_q, f_v, s_q, s_v = (w_in[l, :, offs[t]:offs[t + 1]] for t in (1, 3, 5, 7))
    wc, f_k, f_g, s_k, gates = (w_in_bf16[l, :, offs[t]:offs[t + 1]] for t in (0, 2, 4, 6, 8))
    qscale = HEAD_DIM ** -0.5 * LOG2E
    f_q, s_q = (f_q * qscale).astype(BF16), (s_q * qscale).astype(BF16)
    f_v, s_v = f_v.astype(BF16), s_v.astype(BF16)
    f_q, f_k, f_v = (_permute_heads(w, head_perm, 1) for w in (f_q, f_k, f_v))
    f_g = jnp.take(f_g, head_perm, axis=1)
    wt = jnp.concatenate([f_q, f_v, s_q, s_v], axis=1).T
    wfg = jnp.concatenate([jnp.pad(f_g, ((0, 0), (0, LANES - FOX_HEADS))), s_k], axis=1)
    return gates, wc, f_k, wt, wfg


def kernel(x, mem, mix_norm_g, w_in, forget_bias, conv_w, sink, w_branch, w_mix_out, rel_bias,
           xattn_norm_g, mem_norm_g, w_xq, w_xkv, w_xo, ffn_norm_g, w_ffn_gate, w_ffn_up, w_ffn_down,
           final_norm_g):
    b, s, d = x.shape
    n = b * s
    depth = w_in.shape[0]
    assert d == D_MODEL and s % FOX_TQ == 0 and s % ROW_TILE == 0 and FOX_TQ == 2 * ROW_TILE
    assert s % (SWA_BLOCKS_PER_STEP * WINDOW) == 0 and SWA_BLOCKS_PER_STEP % 2 == 0
    mem_len = mem.shape[1]
    xf = x.reshape(n, d)
    memf = mem.reshape(b * mem_len, d)
    swa_bias = _swa_bias_table(rel_bias)
    w_in_bf16 = w_in.astype(BF16)
    w_branch_bf16 = w_branch.astype(BF16)
    w_out, wxkv, wxo = w_mix_out.astype(BF16), w_xkv.astype(BF16), w_xo.astype(BF16)
    wq = (w_xq * (X_HEAD_DIM ** -0.5 * LOG2E)).astype(BF16)
    wg, wu, wd = w_ffn_gate.astype(BF16), w_ffn_up.astype(BF16), w_ffn_down.astype(BF16)
    for l in range(depth):
        head_perm = jnp.argsort(forget_bias[l])
        wn, wc, wk, wt, wfg = _pack_in_proj(w_in, w_in_bf16, l, head_perm)
        w_br = w_branch_bf16[l].at[1].set(_permute_heads(w_branch_bf16[l, 1], head_perm, 0))
        pn, pt, k_aug, st = _in_proj(xf, mix_norm_g[l], wn, wc, wk, wt, wfg, conv_w[l],
                                     jnp.take(forget_bias[l], head_perm), seq=s, tm=ROW_TILE)
        st = st.reshape(b, s // ROW_TILE, SUBLANES, LANES)
        k_sq = jnp.broadcast_to(jnp.max(st[:, :, 1:2, 0:FOX_HEADS], axis=1, keepdims=True),
                                (b, s // ROW_TILE, 1, FOX_HEADS))
        stats = jnp.concatenate([st[:, :, 0:1, 0:FOX_HEADS], k_sq, st[:, :, None, 0:FOX_HEADS, FOX_HEADS]], axis=2)
        y_fox = _fox_attention(stats, pt, k_aug, b, s, tq=FOX_TQ, hp=FOX_HEADS_PER_STEP)
        y_swa = _swa_attention(sink[l], pt, pn, swa_bias, b, s, nblk=SWA_BLOCKS_PER_STEP)
        kv = _norm_proj(memf, mem_norm_g[l], wxkv, l)
        kT = kv[:, :d].reshape(b, mem_len, d).transpose(0, 2, 1)
        v = kv[:, d:].reshape(b, mem_len, d)
        xf = _layer_tail(xf, pn, y_fox, y_swa, w_br, w_out, xattn_norm_g[l], wq, kT, v, wxo,
                         ffn_norm_g[l], wg, wu, wd, final_norm_g, final_norm=(l == depth - 1),
                         layer=l, seq=s, tm=ROW_TILE)
    return xf.reshape(b, s, d)
```

```python
import functools
import math

import jax
import jax.numpy as jnp
import numpy as np
from jax import lax
from jax.experimental import pallas as pl
from jax.experimental.pallas import tpu as pltpu

F32 = jnp.float32
BF16 = jnp.bfloat16

D_MODEL = 1024
HEAD_DIM = 64
BRANCH = 512
FOX_HEADS = 8
SWA_HEADS = 8
SWA_KV = 2
SWA_GROUP = SWA_HEADS // SWA_KV
WINDOW = 128
N_BUCKETS = 32
X_HEADS = 4
X_HEAD_DIM = D_MODEL // X_HEADS
RMS_EPS = 1e-6
NEG = -1e30
LOG2E = math.log2(math.e)

LANES = 128
SUBLANES = 8
VMEM_LIMIT_BYTES = 56 * 1024 * 1024
FOX_VMEM_LIMIT_BYTES = 58 * 1024 * 1024
TAIL_VMEM_LIMIT_BYTES = 58 * 1024 * 1024
PITCH_PAD = LANES
MERGE_CHUNK = 256
FFN_CHUNK = 256

ROW_TILE = 512
FOX_TQ = 1024
FOX_HEADS_PER_STEP = 4
SWA_BLOCKS_PER_STEP = 16

GATE_OFF = 0
CONV_OFF = 3 * D_MODEL
SWAK_OFF = CONV_OFF + BRANCH
PN_COLS = SWAK_OFF + SWA_KV * HEAD_DIM
FOXQ_ROW = 0
FOXV_ROW = BRANCH
SWAQ_ROW = 2 * BRANCH
SWAV_ROW = 3 * BRANCH
PT_ROWS = SWAV_ROW + SWA_KV * HEAD_DIM

AUG_STRIDE = 8
AUG_EVEN = (HEAD_DIM, HEAD_DIM + AUG_STRIDE, HEAD_DIM + 2 * AUG_STRIDE)
AUG_ODD = (0, AUG_STRIDE, 2 * AUG_STRIDE)
V_ROWS = 80
SKIP_BITS = 160.0
SKIP_SLACK = 1.05


def _params(*sem, vmem_limit=VMEM_LIMIT_BYTES):
    return pltpu.CompilerParams(dimension_semantics=sem, vmem_limit_bytes=vmem_limit)


def _rms(x, g):
    return x * lax.rsqrt(jnp.mean(x * x, axis=-1, keepdims=True) + RMS_EPS) * g


def _const_spec(shape):
    nd = len(shape)
    return pl.BlockSpec(shape, lambda *_: (0,) * nd, pipeline_mode=pl.Buffered(1))


def _layer_spec(stacked_shape, layer):
    return pl.BlockSpec((None,) + tuple(stacked_shape[1:]), lambda *_: (layer, 0, 0), pipeline_mode=pl.Buffered(1))


def _one_hot_rows(shape, axis, positions):
    idx = lax.broadcasted_iota(jnp.int32, shape, axis)
    hit = idx == positions[0]
    for p in positions[1:]:
        hit = hit | (idx == p)
    return jnp.where(hit, 1.0, 0.0)


def _in_proj_kernel(x_ref, g_ref, wn_ref, wc_ref, wk_ref, wt_ref, wfg_ref, cw_ref, fb_ref,
                    pn_ref, pt_ref, ka_ref, st_ref, z_tail, c_tail, *, tiles_per_seq):
    tm = x_ref.shape[0]
    first_of_seq = pl.program_id(0) % tiles_per_seq == 0

    @pl.when(pl.program_id(0) == 0)
    def _():
        z_tail[...] = jnp.zeros(z_tail.shape, F32)
        c_tail[...] = jnp.zeros(c_tail.shape, F32)

    h = _rms(x_ref[...], g_ref[...]).astype(BF16)

    fg_swak = jnp.dot(h, wfg_ref[...], preferred_element_type=F32)
    fg = fg_swak[:, 0:LANES]
    pn_ref[:, SWAK_OFF:PN_COLS] = fg_swak[:, LANES:].astype(BF16)
    k_fox = jnp.dot(h, wk_ref[...], preferred_element_type=F32)
    lane = lax.broadcasted_iota(jnp.int32, (tm, LANES), 1)
    trow = lax.broadcasted_iota(jnp.int32, (tm, LANES), 0)
    c = jnp.where(lane < FOX_HEADS, jax.nn.log_sigmoid(fg + fb_ref[...]) * LOG2E, 0.0)
    d = 1
    while d < tm:
        c = c + jnp.where(trow >= d, pltpu.roll(c, d, axis=0), 0.0)
        d *= 2
    c = c + jnp.where(first_of_seq, 0.0, c_tail[0:1, :])
    c_tail[0:1, :] = c[tm - 1:tm, :]
    head_of_col = lax.broadcasted_iota(jnp.int32, (BRANCH, LANES), 0) // HEAD_DIM
    head_sel = jnp.where(head_of_col == lax.broadcasted_iota(jnp.int32, (BRANCH, LANES), 1), 1.0, 0.0).astype(BF16)
    k_sq = jnp.dot((k_fox * k_fox).astype(BF16), head_sel, preferred_element_type=F32)
    st_ref[...] = jnp.zeros(st_ref.shape, F32)
    st_ref[0:1, :] = c[tm - 1:tm, :]
    st_ref[1:2, :] = jnp.max(k_sq, axis=0, keepdims=True)
    c1 = c.astype(BF16).astype(F32)
    r1 = c - c1
    c2 = r1.astype(BF16).astype(F32)
    c3 = (r1 - c2).astype(BF16).astype(F32)
    packed = -jnp.where(lane < AUG_STRIDE, c1,
                        jnp.where(lane < 2 * AUG_STRIDE, pltpu.roll(c2, AUG_STRIDE, axis=1),
                                  pltpu.roll(c3, 2 * AUG_STRIDE, axis=1)))
    keep_even = _one_hot_rows((1, LANES), 1, AUG_EVEN)
    keep_odd = _one_hot_rows((1, LANES), 1, AUG_ODD)
    for hd in range(FOX_HEADS):
        k_pair = k_fox[:, (hd // 2) * LANES:(hd // 2 + 1) * LANES]
        if hd % 2 == 0:
            aug = pltpu.roll(packed, (AUG_EVEN[0] - hd) % LANES, axis=1) * keep_even
            out = jnp.where(lane < HEAD_DIM, k_pair, aug)
        else:
            aug = pltpu.roll(packed, (AUG_ODD[0] - hd) % LANES, axis=1) * keep_odd
            out = jnp.where(lane >= HEAD_DIM, k_pair, aug)
        ka_ref[0, hd] = out.astype(BF16)

    conv_b, conv_c, conv_u = (jnp.dot(h, wc_ref[:, t * BRANCH:(t + 1) * BRANCH], preferred_element_type=F32)
                              for t in range(3))
    z = conv_c * conv_u
    zh = jnp.where(first_of_seq, 0.0, z_tail[...])
    z_tail[...] = z[tm - SUBLANES:tm, :]
    row = lax.broadcasted_iota(jnp.int32, z.shape, 0)
    z1 = jnp.where(row == 0, zh[7:8], pltpu.roll(z, 1, axis=0))
    z2 = jnp.where(row == 0, zh[6:7], jnp.where(row == 1, zh[7:8], pltpu.roll(z, 2, axis=0)))
    cw = cw_ref[...]
    pn_ref[:, CONV_OFF:SWAK_OFF] = (conv_b * (cw[0:1] * z2 + cw[1:2] * z1 + cw[2:3] * z)).astype(BF16)

    for lo in range(0, CONV_OFF, D_MODEL):
        pn_ref[:, lo:lo + D_MODEL] = jnp.dot(h, wn_ref[:, lo:lo + D_MODEL], preferred_element_type=F32).astype(BF16)
    pt = lax.dot_general(wt_ref[...], h, (((1,), (1,)), ((), ())), preferred_element_type=F32)
    pt_ref[...] = pt.astype(BF16)
    for hd in range(FOX_HEADS):
        q_h = pt[FOXQ_ROW + hd * HEAD_DIM:FOXQ_ROW + (hd + 1) * HEAD_DIM, :]
        q_sq = jnp.sum(q_h * q_h, axis=0, keepdims=True)
        st_ref[hd:hd + 1, FOX_HEADS:FOX_HEADS + 1] = jnp.max(q_sq, axis=1, keepdims=True)


def _in_proj(x, g, wn, wc, wk, wt, wfg, conv_w, forget_bias, seq, tm):
    n, d = x.shape
    tiles = seq // tm
    fb = jnp.pad(forget_bias, (0, LANES - FOX_HEADS)).reshape(1, LANES)
    return pl.pallas_call(
        functools.partial(_in_proj_kernel, tiles_per_seq=tiles),
        grid=(n // tm,),
        in_specs=[
            pl.BlockSpec((tm, d), lambda i: (i, 0)),
            _const_spec((1, d)), _const_spec(wn.shape), _const_spec(wc.shape), _const_spec(wk.shape),
            _const_spec(wt.shape), _const_spec(wfg.shape), _const_spec(conv_w.shape), _const_spec((1, LANES)),
        ],
        out_specs=[
            pl.BlockSpec((tm, PN_COLS), lambda i: (i, 0)),
            pl.BlockSpec((PT_ROWS, tm), lambda i: (0, i)),
            pl.BlockSpec((1, FOX_HEADS, tm, LANES), lambda i: (i // tiles, 0, i % tiles, 0)),
            pl.BlockSpec((SUBLANES, LANES), lambda i: (i, 0)),
        ],
        out_shape=[jax.ShapeDtypeStruct((n, PN_COLS), BF16), jax.ShapeDtypeStruct((PT_ROWS, n), BF16),
                   jax.ShapeDtypeStruct((n // seq, FOX_HEADS, seq, LANES), BF16),
                   jax.ShapeDtypeStruct((n // tm * SUBLANES, LANES), F32)],
        scratch_shapes=[pltpu.VMEM((SUBLANES, BRANCH), F32), pltpu.VMEM((SUBLANES, LANES), F32)],
        compiler_params=_params("arbitrary"),
        name="in_proj",
    )(x, g.reshape(1, d), wn, wc, wk, wt, wfg, conv_w, fb)


def _norm_proj_kernel(x_ref, g_ref, w_ref, o_ref):
    h = _rms(x_ref[...], g_ref[...]).astype(BF16)
    o_ref[...] = jnp.dot(h, w_ref[...], preferred_element_type=F32).astype(BF16)


def _norm_proj(x, g, w, layer):
    n, d = x.shape
    cols = w.shape[2]
    return pl.pallas_call(
        _norm_proj_kernel,
        grid=(1,),
        in_specs=[_const_spec((n, d)), _const_spec((1, d)), _layer_spec(w.shape, layer)],
        out_specs=pl.BlockSpec((n, cols), lambda i: (0, 0)),
        out_shape=jax.ShapeDtypeStruct((n, cols), BF16),
        compiler_params=_params("arbitrary"),
        name="mem_proj",
    )(x, g.reshape(1, d), w)


def _fox_kernel(st_ref, q_ref, qn_ref, k0_ref, kn_ref, v_ref, o_ref, q_aug, k_all, v_aug, s_buf, mt_buf, p_buf,
                acc_ref, first_ref, *, tq, tk, hp):
    bi = pl.program_id(0)
    grp = pl.program_id(1)
    i = pl.program_id(2)
    seq = k_all.shape[1]
    cur = i % 2
    heads = tuple(range(hp))

    def first_live_pair(qi):
        c_q = jnp.maximum(2 * qi - 1, 0)
        first = qi
        for hh in heads:
            h = grp * hp + hh
            q_sq = jnp.maximum(st_ref[bi, 2 * qi, 2, h], st_ref[bi, 2 * qi + 1, 2, h])
            k_sq = st_ref[bi, 0, 1, h]
            bound_sq = (4.0 * SKIP_SLACK * SKIP_SLACK) * q_sq * k_sq

            def count(u, n):
                room = -SKIP_BITS - (st_ref[bi, c_q, 0, h] - st_ref[bi, 2 * u + 1, 0, h])
                dead = jnp.logical_and(room > 0.0, bound_sq < room * room)
                return n + dead.astype(jnp.int32)

            first = jnp.minimum(first, lax.fori_loop(0, qi, count, jnp.int32(0)))
        return first

    def build_queries(src_ref, qslot):
        qrow = lax.broadcasted_iota(jnp.int32, (LANES, tq), 0)
        for pr in range(hp // 2):
            qb = src_ref[pr * LANES:(pr + 1) * LANES, :].astype(F32)
            q_aug[qslot, 2 * pr] = jnp.where(qrow < HEAD_DIM, qb, _one_hot_rows((LANES, tq), 0, AUG_EVEN)).astype(BF16)
            q_aug[qslot, 2 * pr + 1] = jnp.where(qrow >= HEAD_DIM, qb,
                                                 _one_hot_rows((LANES, tq), 0, AUG_ODD)).astype(BF16)

    def scores(t, slot, qslot):
        off = pl.multiple_of(t * tk, tk)
        for hh in heads:
            s = jnp.dot(k_all[hh, pl.ds(off, tk), :], q_aug[qslot, hh], preferred_element_type=F32)
            s_buf[hh, slot, :, 0:tq] = s
            mt_buf[hh, slot, 0:1, :] = jnp.max(s, axis=0, keepdims=True)

    def softmax(slot, ms, mask=None):
        out = []
        for hh in heads:
            s = s_buf[hh, slot, :, 0:tq]
            if mask is None:
                tile_max = mt_buf[hh, slot, 0:1, :]
            else:
                s = jnp.where(mask, s, NEG)
                tile_max = jnp.max(s, axis=0, keepdims=True)
            m_new = jnp.maximum(ms[hh], tile_max)
            p_buf[hh, slot, :, 0:tq] = jnp.exp2(s - m_new).astype(BF16)
            out.append((m_new, jnp.exp2(ms[hh] - m_new)))
        return tuple(o[0] for o in out), tuple(o[1] for o in out)

    def values(t, slot, alphas):
        off = pl.multiple_of(jnp.maximum(t, 0) * tk, tk)
        for hh in heads:
            pv = jnp.dot(v_aug[hh, :, pl.ds(off, tk)], p_buf[hh, slot, :, 0:tq], preferred_element_type=F32)
            acc_ref[hh] = alphas[hh] * acc_ref[hh] + pv

    v_off = pl.multiple_of(i * tq, tq)
    ones_row = _one_hot_rows((V_ROWS - HEAD_DIM, tq), 0, (0,)).astype(BF16)
    for hh in heads:
        v_aug[hh, 0:HEAD_DIM, pl.ds(v_off, tq)] = v_ref[hh * HEAD_DIM:(hh + 1) * HEAD_DIM, :]
        v_aug[hh, HEAD_DIM:V_ROWS, pl.ds(v_off, tq)] = ones_row

    @pl.when(i == 0)
    def _():
        for hh in heads:
            k_all[hh, 0:tq, :] = k0_ref[0, hh]
            p_buf[hh, 1, :, 0:tq] = jnp.zeros((tk, tq), BF16)
            acc_ref[hh] = jnp.zeros(acc_ref.shape[1:], F32)
        build_queries(q_ref, 0)
        scores(0, 0, 0)

    def pair(u, carry):
        ms, alpha_prev = carry
        t = 2 * u
        scores(t + 1, 1, cur)
        ms, alpha0 = softmax(0, ms)
        values(t - 1, 1, alpha_prev)
        scores(t + 2, 0, cur)
        ms, alpha1 = softmax(1, ms)
        values(t, 0, alpha0)
        return ms, alpha1

    carry = (tuple(jnp.full((1, tq), NEG, F32) for _ in heads), tuple(jnp.ones((1, tq), F32) for _ in heads))
    first_now = jnp.where(i == 0, 0, first_ref[0])
    first_next = first_live_pair(jnp.minimum(i + 1, seq // tq - 1))
    ms, alpha_prev = lax.fori_loop(first_now, i, pair, carry)

    t = 2 * i
    key = lax.broadcasted_iota(jnp.int32, (tk, tq), 0)
    qry = lax.broadcasted_iota(jnp.int32, (tk, tq), 1)
    off_b = pl.multiple_of((t + 1) * tk, tk)
    tri = lax.broadcasted_iota(jnp.int32, (tk, tk), 0) <= lax.broadcasted_iota(jnp.int32, (tk, tk), 1)
    s_b = [jnp.where(tri, jnp.dot(k_all[hh, pl.ds(off_b, tk), :], q_aug[cur, hh, :, tk:],
                                  preferred_element_type=F32), NEG) for hh in heads]
    ms, alpha0 = softmax(0, ms, key <= qry)
    kn_off = pl.multiple_of(jnp.minimum(i + 1, seq // tq - 1) * tq, tq)
    for hh in heads:
        k_all[hh, pl.ds(kn_off, tq), :] = kn_ref[0, hh]
    build_queries(qn_ref, 1 - cur)
    first_ref[0] = first_next
    scores(2 * first_next, 0, 1 - cur)
    values(t - 1, 1, alpha_prev)
    values(t, 0, alpha0)
    outs = []
    for hh in heads:
        mt_buf[hh, 1, 0:1, :] = ms[hh]
        m_old = mt_buf[hh, 1, 0:1, tk:]
        m_new = jnp.maximum(m_old, jnp.max(s_b[hh], axis=0, keepdims=True))
        p_b = jnp.exp2(s_b[hh] - m_new).astype(BF16)
        pv = jnp.dot(v_aug[hh, :, pl.ds(off_b, tk)], p_b, preferred_element_type=F32)
        acc_ref[hh, :, tk:] = jnp.exp2(m_old - m_new) * acc_ref[hh, :, tk:] + pv
        acc = acc_ref[hh]
        outs.append(acc[:HEAD_DIM] / acc[HEAD_DIM:HEAD_DIM + 1])
    o_ref[...] = jnp.concatenate(outs, axis=0).T.astype(BF16)


def _fox_attention(stats, pt, k_aug, b, s, tq, hp):
    tk = tq // 2
    per = s // tq
    rows = hp * HEAD_DIM
    qblk = FOXQ_ROW // rows
    vblk = FOXV_ROW // rows
    return pl.pallas_call(
        functools.partial(_fox_kernel, tq=tq, tk=tk, hp=hp),
        grid=(b, FOX_HEADS // hp, per),
        in_specs=[
            pl.BlockSpec(memory_space=pltpu.SMEM),
            pl.BlockSpec((rows, tq), lambda bi, g, i: (qblk + g, bi * per + i)),
            pl.BlockSpec((rows, tq), lambda bi, g, i: (qblk + g, bi * per + jnp.minimum(i + 1, per - 1))),
            pl.BlockSpec((1, hp, tq, LANES), lambda bi, g, i: (bi, g, 0, 0)),
            pl.BlockSpec((1, hp, tq, LANES), lambda bi, g, i: (bi, g, jnp.minimum(i + 1, per - 1), 0)),
            pl.BlockSpec((rows, tq), lambda bi, g, i: (vblk + g, bi * per + i)),
        ],
        out_specs=pl.BlockSpec((tq, rows), lambda bi, g, i: (bi * per + i, g)),
        out_shape=jax.ShapeDtypeStruct((b * s, BRANCH), BF16),
        scratch_shapes=[
            pltpu.VMEM((2, hp, LANES, tq), BF16),
            pltpu.VMEM((hp, s, LANES), BF16),
            pltpu.VMEM((hp, V_ROWS, s), BF16),
            pltpu.VMEM((hp, 2, tk, tq + PITCH_PAD), F32),
            pltpu.VMEM((hp, 2, SUBLANES, tq), F32),
            pltpu.VMEM((hp, 2, tk, tq + PITCH_PAD), BF16),
            pltpu.VMEM((hp, V_ROWS, tq), F32),
            pltpu.SMEM((1,), jnp.int32),
        ],
        compiler_params=_params("arbitrary", "arbitrary", "arbitrary", vmem_limit=FOX_VMEM_LIMIT_BYTES),
        name="fox_attention",
    )(stats, pt, pt, k_aug, k_aug, pt)


def _swa_bias_kernel(rb_ref, bucket_ref, win_ref, o_ref):
    h = pl.program_id(0)
    bucket = bucket_ref[...]
    bias = jnp.zeros(bucket.shape, F32)
    for b in range(N_BUCKETS):
        bias = jnp.where(bucket == b, rb_ref[b, h] * LOG2E, bias)
    o_ref[0, 0] = jnp.where(win_ref[0] > 0, bias, NEG)
    o_ref[1, 0] = jnp.where(win_ref[1] > 0, bias, NEG)


def _swa_bias_table(rel_bias):
    tq = np.arange(WINDOW)[None, :]
    sk = np.arange(2 * WINDOW)[:, None]
    dist = WINDOW + tq - sk
    n = jnp.maximum(jnp.asarray(dist, jnp.int32), 0)
    max_exact = N_BUCKETS // 2
    large = max_exact + (jnp.log(jnp.maximum(n, 1).astype(F32) / max_exact)
                         / math.log(WINDOW / max_exact) * (N_BUCKETS - max_exact)).astype(jnp.int32)
    bucket = jnp.where(n < max_exact, n, jnp.minimum(large, N_BUCKETS - 1))
    in_window = (dist >= 0) & (dist < WINDOW)
    win = np.stack([in_window, in_window & (sk >= WINDOW)]).astype(np.int32)
    full = lambda shape: pl.BlockSpec(shape, lambda h: (0,) * len(shape))
    return pl.pallas_call(
        _swa_bias_kernel,
        grid=(SWA_HEADS,),
        in_specs=[pl.BlockSpec(memory_space=pltpu.SMEM), full(bucket.shape), full(win.shape)],
        out_specs=pl.BlockSpec((2, 1, 2 * WINDOW, WINDOW), lambda h: (0, h // SWA_GROUP, 0, h % SWA_GROUP)),
        out_shape=jax.ShapeDtypeStruct((2, SWA_KV, 2 * WINDOW, SWA_GROUP * WINDOW), F32),
        compiler_params=_params("arbitrary"),
        name="swa_bias_table",
    )(rel_bias, bucket, jnp.asarray(win))


def _swa_kernel(sink_ref, q_ref, kp_ref, kc_ref, vp_ref, vc_ref, bias_ref, o_ref,
                k_buf, v_buf, s_buf, mt_buf, p_buf, dn_buf, *, nblk):
    gw = SWA_GROUP * WINDOW
    ts = nblk * WINDOW
    first = pl.program_id(1) == 0
    k_buf[0:WINDOW, :] = kp_ref[...]
    k_buf[WINDOW:WINDOW + ts, :] = kc_ref[...]
    ones_row = _one_hot_rows((V_ROWS - HEAD_DIM, ts + WINDOW), 0, (0,)).astype(BF16)
    for kv in range(SWA_KV):
        rows = slice(kv * HEAD_DIM, (kv + 1) * HEAD_DIM)
        v_buf[kv, 0:HEAD_DIM, 0:WINDOW] = vp_ref[rows, :]
        v_buf[kv, 0:HEAD_DIM, WINDOW:WINDOW + ts] = vc_ref[rows, :]
        v_buf[kv, HEAD_DIM:V_ROWS, :] = ones_row
    glane = lax.broadcasted_iota(jnp.int32, (1, gw), 1) // WINDOW
    zeros = jnp.zeros((HEAD_DIM, gw), BF16)
    sinks = []
    for kv in range(SWA_KV):
        sink = jnp.zeros((1, gw), F32)
        for g in range(SWA_GROUP):
            sink = jnp.where(glane == g, sink_ref[kv * SWA_GROUP + g] * LOG2E, sink)
        sinks.append(sink)

    def scores(blk, slot):
        lo = pl.multiple_of(blk * WINDOW, WINDOW)
        sel = jnp.logical_and(first, blk == 0).astype(jnp.int32)
        keys = k_buf[pl.ds(lo, 2 * WINDOW), :]
        for kv in range(SWA_KV):
            qg = jnp.concatenate([q_ref[(kv * SWA_GROUP + g) * HEAD_DIM:(kv * SWA_GROUP + g + 1) * HEAD_DIM,
                                        pl.ds(lo, WINDOW)] for g in range(SWA_GROUP)], axis=1)
            q_pad = jnp.concatenate([qg, zeros] if kv == 0 else [zeros, qg], axis=0)
            s = jnp.dot(keys, q_pad, preferred_element_type=F32) + bias_ref[sel, kv]
            s_buf[kv, slot, :, 0:gw] = s
            mt_buf[kv, slot, 0:1, :] = jnp.max(s, axis=0, keepdims=True)

    def softmax(slot):
        for kv in range(SWA_KV):
            m = jnp.maximum(mt_buf[kv, slot, 0:1, :], sinks[kv])
            p = jnp.exp2(s_buf[kv, slot, :, 0:gw] - m)
            dn_buf[kv, slot, 0:1, :] = jnp.exp2(sinks[kv] - m)
            p_buf[kv, slot, :, 0:gw] = p.astype(BF16)

    def values(blk, slot):
        lo = pl.multiple_of(blk * WINDOW, WINDOW)
        for kv in range(SWA_KV):
            pv = jnp.dot(v_buf[kv, :, pl.ds(lo, 2 * WINDOW)], p_buf[kv, slot, :, 0:gw],
                         preferred_element_type=F32)
            o = pv[:HEAD_DIM] / (pv[HEAD_DIM:HEAD_DIM + 1] + dn_buf[kv, slot, 0:1, :])
            for pr in range(SWA_GROUP // 2):
                two = jnp.concatenate([o[:, (2 * pr) * WINDOW:(2 * pr + 1) * WINDOW],
                                       o[:, (2 * pr + 1) * WINDOW:(2 * pr + 2) * WINDOW]], axis=0)
                col = (kv * (SWA_GROUP // 2) + pr) * LANES
                o_ref[pl.ds(lo, WINDOW), col:col + LANES] = two.T.astype(BF16)

    scores(jnp.int32(0), 0)
    scores(jnp.int32(1), 1)
    softmax(0)

    def pair(u, carry):
        blk = 2 * u + 1
        scores(blk + 1, 0)
        softmax(1)
        values(blk - 1, 0)
        scores(blk + 2, 1)
        softmax(0)
        values(blk, 1)
        return carry

    lax.fori_loop(0, nblk // 2 - 1, pair, 0)
    softmax(1)
    values(jnp.int32(nblk - 2), 0)
    values(jnp.int32(nblk - 1), 1)


def _swa_attention(sink, pt, pn, bias, b, s, nblk):
    ts = nblk * WINDOW
    per = s // ts
    perw = s // WINDOW
    qblk = SWAQ_ROW // BRANCH
    vblk = SWAV_ROW // LANES
    kblk = SWAK_OFF // LANES
    prev = lambda bi, i: bi * perw + jnp.maximum(i * nblk - 1, 0)
    return pl.pallas_call(
        functools.partial(_swa_kernel, nblk=nblk),
        grid=(b, per),
        in_specs=[
            pl.BlockSpec(memory_space=pltpu.SMEM),
            pl.BlockSpec((BRANCH, ts), lambda bi, i: (qblk, bi * per + i)),
            pl.BlockSpec((WINDOW, LANES), lambda bi, i: (prev(bi, i), kblk)),
            pl.BlockSpec((ts, LANES), lambda bi, i: (bi * per + i, kblk)),
            pl.BlockSpec((LANES, WINDOW), lambda bi, i: (vblk, prev(bi, i))),
            pl.BlockSpec((LANES, ts), lambda bi, i: (vblk, bi * per + i)),
            _const_spec(bias.shape),
        ],
        out_specs=pl.BlockSpec((ts, BRANCH), lambda bi, i: (bi * per + i, 0)),
        out_shape=jax.ShapeDtypeStruct((b * s, BRANCH), BF16),
        scratch_shapes=[
            pltpu.VMEM((ts + WINDOW, LANES), BF16),
            pltpu.VMEM((SWA_KV, V_ROWS, ts + WINDOW), BF16),
            pltpu.VMEM((SWA_KV, 2, 2 * WINDOW, SWA_GROUP * WINDOW + PITCH_PAD), F32),
            pltpu.VMEM((SWA_KV, 2, SUBLANES, SWA_GROUP * WINDOW), F32),
            pltpu.VMEM((SWA_KV, 2, 2 * WINDOW, SWA_GROUP * WINDOW + PITCH_PAD), BF16),
            pltpu.VMEM((SWA_KV, 2, SUBLANES, SWA_GROUP * WINDOW), F32),
        ],
        compiler_params=_params("arbitrary", "arbitrary"),
        name="swa_attention",
    )(sink, pt, pn, pn, pt, pt, bias)


def _merge_tile(x, gate_ref, branches, wb_ref, wo_ref, m_scr):
    for lo in range(0, D_MODEL, MERGE_CHUNK):
        gates = [jax.nn.sigmoid(gate_ref[:, b * D_MODEL + lo:b * D_MODEL + lo + MERGE_CHUNK].astype(F32))
                 for b in range(len(branches))]
        merged = None
        for b, y in enumerate(branches):
            term = gates[b] * jnp.dot(y, wb_ref[b, :, lo:lo + MERGE_CHUNK], preferred_element_type=F32)
            merged = term if merged is None else merged + term
        m_scr[:, lo:lo + MERGE_CHUNK] = merged.astype(BF16)
    return x + jnp.dot(m_scr[...], wo_ref[...], preferred_element_type=F32)


def _xattn_tile(x, g_ref, wq_ref, kT_ref, v_ref, wo_ref):
    q = jnp.dot(_rms(x, g_ref[...]).astype(BF16), wq_ref[...], preferred_element_type=F32).astype(BF16)
    sls = [slice(h * X_HEAD_DIM, (h + 1) * X_HEAD_DIM) for h in range(X_HEADS)]
    scores = [jnp.dot(q[:, sl], kT_ref[0, sl, :], preferred_element_type=F32) for sl in sls]
    probs = [jnp.exp2(s - jnp.max(s, axis=-1, keepdims=True)) for s in scores]
    denoms = [jnp.sum(p, axis=-1, keepdims=True) for p in probs]
    outs = [jnp.dot(p.astype(BF16), v_ref[0, :, sl], preferred_element_type=F32) for p, sl in zip(probs, sls)]
    heads = [(o / d).astype(BF16) for o, d in zip(outs, denoms)]
    return x + jnp.dot(jnp.concatenate(heads, axis=1), wo_ref[...], preferred_element_type=F32)


def _swiglu_tile(x, g_ref, wg_ref, wu_ref, wd_ref):
    xn = _rms(x, g_ref[...]).astype(BF16)
    y = x
    for lo in range(0, wg_ref.shape[1], FFN_CHUNK):
        gate = jnp.dot(xn, wg_ref[:, lo:lo + FFN_CHUNK], preferred_element_type=F32)
        up = jnp.dot(xn, wu_ref[:, lo:lo + FFN_CHUNK], preferred_element_type=F32)
        hidden = (gate * jax.nn.sigmoid(gate) * up).astype(BF16)
        y = y + jnp.dot(hidden, wd_ref[lo:lo + FFN_CHUNK, :], preferred_element_type=F32)
    return y


def _layer_tail_kernel(x_ref, gate_ref, yc_ref, yf_ref, ys_ref, wb_ref, wo_ref, gx_ref, wq_ref, kT_ref, v_ref,
                       wxo_ref, gf_ref, wg_hbm, wu_hbm, wd_hbm, gl_ref, o_ref, m_scr, wg_ref, wu_ref, wd_ref, ffn_sem,
                       *, final_norm, layer):
    copies = [pltpu.make_async_copy(src.at[layer], dst, ffn_sem.at[j])
              for j, (src, dst) in enumerate(((wg_hbm, wg_ref), (wu_hbm, wu_ref), (wd_hbm, wd_ref)))]

    @pl.when(pl.program_id(0) == 0)
    def _():
        for cp in copies:
            cp.start()

    x = _merge_tile(x_ref[...], gate_ref, (yc_ref[...], yf_ref[...], ys_ref[...]), wb_ref, wo_ref, m_scr)
    x = _xattn_tile(x, gx_ref, wq_ref, kT_ref, v_ref, wxo_ref)

    @pl.when(pl.program_id(0) == 0)
    def _():
        for cp in copies:
            cp.wait()

    x = _swiglu_tile(x, gf_ref, wg_ref, wu_ref, wd_ref)
    o_ref[...] = _rms(x, gl_ref[...]) if final_norm else x


def _layer_tail(x, pn, y_fox, y_swa, w_branch, w_out, gx, wq, kT, v, wxo, gf, wg, wu, wd, g_last, final_norm,
                layer, seq, tm):
    n, d = x.shape
    tiles = seq // tm
    vec = _const_spec((1, d))
    return pl.pallas_call(
        functools.partial(_layer_tail_kernel, final_norm=final_norm, layer=layer),
        grid=(n // tm,),
        in_specs=[
            pl.BlockSpec((tm, d), lambda i: (i, 0)),
            pl.BlockSpec((tm, 3 * D_MODEL), lambda i: (i, GATE_OFF // (3 * D_MODEL))),
            pl.BlockSpec((tm, BRANCH), lambda i: (i, CONV_OFF // BRANCH)),
            pl.BlockSpec((tm, BRANCH), lambda i: (i, 0)),
            pl.BlockSpec((tm, BRANCH), lambda i: (i, 0)),
            _const_spec(w_branch.shape), _layer_spec(w_out.shape, layer),
            vec, _layer_spec(wq.shape, layer),
            pl.BlockSpec((1,) + kT.shape[1:], lambda i: (i // tiles, 0, 0)),
            pl.BlockSpec((1,) + v.shape[1:], lambda i: (i // tiles, 0, 0)),
            _layer_spec(wxo.shape, layer),
            vec, pl.BlockSpec(memory_space=pl.ANY), pl.BlockSpec(memory_space=pl.ANY),
            pl.BlockSpec(memory_space=pl.ANY), vec,
        ],
        out_specs=pl.BlockSpec((tm, d), lambda i: (i, 0)),
        out_shape=jax.ShapeDtypeStruct((n, d), F32),
        scratch_shapes=[pltpu.VMEM((tm, d), BF16), pltpu.VMEM(wg.shape[1:], BF16), pltpu.VMEM(wu.shape[1:], BF16),
                        pltpu.VMEM(wd.shape[1:], BF16), pltpu.SemaphoreType.DMA((3,))],
        compiler_params=_params("arbitrary", vmem_limit=TAIL_VMEM_LIMIT_BYTES),
        name="layer_tail",
    )(x, pn, pn, y_fox, y_swa, w_branch, w_out, gx.reshape(1, d), wq, kT, v, wxo,
      gf.reshape(1, d), wg, wu, wd, g_last.reshape(1, d))


def _permute_heads(w, perm, axis):
    shape = w.shape
    split = shape[:axis] + (FOX_HEADS, HEAD_DIM) + shape[axis + 1:]
    return jnp.take(w.reshape(split), perm, axis=axis).reshape(shape)


def _pack_in_proj(w_in, w_in_bf16, l, head_perm):
    sizes = [3 * BRANCH] + [BRANCH] * 3 + [FOX_HEADS] + [BRANCH, SWA_KV * HEAD_DIM, SWA_KV * HEAD_DIM] + [3 * D_MODEL]
    offs = np.concatenate([[0], np.cumsum(sizes)])
    f_q, f_v, s_q, s_v = (w_in[l, :, offs[t]:offs[t + 1]] for t in (1, 3, 5, 7))
    wc, f_k, f_g, s_k, gates = (w_in_bf16[l, :, offs[t]:offs[t + 1]] for t in (0, 2, 4, 6, 8))
    qscale = HEAD_DIM ** -0.5 * LOG2E
    f_q, s_q = (f_q * qscale).astype(BF16), (s_q * qscale).astype(BF16)
    f_v, s_v = f_v.astype(BF16), s_v.astype(BF16)
    f_q, f_k, f_v = (_permute_heads(w, head_perm, 1) for w in (f_q, f_k, f_v))
    f_g = jnp.take(f_g, head_perm, axis=1)
    wt = jnp.concatenate([f_q, f_v, s_q, s_v], axis=1).T
    wfg = jnp.concatenate([jnp.pad(f_g, ((0, 0), (0, LANES - FOX_HEADS))), s_k], axis=1)
    return gates, wc, f_k, wt, wfg


def kernel(x, mem, mix_norm_g, w_in, forget_bias, conv_w, sink, w_branch, w_mix_out, rel_bias,
           xattn_norm_g, mem_norm_g, w_xq, w_xkv, w_xo, ffn_norm_g, w_ffn_gate, w_ffn_up, w_ffn_down,
           final_norm_g):
    b, s, d = x.shape
    n = b * s
    depth = w_in.shape[0]
    assert d == D_MODEL and s % FOX_TQ == 0 and s % ROW_TILE == 0 and FOX_TQ == 2 * ROW_TILE
    assert s % (SWA_BLOCKS_PER_STEP * WINDOW) == 0 and SWA_BLOCKS_PER_STEP % 2 == 0
    mem_len = mem.shape[1]
    xf = x.reshape(n, d)
    memf = mem.reshape(b * mem_len, d)
    swa_bias = _swa_bias_table(rel_bias)
    w_in_bf16 = w_in.astype(BF16)
    w_branch_bf16 = w_branch.astype(BF16)
    w_out, wxkv, wxo = w_mix_out.astype(BF16), w_xkv.astype(BF16), w_xo.astype(BF16)
    wq = (w_xq * (X_HEAD_DIM ** -0.5 * LOG2E)).astype(BF16)
    wg, wu, wd = w_ffn_gate.astype(BF16), w_ffn_up.astype(BF16), w_ffn_down.astype(BF16)
    for l in range(depth):
        head_perm = jnp.argsort(forget_bias[l])
        wn, wc, wk, wt, wfg = _pack_in_proj(w_in, w_in_bf16, l, head_perm)
        w_br = w_branch_bf16[l].at[1].set(_permute_heads(w_branch_bf16[l, 1], head_perm, 0))
        pn, pt, k_aug, st = _in_proj(xf, mix_norm_g[l], wn, wc, wk, wt, wfg, conv_w[l],
                                     jnp.take(forget_bias[l], head_perm), seq=s, tm=ROW_TILE)
        st = st.reshape(b, s // ROW_TILE, SUBLANES, LANES)
        k_sq = jnp.broadcast_to(jnp.max(st[:, :, 1:2, 0:FOX_HEADS], axis=1, keepdims=True),
                                (b, s // ROW_TILE, 1, FOX_HEADS))
        stats = jnp.concatenate([st[:, :, 0:1, 0:FOX_HEADS], k_sq, st[:, :, None, 0:FOX_HEADS, FOX_HEADS]], axis=2)
        y_fox = _fox_attention(stats, pt, k_aug, b, s, tq=FOX_TQ, hp=FOX_HEADS_PER_STEP)
        y_swa = _swa_attention(sink[l], pt, pn, swa_bias, b, s, nblk=SWA_BLOCKS_PER_STEP)
        kv = _norm_proj(memf, mem_norm_g[l], wxkv, l)
        kT = kv[:, :d].reshape(b, mem_len, d).transpose(0, 2, 1)
        v = kv[:, d:].reshape(b, mem_len, d)
        xf = _layer_tail(xf, pn, y_fox, y_swa, w_br, w_out, xattn_norm_g[l], wq, kT, v, wxo,
                         ffn_norm_g[l], wg, wu, wd, final_norm_g, final_norm=(l == depth - 1),
                         layer=l, seq=s, tm=ROW_TILE)
    return xf.reshape(b, s, d)
```
